```python
import jax, jax.numpy as jnp
from jax import lax
import numpy as np

D_MODEL = 1024
BATCH = 8
SEQ = 4096
DEPTH = 1

CHUNK = 64

CONV_GROUPS = 8
CONV_GROUP_DIM = 64
CONV_WIDTH = CONV_GROUPS * CONV_GROUP_DIM
CONV_K = 3
ATTN_HEADS = 8
HEAD_DIM = 64
ATTN_WIDTH = ATTN_HEADS * HEAD_DIM
MIX_WIDTH = CONV_WIDTH + ATTN_WIDTH
IN_COLS = 3 * CONV_WIDTH + 3 * ATTN_WIDTH
D_FF = 4 * D_MODEL
Q_BLOCK = 128
ALPHA = float((2 * DEPTH) ** 0.25)
BETA = float((8 * DEPTH) ** -0.25)
LN_EPS = 1e-5
RMS_EPS = 1e-6

kernel_name = "hybrid_shortconv_stickbreaking_deepnorm_block"


def layer_norm(x, g, b):
    xf = x.astype(jnp.float32)
    mu = jnp.mean(xf, axis=-1, keepdims=True)
    var = jnp.mean(jnp.square(xf - mu), axis=-1, keepdims=True)
    y = (xf - mu) * lax.rsqrt(var + LN_EPS)
    return (y * g.astype(jnp.float32) + b.astype(jnp.float32)).astype(x.dtype)


def group_rmsnorm(y, g, n_groups):
    bsz, seq, w = y.shape
    yf = y.astype(jnp.float32).reshape(bsz, seq, n_groups, w // n_groups)
    yf = yf * lax.rsqrt(jnp.mean(jnp.square(yf), axis=-1, keepdims=True) + RMS_EPS)
    return (yf.reshape(bsz, seq, w) * g.astype(jnp.float32)).astype(y.dtype)


def short_gated_conv(b_gate, c_gate, h, w_conv):
    seq = h.shape[1]
    u = c_gate * h
    u_pad = jnp.pad(u, ((0, 0), (CONV_K - 1, 0), (0, 0)))
    y = sum(w_conv[j] * u_pad[:, j:j + seq, :] for j in range(CONV_K))
    return b_gate * y


def stick_breaking_attention(q, k, v):
    bsz, seq, nh, dh = q.shape
    scale = dh ** -0.5
    qh = jnp.transpose(q, (0, 2, 1, 3))
    kh = jnp.transpose(k, (0, 2, 1, 3))
    vh = jnp.transpose(v, (0, 2, 1, 3))
    outs = []
    for i in range(seq // Q_BLOCK):
        q0 = i * Q_BLOCK
        kend = q0 + Q_BLOCK
        qb = qh[:, :, q0:kend]
        kb = kh[:, :, :kend]
        vb = vh[:, :, :kend]
        z = jnp.einsum('bhqd,bhkd->bhqk', qb, kb).astype(jnp.float32) * scale
        t_idx = q0 + jnp.arange(Q_BLOCK)[:, None]
        s_idx = jnp.arange(kend)[None, :]
        mask = s_idx < t_idx
        log_keep = jnp.where(mask, jax.nn.log_sigmoid(-z), 0.0)
        suffix = lax.cumsum(log_keep, axis=3, reverse=True) - log_keep
        a = jnp.where(mask, jnp.exp(jax.nn.log_sigmoid(z) + suffix), 0.0)
        outs.append(jnp.einsum('bhqk,bhkd->bhqd', a.astype(vb.dtype), vb))
    o = jnp.concatenate(outs, axis=2)
    return jnp.transpose(o, (0, 2, 1, 3)).reshape(bsz, seq, nh * dh)


def _fwd_setup_inputs(seed: int = 0) -> dict:
    key = jax.random.key(seed)
    ks = jax.random.split(key, 13)
    f32 = jnp.float32
    x = jax.random.normal(ks[0], (BATCH, SEQ, D_MODEL), f32)
    w_in = jax.random.normal(ks[1], (DEPTH, D_MODEL, IN_COLS), f32) * D_MODEL ** -0.5
    conv_w = jax.random.normal(ks[2], (DEPTH, CONV_K, CONV_WIDTH), f32) * CONV_K ** -0.5
    g_conv = 1.0 + 0.02 * jax.random.normal(ks[3], (DEPTH, CONV_WIDTH), f32)
    g_attn = 1.0 + 0.02 * jax.random.normal(ks[4], (DEPTH, ATTN_WIDTH), f32)
    w_out = jax.random.normal(ks[5], (DEPTH, MIX_WIDTH, D_MODEL), f32) * (MIX_WIDTH ** -0.5) * BETA
    ln1_g = 1.0 + 0.02 * jax.random.normal(ks[6], (DEPTH, D_MODEL), f32)
    ln1_b = 0.02 * jax.random.normal(ks[7], (DEPTH, D_MODEL), f32)
    w_up = jax.random.normal(ks[8], (DEPTH, D_MODEL, D_FF), f32) * D_MODEL ** -0.5
    w_down = jax.random.normal(ks[9], (DEPTH, D_FF, D_MODEL), f32) * (D_FF ** -0.5) * BETA
    ln2_g = 1.0 + 0.02 * jax.random.normal(ks[10], (DEPTH, D_MODEL), f32)
    ln2_b = 0.02 * jax.random.normal(ks[11], (DEPTH, D_MODEL), f32)
    return {"x": x, "w_in": w_in, "conv_w": conv_w, "g_conv": g_conv,
            "g_attn": g_attn, "w_out": w_out, "ln1_g": ln1_g, "ln1_b": ln1_b,
            "w_up": w_up, "w_down": w_down, "ln2_g": ln2_g, "ln2_b": ln2_b}


def _fwd_reference(x, w_in, conv_w, g_conv, g_attn, w_out, ln1_g, ln1_b,
              w_up, w_down, ln2_g, ln2_b):
    bsz, seq, _ = x.shape
    split_at = [CONV_WIDTH, 2 * CONV_WIDTH, 3 * CONV_WIDTH,
                3 * CONV_WIDTH + ATTN_WIDTH, 3 * CONV_WIDTH + 2 * ATTN_WIDTH]
    for l in range(DEPTH):
        proj = jnp.einsum('bsd,dc->bsc', x, w_in[l])
        b_gate, c_gate, h_conv, q, k, v = jnp.split(proj, split_at, axis=-1)
        y_conv = short_gated_conv(b_gate, c_gate, h_conv, conv_w[l])
        qh = q.reshape(bsz, seq, ATTN_HEADS, HEAD_DIM)
        kh = k.reshape(bsz, seq, ATTN_HEADS, HEAD_DIM)
        vh = v.reshape(bsz, seq, ATTN_HEADS, HEAD_DIM)
        y_attn = stick_breaking_attention(qh, kh, vh)
        y_mix = jnp.concatenate([group_rmsnorm(y_conv, g_conv[l], CONV_GROUPS),
                                 group_rmsnorm(y_attn, g_attn[l], ATTN_HEADS)], axis=-1)
        mix_out = jnp.einsum('bsc,cd->bsd', y_mix, w_out[l])
        x = layer_norm(ALPHA * x + mix_out, ln1_g[l], ln1_b[l])
        hid = jnp.square(jax.nn.relu(jnp.einsum('bsd,df->bsf', x, w_up[l])))
        ffn_out = jnp.einsum('bsf,fd->bsd', hid, w_down[l])
        x = layer_norm(ALPHA * x + ffn_out, ln2_g[l], ln2_b[l])
    return x


import jax as _jax
import jax.numpy as _jnp

TWIN_FORMAT = 'train_step'
FWD_PARAMS = ['x', 'w_in', 'conv_w', 'g_conv', 'g_attn', 'w_out', 'ln1_g', 'ln1_b', 'w_up', 'w_down', 'ln2_g', 'ln2_b']
TWIN_WEIGHTS = ['w_in', 'conv_w', 'g_conv', 'g_attn', 'w_out', 'ln1_g', 'ln1_b', 'w_up', 'w_down', 'ln2_g', 'ln2_b']
TWIN_DIFF_INPUT = 'x'
TWIN_INPUTS = ['x', 'w_in', 'conv_w', 'g_conv', 'g_attn', 'w_out', 'ln1_g', 'ln1_b', 'w_up', 'w_down', 'ln2_g', 'ln2_b', 'loss_target', 'm_w_in', 'm_conv_w', 'm_g_conv', 'm_g_attn', 'm_w_out', 'm_ln1_g', 'm_ln1_b', 'm_w_up', 'm_w_down', 'm_ln2_g', 'm_ln2_b', 'v_w_in', 'v_conv_w', 'v_g_conv', 'v_g_attn', 'v_w_out', 'v_ln1_g', 'v_ln1_b', 'v_w_up', 'v_w_down', 'v_ln2_g', 'v_ln2_b']
TWIN_OUTPUTS = ['loss', 'grad_x', 'grad_w_in', 'grad_conv_w', 'grad_g_conv', 'grad_g_attn', 'grad_w_out', 'grad_ln1_g', 'grad_ln1_b', 'grad_w_up', 'grad_w_down', 'grad_ln2_g', 'grad_ln2_b', 'delta_w_in', 'delta_conv_w', 'delta_g_conv', 'delta_g_attn', 'delta_w_out', 'delta_ln1_g', 'delta_ln1_b', 'delta_w_up', 'delta_w_down', 'delta_ln2_g', 'delta_ln2_b', 'new_m_w_in', 'new_m_conv_w', 'new_m_g_conv', 'new_m_g_attn', 'new_m_w_out', 'new_m_ln1_g', 'new_m_ln1_b', 'new_m_w_up', 'new_m_w_down', 'new_m_ln2_g', 'new_m_ln2_b', 'new_v_w_in', 'new_v_conv_w', 'new_v_g_conv', 'new_v_g_attn', 'new_v_w_out', 'new_v_ln1_g', 'new_v_ln1_b', 'new_v_w_up', 'new_v_w_down', 'new_v_ln2_g', 'new_v_ln2_b']
TWIN_LEAF_KINDS = {'loss': 'loss', 'grad_x': 'grad_x', 'grad_w_in': 'grad_w', 'grad_conv_w': 'grad_w', 'grad_g_conv': 'grad_w', 'grad_g_attn': 'grad_w', 'grad_w_out': 'grad_w', 'grad_ln1_g': 'grad_w', 'grad_ln1_b': 'grad_w', 'grad_w_up': 'grad_w', 'grad_w_down': 'grad_w', 'grad_ln2_g': 'grad_w', 'grad_ln2_b': 'grad_w', 'delta_w_in': 'delta_w', 'delta_conv_w': 'delta_w', 'delta_g_conv': 'delta_w', 'delta_g_attn': 'delta_w', 'delta_w_out': 'delta_w', 'delta_ln1_g': 'delta_w', 'delta_ln1_b': 'delta_w', 'delta_w_up': 'delta_w', 'delta_w_down': 'delta_w', 'delta_ln2_g': 'delta_w', 'delta_ln2_b': 'delta_w', 'new_m_w_in': 'new_m', 'new_m_conv_w': 'new_m', 'new_m_g_conv': 'new_m', 'new_m_g_attn': 'new_m', 'new_m_w_out': 'new_m', 'new_m_ln1_g': 'new_m', 'new_m_ln1_b': 'new_m', 'new_m_w_up': 'new_m', 'new_m_w_down': 'new_m', 'new_m_ln2_g': 'new_m', 'new_m_ln2_b': 'new_m', 'new_v_w_in': 'new_v', 'new_v_conv_w': 'new_v', 'new_v_g_conv': 'new_v', 'new_v_g_attn': 'new_v', 'new_v_w_out': 'new_v', 'new_v_ln1_g': 'new_v', 'new_v_ln1_b': 'new_v', 'new_v_w_up': 'new_v', 'new_v_w_down': 'new_v', 'new_v_ln2_g': 'new_v', 'new_v_ln2_b': 'new_v'}


def _forward(args):
    return _fwd_reference(*[args[k] for k in FWD_PARAMS])


def _output_shape():
    out = _jax.eval_shape(lambda: _forward(_fwd_setup_inputs(0)))
    return out.shape, out.dtype

N_MICROBATCH = 1
ADAM_LR = 0.001
ADAM_B1 = 0.9
ADAM_B2 = 0.999
ADAM_EPS = 1e-08
ADAM_WD = 0.01
ADAM_STEP = 10
PER_EXAMPLE_BATCH_AXIS = {'x': 0, 'loss_target': 0}
SHARED_INPUTS = []
_WEIGHT_DTYPES = {'w_in': _jnp.float32, 'conv_w': _jnp.float32, 'g_conv': _jnp.float32, 'g_attn': _jnp.float32, 'w_out': _jnp.float32, 'ln1_g': _jnp.float32, 'ln1_b': _jnp.float32, 'w_up': _jnp.float32, 'w_down': _jnp.float32, 'ln2_g': _jnp.float32, 'ln2_b': _jnp.float32}
MOMENT_SCALE = {'w_in': 7.394712e-02, 'conv_w': 9.054593e-02, 'g_conv': 9.095872e-02, 'g_attn': 8.853932e-02, 'w_out': 1.390734e-01, 'ln1_g': 7.211231e-01, 'ln1_b': 4.473229e-01, 'w_up': 5.349985e-02, 'w_down': 2.026268e-01, 'ln2_g': 3.211700e+01, 'ln2_b': 6.642086e+00}


def _to_microbatches(a, axis):
    t = _jnp.moveaxis(a, axis, 0)
    t = t.reshape((N_MICROBATCH, t.shape[0] // N_MICROBATCH) + t.shape[1:])
    return _jnp.moveaxis(t, 1, axis + 1)


def setup_inputs(seed: int = 0) -> dict:
    inp = _fwd_setup_inputs(seed)
    key = _jax.random.fold_in(_jax.random.key(seed), 7919)
    shape, _ = _output_shape()
    out = dict(inp)
    out["loss_target"] = _jax.random.normal(_jax.random.fold_in(key, 0), shape, _jnp.float32)
    for i, name in enumerate(TWIN_WEIGHTS):
        w = inp[name].astype(_jnp.float32)
        if MOMENT_SCALE is None:
            s = _jnp.sqrt(_jnp.mean(_jnp.square(w)) + 1e-30)
        else:
            s = MOMENT_SCALE[name]
        km, kv = _jax.random.split(_jax.random.fold_in(key, i + 1))
        out[name] = w
        out["m_" + name] = s * _jax.random.normal(km, w.shape, _jnp.float32)
        out["v_" + name] = (s * s) * _jax.random.uniform(kv, w.shape, _jnp.float32, 0.5, 1.5)
    if N_MICROBATCH > 1:
        for name, axis in PER_EXAMPLE_BATCH_AXIS.items():
            out[name] = _to_microbatches(out[name], axis)
    return {'x': out['x'], 'w_in': out['w_in'], 'conv_w': out['conv_w'], 'g_conv': out['g_conv'], 'g_attn': out['g_attn'], 'w_out': out['w_out'], 'ln1_g': out['ln1_g'], 'ln1_b': out['ln1_b'], 'w_up': out['w_up'], 'w_down': out['w_down'], 'ln2_g': out['ln2_g'], 'ln2_b': out['ln2_b'], 'loss_target': out['loss_target'], 'm_w_in': out['m_w_in'], 'm_conv_w': out['m_conv_w'], 'm_g_conv': out['m_g_conv'], 'm_g_attn': out['m_g_attn'], 'm_w_out': out['m_w_out'], 'm_ln1_g': out['m_ln1_g'], 'm_ln1_b': out['m_ln1_b'], 'm_w_up': out['m_w_up'], 'm_w_down': out['m_w_down'], 'm_ln2_g': out['m_ln2_g'], 'm_ln2_b': out['m_ln2_b'], 'v_w_in': out['v_w_in'], 'v_conv_w': out['v_conv_w'], 'v_g_conv': out['v_g_conv'], 'v_g_attn': out['v_g_attn'], 'v_w_out': out['v_w_out'], 'v_ln1_g': out['v_ln1_g'], 'v_ln1_b': out['v_ln1_b'], 'v_w_up': out['v_w_up'], 'v_w_down': out['v_w_down'], 'v_ln2_g': out['v_ln2_g'], 'v_ln2_b': out['v_ln2_b']}


def _loss(weights, diff, rest, loss_target):
    with _jax.named_scope("forward"):
        args = {**rest, TWIN_DIFF_INPUT: diff, **{k: w.astype(_WEIGHT_DTYPES[k]) for k, w in weights.items()}}
        y = _forward(args)
    with _jax.named_scope("loss_head"):
        err = _jnp.square(y.astype(_jnp.float32) - loss_target)
        return 0.5 * _jnp.sum(_jnp.mean(err, axis=-1)) if err.ndim else 0.5 * err


def _adamw(w, g, m, v):
    m = ADAM_B1 * m + (1.0 - ADAM_B1) * g
    v = ADAM_B2 * v + (1.0 - ADAM_B2) * _jnp.square(g)
    m_hat = m / (1.0 - ADAM_B1 ** ADAM_STEP)
    v_hat = v / (1.0 - ADAM_B2 ** ADAM_STEP)
    delta = -ADAM_LR * (m_hat / (_jnp.sqrt(v_hat) + ADAM_EPS) + ADAM_WD * w)
    return delta, m, v


def reference(x, w_in, conv_w, g_conv, g_attn, w_out, ln1_g, ln1_b, w_up, w_down, ln2_g, ln2_b, loss_target, m_w_in, m_conv_w, m_g_conv, m_g_attn, m_w_out, m_ln1_g, m_ln1_b, m_w_up, m_w_down, m_ln2_g, m_ln2_b, v_w_in, v_conv_w, v_g_conv, v_g_attn, v_w_out, v_ln1_g, v_ln1_b, v_w_up, v_w_down, v_ln2_g, v_ln2_b):
    given = dict(x=x, w_in=w_in, conv_w=conv_w, g_conv=g_conv, g_attn=g_attn, w_out=w_out, ln1_g=ln1_g, ln1_b=ln1_b, w_up=w_up, w_down=w_down, ln2_g=ln2_g, ln2_b=ln2_b, loss_target=loss_target, m_w_in=m_w_in, m_conv_w=m_conv_w, m_g_conv=m_g_conv, m_g_attn=m_g_attn, m_w_out=m_w_out, m_ln1_g=m_ln1_g, m_ln1_b=m_ln1_b, m_w_up=m_w_up, m_w_down=m_w_down, m_ln2_g=m_ln2_g, m_ln2_b=m_ln2_b, v_w_in=v_w_in, v_conv_w=v_conv_w, v_g_conv=v_g_conv, v_g_attn=v_g_attn, v_w_out=v_w_out, v_ln1_g=v_ln1_g, v_ln1_b=v_ln1_b, v_w_up=v_w_up, v_w_down=v_w_down, v_ln2_g=v_ln2_g, v_ln2_b=v_ln2_b)
    weights = {n: given[n] for n in TWIN_WEIGHTS}
    shared = {n: given[n] for n in SHARED_INPUTS}
    per_example = {n: given[n] for n in ['x']}
    grad_fn = _jax.value_and_grad(_loss, argnums=(0, 1))

    def one_microbatch(ex, loss_target):
        ex = dict(ex)
        diff = ex.pop(TWIN_DIFF_INPUT)
        return grad_fn(weights, diff, {**shared, **ex}, loss_target)

    if N_MICROBATCH == 1:
        loss, (grad_w, grad_x) = one_microbatch(per_example, given["loss_target"])
    else:
        def body(carry, xs):
            loss_sum, grad_sum = carry
            l_k, (gw_k, gx_k) = one_microbatch(xs[0], xs[1])
            with _jax.named_scope("update"):
                return (loss_sum + l_k, _jax.tree.map(_jnp.add, grad_sum, gw_k)), gx_k

        init = (_jnp.zeros((), _jnp.float32), _jax.tree.map(_jnp.zeros_like, weights))
        (loss, grad_w), grad_x = _jax.lax.scan(body, init, (per_example, given["loss_target"]))
    with _jax.named_scope("update"):
        delta_w, new_m, new_v = {}, {}, {}
        for n in TWIN_WEIGHTS:
            delta_w[n], new_m[n], new_v[n] = _adamw(weights[n], grad_w[n], given["m_" + n], given["v_" + n])
    return (loss, grad_x, *[grad_w[n] for n in TWIN_WEIGHTS], *[delta_w[n] for n in TWIN_WEIGHTS],
            *[new_m[n] for n in TWIN_WEIGHTS], *[new_v[n] for n in TWIN_WEIGHTS])
```

```python
import functools

import jax
import jax.numpy as jnp
from jax import lax
from jax.experimental import pallas as pl
from jax.experimental.pallas import tpu as pltpu

F32 = jnp.float32
BF16 = jnp.bfloat16
MESH = pl.DeviceIdType.MESH

N_DEV = 8
HEAD_DIM = 64
LANES = 128
Q_BLOCK = 128
ALPHA = 2.0 ** 0.25
LN_EPS = 1e-5
RMS_EPS = 1e-6
ADAM_LR, ADAM_B1, ADAM_B2, ADAM_EPS, ADAM_WD, ADAM_STEP = 0.001, 0.9, 0.999, 1e-08, 0.01, 10
VMEM_LIMIT_BYTES = 48 * 1024 * 1024

_NT = ((1,), (1,))
_NN = ((1,), (0,))
_TN = ((0,), (0,))


def _params(sem=None):
    return pltpu.CompilerParams(dimension_semantics=sem, vmem_limit_bytes=VMEM_LIMIT_BYTES)


def _dot(a, b, contract):
    return lax.dot_general(a.astype(BF16), b.astype(BF16), (contract, ((), ())), preferred_element_type=F32)


def _matmul(name, a, b, a_spec, b_spec, contract, grid, acc_shape, outs, epilogue, extras=()):
    n_extra, n_out, nk = len(extras), len(outs), grid[2]

    def body(a_ref, b_ref, *rest):
        extra_refs = rest[:n_extra]
        out_refs = rest[n_extra:n_extra + n_out]
        i, k = pl.program_id(0), pl.program_id(2)
        part = _dot(a_ref[...], b_ref[...], contract)
        if nk == 1:
            epilogue(part, extra_refs, out_refs, i)
            return
        acc_ref = rest[-1]

        @pl.when(k == 0)
        def _():
            acc_ref[...] = part

        @pl.when(k > 0)
        def _():
            acc_ref[...] += part

        @pl.when(k == nk - 1)
        def _():
            epilogue(acc_ref[...], extra_refs, out_refs, i)

    return pl.pallas_call(
        body, name=name, grid=grid,
        in_specs=[a_spec, b_spec] + [s for _, s in extras],
        out_specs=[s for _, s in outs],
        out_shape=[o for o, _ in outs],
        scratch_shapes=[pltpu.VMEM(acc_shape, F32)] if nk > 1 else [],
        compiler_params=_params(("parallel", "parallel", "arbitrary")),
    )(a, b, *[e for e, _ in extras])


def _ln_fwd(r, g, b):
    mu = jnp.mean(r, axis=-1, keepdims=True)
    xc = r - mu
    rstd = lax.rsqrt(jnp.mean(xc * xc, axis=-1, keepdims=True) + LN_EPS)
    xhat = xc * rstd
    return xhat * g + b, xhat, rstd


def _ln_bwd(dy, xhat, rstd, g):
    dxh = dy * g
    m1 = jnp.mean(dxh, axis=-1, keepdims=True)
    m2 = jnp.mean(dxh * xhat, axis=-1, keepdims=True)
    return rstd * (dxh - m1 - xhat * m2)


def _rows8(v):
    n, c = v.shape
    return jnp.sum(v.reshape(n // 8, 8, c), axis=0)


def _accumulate(ref, i, n_steps, part8, all_lanes=False):
    @pl.when(i == 0)
    def _():
        ref[...] = part8

    @pl.when(i > 0)
    def _():
        ref[...] += part8

    @pl.when(i == n_steps - 1)
    def _():
        tot = jnp.sum(ref[...], axis=0, keepdims=True)
        if all_lanes:
            tot = jnp.sum(tot, axis=1, keepdims=True)
        ref[...] = jnp.broadcast_to(tot, ref.shape)


def _group_sums(v, lane_lo):
    s0 = jnp.sum(jnp.where(lane_lo, v, 0.0), axis=-1, keepdims=True)
    s1 = jnp.sum(jnp.where(lane_lo, 0.0, v), axis=-1, keepdims=True)
    return jnp.where(lane_lo, s0, s1)


def _place():
    return lax.axis_index("x"), lax.axis_index("y"), lax.axis_index("c")


def _all_gather(name, shards, out_dtype):
    n = len(shards)

    def body(*refs):
        in_refs, out_refs, stage = refs[:n], refs[n:2 * n], refs[2 * n:3 * n]
        send_sems, recv_sems, local_sems = refs[3 * n:]
        x, y, c = _place()
        me, sibling = (x, y, c), (x, y, 1 - c)
        chips = [(1 - x, y), (x, 1 - y), (1 - x, 1 - y)]

        def copy(a, k, block, to, src=None):
            slot = out_refs[a].at[4 * block[0] + 2 * block[1] + block[2]]
            return pltpu.make_async_remote_copy(
                src_ref=slot if src is None else src, dst_ref=slot,
                send_sem=send_sems.at[a, k], recv_sem=recv_sems.at[a, k],
                device_id=to, device_id_type=MESH)

        mine, first, passed = [], [], []
        for a in range(n):
            stage[a][...] = in_refs[a][...].astype(out_dtype)
            mine.append(pltpu.make_async_copy(stage[a], out_refs[a].at[4 * x + 2 * y + c], local_sems.at[a]))
            mine[a].start()
            first.append([copy(a, 0, me, sibling, src=stage[a])]
                         + [copy(a, 1 + j, me, (*chip, c), src=stage[a]) for j, chip in enumerate(chips)])
            for cp in first[a]:
                cp.start()
            passed.append([copy(a, 4 + j, (*chip, c), sibling) for j, chip in enumerate(chips)])
        for j, chip in enumerate(chips):
            for a in range(n):
                copy(a, 1 + j, (*chip, c), me).wait_recv()
                passed[a][j].start()
        for a in range(n):
            copy(a, 0, sibling, me).wait_recv()
            for j, chip in enumerate(chips):
                copy(a, 4 + j, (*chip, 1 - c), me).wait_recv()
        for a in range(n):
            for cp in first[a] + passed[a]:
                cp.wait_send()
            mine[a].wait()

    return pl.pallas_call(
        body, name=name,
        in_specs=[pl.BlockSpec(memory_space=pltpu.VMEM)] * n,
        out_specs=[pl.BlockSpec(memory_space=pl.ANY)] * n,
        out_shape=[jax.ShapeDtypeStruct((N_DEV,) + s.shape, out_dtype) for s in shards],
        scratch_shapes=[pltpu.VMEM(s.shape, out_dtype) for s in shards]
        + [pltpu.SemaphoreType.DMA((n, 7)), pltpu.SemaphoreType.DMA((n, 7)), pltpu.SemaphoreType.DMA((n,))],
        compiler_params=pltpu.CompilerParams(vmem_limit_bytes=VMEM_LIMIT_BYTES),
    )(*shards)


def _pair_exchange(name, grads):
    n = len(grads)

    def body(*refs):
        g_refs, land_refs = refs[:n], refs[n:2 * n]
        send_sems, recv_sems = refs[2 * n:]
        x, y, c = _place()
        copies = []
        for a in range(n):
            for q in range(4):
                cp = pltpu.make_async_remote_copy(
                    src_ref=g_refs[a].at[2 * q + (1 - c)], dst_ref=land_refs[a].at[q],
                    send_sem=send_sems.at[a, q], recv_sem=recv_sems.at[a, q],
                    device_id=(x, y, 1 - c), device_id_type=MESH)
                cp.start()
                copies.append(cp)
        for cp in copies:
            cp.wait()

    return pl.pallas_call(
        body, name=name,
        in_specs=[pl.BlockSpec(memory_space=pl.ANY)] * n,
        out_specs=[pl.BlockSpec(memory_space=pl.ANY)] * n,
        out_shape=[jax.ShapeDtypeStruct((4,) + g.shape[1:], g.dtype) for g in grads],
        scratch_shapes=[pltpu.SemaphoreType.DMA((n, 4)), pltpu.SemaphoreType.DMA((n, 4))],
    )(*grads)


def _chip_exchange(name, pair_sums):
    n = len(pair_sums)

    def body(*refs):
        p_refs, land_refs = refs[:n], refs[n:2 * n]
        send_sems, recv_sems = refs[2 * n:]
        x, y, c = _place()
        chips = [(1 - x, y), (x, 1 - y), (1 - x, 1 - y)]
        copies = []
        for a in range(n):
            for k, (px, py) in enumerate(chips):
                cp = pltpu.make_async_remote_copy(
                    src_ref=p_refs[a].at[2 * px + py], dst_ref=land_refs[a].at[k],
                    send_sem=send_sems.at[a, k], recv_sem=recv_sems.at[a, k],
                    device_id=(px, py, c), device_id_type=MESH)
                cp.start()
                copies.append(cp)
        for cp in copies:
            cp.wait()

    return pl.pallas_call(
        body, name=name,
        in_specs=[pl.BlockSpec(memory_space=pl.ANY)] * n,
        out_specs=[pl.BlockSpec(memory_space=pl.ANY)] * n,
        out_shape=[jax.ShapeDtypeStruct((3,) + p.shape[1:], p.dtype) for p in pair_sums],
        scratch_shapes=[pltpu.SemaphoreType.DMA((n, 3)), pltpu.SemaphoreType.DMA((n, 3))],
    )(*pair_sums)


def _pair_sum(name, g_f32, landed, c_idx, q_me):
    _, rows, cols = g_f32.shape
    tr = min(rows, 256)

    def body(c_ref, q_ref, g_ref, l_ref, wire_ref, own_ref):
        q = pl.program_id(1)
        s = g_ref[...] + l_ref[...].astype(F32)
        wire_ref[...] = s.astype(BF16)

        @pl.when(q == q_ref[0])
        def _():
            own_ref[...] = s

    return pl.pallas_call(
        body, name=name,
        grid_spec=pltpu.PrefetchScalarGridSpec(
            num_scalar_prefetch=2, grid=(rows // tr, 4),
            in_specs=[pl.BlockSpec((None, tr, cols), lambda i, q, c_ref, q_ref: (2 * q + c_ref[0], i, 0)),
                      pl.BlockSpec((None, tr, cols), lambda i, q, c_ref, q_ref: (q, i, 0))],
            out_specs=[pl.BlockSpec((None, tr, cols), lambda i, q, c_ref, q_ref: (q, i, 0)),
                       pl.BlockSpec((tr, cols), lambda i, q, c_ref, q_ref: (i, 0))]),
        out_shape=[jax.ShapeDtypeStruct((4, rows, cols), BF16), jax.ShapeDtypeStruct((rows, cols), F32)],
        compiler_params=_params(("parallel", "arbitrary")),
    )(c_idx, q_me, g_f32, landed)


def _adamw(w, g, m, v):
    m = ADAM_B1 * m + (1.0 - ADAM_B1) * g
    v = ADAM_B2 * v + (1.0 - ADAM_B2) * (g * g)
    m_hat = m / (1.0 - ADAM_B1 ** ADAM_STEP)
    v_hat = v / (1.0 - ADAM_B2 ** ADAM_STEP)
    delta = -ADAM_LR * (m_hat / (jnp.sqrt(v_hat) + ADAM_EPS) + ADAM_WD * w)
    return delta, m, v


def _shard_update(name, own, landed, w, m, v):
    rows, cols = own.shape
    tr = min(rows, 256)

    def body(own_ref, l_ref, w_ref, m_ref, v_ref, g_out, d_out, m_out, v_out):
        g = own_ref[...] + l_ref[0].astype(F32) + l_ref[1].astype(F32) + l_ref[2].astype(F32)
        d, m_new, v_new = _adamw(w_ref[...], g, m_ref[...], v_ref[...])
        g_out[...] = g
        d_out[...] = d
        m_out[...] = m_new
        v_out[...] = v_new

    tile = pl.BlockSpec((tr, cols), lambda i: (i, 0))
    return pl.pallas_call(
        body, name=name, grid=(rows // tr,),
        in_specs=[tile, pl.BlockSpec((3, tr, cols), lambda i: (0, i, 0)), tile, tile, tile],
        out_specs=[tile] * 4,
        out_shape=[jax.ShapeDtypeStruct((rows, cols), F32)] * 4,
        compiler_params=_params(("parallel",)),
    )(own, landed, w, m, v)


def _small_update(parts, w, m, v):
    def body(p_ref, w_ref, m_ref, v_ref, g_out, d_out, m_out, v_out):
        g = p_ref[0]
        for d in range(1, N_DEV):
            g = g + p_ref[d]
        dl, m_new, v_new = _adamw(w_ref[...], g, m_ref[...], v_ref[...])
        g_out[...] = g
        d_out[...] = dl
        m_out[...] = m_new
        v_out[...] = v_new

    return pl.pallas_call(
        body, name="small_update",
        out_shape=[jax.ShapeDtypeStruct(w.shape, F32)] * 4,
        compiler_params=pltpu.CompilerParams(vmem_limit_bytes=VMEM_LIMIT_BYTES),
    )(parts, w, m, v)


CONV_CHUNK = 512


def _conv_taps(u_s, lo, chunk):
    ext = u_s[pl.ds(lo, chunk + 8), :]
    return ext[8:], pltpu.roll(ext, 1, 0)[8:], pltpu.roll(ext, 2, 0)[8:]


def _conv_fwd(gates, conv_w, g_conv, width):
    t_len = gates.shape[0]
    nblk = width // LANES
    n_chunks = t_len // CONV_CHUNK

    def body(bg_ref, cg_ref, h_ref, w_ref, g_ref, y_ref, u_s):
        lane_lo = lax.broadcasted_iota(jnp.int32, (CONV_CHUNK, LANES), 1) < HEAD_DIM
        u_s[0:8, :] = jnp.zeros((8, LANES), F32)

        def fill(r, _):
            lo = pl.multiple_of(r * CONV_CHUNK, CONV_CHUNK)
            u_s[pl.ds(pl.multiple_of(lo + 8, 8), CONV_CHUNK), :] = cg_ref[pl.ds(lo, CONV_CHUNK), :] * h_ref[pl.ds(lo, CONV_CHUNK), :]
            return 0

        lax.fori_loop(0, n_chunks, fill, 0)
        w = w_ref[...]
        gain = g_ref[...]

        def step(r, _):
            lo = pl.multiple_of(r * CONV_CHUNK, CONV_CHUNK)
            u0, u1, u2 = _conv_taps(u_s, lo, CONV_CHUNK)
            y = bg_ref[pl.ds(lo, CONV_CHUNK), :] * (w[2:3] * u0 + w[1:2] * u1 + w[0:1] * u2)
            rs = lax.rsqrt(_group_sums(y * y, lane_lo) * (1.0 / HEAD_DIM) + RMS_EPS)
            y_ref[pl.ds(lo, CONV_CHUNK), :] = (y * rs * gain).astype(BF16)
            return 0

        lax.fori_loop(0, n_chunks, step, 0)

    col = lambda off: pl.BlockSpec((t_len, LANES), lambda p: (0, off + p))
    return pl.pallas_call(
        body, name="conv_fwd", grid=(nblk,),
        in_specs=[col(0), col(nblk), col(2 * nblk),
                  pl.BlockSpec((3, LANES), lambda p: (0, p)), pl.BlockSpec((1, LANES), lambda p: (0, p))],
        out_specs=col(0),
        out_shape=jax.ShapeDtypeStruct((t_len, 2 * width), BF16),
        scratch_shapes=[pltpu.VMEM((t_len + 8, LANES), F32)],
        compiler_params=_params(("parallel",)),
    )(gates, gates, gates, conv_w, g_conv)


def _conv_bwd(gates, dymix, conv_w, g_conv, width):
    t_len = gates.shape[0]
    nblk = width // LANES
    n_chunks = t_len // CONV_CHUNK

    def body(bg_ref, cg_ref, h_ref, dy_ref, w_ref, g_ref, dbg_ref, dcg_ref, dh_ref, gg_ref, gw_ref, u_s, dc_s):
        lane_lo = lax.broadcasted_iota(jnp.int32, (CONV_CHUNK, LANES), 1) < HEAD_DIM
        u_s[0:8, :] = jnp.zeros((8, LANES), F32)
        dc_s[t_len:t_len + 8, :] = jnp.zeros((8, LANES), F32)

        def fill(r, _):
            lo = pl.multiple_of(r * CONV_CHUNK, CONV_CHUNK)
            u_s[pl.ds(pl.multiple_of(lo + 8, 8), CONV_CHUNK), :] = cg_ref[pl.ds(lo, CONV_CHUNK), :] * h_ref[pl.ds(lo, CONV_CHUNK), :]
            return 0

        lax.fori_loop(0, n_chunks, fill, 0)
        w = w_ref[...]
        gain = g_ref[...]

        def step(r, acc):
            gg, gw0, gw1, gw2 = acc
            lo = pl.multiple_of(r * CONV_CHUNK, CONV_CHUNK)
            u0, u1, u2 = _conv_taps(u_s, lo, CONV_CHUNK)
            bg = bg_ref[pl.ds(lo, CONV_CHUNK), :]
            conv = w[2:3] * u0 + w[1:2] * u1 + w[0:1] * u2
            y = bg * conv
            rs = lax.rsqrt(_group_sums(y * y, lane_lo) * (1.0 / HEAD_DIM) + RMS_EPS)
            yn = y * rs
            dyn_raw = dy_ref[pl.ds(lo, CONV_CHUNK), :]
            dyn = dyn_raw * gain
            d_y = rs * (dyn - yn * (_group_sums(dyn * yn, lane_lo) * (1.0 / HEAD_DIM)))
            dbg_ref[pl.ds(lo, CONV_CHUNK), :] = (d_y * conv).astype(BF16)
            dconv = d_y * bg
            dc_s[pl.ds(lo, CONV_CHUNK), :] = dconv
            return (gg + _rows8(dyn_raw * yn), gw0 + _rows8(dconv * u2), gw1 + _rows8(dconv * u1),
                    gw2 + _rows8(dconv * u0))

        zero = jnp.zeros((8, LANES), F32)
        gg, gw0, gw1, gw2 = lax.fori_loop(0, n_chunks, step, (zero, zero, zero, zero))
        gg_ref[...] = jnp.broadcast_to(jnp.sum(gg, axis=0, keepdims=True), (8, LANES))
        row = lax.broadcasted_iota(jnp.int32, (8, LANES), 0)
        tot = lambda v: jnp.broadcast_to(jnp.sum(v, axis=0, keepdims=True), (8, LANES))
        gw_ref[...] = jnp.where(row == 0, tot(gw0), jnp.where(row == 1, tot(gw1), jnp.where(row == 2, tot(gw2), 0.0)))

        def back(r, _):
            lo = pl.multiple_of(r * CONV_CHUNK, CONV_CHUNK)
            ext = dc_s[pl.ds(lo, CONV_CHUNK + 8), :]
            d0 = ext[:CONV_CHUNK]
            d1 = pltpu.roll(ext, CONV_CHUNK + 7, 0)[:CONV_CHUNK]
            d2 = pltpu.roll(ext, CONV_CHUNK + 6, 0)[:CONV_CHUNK]
            du = w[2:3] * d0 + w[1:2] * d1 + w[0:1] * d2
            dcg_ref[pl.ds(lo, CONV_CHUNK), :] = (du * h_ref[pl.ds(lo, CONV_CHUNK), :]).astype(BF16)
            dh_ref[pl.ds(lo, CONV_CHUNK), :] = (du * cg_ref[pl.ds(lo, CONV_CHUNK), :]).astype(BF16)
            return 0

        lax.fori_loop(0, n_chunks, back, 0)

    col = lambda off: pl.BlockSpec((t_len, LANES), lambda p: (0, off + p))
    small = pl.BlockSpec((8, LANES), lambda p: (0, p))
    return pl.pallas_call(
        body, name="conv_bwd", grid=(nblk,),
        in_specs=[col(0), col(nblk), col(2 * nblk), col(0),
                  pl.BlockSpec((3, LANES), lambda p: (0, p)), pl.BlockSpec((1, LANES), lambda p: (0, p))],
        out_specs=[col(0), col(0), col(0), small, small],
        out_shape=[jax.ShapeDtypeStruct((t_len, width), BF16)] * 3 + [jax.ShapeDtypeStruct((8, width), F32)] * 2,
        scratch_shapes=[pltpu.VMEM((t_len + 8, LANES), F32), pltpu.VMEM((t_len + 8, LANES), F32)],
        compiler_params=_params(("parallel",)),
    )(gates, gates, gates, dymix, conv_w, g_conv)


def _cumsum_matrix(inclusive):
    j = lax.broadcasted_iota(jnp.int32, (2 * LANES, 2 * LANES), 0) % LANES
    s = lax.broadcasted_iota(jnp.int32, (2 * LANES, 2 * LANES), 1)
    keep = (s >= LANES) | ((j >= s) if inclusive else (j > s))
    return keep.astype(BF16)


def _suffix_sums(val, mat):
    hi = val.astype(BF16)
    lo = (val - hi.astype(F32)).astype(BF16)
    both = jnp.dot(jnp.concatenate([hi, lo], axis=1), mat, preferred_element_type=F32)
    return both[:, :LANES], both[:, LANES:]


def _stick_tile(qm, kb, carry, umat, causal):
    z = lax.dot_general(qm, kb, (_NT, ((), ())), preferred_element_type=F32)
    e = jnp.exp(-jnp.abs(z))
    log1pe = jnp.log(1.0 + e)
    lk = -(jnp.maximum(z, 0.0) + log1pe)
    lb = z + lk
    if causal is not None:
        lk = jnp.where(causal, lk, 0.0)
    within, total = _suffix_sums(lk, umat)
    a = jnp.exp(lb + within + carry)
    if causal is not None:
        a = jnp.where(causal, a, 0.0)
    return z, e, a, carry + total


def _attn_fwd(qkv, g_attn, ymix_in, width):
    t_len = qkv.shape[0]
    npair = width // LANES
    nqb = t_len // Q_BLOCK

    def body(q_ref, k_ref, v_ref, g_ref, umat_ref, ymix_in_ref, o_ref, o_fine_ref, ymix_ref):
        i = pl.program_id(1)
        lane = lax.broadcasted_iota(jnp.int32, (Q_BLOCK, LANES), 1)
        lane_lo = lane < HEAD_DIM
        causal = lane < lax.broadcasted_iota(jnp.int32, (Q_BLOCK, LANES), 0)
        umat = umat_ref[...]
        q = q_ref[...]
        outs, rests = [], []
        for head_mask in (lane_lo, jnp.logical_not(lane_lo)):
            qm = jnp.where(head_mask, q, jnp.zeros_like(q)) * jnp.asarray(HEAD_DIM ** -0.5, BF16)

            def tile(rows, state, tile_mask, qm=qm):
                carry, o_acc, o_rest = state
                _, _, a, carry = _stick_tile(qm, k_ref[pl.ds(rows, Q_BLOCK), :], carry, umat, tile_mask)
                vb = v_ref[pl.ds(rows, Q_BLOCK), :]
                a_hi = a.astype(BF16)
                a_lo = (a - a_hi.astype(F32)).astype(BF16)
                return (carry, o_acc + jnp.dot(a_hi, vb, preferred_element_type=F32),
                        o_rest + jnp.dot(a_lo, vb, preferred_element_type=F32))

            zero = jnp.zeros((Q_BLOCK, LANES), F32)
            state = tile(pl.multiple_of(i * Q_BLOCK, Q_BLOCK), (zero, zero, zero), causal)
            state = lax.fori_loop(
                1, i + 1, lambda n, st, tile=tile: tile(pl.multiple_of((i - n) * Q_BLOCK, Q_BLOCK), st, None), state)
            outs.append(state[1])
            rests.append(state[2])
        o = jnp.where(lane_lo, outs[0], outs[1])
        o_ref[...] = o
        o_fine_ref[...] = o + jnp.where(lane_lo, rests[0], rests[1])
        rs = lax.rsqrt(_group_sums(o * o, lane_lo) * (1.0 / HEAD_DIM) + RMS_EPS)
        ymix_ref[...] = (o * rs * g_ref[...]).astype(BF16)

    return pl.pallas_call(
        body, name="attn_fwd", grid=(npair, nqb),
        in_specs=[pl.BlockSpec((Q_BLOCK, LANES), lambda p, i: (i, p)),
                  pl.BlockSpec((t_len, LANES), lambda p, i: (0, npair + p)),
                  pl.BlockSpec((t_len, LANES), lambda p, i: (0, 2 * npair + p)),
                  pl.BlockSpec((1, LANES), lambda p, i: (0, p)),
                  pl.BlockSpec((2 * LANES, 2 * LANES), lambda p, i: (0, 0)),
                  pl.BlockSpec(memory_space=pl.ANY)],
        out_specs=[pl.BlockSpec((Q_BLOCK, LANES), lambda p, i: (i, p)),
                   pl.BlockSpec((Q_BLOCK, LANES), lambda p, i: (i, p)),
                   pl.BlockSpec((Q_BLOCK, LANES), lambda p, i: (i, npair + p))],
        out_shape=[jax.ShapeDtypeStruct((t_len, width), F32), jax.ShapeDtypeStruct((t_len, width), F32),
                   jax.ShapeDtypeStruct(ymix_in.shape, BF16)],
        input_output_aliases={5: 2},
        compiler_params=_params(("parallel", "arbitrary")),
    )(qkv, qkv, qkv, g_attn, _cumsum_matrix(False), ymix_in)


def _attn_bwd(qkv, o, o_fine, dymix, g_attn, width):
    t_len = qkv.shape[0]
    npair = width // LANES
    nqb = t_len // Q_BLOCK
    scale = HEAD_DIM ** -0.5

    def body(q_ref, k_ref, v_ref, o_ref, o_fine_ref, dy_ref, g_ref, umat_ref, gmat_ref,
             dq_ref, dk_ref, dv_ref, gg_ref, dk_s, dv_s):
        i = pl.program_id(1)
        lane = lax.broadcasted_iota(jnp.int32, (Q_BLOCK, LANES), 1)
        lane_lo = lane < HEAD_DIM
        causal = lane < lax.broadcasted_iota(jnp.int32, (Q_BLOCK, LANES), 0)
        umat, gmat = umat_ref[...], gmat_ref[...]

        @pl.when(i == 0)
        def _():
            dk_s[...] = jnp.zeros_like(dk_s)
            dv_s[...] = jnp.zeros_like(dv_s)

        o_blk = o_ref[...]
        rs = lax.rsqrt(_group_sums(o_blk * o_blk, lane_lo) * (1.0 / HEAD_DIM) + RMS_EPS)
        yn = o_blk * rs
        dy_raw = dy_ref[...]
        dyn = dy_raw * g_ref[...]
        d_o = rs * (dyn - yn * (_group_sums(dyn * yn, lane_lo) * (1.0 / HEAD_DIM)))
        _accumulate(gg_ref, i, nqb, _rows8(dy_raw * yn))
        do_bf = d_o.astype(BF16)
        do_o = do_bf.astype(F32) * o_fine_ref[...]
        q = q_ref[...]

        dqs = []
        for head_mask in (lane_lo, jnp.logical_not(lane_lo)):
            total = jnp.sum(jnp.where(head_mask, do_o, 0.0), axis=-1, keepdims=True)
            q_head = jnp.where(head_mask, q, jnp.zeros_like(q))
            qm = q_head * jnp.asarray(scale, BF16)
            do_head = jnp.where(head_mask, do_bf, jnp.zeros_like(do_bf))

            def tile(rows, state, tile_mask, qm=qm, q_head=q_head, do_head=do_head, total=total):
                carry, carry_g, dq_acc = state
                kb = k_ref[pl.ds(rows, Q_BLOCK), :]
                vb = v_ref[pl.ds(rows, Q_BLOCK), :]
                z, e, a, carry = _stick_tile(qm, kb, carry, umat, tile_mask)
                da = lax.dot_general(do_head, vb, (_NT, ((), ())), preferred_element_type=F32)
                g = a * da
                within_g, total_g = _suffix_sums(g, gmat)
                before = total - (within_g + carry_g)
                inv = 1.0 / (1.0 + e)
                sig = jnp.where(z >= 0.0, inv, e * inv)
                dz = g * (1.0 - sig) - before * sig
                if tile_mask is not None:
                    dz = jnp.where(tile_mask, dz, 0.0)
                dzs = (dz * scale).astype(BF16)
                dq_acc = dq_acc + jnp.dot(dzs, kb, preferred_element_type=F32)
                dk_s[pl.ds(rows, Q_BLOCK), :] += lax.dot_general(dzs, q_head, (_TN, ((), ())),
                                                                  preferred_element_type=F32)
                dv_s[pl.ds(rows, Q_BLOCK), :] += lax.dot_general(a.astype(BF16), do_head, (_TN, ((), ())),
                                                                  preferred_element_type=F32)
                return carry, carry_g + total_g, dq_acc

            zero = jnp.zeros((Q_BLOCK, LANES), F32)
            state = tile(pl.multiple_of(i * Q_BLOCK, Q_BLOCK), (zero, zero, zero), causal)
            state = lax.fori_loop(
                1, i + 1, lambda n, st, tile=tile: tile(pl.multiple_of((i - n) * Q_BLOCK, Q_BLOCK), st, None), state)
            dqs.append(state[2])
        dq_ref[...] = jnp.where(lane_lo, dqs[0], dqs[1]).astype(BF16)

        @pl.when(i == nqb - 1)
        def _():
            dk_ref[...] = dk_s[...].astype(BF16)
            dv_ref[...] = dv_s[...].astype(BF16)

    whole = pl.BlockSpec((t_len, LANES), lambda p, i: (0, p))
    return pl.pallas_call(
        body, name="attn_bwd", grid=(npair, nqb),
        in_specs=[pl.BlockSpec((Q_BLOCK, LANES), lambda p, i: (i, p)),
                  pl.BlockSpec((t_len, LANES), lambda p, i: (0, npair + p)),
                  pl.BlockSpec((t_len, LANES), lambda p, i: (0, 2 * npair + p)),
                  pl.BlockSpec((Q_BLOCK, LANES), lambda p, i: (i, p)),
                  pl.BlockSpec((Q_BLOCK, LANES), lambda p, i: (i, p)),
                  pl.BlockSpec((Q_BLOCK, LANES), lambda p, i: (i, npair + p)),
                  pl.BlockSpec((1, LANES), lambda p, i: (0, p)),
                  pl.BlockSpec((2 * LANES, 2 * LANES), lambda p, i: (0, 0)),
                  pl.BlockSpec((2 * LANES, 2 * LANES), lambda p, i: (0, 0))],
        out_specs=[pl.BlockSpec((Q_BLOCK, LANES), lambda p, i: (i, p)), whole, whole,
                   pl.BlockSpec((8, LANES), lambda p, i: (0, p))],
        out_shape=[jax.ShapeDtypeStruct((t_len, width), BF16)] * 3 + [jax.ShapeDtypeStruct((8, width), F32)],
        scratch_shapes=[pltpu.VMEM((t_len, LANES), F32), pltpu.VMEM((t_len, LANES), F32)],
        compiler_params=_params(("parallel", "arbitrary")),
    )(qkv, qkv, qkv, o, o_fine, dymix, g_attn, _cumsum_matrix(False), _cumsum_matrix(True))


def kernel(x, w_in, conv_w, g_conv, g_attn, w_out, ln1_g, ln1_b, w_up, w_down, ln2_g, ln2_b, loss_target, m_w_in, m_conv_w, m_g_conv, m_g_attn, m_w_out, m_ln1_g, m_ln1_b, m_w_up, m_w_down, m_ln2_g, m_ln2_b, v_w_in, v_conv_w, v_g_conv, v_g_attn, v_w_out, v_ln1_g, v_ln1_b, v_w_up, v_w_down, v_ln2_g, v_ln2_b):
    x2, target = x[0], loss_target[0]
    t_len, d_model = x2.shape
    width = g_conv.shape[1]
    in_shard = w_in.shape[2]
    up_shard = w_up.shape[2]
    d_ff = up_shard * N_DEV
    cw = conv_w.shape[2]
    xi, yi, ci = _place()
    dev = 4 * xi + 2 * yi + ci
    tm = 256

    win_g, wout_g, wup_g, wdown_g = _all_gather(
        "gather_weights", [w_in[0], w_out[0], w_up[0], w_down[0]], BF16)
    (convw_g,) = _all_gather("gather_conv_w", [jnp.pad(conv_w[0], ((0, 5), (0, LANES - cw)))], F32)
    conv_full = jnp.transpose(convw_g[:, :3, :cw], (1, 0, 2)).reshape(3, width)
    wout_full = wout_g.reshape(d_model, d_model)
    wdown_full = wdown_g.reshape(d_ff, d_model)

    n_gate = (3 * width) // in_shard

    def cast_epilogue(acc, extra, outs, i):
        outs[0][...] = acc.astype(outs[0].dtype)

    def proj_in(name, first_shard, dtype):
        return _matmul(
            name, x2, win_g,
            pl.BlockSpec((tm, d_model), lambda i, j, k: (i, 0)),
            pl.BlockSpec((None, d_model, in_shard), lambda i, j, k: (first_shard + j, 0, 0)),
            _NN, (t_len // tm, n_gate, 1), None,
            [(jax.ShapeDtypeStruct((t_len, 3 * width), dtype), pl.BlockSpec((tm, in_shard), lambda i, j, k: (i, j)))],
            cast_epilogue)[0]

    gates = proj_in("proj_in_gates", 0, F32)
    qkv = proj_in("proj_in_qkv", n_gate, BF16)

    ymix = _conv_fwd(gates, conv_full, g_conv, width)
    o_attn, o_fine, ymix = _attn_fwd(qkv, g_attn, ymix, width)

    row_tile = pl.BlockSpec((tm, d_model), lambda i, j, k: (i, 0))
    vec = pl.BlockSpec((1, d_model), lambda i, j, k: (0, 0))
    full_sq = pl.BlockSpec((d_model, d_model), lambda i, j, k: (0, 0))

    def out_ln1_epilogue(acc, extra, outs, i):
        x_ref, g_ref, b_ref = extra
        r1 = ALPHA * x_ref[...] + acc
        x1, _, _ = _ln_fwd(r1, g_ref[...], b_ref[...])
        outs[0][...] = r1
        outs[1][...] = x1
        outs[2][...] = x1.astype(BF16)

    r1, x1, x1_bf = _matmul(
        "proj_out_ln1", ymix, wout_full, row_tile, full_sq, _NN, (t_len // tm, 1, 1), None,
        [(jax.ShapeDtypeStruct((t_len, d_model), F32), row_tile),
         (jax.ShapeDtypeStruct((t_len, d_model), F32), row_tile),
         (jax.ShapeDtypeStruct((t_len, d_model), BF16), row_tile)],
        out_ln1_epilogue, extras=[(x2, row_tile), (ln1_g, vec), (ln1_b, vec)])

    def up_epilogue(acc, extra, outs, i):
        outs[0][...] = acc
        relu = jnp.maximum(acc, 0.0)
        outs[1][...] = (relu * relu).astype(BF16)

    up_tile = pl.BlockSpec((tm, up_shard), lambda i, j, k: (i, j))
    up, hid = _matmul(
        "ffn_up", x1_bf, wup_g, row_tile,
        pl.BlockSpec((None, d_model, up_shard), lambda i, j, k: (j, 0, 0)),
        _NN, (t_len // tm, N_DEV, 1), None,
        [(jax.ShapeDtypeStruct((t_len, d_ff), F32), up_tile),
         (jax.ShapeDtypeStruct((t_len, d_ff), BF16), up_tile)],
        up_epilogue)

    n_row_tiles = t_len // tm
    acc8 = pl.BlockSpec((8, d_model), lambda i, j, k: (0, 0))

    def down_ln2_epilogue(acc, extra, outs, i):
        x1_ref, t_ref, g_ref, b_ref = extra
        dr2_ref, dr2_bf_ref, loss_ref, gg_ref, gb_ref = outs
        gain = g_ref[...]
        r2 = ALPHA * x1_ref[...] + acc
        y, xhat, rstd = _ln_fwd(r2, gain, b_ref[...])
        diff = y - t_ref[...]
        d_y = diff * (1.0 / d_model)
        dr2 = _ln_bwd(d_y, xhat, rstd, gain)
        dr2_ref[...] = dr2
        dr2_bf_ref[...] = dr2.astype(BF16)
        _accumulate(gg_ref, i, n_row_tiles, _rows8(d_y * xhat))
        _accumulate(gb_ref, i, n_row_tiles, _rows8(d_y))
        _accumulate(loss_ref, i, n_row_tiles, _rows8(diff * diff), all_lanes=True)

    dr2, dr2_bf, loss_cols, g_ln2_g, g_ln2_b = _matmul(
        "ffn_down_ln2_loss", hid, wdown_full,
        pl.BlockSpec((tm, d_ff), lambda i, j, k: (i, 0)),
        pl.BlockSpec((d_ff, d_model), lambda i, j, k: (0, 0)),
        _NN, (n_row_tiles, 1, 1), None,
        [(jax.ShapeDtypeStruct((t_len, d_model), F32), row_tile),
         (jax.ShapeDtypeStruct((t_len, d_model), BF16), row_tile),
         (jax.ShapeDtypeStruct((8, d_model), F32), acc8),
         (jax.ShapeDtypeStruct((8, d_model), F32), acc8),
         (jax.ShapeDtypeStruct((8, d_model), F32), acc8)],
        down_ln2_epilogue, extras=[(x1, row_tile), (target, row_tile), (ln2_g, vec), (ln2_b, vec)])

    def dup_epilogue(acc, extra, outs, i):
        outs[0][...] = (acc * (2.0 * jnp.maximum(extra[0][...], 0.0))).astype(BF16)

    ff_tile = pl.BlockSpec((tm, 1024), lambda i, j, k: (i, j))
    (dup,) = _matmul(
        "ffn_dhid", dr2_bf, wdown_full, row_tile,
        pl.BlockSpec((1024, d_model), lambda i, j, k: (j, 0)),
        _NT, (t_len // tm, d_ff // 1024, 1), None,
        [(jax.ShapeDtypeStruct((t_len, d_ff), BF16), ff_tile)],
        dup_epilogue, extras=[(up, ff_tile)])

    def grad_epilogue(acc, extra, outs, i):
        outs[0][...] = acc
        outs[1][...] = acc.astype(BF16)

    tk = 512

    def weight_grad(name, a, b, rows, cols, tmw, shard_rows, shard_cols):
        by_rows = shard_cols == cols
        n_i = rows // tmw
        if by_rows:
            per = shard_rows // tmw
            out_spec = pl.BlockSpec((None, tmw, cols), lambda i, j, k: (i // per, i % per, 0))
            b_spec = pl.BlockSpec((tk, cols), lambda i, j, k: (k, 0))
            grid = (n_i, 1, t_len // tk)
            acc_shape = (tmw, cols)
        else:
            out_spec = pl.BlockSpec((None, tmw, shard_cols), lambda i, j, k: (j, i, 0))
            b_spec = pl.BlockSpec((tk, shard_cols), lambda i, j, k: (k, j))
            grid = (n_i, N_DEV, t_len // tk)
            acc_shape = (tmw, shard_cols)
        shape = (N_DEV, shard_rows, shard_cols)
        return _matmul(
            name, a, b, pl.BlockSpec((tk, tmw), lambda i, j, k: (k, i)), b_spec, _TN, grid, acc_shape,
            [(jax.ShapeDtypeStruct(shape, F32), out_spec), (jax.ShapeDtypeStruct(shape, BF16), out_spec)],
            grad_epilogue)

    gw_down, gw_down_bf = weight_grad("grad_w_down", hid, dr2_bf, d_ff, d_model, 512, d_ff // N_DEV, d_model)
    gw_up, gw_up_bf = weight_grad("grad_w_up", x1_bf, dup, d_model, d_ff, 512, d_model, up_shard)

    def dx1_ln1_epilogue(acc, extra, outs, i):
        dr2_ref, r1_ref, g_ref, b_ref = extra
        dr1_ref, dr1_bf_ref, gg_ref, gb_ref = outs
        gain = g_ref[...]
        dx1 = ALPHA * dr2_ref[...] + acc
        _, xhat, rstd = _ln_fwd(r1_ref[...], gain, b_ref[...])
        dr1 = _ln_bwd(dx1, xhat, rstd, gain)
        dr1_ref[...] = dr1
        dr1_bf_ref[...] = dr1.astype(BF16)
        _accumulate(gg_ref, i, n_row_tiles, _rows8(dx1 * xhat))
        _accumulate(gb_ref, i, n_row_tiles, _rows8(dx1))

    dr1, dr1_bf, g_ln1_g, g_ln1_b = _matmul(
        "ffn_dx1_ln1", dup, wup_g,
        pl.BlockSpec((tm, up_shard), lambda i, j, k: (i, k)),
        pl.BlockSpec((None, d_model, up_shard), lambda i, j, k: (k, 0, 0)),
        _NT, (n_row_tiles, 1, N_DEV), (tm, d_model),
        [(jax.ShapeDtypeStruct((t_len, d_model), F32), row_tile),
         (jax.ShapeDtypeStruct((t_len, d_model), BF16), row_tile),
         (jax.ShapeDtypeStruct((8, d_model), F32), acc8),
         (jax.ShapeDtypeStruct((8, d_model), F32), acc8)],
        dx1_ln1_epilogue, extras=[(dr2, row_tile), (r1, row_tile), (ln1_g, vec), (ln1_b, vec)])

    gw_out, gw_out_bf = weight_grad("grad_w_out", ymix, dr1_bf, d_model, d_model, 128, d_model // N_DEV, d_model)

    def plain_epilogue(acc, extra, outs, i):
        outs[0][...] = acc

    (dymix,) = _matmul(
        "proj_out_bwd", dr1_bf, wout_full, row_tile, full_sq, _NT, (n_row_tiles, 1, 1), None,
        [(jax.ShapeDtypeStruct((t_len, d_model), F32), row_tile)], plain_epilogue)

    d_bg, d_cg, d_h, g_gconv, g_convw = _conv_bwd(gates, dymix, conv_full, g_conv, width)
    d_q, d_k, d_v, g_gattn = _attn_bwd(qkv, o_attn, o_fine, dymix, g_attn, width)
    dproj = jnp.concatenate([d_bg, d_cg, d_h, d_q, d_k, d_v], axis=1)

    gw_in, gw_in_bf = weight_grad("grad_w_in", x2, dproj, d_model, 3 * width * 2, 512, d_model, in_shard)

    def dx_epilogue(acc, extra, outs, i):
        outs[0][...] = ALPHA * extra[0][...] + acc

    (grad_x,) = _matmul(
        "proj_in_bwd", dproj, win_g,
        pl.BlockSpec((tm, in_shard), lambda i, j, k: (i, k)),
        pl.BlockSpec((None, d_model, in_shard), lambda i, j, k: (k, 0, 0)),
        _NT, (n_row_tiles, 1, N_DEV), (tm, d_model),
        [(jax.ShapeDtypeStruct((t_len, d_model), F32), row_tile)],
        dx_epilogue, extras=[(dr1, row_tile)])

    c_idx = jnp.reshape(ci, (1,)).astype(jnp.int32)
    q_me = jnp.reshape(2 * xi + yi, (1,)).astype(jnp.int32)
    grads_f32 = [gw_in, gw_out, gw_up, gw_down]
    landed = _pair_exchange("reduce_pair", [gw_in_bf, gw_out_bf, gw_up_bf, gw_down_bf])
    names = ["w_in", "w_out", "w_up", "w_down"]
    wire, own = [], []
    for nm, g_f32, land in zip(names, grads_f32, landed):
        wr, ow = _pair_sum("pair_sum_" + nm, g_f32, land, c_idx, q_me)
        wire.append(wr)
        own.append(ow)
    landed2 = _chip_exchange("reduce_chips", wire)
    big = {}
    for nm, ow, land, w, m, v in zip(names, own, landed2, (w_in, w_out, w_up, w_down),
                                     (m_w_in, m_w_out, m_w_up, m_w_down), (v_w_in, v_w_out, v_w_up, v_w_down)):
        big[nm] = [r[None] for r in _shard_update("update_" + nm, ow, land, w[0], m[0], v[0])]

    def pack(l1g, l1b, l2g, l2b, gc, ga, taps):
        return jnp.concatenate([l1g, l1b, l2g, l2b, jnp.concatenate([gc, ga], axis=1),
                                jnp.concatenate([taps, jnp.zeros((3, d_model - width), F32)], axis=1)], axis=0)

    local = pack(g_ln1_g[:1], g_ln1_b[:1], g_ln2_g[:1], g_ln2_b[:1], g_gconv[:1], g_gattn[:1], g_convw[:3])
    (parts,) = _all_gather("gather_small_grads", [local], F32)

    def spread(a):
        return lax.dynamic_update_slice(jnp.zeros((3, width), F32), a, (0, dev * cw))

    w_small = pack(ln1_g, ln1_b, ln2_g, ln2_b, g_conv, g_attn, spread(conv_w[0]))
    m_small = pack(m_ln1_g, m_ln1_b, m_ln2_g, m_ln2_b, m_g_conv, m_g_attn, spread(m_conv_w[0]))
    v_small = pack(v_ln1_g, v_ln1_b, v_ln2_g, v_ln2_b, v_g_conv, v_g_attn, spread(v_conv_w[0]))
    small = _small_update(parts, w_small, m_small, v_small)

    def unpack(a):
        taps = lax.dynamic_slice(a[5:8, :width], (0, dev * cw), (3, cw))[None]
        return {"ln1_g": a[0:1], "ln1_b": a[1:2], "ln2_g": a[2:3], "ln2_b": a[3:4],
                "g_conv": a[4:5, :width], "g_attn": a[4:5, width:], "conv_w": taps}

    small = [unpack(a) for a in small]
    loss = lax.psum((0.5 / d_model) * loss_cols[0, 0], ("x", "y", "c"))

    order = ["w_in", "conv_w", "g_conv", "g_attn", "w_out", "ln1_g", "ln1_b", "w_up", "w_down", "ln2_g", "ln2_b"]
    result = [loss, grad_x[None]]
    for kind in range(4):
        for nm in order:
            result.append(big[nm][kind] if nm in big else small[kind][nm])
    return tuple(result)
```

```python
import functools

import jax
import jax.numpy as jnp
from jax import lax
from jax.experimental import pallas as pl
from jax.experimental.pallas import tpu as pltpu

F32 = jnp.float32
BF16 = jnp.bfloat16
MESH = pl.DeviceIdType.MESH

N_DEV = 8
HEAD_DIM = 64
LANES = 128
Q_BLOCK = 128
ALPHA = 2.0 ** 0.25
LN_EPS = 1e-5
RMS_EPS = 1e-6
ADAM_LR, ADAM_B1, ADAM_B2, ADAM_EPS, ADAM_WD, ADAM_STEP = 0.001, 0.9, 0.999, 1e-08, 0.01, 10
VMEM_LIMIT_BYTES = 48 * 1024 * 1024

_NT = ((1,), (1,))
_NN = ((1,), (0,))
_TN = ((0,), (0,))


def _params(sem=None):
    return pltpu.CompilerParams(dimension_semantics=sem, vmem_limit_bytes=VMEM_LIMIT_BYTES)


def _dot(a, b, contract):
    return lax.dot_general(a.astype(BF16), b.astype(BF16), (contract, ((), ())), preferred_element_type=F32)


def _matmul(name, a, b, a_spec, b_spec, contract, grid, acc_shape, outs, epilogue, extras=()):
    n_extra, n_out, nk = len(extras), len(outs), grid[2]

    def body(a_ref, b_ref, *rest):
        extra_refs = rest[:n_extra]
        out_refs = rest[n_extra:n_extra + n_out]
        i, k = pl.program_id(0), pl.program_id(2)
        part = _dot(a_ref[...], b_ref[...], contract)
        if nk == 1:
            epilogue(part, extra_refs, out_refs, i)
            return
        acc_ref = rest[-1]

        @pl.when(k == 0)
        def _():
            acc_ref[...] = part

        @pl.when(k > 0)
        def _():
            acc_ref[...] += part

        @pl.when(k == nk - 1)
        def _():
            epilogue(acc_ref[...], extra_refs, out_refs, i)

    return pl.pallas_call(
        body, name=name, grid=grid,
        in_specs=[a_spec, b_spec] + [s for _, s in extras],
        out_specs=[s for _, s in outs],
        out_shape=[o for o, _ in outs],
        scratch_shapes=[pltpu.VMEM(acc_shape, F32)] if nk > 1 else [],
        compiler_params=_params(("parallel", "parallel", "arbitrary")),
    )(a, b, *[e for e, _ in extras])


def _ln_fwd(r, g, b):
    mu = jnp.mean(r, axis=-1, keepdims=True)
    xc = r - mu
    rstd = lax.rsqrt(jnp.mean(xc * xc, axis=-1, keepdims=True) + LN_EPS)
    xhat = xc * rstd
    return xhat * g + b, xhat, rstd


def _ln_bwd(dy, xhat, rstd, g):
    dxh = dy * g
    m1 = jnp.mean(dxh, axis=-1, keepdims=True)
    m2 = jnp.mean(dxh * xhat, axis=-1, keepdims=True)
    return rstd * (dxh - m1 - xhat * m2)


def _rows8(v):
    n, c = v.shape
    return jnp.sum(v.reshape(n // 8, 8, c), axis=0)


def _accumulate(ref, i, n_steps, part8, all_lanes=False):
    @pl.when(i == 0)
    def _():
        ref[...] = part8

    @pl.when(i > 0)
    def _():
        ref[...] += part8

    @pl.when(i == n_steps - 1)
    def _():
        tot = jnp.sum(ref[...], axis=0, keepdims=True)
        if all_lanes:
            tot = jnp.sum(tot, axis=1, keepdims=True)
        ref[...] = jnp.broadcast_to(tot, ref.shape)


def _group_sums(v, lane_lo):
    s0 = jnp.sum(jnp.where(lane_lo, v, 0.0), axis=-1, keepdims=True)
    s1 = jnp.sum(jnp.where(lane_lo, 0.0, v), axis=-1, keepdims=True)
    return jnp.where(lane_lo, s0, s1)


def _place():
    return lax.axis_index("x"), lax.axis_index("y"), lax.axis_index("c")


def _all_gather(name, shards, out_dtype):
    n = len(shards)

    def body(*refs):
        in_refs, out_refs, stage = refs[:n], refs[n:2 * n], refs[2 * n:3 * n]
        send_sems, recv_sems, local_sems = refs[3 * n:]
        x, y, c = _place()
        me, sibling = (x, y, c), (x, y, 1 - c)
        chips = [(1 - x, y), (x, 1 - y), (1 - x, 1 - y)]

        def copy(a, k, block, to, src=None):
            slot = out_refs[a].at[4 * block[0] + 2 * block[1] + block[2]]
            return pltpu.make_async_remote_copy(
                src_ref=slot if src is None else src, dst_ref=slot,
                send_sem=send_sems.at[a, k], recv_sem=recv_sems.at[a, k],
                device_id=to, device_id_type=MESH)

        mine, first, passed = [], [], []
        for a in range(n):
            stage[a][...] = in_refs[a][...].astype(out_dtype)
            mine.append(pltpu.make_async_copy(stage[a], out_refs[a].at[4 * x + 2 * y + c], local_sems.at[a]))
            mine[a].start()
            first.append([copy(a, 0, me, sibling, src=stage[a])]
                         + [copy(a, 1 + j, me, (*chip, c), src=stage[a]) for j, chip in enumerate(chips)])
            for cp in first[a]:
                cp.start()
            passed.append([copy(a, 4 + j, (*chip, c), sibling) for j, chip in enumerate(chips)])
        for j, chip in enumerate(chips):
            for a in range(n):
                copy(a, 1 + j, (*chip, c), me).wait_recv()
                passed[a][j].start()
        for a in range(n):
            copy(a, 0, sibling, me).wait_recv()
            for j, chip in enumerate(chips):
                copy(a, 4 + j, (*chip, 1 - c), me).wait_recv()
        for a in range(n):
            for cp in first[a] + passed[a]:
                cp.wait_send()
            mine[a].wait()

    return pl.pallas_call(
        body, name=name,
        in_specs=[pl.BlockSpec(memory_space=pltpu.VMEM)] * n,
        out_specs=[pl.BlockSpec(memory_space=pl.ANY)] * n,
        out_shape=[jax.ShapeDtypeStruct((N_DEV,) + s.shape, out_dtype) for s in shards],
        scratch_shapes=[pltpu.VMEM(s.shape, out_dtype) for s in shards]
        + [pltpu.SemaphoreType.DMA((n, 7)), pltpu.SemaphoreType.DMA((n, 7)), pltpu.SemaphoreType.DMA((n,))],
        compiler_params=pltpu.CompilerParams(vmem_limit_bytes=VMEM_LIMIT_BYTES),
    )(*shards)


def _pair_exchange(name, grads):
    n = len(grads)

    def body(*refs):
        g_refs, land_refs = refs[:n], refs[n:2 * n]
        send_sems, recv_sems = refs[2 * n:]
        x, y, c = _place()
        copies = []
        for a in range(n):
            for q in range(4):
                cp = pltpu.make_async_remote_copy(
                    src_ref=g_refs[a].at[2 * q + (1 - c)], dst_ref=land_refs[a].at[q],
                    send_sem=send_sems.at[a, q], recv_sem=recv_sems.at[a, q],
                    device_id=(x, y, 1 - c), device_id_type=MESH)
                cp.start()
                copies.append(cp)
        for cp in copies:
            cp.wait()

    return pl.pallas_call(
        body, name=name,
        in_specs=[pl.BlockSpec(memory_space=pl.ANY)] * n,
        out_specs=[pl.BlockSpec(memory_space=pl.ANY)] * n,
        out_shape=[jax.ShapeDtypeStruct((4,) + g.shape[1:], g.dtype) for g in grads],
        scratch_shapes=[pltpu.SemaphoreType.DMA((n, 4)), pltpu.SemaphoreType.DMA((n, 4))],
    )(*grads)


def _chip_exchange(name, pair_sums):
    n = len(pair_sums)

    def body(*refs):
        p_refs, land_refs = refs[:n], refs[n:2 * n]
        send_sems, recv_sems = refs[2 * n:]
        x, y, c = _place()
        chips = [(1 - x, y), (x, 1 - y), (1 - x, 1 - y)]
        copies = []
        for a in range(n):
            for k, (px, py) in enumerate(chips):
                cp = pltpu.make_async_remote_copy(
                    src_ref=p_refs[a].at[2 * px + py], dst_ref=land_refs[a].at[k],
                    send_sem=send_sems.at[a, k], recv_sem=recv_sems.at[a, k],
                    device_id=(px, py, c), device_id_type=MESH)
                cp.start()
                copies.append(cp)
        for cp in copies:
            cp.wait()

    return pl.pallas_call(
        body, name=name,
        in_specs=[pl.BlockSpec(memory_space=pl.ANY)] * n,
        out_specs=[pl.BlockSpec(memory_space=pl.ANY)] * n,
        out_shape=[jax.ShapeDtypeStruct((3,) + p.shape[1:], p.dtype) for p in pair_sums],
        scratch_shapes=[pltpu.SemaphoreType.DMA((n, 3)), pltpu.SemaphoreType.DMA((n, 3))],
    )(*pair_sums)


def _pair_sum(name, g_f32, landed, c_idx, q_me):
    _, rows, cols = g_f32.shape
    tr = min(rows, 256)

    def body(c_ref, q_ref, g_ref, l_ref, wire_ref, own_ref):
        q = pl.program_id(1)
        s = g_ref[...] + l_ref[...].astype(F32)
        wire_ref[...] = s.astype(BF16)

        @pl.when(q == q_ref[0])
        def _():
            own_ref[...] = s

    return pl.pallas_call(
        body, name=name,
        grid_spec=pltpu.PrefetchScalarGridSpec(
            num_scalar_prefetch=2, grid=(rows // tr, 4),
            in_specs=[pl.BlockSpec((None, tr, cols), lambda i, q, c_ref, q_ref: (2 * q + c_ref[0], i, 0)),
                      pl.BlockSpec((None, tr, cols), lambda i, q, c_ref, q_ref: (q, i, 0))],
            out_specs=[pl.BlockSpec((None, tr, cols), lambda i, q, c_ref, q_ref: (q, i, 0)),
                       pl.BlockSpec((tr, cols), lambda i, q, c_ref, q_ref: (i, 0))]),
        out_shape=[jax.ShapeDtypeStruct((4, rows, cols), BF16), jax.ShapeDtypeStruct((rows, cols), F32)],
        compiler_params=_params(("parallel", "arbitrary")),
    )(c_idx, q_me, g_f32, landed)


def _adamw(w, g, m, v):
    m = ADAM_B1 * m + (1.0 - ADAM_B1) * g
    v = ADAM_B2 * v + (1.0 - ADAM_B2) * (g * g)
    m_hat = m / (1.0 - ADAM_B1 ** ADAM_STEP)
    v_hat = v / (1.0 - ADAM_B2 ** ADAM_STEP)
    delta = -ADAM_LR * (m_hat / (jnp.sqrt(v_hat) + ADAM_EPS) + ADAM_WD * w)
    return delta, m, v


def _shard_update(name, own, landed, w, m, v):
    rows, cols = own.shape
    tr = min(rows, 256)

    def body(own_ref, l_ref, w_ref, m_ref, v_ref, g_out, d_out, m_out, v_out):
        g = own_ref[...] + l_ref[0].astype(F32) + l_ref[1].astype(F32) + l_ref[2].astype(F32)
        d, m_new, v_new = _adamw(w_ref[...], g, m_ref[...], v_ref[...])
        g_out[...] = g
        d_out[...] = d
        m_out[...] = m_new
        v_out[...] = v_new

    tile = pl.BlockSpec((tr, cols), lambda i: (i, 0))
    return pl.pallas_call(
        body, name=name, grid=(rows // tr,),
        in_specs=[tile, pl.BlockSpec((3, tr, cols), lambda i: (0, i, 0)), tile, tile, tile],
        out_specs=[tile] * 4,
        out_shape=[jax.ShapeDtypeStruct((rows, cols), F32)] * 4,
        compiler_params=_params(("parallel",)),
    )(own, landed, w, m, v)


def _small_update(parts, w, m, v):
    def body(p_ref, w_ref, m_ref, v_ref, g_out, d_out, m_out, v_out):
        g = p_ref[0]
        for d in range(1, N_DEV):
            g = g + p_ref[d]
        dl, m_new, v_new = _adamw(w_ref[...], g, m_ref[...], v_ref[...])
        g_out[...] = g
        d_out[...] = dl
        m_out[...] = m_new
        v_out[...] = v_new

    return pl.pallas_call(
        body, name="small_update",
        out_shape=[jax.ShapeDtypeStruct(w.shape, F32)] * 4,
        compiler_params=pltpu.CompilerParams(vmem_limit_bytes=VMEM_LIMIT_BYTES),
    )(parts, w, m, v)


CONV_CHUNK = 512


def _conv_taps(u_s, lo, chunk):
    ext = u_s[pl.ds(lo, chunk + 8), :]
    return ext[8:], pltpu.roll(ext, 1, 0)[8:], pltpu.roll(ext, 2, 0)[8:]


def _conv_fwd(gates, conv_w, g_conv, width):
    t_len = gates.shape[0]
    nblk = width // LANES
    n_chunks = t_len // CONV_CHUNK

    def body(bg_ref, cg_ref, h_ref, w_ref, g_ref, y_ref, u_s):
        lane_lo = lax.broadcasted_iota(jnp.int32, (CONV_CHUNK, LANES), 1) < HEAD_DIM
        u_s[0:8, :] = jnp.zeros((8, LANES), F32)

        def fill(r, _):
            lo = pl.multiple_of(r * CONV_CHUNK, CONV_CHUNK)
            u_s[pl.ds(pl.multiple_of(lo + 8, 8), CONV_CHUNK), :] = cg_ref[pl.ds(lo, CONV_CHUNK), :] * h_ref[pl.ds(lo, CONV_CHUNK), :]
            return 0

        lax.fori_loop(0, n_chunks, fill, 0)
        w = w_ref[...]
        gain = g_ref[...]

        def step(r, _):
            lo = pl.multiple_of(r * CONV_CHUNK, CONV_CHUNK)
            u0, u1, u2 = _conv_taps(u_s, lo, CONV_CHUNK)
            y = bg_ref[pl.ds(lo, CONV_CHUNK), :] * (w[2:3] * u0 + w[1:2] * u1 + w[0:1] * u2)
            rs = lax.rsqrt(_group_sums(y * y, lane_lo) * (1.0 / HEAD_DIM) + RMS_EPS)
            y_ref[pl.ds(lo, CONV_CHUNK), :] = (y * rs * gain).astype(BF16)
            return 0

        lax.fori_loop(0, n_chunks, step, 0)

    col = lambda off: pl.BlockSpec((t_len, LANES), lambda p: (0, off + p))
    return pl.pallas_call(
        body, name="conv_fwd", grid=(nblk,),
        in_specs=[col(0), col(nblk), col(2 * nblk),
                  pl.BlockSpec((3, LANES), lambda p: (0, p)), pl.BlockSpec((1, LANES), lambda p: (0, p))],
        out_specs=col(0),
        out_shape=jax.ShapeDtypeStruct((t_len, 2 * width), BF16),
        scratch_shapes=[pltpu.VMEM((t_len + 8, LANES), F32)],
        compiler_params=_params(("parallel",)),
    )(gates, gates, gates, conv_w, g_conv)


def _conv_bwd(gates, dymix, conv_w, g_conv, width):
    t_len = gates.shape[0]
    nblk = width // LANES
    n_chunks = t_len // CONV_CHUNK

    def body(bg_ref, cg_ref, h_ref, dy_ref, w_ref, g_ref, dbg_ref, dcg_ref, dh_ref, gg_ref, gw_ref, u_s, dc_s):
        lane_lo = lax.broadcasted_iota(jnp.int32, (CONV_CHUNK, LANES), 1) < HEAD_DIM
        u_s[0:8, :] = jnp.zeros((8, LANES), F32)
        dc_s[t_len:t_len + 8, :] = jnp.zeros((8, LANES), F32)

        def fill(r, _):
            lo = pl.multiple_of(r * CONV_CHUNK, CONV_CHUNK)
            u_s[pl.ds(pl.multiple_of(lo + 8, 8), CONV_CHUNK), :] = cg_ref[pl.ds(lo, CONV_CHUNK), :] * h_ref[pl.ds(lo, CONV_CHUNK), :]
            return 0

        lax.fori_loop(0, n_chunks, fill, 0)
        w = w_ref[...]
        gain = g_ref[...]

        def step(r, acc):
            gg, gw0, gw1, gw2 = acc
            lo = pl.multiple_of(r * CONV_CHUNK, CONV_CHUNK)
            u0, u1, u2 = _conv_taps(u_s, lo, CONV_CHUNK)
            bg = bg_ref[pl.ds(lo, CONV_CHUNK), :]
            conv = w[2:3] * u0 + w[1:2] * u1 + w[0:1] * u2
            y = bg * conv
            rs = lax.rsqrt(_group_sums(y * y, lane_lo) * (1.0 / HEAD_DIM) + RMS_EPS)
            yn = y * rs
            dyn_raw = dy_ref[pl.ds(lo, CONV_CHUNK), :]
            dyn = dyn_raw * gain
            d_y = rs * (dyn - yn * (_group_sums(dyn * yn, lane_lo) * (1.0 / HEAD_DIM)))
            dbg_ref[pl.ds(lo, CONV_CHUNK), :] = (d_y * conv).astype(BF16)
            dconv = d_y * bg
            dc_s[pl.ds(lo, CONV_CHUNK), :] = dconv
            return (gg + _rows8(dyn_raw * yn), gw0 + _rows8(dconv * u2), gw1 + _rows8(dconv * u1),
                    gw2 + _rows8(dconv * u0))

        zero = jnp.zeros((8, LANES), F32)
        gg, gw0, gw1, gw2 = lax.fori_loop(0, n_chunks, step, (zero, zero, zero, zero))
        gg_ref[...] = jnp.broadcast_to(jnp.sum(gg, axis=0, keepdims=True), (8, LANES))
        row = lax.broadcasted_iota(jnp.int32, (8, LANES), 0)
        tot = lambda v: jnp.broadcast_to(jnp.sum(v, axis=0, keepdims=True), (8, LANES))
        gw_ref[...] = jnp.where(row == 0, tot(gw0), jnp.where(row == 1, tot(gw1), jnp.where(row == 2, tot(gw2), 0.0)))

        def back(r, _):
            lo = pl.multiple_of(r * CONV_CHUNK, CONV_CHUNK)
            ext = dc_s[pl.ds(lo, CONV_CHUNK + 8), :]
            d0 = ext[:CONV_CHUNK]
            d1 = pltpu.roll(ext, CONV_CHUNK + 7, 0)[:CONV_CHUNK]
            d2 = pltpu.roll(ext, CONV_CHUNK + 6, 0)[:CONV_CHUNK]
            du = w[2:3] * d0 + w[1:2] * d1 + w[0:1] * d2
            dcg_ref[pl.ds(lo, CONV_CHUNK), :] = (du * h_ref[pl.ds(lo, CONV_CHUNK), :]).astype(BF16)
            dh_ref[pl.ds(lo, CONV_CHUNK), :] = (du * cg_ref[pl.ds(lo, CONV_CHUNK), :]).astype(BF16)
            return 0

        lax.fori_loop(0, n_chunks, back, 0)

    col = lambda off: pl.BlockSpec((t_len, LANES), lambda p: (0, off + p))
    small = pl.BlockSpec((8, LANES), lambda p: (0, p))
    return pl.pallas_call(
        body, name="conv_bwd", grid=(nblk,),
        in_specs=[col(0), col(nblk), col(2 * nblk), col(0),
                  pl.BlockSpec((3, LANES), lambda p: (0, p)), pl.BlockSpec((1, LANES), lambda p: (0, p))],
        out_specs=[col(0), col(0), col(0), small, small],
        out_shape=[jax.ShapeDtypeStruct((t_len, width), BF16)] * 3 + [jax.ShapeDtypeStruct((8, width), F32)] * 2,
        scratch_shapes=[pltpu.VMEM((t_len + 8, LANES), F32), pltpu.VMEM((t_len + 8, LANES), F32)],
        compiler_params=_params(("parallel",)),
    )(gates, gates, gates, dymix, conv_w, g_conv)


def _cumsum_matrix(inclusive):
    j = lax.broadcasted_iota(jnp.int32, (2 * LANES, 2 * LANES), 0) % LANES
    s = lax.broadcasted_iota(jnp.int32, (2 * LANES, 2 * LANES), 1)
    keep = (s >= LANES) | ((j >= s) if inclusive else (j > s))
    return keep.astype(BF16)


def _suffix_sums(val, mat):
    hi = val.astype(BF16)
    lo = (val - hi.astype(F32)).astype(BF16)
    both = jnp.dot(jnp.concatenate([hi, lo], axis=1), mat, preferred_element_type=F32)
    return both[:, :LANES], both[:, LANES:]


KEY_CHUNK = 512
SUB_TILES = KEY_CHUNK // Q_BLOCK


def _chunked_suffix(val, carry, mat):
    parts = []
    for b in reversed(range(SUB_TILES)):
        within, total = _suffix_sums(val[:, b * Q_BLOCK:(b + 1) * Q_BLOCK], mat)
        parts.append(within + carry)
        carry = carry + total
    return jnp.concatenate(parts[::-1], axis=1), carry


def _stick_chunk(qm, kc, carry, umat, mask):
    z = lax.dot_general(qm, kc, (_NT, ((), ())), preferred_element_type=F32)
    e = jnp.exp(-jnp.abs(z))
    lk = -(jnp.maximum(z, 0.0) + jnp.log(1.0 + e))
    lb = z + lk
    if mask is not None:
        lk = jnp.where(mask, lk, 0.0)
    suffix, carry = _chunked_suffix(lk, carry, umat)
    a = jnp.exp(lb + suffix)
    if mask is not None:
        a = jnp.where(mask, a, 0.0)
    return z, e, a, carry


def _chunk_mask(i):
    off = (i % SUB_TILES) * Q_BLOCK
    diff = (lax.broadcasted_iota(jnp.int32, (Q_BLOCK, KEY_CHUNK), 1)
            - lax.broadcasted_iota(jnp.int32, (Q_BLOCK, KEY_CHUNK), 0))
    return diff < off


def _head_rows(block, lane_lo_rows):
    zero = jnp.zeros_like(block)
    return jnp.concatenate([jnp.where(lane_lo_rows, block, zero), jnp.where(lane_lo_rows, zero, block)], axis=0)


def _attn_fwd(qkv, g_attn, ymix_in, width):
    t_len = qkv.shape[0]
    npair = width // LANES
    nqb = t_len // Q_BLOCK

    def body(q_ref, k_ref, v_ref, g_ref, umat_ref, ymix_in_ref, o_ref, o_fine_ref, ymix_ref):
        i = pl.program_id(1)
        lane_lo = lax.broadcasted_iota(jnp.int32, (Q_BLOCK, LANES), 1) < HEAD_DIM
        lane_lo_keys = lax.broadcasted_iota(jnp.int32, (KEY_CHUNK, LANES), 1) < HEAD_DIM
        umat = umat_ref[...]
        q = q_ref[...] * jnp.asarray(HEAD_DIM ** -0.5, BF16)
        q_heads = (jnp.where(lane_lo, q, jnp.zeros_like(q)), jnp.where(lane_lo, jnp.zeros_like(q), q))
        chunk_i = i // SUB_TILES

        def step(c, state, mask):
            carries, o_acc, o_rest = state
            rows = pl.multiple_of(c * KEY_CHUNK, KEY_CHUNK)
            kc = k_ref[pl.ds(rows, KEY_CHUNK), :]
            v_heads = _head_rows(v_ref[pl.ds(rows, KEY_CHUNK), :], lane_lo_keys)
            new_carries, his, los = [], [], []
            for h in range(2):
                _, _, a, carry = _stick_chunk(q_heads[h], kc, carries[h], umat, mask)
                a_hi = a.astype(BF16)
                his.append(a_hi)
                los.append((a - a_hi.astype(F32)).astype(BF16))
                new_carries.append(carry)
            o_acc = o_acc + jnp.dot(jnp.concatenate(his, axis=1), v_heads, preferred_element_type=F32)
            o_rest = o_rest + jnp.dot(jnp.concatenate(los, axis=1), v_heads, preferred_element_type=F32)
            return tuple(new_carries), o_acc, o_rest

        zero = jnp.zeros((Q_BLOCK, LANES), F32)
        state = step(chunk_i, ((zero, zero), zero, zero), _chunk_mask(i))
        _, o, o_rest = lax.fori_loop(1, chunk_i + 1, lambda n, st: step(chunk_i - n, st, None), state)
        o_ref[...] = o
        o_fine_ref[...] = o + o_rest
        rs = lax.rsqrt(_group_sums(o * o, lane_lo) * (1.0 / HEAD_DIM) + RMS_EPS)
        ymix_ref[...] = (o * rs * g_ref[...]).astype(BF16)

    return pl.pallas_call(
        body, name="attn_fwd", grid=(npair, nqb),
        in_specs=[pl.BlockSpec((Q_BLOCK, LANES), lambda p, i: (i, p)),
                  pl.BlockSpec((t_len, LANES), lambda p, i: (0, npair + p)),
                  pl.BlockSpec((t_len, LANES), lambda p, i: (0, 2 * npair + p)),
                  pl.BlockSpec((1, LANES), lambda p, i: (0, p)),
                  pl.BlockSpec((2 * LANES, 2 * LANES), lambda p, i: (0, 0)),
                  pl.BlockSpec(memory_space=pl.ANY)],
        out_specs=[pl.BlockSpec((Q_BLOCK, LANES), lambda p, i: (i, p)),
                   pl.BlockSpec((Q_BLOCK, LANES), lambda p, i: (i, p)),
                   pl.BlockSpec((Q_BLOCK, LANES), lambda p, i: (i, npair + p))],
        out_shape=[jax.ShapeDtypeStruct((t_len, width), F32), jax.ShapeDtypeStruct((t_len, width), F32),
                   jax.ShapeDtypeStruct(ymix_in.shape, BF16)],
        input_output_aliases={5: 2},
        compiler_params=_params(("parallel", "arbitrary")),
    )(qkv, qkv, qkv, g_attn, _cumsum_matrix(False), ymix_in)


def _attn_bwd(qkv, o, o_fine, dymix, g_attn, width):
    t_len = qkv.shape[0]
    npair = width // LANES
    nqb = t_len // Q_BLOCK
    scale = HEAD_DIM ** -0.5

    def body(q_ref, k_ref, v_ref, o_ref, o_fine_ref, dy_ref, g_ref, umat_ref, gmat_ref,
             dq_ref, dk_ref, dv_ref, gg_ref, dk_s, dv_s):
        i = pl.program_id(1)
        lane_lo = lax.broadcasted_iota(jnp.int32, (Q_BLOCK, LANES), 1) < HEAD_DIM
        lane_lo_keys = lax.broadcasted_iota(jnp.int32, (KEY_CHUNK, LANES), 1) < HEAD_DIM
        umat, gmat = umat_ref[...], gmat_ref[...]

        @pl.when(i == 0)
        def _():
            dk_s[...] = jnp.zeros_like(dk_s)
            dv_s[...] = jnp.zeros_like(dv_s)

        o_blk = o_ref[...]
        rs = lax.rsqrt(_group_sums(o_blk * o_blk, lane_lo) * (1.0 / HEAD_DIM) + RMS_EPS)
        yn = o_blk * rs
        dy_raw = dy_ref[...]
        dyn = dy_raw * g_ref[...]
        d_o = rs * (dyn - yn * (_group_sums(dyn * yn, lane_lo) * (1.0 / HEAD_DIM)))
        _accumulate(gg_ref, i, nqb, _rows8(dy_raw * yn))
        do_bf = d_o.astype(BF16)
        do_o = do_bf.astype(F32) * o_fine_ref[...]
        totals = (jnp.sum(jnp.where(lane_lo, do_o, 0.0), axis=-1, keepdims=True),
                  jnp.sum(jnp.where(lane_lo, 0.0, do_o), axis=-1, keepdims=True))
        q = q_ref[...]
        q_rows = _head_rows(q, lane_lo)
        do_rows = _head_rows(do_bf, lane_lo)
        q_scaled = q_rows * jnp.asarray(scale, BF16)
        chunk_i = i // SUB_TILES

        def step(c, state, mask):
            carries, carries_g, dq_acc = state
            rows = pl.multiple_of(c * KEY_CHUNK, KEY_CHUNK)
            kc = k_ref[pl.ds(rows, KEY_CHUNK), :]
            vc = v_ref[pl.ds(rows, KEY_CHUNK), :]
            new_carries, new_carries_g, dzs, a_bf = [], [], [], []
            for h in range(2):
                head = slice(h * Q_BLOCK, (h + 1) * Q_BLOCK)
                z, e, a, carry = _stick_chunk(q_scaled[head], kc, carries[h], umat, mask)
                da = lax.dot_general(do_rows[head], vc, (_NT, ((), ())), preferred_element_type=F32)
                g = a * da
                suffix_g, carry_g = _chunked_suffix(g, carries_g[h], gmat)
                before = totals[h] - suffix_g
                inv = 1.0 / (1.0 + e)
                sig = jnp.where(z >= 0.0, inv, e * inv)
                dz = g * (1.0 - sig) - before * sig
                if mask is not None:
                    dz = jnp.where(mask, dz, 0.0)
                dzs.append((dz * scale).astype(BF16))
                a_bf.append(a.astype(BF16))
                new_carries.append(carry)
                new_carries_g.append(carry_g)
            dq_acc = dq_acc + jnp.dot(jnp.concatenate(dzs, axis=1), _head_rows(kc, lane_lo_keys),
                                      preferred_element_type=F32)
            dk_s[pl.ds(rows, KEY_CHUNK), :] += lax.dot_general(
                jnp.concatenate(dzs, axis=0), q_rows, (_TN, ((), ())), preferred_element_type=F32)
            dv_s[pl.ds(rows, KEY_CHUNK), :] += lax.dot_general(
                jnp.concatenate(a_bf, axis=0), do_rows, (_TN, ((), ())), preferred_element_type=F32)
            return tuple(new_carries), tuple(new_carries_g), dq_acc

        zero = jnp.zeros((Q_BLOCK, LANES), F32)
        state = step(chunk_i, ((zero, zero), (zero, zero), zero), _chunk_mask(i))
        state = lax.fori_loop(1, chunk_i + 1, lambda n, st: step(chunk_i - n, st, None), state)
        dq_ref[...] = state[2].astype(BF16)

        @pl.when(i == nqb - 1)
        def _():
            dk_ref[...] = dk_s[...].astype(BF16)
            dv_ref[...] = dv_s[...].astype(BF16)

    whole = pl.BlockSpec((t_len, LANES), lambda p, i: (0, p))
    return pl.pallas_call(
        body, name="attn_bwd", grid=(npair, nqb),
        in_specs=[pl.BlockSpec((Q_BLOCK, LANES), lambda p, i: (i, p)),
                  pl.BlockSpec((t_len, LANES), lambda p, i: (0, npair + p)),
                  pl.BlockSpec((t_len, LANES), lambda p, i: (0, 2 * npair + p)),
                  pl.BlockSpec((Q_BLOCK, LANES), lambda p, i: (i, p)),
                  pl.BlockSpec((Q_BLOCK, LANES), lambda p, i: (i, p)),
                  pl.BlockSpec((Q_BLOCK, LANES), lambda p, i: (i, npair + p)),
                  pl.BlockSpec((1, LANES), lambda p, i: (0, p)),
                  pl.BlockSpec((2 * LANES, 2 * LANES), lambda p, i: (0, 0)),
                  pl.BlockSpec((2 * LANES, 2 * LANES), lambda p, i: (0, 0))],
        out_specs=[pl.BlockSpec((Q_BLOCK, LANES), lambda p, i: (i, p)), whole, whole,
                   pl.BlockSpec((8, LANES), lambda p, i: (0, p))],
        out_shape=[jax.ShapeDtypeStruct((t_len, width), BF16)] * 3 + [jax.ShapeDtypeStruct((8, width), F32)],
        scratch_shapes=[pltpu.VMEM((t_len, LANES), F32), pltpu.VMEM((t_len, LANES), F32)],
        compiler_params=_params(("parallel", "arbitrary")),
    )(qkv, qkv, qkv, o, o_fine, dymix, g_attn, _cumsum_matrix(False), _cumsum_matrix(True))


def kernel(x, w_in, conv_w, g_conv, g_attn, w_out, ln1_g, ln1_b, w_up, w_down, ln2_g, ln2_b, loss_target, m_w_in, m_conv_w, m_g_conv, m_g_attn, m_w_out, m_ln1_g, m_ln1_b, m_w_up, m_w_down, m_ln2_g, m_ln2_b, v_w_in, v_conv_w, v_g_conv, v_g_attn, v_w_out, v_ln1_g, v_ln1_b, v_w_up, v_w_down, v_ln2_g, v_ln2_b):
    x2, target = x[0], loss_target[0]
    t_len, d_model = x2.shape
    width = g_conv.shape[1]
    in_shard = w_in.shape[2]
    up_shard = w_up.shape[2]
    d_ff = up_shard * N_DEV
    cw = conv_w.shape[2]
    xi, yi, ci = _place()
    dev = 4 * xi + 2 * yi + ci
    tm = 256

    win_g, wout_g, wup_g, wdown_g = _all_gather(
        "gather_weights", [w_in[0], w_out[0], w_up[0], w_down[0]], BF16)
    (convw_g,) = _all_gather("gather_conv_w", [jnp.pad(conv_w[0], ((0, 5), (0, LANES - cw)))], F32)
    conv_full = jnp.transpose(convw_g[:, :3, :cw], (1, 0, 2)).reshape(3, width)
    wout_full = wout_g.reshape(d_model, d_model)
    wdown_full = wdown_g.reshape(d_ff, d_model)

    n_gate = (3 * width) // in_shard

    def cast_epilogue(acc, extra, outs, i):
        outs[0][...] = acc.astype(outs[0].dtype)

    def proj_in(name, first_shard, dtype):
        return _matmul(
            name, x2, win_g,
            pl.BlockSpec((tm, d_model), lambda i, j, k: (i, 0)),
            pl.BlockSpec((None, d_model, in_shard), lambda i, j, k: (first_shard + j, 0, 0)),
            _NN, (t_len // tm, n_gate, 1), None,
            [(jax.ShapeDtypeStruct((t_len, 3 * width), dtype), pl.BlockSpec((tm, in_shard), lambda i, j, k: (i, j)))],
            cast_epilogue)[0]

    gates = proj_in("proj_in_gates", 0, F32)
    qkv = proj_in("proj_in_qkv", n_gate, BF16)

    ymix = _conv_fwd(gates, conv_full, g_conv, width)
    o_attn, o_fine, ymix = _attn_fwd(qkv, g_attn, ymix, width)

    row_tile = pl.BlockSpec((tm, d_model), lambda i, j, k: (i, 0))
    vec = pl.BlockSpec((1, d_model), lambda i, j, k: (0, 0))
    full_sq = pl.BlockSpec((d_model, d_model), lambda i, j, k: (0, 0))

    def out_ln1_epilogue(acc, extra, outs, i):
        x_ref, g_ref, b_ref = extra
        r1 = ALPHA * x_ref[...] + acc
        x1, _, _ = _ln_fwd(r1, g_ref[...], b_ref[...])
        outs[0][...] = r1
        outs[1][...] = x1
        outs[2][...] = x1.astype(BF16)

    r1, x1, x1_bf = _matmul(
        "proj_out_ln1", ymix, wout_full, row_tile, full_sq, _NN, (t_len // tm, 1, 1), None,
        [(jax.ShapeDtypeStruct((t_len, d_model), F32), row_tile),
         (jax.ShapeDtypeStruct((t_len, d_model), F32), row_tile),
         (jax.ShapeDtypeStruct((t_len, d_model), BF16), row_tile)],
        out_ln1_epilogue, extras=[(x2, row_tile), (ln1_g, vec), (ln1_b, vec)])

    def up_epilogue(acc, extra, outs, i):
        outs[0][...] = acc
        relu = jnp.maximum(acc, 0.0)
        outs[1][...] = (relu * relu).astype(BF16)

    up_tile = pl.BlockSpec((tm, up_shard), lambda i, j, k: (i, j))
    up, hid = _matmul(
        "ffn_up", x1_bf, wup_g, row_tile,
        pl.BlockSpec((None, d_model, up_shard), lambda i, j, k: (j, 0, 0)),
        _NN, (t_len // tm, N_DEV, 1), None,
        [(jax.ShapeDtypeStruct((t_len, d_ff), F32), up_tile),
         (jax.ShapeDtypeStruct((t_len, d_ff), BF16), up_tile)],
        up_epilogue)

    n_row_tiles = t_len // tm
    acc8 = pl.BlockSpec((8, d_model), lambda i, j, k: (0, 0))

    def down_ln2_epilogue(acc, extra, outs, i):
        x1_ref, t_ref, g_ref, b_ref = extra
        dr2_ref, dr2_bf_ref, loss_ref, gg_ref, gb_ref = outs
        gain = g_ref[...]
        r2 = ALPHA * x1_ref[...] + acc
        y, xhat, rstd = _ln_fwd(r2, gain, b_ref[...])
        diff = y - t_ref[...]
        d_y = diff * (1.0 / d_model)
        dr2 = _ln_bwd(d_y, xhat, rstd, gain)
        dr2_ref[...] = dr2
        dr2_bf_ref[...] = dr2.astype(BF16)
        _accumulate(gg_ref, i, n_row_tiles, _rows8(d_y * xhat))
        _accumulate(gb_ref, i, n_row_tiles, _rows8(d_y))
        _accumulate(loss_ref, i, n_row_tiles, _rows8(diff * diff), all_lanes=True)

    dr2, dr2_bf, loss_cols, g_ln2_g, g_ln2_b = _matmul(
        "ffn_down_ln2_loss", hid, wdown_full,
        pl.BlockSpec((tm, d_ff), lambda i, j, k: (i, 0)),
        pl.BlockSpec((d_ff, d_model), lambda i, j, k: (0, 0)),
        _NN, (n_row_tiles, 1, 1), None,
        [(jax.ShapeDtypeStruct((t_len, d_model), F32), row_tile),
         (jax.ShapeDtypeStruct((t_len, d_model), BF16), row_tile),
         (jax.ShapeDtypeStruct((8, d_model), F32), acc8),
         (jax.ShapeDtypeStruct((8, d_model), F32), acc8),
         (jax.ShapeDtypeStruct((8, d_model), F32), acc8)],
        down_ln2_epilogue, extras=[(x1, row_tile), (target, row_tile), (ln2_g, vec), (ln2_b, vec)])

    def dup_epilogue(acc, extra, outs, i):
        outs[0][...] = (acc * (2.0 * jnp.maximum(extra[0][...], 0.0))).astype(BF16)

    ff_tile = pl.BlockSpec((tm, 1024), lambda i, j, k: (i, j))
    (dup,) = _matmul(
        "ffn_dhid", dr2_bf, wdown_full, row_tile,
        pl.BlockSpec((1024, d_model), lambda i, j, k: (j, 0)),
        _NT, (t_len // tm, d_ff // 1024, 1), None,
        [(jax.ShapeDtypeStruct((t_len, d_ff), BF16), ff_tile)],
        dup_epilogue, extras=[(up, ff_tile)])

    def grad_epilogue(acc, extra, outs, i):
        outs[0][...] = acc
        outs[1][...] = acc.astype(BF16)

    tk = 512

    def weight_grad(name, a, b, rows, cols, tmw, shard_rows, shard_cols):
        by_rows = shard_cols == cols
        n_i = rows // tmw
        if by_rows:
            per = shard_rows // tmw
            out_spec = pl.BlockSpec((None, tmw, cols), lambda i, j, k: (i // per, i % per, 0))
            b_spec = pl.BlockSpec((tk, cols), lambda i, j, k: (k, 0))
            grid = (n_i, 1, t_len // tk)
            acc_shape = (tmw, cols)
        else:
            out_spec = pl.BlockSpec((None, tmw, shard_cols), lambda i, j, k: (j, i, 0))
            b_spec = pl.BlockSpec((tk, shard_cols), lambda i, j, k: (k, j))
            grid = (n_i, N_DEV, t_len // tk)
            acc_shape = (tmw, shard_cols)
        shape = (N_DEV, shard_rows, shard_cols)
        return _matmul(
            name, a, b, pl.BlockSpec((tk, tmw), lambda i, j, k: (k, i)), b_spec, _TN, grid, acc_shape,
            [(jax.ShapeDtypeStruct(shape, F32), out_spec), (jax.ShapeDtypeStruct(shape, BF16), out_spec)],
            grad_epilogue)

    gw_down, gw_down_bf = weight_grad("grad_w_down", hid, dr2_bf, d_ff, d_model, 512, d_ff // N_DEV, d_model)
    gw_up, gw_up_bf = weight_grad("grad_w_up", x1_bf, dup, d_model, d_ff, 512, d_model, up_shard)

    def dx1_ln1_epilogue(acc, extra, outs, i):
        dr2_ref, r1_ref, g_ref, b_ref = extra
        dr1_ref, dr1_bf_ref, gg_ref, gb_ref = outs
        gain = g_ref[...]
        dx1 = ALPHA * dr2_ref[...] + acc
        _, xhat, rstd = _ln_fwd(r1_ref[...], gain, b_ref[...])
        dr1 = _ln_bwd(dx1, xhat, rstd, gain)
        dr1_ref[...] = dr1
        dr1_bf_ref[...] = dr1.astype(BF16)
        _accumulate(gg_ref, i, n_row_tiles, _rows8(dx1 * xhat))
        _accumulate(gb_ref, i, n_row_tiles, _rows8(dx1))

    dr1, dr1_bf, g_ln1_g, g_ln1_b = _matmul(
        "ffn_dx1_ln1", dup, wup_g,
        pl.BlockSpec((tm, up_shard), lambda i, j, k: (i, k)),
        pl.BlockSpec((None, d_model, up_shard), lambda i, j, k: (k, 0, 0)),
        _NT, (n_row_tiles, 1, N_DEV), (tm, d_model),
        [(jax.ShapeDtypeStruct((t_len, d_model), F32), row_tile),
         (jax.ShapeDtypeStruct((t_len, d_model), BF16), row_tile),
         (jax.ShapeDtypeStruct((8, d_model), F32), acc8),
         (jax.ShapeDtypeStruct((8, d_model), F32), acc8)],
        dx1_ln1_epilogue, extras=[(dr2, row_tile), (r1, row_tile), (ln1_g, vec), (ln1_b, vec)])

    gw_out, gw_out_bf = weight_grad("grad_w_out", ymix, dr1_bf, d_model, d_model, 128, d_model // N_DEV, d_model)

    def plain_epilogue(acc, extra, outs, i):
        outs[0][...] = acc

    (dymix,) = _matmul(
        "proj_out_bwd", dr1_bf, wout_full, row_tile, full_sq, _NT, (n_row_tiles, 1, 1), None,
        [(jax.ShapeDtypeStruct((t_len, d_model), F32), row_tile)], plain_epilogue)

    d_bg, d_cg, d_h, g_gconv, g_convw = _conv_bwd(gates, dymix, conv_full, g_conv, width)
    d_q, d_k, d_v, g_gattn = _attn_bwd(qkv, o_attn, o_fine, dymix, g_attn, width)
    dproj = jnp.concatenate([d_bg, d_cg, d_h, d_q, d_k, d_v], axis=1)

    gw_in, gw_in_bf = weight_grad("grad_w_in", x2, dproj, d_model, 3 * width * 2, 512, d_model, in_shard)

    def dx_epilogue(acc, extra, outs, i):
        outs[0][...] = ALPHA * extra[0][...] + acc

    (grad_x,) = _matmul(
        "proj_in_bwd", dproj, win_g,
        pl.BlockSpec((tm, in_shard), lambda i, j, k: (i, k)),
        pl.BlockSpec((None, d_model, in_shard), lambda i, j, k: (k, 0, 0)),
        _NT, (n_row_tiles, 1, N_DEV), (tm, d_model),
        [(jax.ShapeDtypeStruct((t_len, d_model), F32), row_tile)],
        dx_epilogue, extras=[(dr1, row_tile)])

    c_idx = jnp.reshape(ci, (1,)).astype(jnp.int32)
    q_me = jnp.reshape(2 * xi + yi, (1,)).astype(jnp.int32)
    grads_f32 = [gw_in, gw_out, gw_up, gw_down]
    landed = _pair_exchange("reduce_pair", [gw_in_bf, gw_out_bf, gw_up_bf, gw_down_bf])
    names = ["w_in", "w_out", "w_up", "w_down"]
    wire, own = [], []
    for nm, g_f32, land in zip(names, grads_f32, landed):
        wr, ow = _pair_sum("pair_sum_" + nm, g_f32, land, c_idx, q_me)
        wire.append(wr)
        own.append(ow)
    landed2 = _chip_exchange("reduce_chips", wire)
    big = {}
    for nm, ow, land, w, m, v in zip(names, own, landed2, (w_in, w_out, w_up, w_down),
                                     (m_w_in, m_w_out, m_w_up, m_w_down), (v_w_in, v_w_out, v_w_up, v_w_down)):
        big[nm] = [r[None] for r in _shard_update("update_" + nm, ow, land, w[0], m[0], v[0])]

    def pack(l1g, l1b, l2g, l2b, gc, ga, taps):
        return jnp.concatenate([l1g, l1b, l2g, l2b, jnp.concatenate([gc, ga], axis=1),
                                jnp.concatenate([taps, jnp.zeros((3, d_model - width), F32)], axis=1)], axis=0)

    local = pack(g_ln1_g[:1], g_ln1_b[:1], g_ln2_g[:1], g_ln2_b[:1], g_gconv[:1], g_gattn[:1], g_convw[:3])
    (parts,) = _all_gather("gather_small_grads", [local], F32)

    def spread(a):
        return lax.dynamic_update_slice(jnp.zeros((3, width), F32), a, (0, dev * cw))

    w_small = pack(ln1_g, ln1_b, ln2_g, ln2_b, g_conv, g_attn, spread(conv_w[0]))
    m_small = pack(m_ln1_g, m_ln1_b, m_ln2_g, m_ln2_b, m_g_conv, m_g_attn, spread(m_conv_w[0]))
    v_small = pack(v_ln1_g, v_ln1_b, v_ln2_g, v_ln2_b, v_g_conv, v_g_attn, spread(v_conv_w[0]))
    small = _small_update(parts, w_small, m_small, v_small)

    def unpack(a):
        taps = lax.dynamic_slice(a[5:8, :width], (0, dev * cw), (3, cw))[None]
        return {"ln1_g": a[0:1], "ln1_b": a[1:2], "ln2_g": a[2:3], "ln2_b": a[3:4],
                "g_conv": a[4:5, :width], "g_attn": a[4:5, width:], "conv_w": taps}

    small = [unpack(a) for a in small]
    loss = lax.psum((0.5 / d_model) * loss_cols[0, 0], ("x", "y", "c"))

    order = ["w_in", "conv_w", "g_conv", "g_attn", "w_out", "ln1_g", "ln1_b", "w_up", "w_down", "ln2_g", "ln2_b"]
    result = [loss, grad_x[None]]
    for kind in range(4):
        for nm in order:
            result.append(big[nm][kind] if nm in big else small[kind][nm])
    return tuple(result)
```

```python
import functools

import jax
import jax.numpy as jnp
from jax import lax
from jax.experimental import pallas as pl
from jax.experimental.pallas import tpu as pltpu

F32 = jnp.float32
BF16 = jnp.bfloat16
MESH = pl.DeviceIdType.MESH

N_DEV = 8
HEAD_DIM = 64
LANES = 128
Q_BLOCK = 128
ALPHA = 2.0 ** 0.25
LN_EPS = 1e-5
RMS_EPS = 1e-6
ADAM_LR, ADAM_B1, ADAM_B2, ADAM_EPS, ADAM_WD, ADAM_STEP = 0.001, 0.9, 0.999, 1e-08, 0.01, 10
VMEM_LIMIT_BYTES = 48 * 1024 * 1024

_NT = ((1,), (1,))
_NN = ((1,), (0,))
_TN = ((0,), (0,))


def _params(sem=None):
    return pltpu.CompilerParams(dimension_semantics=sem, vmem_limit_bytes=VMEM_LIMIT_BYTES)


def _dot(a, b, contract):
    return lax.dot_general(a.astype(BF16), b.astype(BF16), (contract, ((), ())), preferred_element_type=F32)


def _matmul(name, a, b, a_spec, b_spec, contract, grid, acc_shape, outs, epilogue, extras=()):
    n_extra, n_out, nk = len(extras), len(outs), grid[2]

    def body(a_ref, b_ref, *rest):
        extra_refs = rest[:n_extra]
        out_refs = rest[n_extra:n_extra + n_out]
        i, k = pl.program_id(0), pl.program_id(2)
        part = _dot(a_ref[...], b_ref[...], contract)
        if nk == 1:
            epilogue(part, extra_refs, out_refs, i)
            return
        acc_ref = rest[-1]

        @pl.when(k == 0)
        def _():
            acc_ref[...] = part

        @pl.when(k > 0)
        def _():
            acc_ref[...] += part

        @pl.when(k == nk - 1)
        def _():
            epilogue(acc_ref[...], extra_refs, out_refs, i)

    return pl.pallas_call(
        body, name=name, grid=grid,
        in_specs=[a_spec, b_spec] + [s for _, s in extras],
        out_specs=[s for _, s in outs],
        out_shape=[o for o, _ in outs],
        scratch_shapes=[pltpu.VMEM(acc_shape, F32)] if nk > 1 else [],
        compiler_params=_params(("parallel", "parallel", "arbitrary")),
    )(a, b, *[e for e, _ in extras])


def _ln_fwd(r, g, b):
    mu = jnp.mean(r, axis=-1, keepdims=True)
    xc = r - mu
    rstd = lax.rsqrt(jnp.mean(xc * xc, axis=-1, keepdims=True) + LN_EPS)
    xhat = xc * rstd
    return xhat * g + b, xhat, rstd


def _ln_bwd(dy, xhat, rstd, g):
    dxh = dy * g
    m1 = jnp.mean(dxh, axis=-1, keepdims=True)
    m2 = jnp.mean(dxh * xhat, axis=-1, keepdims=True)
    return rstd * (dxh - m1 - xhat * m2)


def _rows8(v):
    n, c = v.shape
    return jnp.sum(v.reshape(n // 8, 8, c), axis=0)


def _accumulate(ref, i, n_steps, part8, all_lanes=False):
    @pl.when(i == 0)
    def _():
        ref[...] = part8

    @pl.when(i > 0)
    def _():
        ref[...] += part8

    @pl.when(i == n_steps - 1)
    def _():
        tot = jnp.sum(ref[...], axis=0, keepdims=True)
        if all_lanes:
            tot = jnp.sum(tot, axis=1, keepdims=True)
        ref[...] = jnp.broadcast_to(tot, ref.shape)


def _group_sums(v, lane_lo):
    s0 = jnp.sum(jnp.where(lane_lo, v, 0.0), axis=-1, keepdims=True)
    s1 = jnp.sum(jnp.where(lane_lo, 0.0, v), axis=-1, keepdims=True)
    return jnp.where(lane_lo, s0, s1)


def _place():
    return lax.axis_index("x"), lax.axis_index("y"), lax.axis_index("c")


def _all_gather(name, shards, out_dtype):
    n = len(shards)

    def body(*refs):
        in_refs, out_refs, stage = refs[:n], refs[n:2 * n], refs[2 * n:3 * n]
        send_sems, recv_sems, local_sems = refs[3 * n:]
        x, y, c = _place()
        me, sibling = (x, y, c), (x, y, 1 - c)
        chips = [(1 - x, y), (x, 1 - y), (1 - x, 1 - y)]

        def copy(a, k, block, to, src=None):
            slot = out_refs[a].at[4 * block[0] + 2 * block[1] + block[2]]
            return pltpu.make_async_remote_copy(
                src_ref=slot if src is None else src, dst_ref=slot,
                send_sem=send_sems.at[a, k], recv_sem=recv_sems.at[a, k],
                device_id=to, device_id_type=MESH)

        mine, first, passed = [], [], []
        for a in range(n):
            stage[a][...] = in_refs[a][...].astype(out_dtype)
            mine.append(pltpu.make_async_copy(stage[a], out_refs[a].at[4 * x + 2 * y + c], local_sems.at[a]))
            mine[a].start()
            first.append([copy(a, 0, me, sibling, src=stage[a])]
                         + [copy(a, 1 + j, me, (*chip, c), src=stage[a]) for j, chip in enumerate(chips)])
            for cp in first[a]:
                cp.start()
            passed.append([copy(a, 4 + j, (*chip, c), sibling) for j, chip in enumerate(chips)])
        for j, chip in enumerate(chips):
            for a in range(n):
                copy(a, 1 + j, (*chip, c), me).wait_recv()
                passed[a][j].start()
        for a in range(n):
            copy(a, 0, sibling, me).wait_recv()
            for j, chip in enumerate(chips):
                copy(a, 4 + j, (*chip, 1 - c), me).wait_recv()
        for a in range(n):
            for cp in first[a] + passed[a]:
                cp.wait_send()
            mine[a].wait()

    return pl.pallas_call(
        body, name=name,
        in_specs=[pl.BlockSpec(memory_space=pltpu.VMEM)] * n,
        out_specs=[pl.BlockSpec(memory_space=pl.ANY)] * n,
        out_shape=[jax.ShapeDtypeStruct((N_DEV,) + s.shape, out_dtype) for s in shards],
        scratch_shapes=[pltpu.VMEM(s.shape, out_dtype) for s in shards]
        + [pltpu.SemaphoreType.DMA((n, 7)), pltpu.SemaphoreType.DMA((n, 7)), pltpu.SemaphoreType.DMA((n,))],
        compiler_params=pltpu.CompilerParams(vmem_limit_bytes=VMEM_LIMIT_BYTES),
    )(*shards)


def _pair_exchange(name, grads):
    n = len(grads)

    def body(*refs):
        g_refs, land_refs = refs[:n], refs[n:2 * n]
        send_sems, recv_sems = refs[2 * n:]
        x, y, c = _place()
        copies = []
        for a in range(n):
            for q in range(4):
                cp = pltpu.make_async_remote_copy(
                    src_ref=g_refs[a].at[2 * q + (1 - c)], dst_ref=land_refs[a].at[q],
                    send_sem=send_sems.at[a, q], recv_sem=recv_sems.at[a, q],
                    device_id=(x, y, 1 - c), device_id_type=MESH)
                cp.start()
                copies.append(cp)
        for cp in copies:
            cp.wait()

    return pl.pallas_call(
        body, name=name,
        in_specs=[pl.BlockSpec(memory_space=pl.ANY)] * n,
        out_specs=[pl.BlockSpec(memory_space=pl.ANY)] * n,
        out_shape=[jax.ShapeDtypeStruct((4,) + g.shape[1:], g.dtype) for g in grads],
        scratch_shapes=[pltpu.SemaphoreType.DMA((n, 4)), pltpu.SemaphoreType.DMA((n, 4))],
    )(*grads)


def _chip_exchange(name, pair_sums):
    n = len(pair_sums)

    def body(*refs):
        p_refs, land_refs = refs[:n], refs[n:2 * n]
        send_sems, recv_sems = refs[2 * n:]
        x, y, c = _place()
        chips = [(1 - x, y), (x, 1 - y), (1 - x, 1 - y)]
        copies = []
        for a in range(n):
            for k, (px, py) in enumerate(chips):
                cp = pltpu.make_async_remote_copy(
                    src_ref=p_refs[a].at[2 * px + py], dst_ref=land_refs[a].at[k],
                    send_sem=send_sems.at[a, k], recv_sem=recv_sems.at[a, k],
                    device_id=(px, py, c), device_id_type=MESH)
                cp.start()
                copies.append(cp)
        for cp in copies:
            cp.wait()

    return pl.pallas_call(
        body, name=name,
        in_specs=[pl.BlockSpec(memory_space=pl.ANY)] * n,
        out_specs=[pl.BlockSpec(memory_space=pl.ANY)] * n,
        out_shape=[jax.ShapeDtypeStruct((3,) + p.shape[1:], p.dtype) for p in pair_sums],
        scratch_shapes=[pltpu.SemaphoreType.DMA((n, 3)), pltpu.SemaphoreType.DMA((n, 3))],
    )(*pair_sums)


def _pair_sum(name, g_f32, landed, c_idx, q_me):
    _, rows, cols = g_f32.shape
    tr = min(rows, 256)

    def body(c_ref, q_ref, g_ref, l_ref, wire_ref, own_ref):
        q = pl.program_id(1)
        s = g_ref[...] + l_ref[...].astype(F32)
        wire_ref[...] = s.astype(BF16)

        @pl.when(q == q_ref[0])
        def _():
            own_ref[...] = s

    return pl.pallas_call(
        body, name=name,
        grid_spec=pltpu.PrefetchScalarGridSpec(
            num_scalar_prefetch=2, grid=(rows // tr, 4),
            in_specs=[pl.BlockSpec((None, tr, cols), lambda i, q, c_ref, q_ref: (2 * q + c_ref[0], i, 0)),
                      pl.BlockSpec((None, tr, cols), lambda i, q, c_ref, q_ref: (q, i, 0))],
            out_specs=[pl.BlockSpec((None, tr, cols), lambda i, q, c_ref, q_ref: (q, i, 0)),
                       pl.BlockSpec((tr, cols), lambda i, q, c_ref, q_ref: (i, 0))]),
        out_shape=[jax.ShapeDtypeStruct((4, rows, cols), BF16), jax.ShapeDtypeStruct((rows, cols), F32)],
        compiler_params=_params(("parallel", "arbitrary")),
    )(c_idx, q_me, g_f32, landed)


def _adamw(w, g, m, v):
    m = ADAM_B1 * m + (1.0 - ADAM_B1) * g
    v = ADAM_B2 * v + (1.0 - ADAM_B2) * (g * g)
    m_hat = m / (1.0 - ADAM_B1 ** ADAM_STEP)
    v_hat = v / (1.0 - ADAM_B2 ** ADAM_STEP)
    delta = -ADAM_LR * (m_hat / (jnp.sqrt(v_hat) + ADAM_EPS) + ADAM_WD * w)
    return delta, m, v


def _shard_update(name, own, landed, w, m, v):
    rows, cols = own.shape
    tr = min(rows, 256)

    def body(own_ref, l_ref, w_ref, m_ref, v_ref, g_out, d_out, m_out, v_out):
        g = own_ref[...] + l_ref[0].astype(F32) + l_ref[1].astype(F32) + l_ref[2].astype(F32)
        d, m_new, v_new = _adamw(w_ref[...], g, m_ref[...], v_ref[...])
        g_out[...] = g
        d_out[...] = d
        m_out[...] = m_new
        v_out[...] = v_new

    tile = pl.BlockSpec((tr, cols), lambda i: (i, 0))
    return pl.pallas_call(
        body, name=name, grid=(rows // tr,),
        in_specs=[tile, pl.BlockSpec((3, tr, cols), lambda i: (0, i, 0)), tile, tile, tile],
        out_specs=[tile] * 4,
        out_shape=[jax.ShapeDtypeStruct((rows, cols), F32)] * 4,
        compiler_params=_params(("parallel",)),
    )(own, landed, w, m, v)


def _small_update(parts, w, m, v):
    def body(p_ref, w_ref, m_ref, v_ref, g_out, d_out, m_out, v_out):
        g = p_ref[0]
        for d in range(1, N_DEV):
            g = g + p_ref[d]
        dl, m_new, v_new = _adamw(w_ref[...], g, m_ref[...], v_ref[...])
        g_out[...] = g
        d_out[...] = dl
        m_out[...] = m_new
        v_out[...] = v_new

    return pl.pallas_call(
        body, name="small_update",
        out_shape=[jax.ShapeDtypeStruct(w.shape, F32)] * 4,
        compiler_params=pltpu.CompilerParams(vmem_limit_bytes=VMEM_LIMIT_BYTES),
    )(parts, w, m, v)


CONV_CHUNK = 512


def _conv_taps(u_s, lo, chunk):
    ext = u_s[pl.ds(lo, chunk + 8), :]
    return ext[8:], pltpu.roll(ext, 1, 0)[8:], pltpu.roll(ext, 2, 0)[8:]


def _conv_fwd(gates, conv_w, g_conv, width):
    t_len = gates.shape[0]
    nblk = width // LANES
    n_chunks = t_len // CONV_CHUNK

    def body(bg_ref, cg_ref, h_ref, w_ref, g_ref, y_ref, u_s):
        lane_lo = lax.broadcasted_iota(jnp.int32, (CONV_CHUNK, LANES), 1) < HEAD_DIM
        u_s[0:8, :] = jnp.zeros((8, LANES), F32)

        def fill(r, _):
            lo = pl.multiple_of(r * CONV_CHUNK, CONV_CHUNK)
            u_s[pl.ds(pl.multiple_of(lo + 8, 8), CONV_CHUNK), :] = cg_ref[pl.ds(lo, CONV_CHUNK), :] * h_ref[pl.ds(lo, CONV_CHUNK), :]
            return 0

        lax.fori_loop(0, n_chunks, fill, 0)
        w = w_ref[...]
        gain = g_ref[...]

        def step(r, _):
            lo = pl.multiple_of(r * CONV_CHUNK, CONV_CHUNK)
            u0, u1, u2 = _conv_taps(u_s, lo, CONV_CHUNK)
            y = bg_ref[pl.ds(lo, CONV_CHUNK), :] * (w[2:3] * u0 + w[1:2] * u1 + w[0:1] * u2)
            rs = lax.rsqrt(_group_sums(y * y, lane_lo) * (1.0 / HEAD_DIM) + RMS_EPS)
            y_ref[pl.ds(lo, CONV_CHUNK), :] = (y * rs * gain).astype(BF16)
            return 0

        lax.fori_loop(0, n_chunks, step, 0)

    col = lambda off: pl.BlockSpec((t_len, LANES), lambda p: (0, off + p))
    return pl.pallas_call(
        body, name="conv_fwd", grid=(nblk,),
        in_specs=[col(0), col(nblk), col(2 * nblk),
                  pl.BlockSpec((3, LANES), lambda p: (0, p)), pl.BlockSpec((1, LANES), lambda p: (0, p))],
        out_specs=col(0),
        out_shape=jax.ShapeDtypeStruct((t_len, 2 * width), BF16),
        scratch_shapes=[pltpu.VMEM((t_len + 8, LANES), F32)],
        compiler_params=_params(("parallel",)),
    )(gates, gates, gates, conv_w, g_conv)


def _conv_bwd(gates, dymix, conv_w, g_conv, width):
    t_len = gates.shape[0]
    nblk = width // LANES
    n_chunks = t_len // CONV_CHUNK

    def body(bg_ref, cg_ref, h_ref, dy_ref, w_ref, g_ref, dbg_ref, dcg_ref, dh_ref, gg_ref, gw_ref, u_s, dc_s):
        lane_lo = lax.broadcasted_iota(jnp.int32, (CONV_CHUNK, LANES), 1) < HEAD_DIM
        u_s[0:8, :] = jnp.zeros((8, LANES), F32)
        dc_s[t_len:t_len + 8, :] = jnp.zeros((8, LANES), F32)

        def fill(r, _):
            lo = pl.multiple_of(r * CONV_CHUNK, CONV_CHUNK)
            u_s[pl.ds(pl.multiple_of(lo + 8, 8), CONV_CHUNK), :] = cg_ref[pl.ds(lo, CONV_CHUNK), :] * h_ref[pl.ds(lo, CONV_CHUNK), :]
            return 0

        lax.fori_loop(0, n_chunks, fill, 0)
        w = w_ref[...]
        gain = g_ref[...]

        def step(r, acc):
            gg, gw0, gw1, gw2 = acc
            lo = pl.multiple_of(r * CONV_CHUNK, CONV_CHUNK)
            u0, u1, u2 = _conv_taps(u_s, lo, CONV_CHUNK)
            bg = bg_ref[pl.ds(lo, CONV_CHUNK), :]
            conv = w[2:3] * u0 + w[1:2] * u1 + w[0:1] * u2
            y = bg * conv
            rs = lax.rsqrt(_group_sums(y * y, lane_lo) * (1.0 / HEAD_DIM) + RMS_EPS)
            yn = y * rs
            dyn_raw = dy_ref[pl.ds(lo, CONV_CHUNK), :]
            dyn = dyn_raw * gain
            d_y = rs * (dyn - yn * (_group_sums(dyn * yn, lane_lo) * (1.0 / HEAD_DIM)))
            dbg_ref[pl.ds(lo, CONV_CHUNK), :] = (d_y * conv).astype(BF16)
            dconv = d_y * bg
            dc_s[pl.ds(lo, CONV_CHUNK), :] = dconv
            return (gg + _rows8(dyn_raw * yn), gw0 + _rows8(dconv * u2), gw1 + _rows8(dconv * u1),
                    gw2 + _rows8(dconv * u0))

        zero = jnp.zeros((8, LANES), F32)
        gg, gw0, gw1, gw2 = lax.fori_loop(0, n_chunks, step, (zero, zero, zero, zero))
        gg_ref[...] = jnp.broadcast_to(jnp.sum(gg, axis=0, keepdims=True), (8, LANES))
        row = lax.broadcasted_iota(jnp.int32, (8, LANES), 0)
        tot = lambda v: jnp.broadcast_to(jnp.sum(v, axis=0, keepdims=True), (8, LANES))
        gw_ref[...] = jnp.where(row == 0, tot(gw0), jnp.where(row == 1, tot(gw1), jnp.where(row == 2, tot(gw2), 0.0)))

        def back(r, _):
            lo = pl.multiple_of(r * CONV_CHUNK, CONV_CHUNK)
            ext = dc_s[pl.ds(lo, CONV_CHUNK + 8), :]
            d0 = ext[:CONV_CHUNK]
            d1 = pltpu.roll(ext, CONV_CHUNK + 7, 0)[:CONV_CHUNK]
            d2 = pltpu.roll(ext, CONV_CHUNK + 6, 0)[:CONV_CHUNK]
            du = w[2:3] * d0 + w[1:2] * d1 + w[0:1] * d2
            dcg_ref[pl.ds(lo, CONV_CHUNK), :] = (du * h_ref[pl.ds(lo, CONV_CHUNK), :]).astype(BF16)
            dh_ref[pl.ds(lo, CONV_CHUNK), :] = (du * cg_ref[pl.ds(lo, CONV_CHUNK), :]).astype(BF16)
            return 0

        lax.fori_loop(0, n_chunks, back, 0)

    col = lambda off: pl.BlockSpec((t_len, LANES), lambda p: (0, off + p))
    small = pl.BlockSpec((8, LANES), lambda p: (0, p))
    return pl.pallas_call(
        body, name="conv_bwd", grid=(nblk,),
        in_specs=[col(0), col(nblk), col(2 * nblk), col(0),
                  pl.BlockSpec((3, LANES), lambda p: (0, p)), pl.BlockSpec((1, LANES), lambda p: (0, p))],
        out_specs=[col(0), col(0), col(0), small, small],
        out_shape=[jax.ShapeDtypeStruct((t_len, width), BF16)] * 3 + [jax.ShapeDtypeStruct((8, width), F32)] * 2,
        scratch_shapes=[pltpu.VMEM((t_len + 8, LANES), F32), pltpu.VMEM((t_len + 8, LANES), F32)],
        compiler_params=_params(("parallel",)),
    )(gates, gates, gates, dymix, conv_w, g_conv)


def _cumsum_matrix(inclusive):
    j = lax.broadcasted_iota(jnp.int32, (2 * LANES, 2 * LANES), 0) % LANES
    s = lax.broadcasted_iota(jnp.int32, (2 * LANES, 2 * LANES), 1)
    keep = (s >= LANES) | ((j >= s) if inclusive else (j > s))
    return keep.astype(BF16)


def _suffix_sums(val, mat):
    hi = val.astype(BF16)
    lo = (val - hi.astype(F32)).astype(BF16)
    both = jnp.dot(jnp.concatenate([hi, lo], axis=1), mat, preferred_element_type=F32)
    return both[:, :LANES], both[:, LANES:]


KEY_CHUNK = 512
SUB_TILES = KEY_CHUNK // Q_BLOCK


def _chunked_suffix(val, carry, mat):
    parts = []
    for b in reversed(range(SUB_TILES)):
        within, total = _suffix_sums(val[:, b * Q_BLOCK:(b + 1) * Q_BLOCK], mat)
        parts.append(within + carry)
        carry = carry + total
    return jnp.concatenate(parts[::-1], axis=1), carry


def _stick_chunk(qm, kc, carry, umat, mask):
    z = lax.dot_general(qm, kc, (_NT, ((), ())), preferred_element_type=F32)
    e = jnp.exp(-jnp.abs(z))
    lk = -(jnp.maximum(z, 0.0) + jnp.log(1.0 + e))
    lb = z + lk
    if mask is not None:
        lk = jnp.where(mask, lk, 0.0)
    suffix, carry = _chunked_suffix(lk, carry, umat)
    a = jnp.exp(lb + suffix)
    if mask is not None:
        a = jnp.where(mask, a, 0.0)
    return z, e, a, carry


DEAD_BELOW = -105.0


def _sweep(i, step, state, carries_of):
    top = (i + 1) * Q_BLOCK
    n_chunks = (top + KEY_CHUNK - 1) // KEY_CHUNK
    col = lax.broadcasted_iota(jnp.int32, (Q_BLOCK, KEY_CHUNK), 1)
    q_row = i * Q_BLOCK + lax.broadcasted_iota(jnp.int32, (Q_BLOCK, KEY_CHUNK), 0)

    def body(loop):
        n, _, st = loop
        end = top - n * KEY_CHUNK
        start = pl.multiple_of(jnp.maximum(end - KEY_CHUNK, 0), Q_BLOCK)
        st = step(start, (col + start) < jnp.minimum(q_row, end), st)
        c0, c1 = carries_of(st)
        alive = (jnp.max(jnp.maximum(c0, c1)) >= DEAD_BELOW).astype(jnp.int32)
        return n + 1, alive, st

    return lax.while_loop(lambda loop: (loop[0] < n_chunks) & (loop[1] > 0), body,
                          (jnp.int32(0), jnp.int32(1), state))[2]


def _head_rows(block, lane_lo_rows):
    zero = jnp.zeros_like(block)
    return jnp.concatenate([jnp.where(lane_lo_rows, block, zero), jnp.where(lane_lo_rows, zero, block)], axis=0)


def _attn_fwd(qkv, g_attn, ymix_in, width):
    t_len = qkv.shape[0]
    npair = width // LANES
    nqb = t_len // Q_BLOCK

    def body(q_ref, k_ref, v_ref, g_ref, umat_ref, ymix_in_ref, o_ref, o_fine_ref, ymix_ref):
        i = pl.program_id(1)
        lane_lo = lax.broadcasted_iota(jnp.int32, (Q_BLOCK, LANES), 1) < HEAD_DIM
        lane_lo_keys = lax.broadcasted_iota(jnp.int32, (KEY_CHUNK, LANES), 1) < HEAD_DIM
        umat = umat_ref[...]
        q = q_ref[...] * jnp.asarray(HEAD_DIM ** -0.5, BF16)
        q_heads = (jnp.where(lane_lo, q, jnp.zeros_like(q)), jnp.where(lane_lo, jnp.zeros_like(q), q))

        def step(rows, mask, state):
            carries, o_acc, o_rest = state
            kc = k_ref[pl.ds(rows, KEY_CHUNK), :]
            v_heads = _head_rows(v_ref[pl.ds(rows, KEY_CHUNK), :], lane_lo_keys)
            new_carries, his, los = [], [], []
            for h in range(2):
                _, _, a, carry = _stick_chunk(q_heads[h], kc, carries[h], umat, mask)
                a_hi = a.astype(BF16)
                his.append(a_hi)
                los.append((a - a_hi.astype(F32)).astype(BF16))
                new_carries.append(carry)
            o_acc = o_acc + jnp.dot(jnp.concatenate(his, axis=1), v_heads, preferred_element_type=F32)
            o_rest = o_rest + jnp.dot(jnp.concatenate(los, axis=1), v_heads, preferred_element_type=F32)
            return tuple(new_carries), o_acc, o_rest

        zero = jnp.zeros((Q_BLOCK, LANES), F32)
        _, o, o_rest = _sweep(i, step, ((zero, zero), zero, zero), lambda st: st[0])
        o_ref[...] = o
        o_fine_ref[...] = o + o_rest
        rs = lax.rsqrt(_group_sums(o * o, lane_lo) * (1.0 / HEAD_DIM) + RMS_EPS)
        ymix_ref[...] = (o * rs * g_ref[...]).astype(BF16)

    return pl.pallas_call(
        body, name="attn_fwd", grid=(npair, nqb),
        in_specs=[pl.BlockSpec((Q_BLOCK, LANES), lambda p, i: (i, p)),
                  pl.BlockSpec((t_len, LANES), lambda p, i: (0, npair + p)),
                  pl.BlockSpec((t_len, LANES), lambda p, i: (0, 2 * npair + p)),
                  pl.BlockSpec((1, LANES), lambda p, i: (0, p)),
                  pl.BlockSpec((2 * LANES, 2 * LANES), lambda p, i: (0, 0)),
                  pl.BlockSpec(memory_space=pl.ANY)],
        out_specs=[pl.BlockSpec((Q_BLOCK, LANES), lambda p, i: (i, p)),
                   pl.BlockSpec((Q_BLOCK, LANES), lambda p, i: (i, p)),
                   pl.BlockSpec((Q_BLOCK, LANES), lambda p, i: (i, npair + p))],
        out_shape=[jax.ShapeDtypeStruct((t_len, width), F32), jax.ShapeDtypeStruct((t_len, width), F32),
                   jax.ShapeDtypeStruct(ymix_in.shape, BF16)],
        input_output_aliases={5: 2},
        compiler_params=_params(("parallel", "arbitrary")),
    )(qkv, qkv, qkv, g_attn, _cumsum_matrix(False), ymix_in)


def _attn_bwd(qkv, o, o_fine, dymix, g_attn, width):
    t_len = qkv.shape[0]
    npair = width // LANES
    nqb = t_len // Q_BLOCK
    scale = HEAD_DIM ** -0.5

    def body(q_ref, k_ref, v_ref, o_ref, o_fine_ref, dy_ref, g_ref, umat_ref, gmat_ref,
             dq_ref, dk_ref, dv_ref, gg_ref, dk_s, dv_s):
        i = pl.program_id(1)
        lane_lo = lax.broadcasted_iota(jnp.int32, (Q_BLOCK, LANES), 1) < HEAD_DIM
        lane_lo_keys = lax.broadcasted_iota(jnp.int32, (KEY_CHUNK, LANES), 1) < HEAD_DIM
        umat, gmat = umat_ref[...], gmat_ref[...]

        @pl.when(i == 0)
        def _():
            dk_s[...] = jnp.zeros_like(dk_s)
            dv_s[...] = jnp.zeros_like(dv_s)

        o_blk = o_ref[...]
        rs = lax.rsqrt(_group_sums(o_blk * o_blk, lane_lo) * (1.0 / HEAD_DIM) + RMS_EPS)
        yn = o_blk * rs
        dy_raw = dy_ref[...]
        dyn = dy_raw * g_ref[...]
        d_o = rs * (dyn - yn * (_group_sums(dyn * yn, lane_lo) * (1.0 / HEAD_DIM)))
        _accumulate(gg_ref, i, nqb, _rows8(dy_raw * yn))
        do_bf = d_o.astype(BF16)
        do_o = do_bf.astype(F32) * o_fine_ref[...]
        totals = (jnp.sum(jnp.where(lane_lo, do_o, 0.0), axis=-1, keepdims=True),
                  jnp.sum(jnp.where(lane_lo, 0.0, do_o), axis=-1, keepdims=True))
        q = q_ref[...]
        q_rows = _head_rows(q, lane_lo)
        do_rows = _head_rows(do_bf, lane_lo)
        q_scaled = q_rows * jnp.asarray(scale, BF16)

        def step(rows, mask, state):
            carries, carries_g, dq_acc = state
            kc = k_ref[pl.ds(rows, KEY_CHUNK), :]
            vc = v_ref[pl.ds(rows, KEY_CHUNK), :]
            new_carries, new_carries_g, dzs, a_bf = [], [], [], []
            for h in range(2):
                head = slice(h * Q_BLOCK, (h + 1) * Q_BLOCK)
                z, e, a, carry = _stick_chunk(q_scaled[head], kc, carries[h], umat, mask)
                da = lax.dot_general(do_rows[head], vc, (_NT, ((), ())), preferred_element_type=F32)
                g = a * da
                suffix_g, carry_g = _chunked_suffix(g, carries_g[h], gmat)
                before = totals[h] - suffix_g
                inv = 1.0 / (1.0 + e)
                sig = jnp.where(z >= 0.0, inv, e * inv)
                dz = g * (1.0 - sig) - before * sig
                if mask is not None:
                    dz = jnp.where(mask, dz, 0.0)
                dzs.append((dz * scale).astype(BF16))
                a_bf.append(a.astype(BF16))
                new_carries.append(carry)
                new_carries_g.append(carry_g)
            dq_acc = dq_acc + jnp.dot(jnp.concatenate(dzs, axis=1), _head_rows(kc, lane_lo_keys),
                                      preferred_element_type=F32)
            dk_s[pl.ds(rows, KEY_CHUNK), :] += lax.dot_general(
                jnp.concatenate(dzs, axis=0), q_rows, (_TN, ((), ())), preferred_element_type=F32)
            dv_s[pl.ds(rows, KEY_CHUNK), :] += lax.dot_general(
                jnp.concatenate(a_bf, axis=0), do_rows, (_TN, ((), ())), preferred_element_type=F32)
            return tuple(new_carries), tuple(new_carries_g), dq_acc

        zero = jnp.zeros((Q_BLOCK, LANES), F32)
        state = _sweep(i, step, ((zero, zero), (zero, zero), zero), lambda st: st[0])
        dq_ref[...] = state[2].astype(BF16)

        @pl.when(i == nqb - 1)
        def _():
            dk_ref[...] = dk_s[...].astype(BF16)
            dv_ref[...] = dv_s[...].astype(BF16)

    whole = pl.BlockSpec((t_len, LANES), lambda p, i: (0, p))
    return pl.pallas_call(
        body, name="attn_bwd", grid=(npair, nqb),
        in_specs=[pl.BlockSpec((Q_BLOCK, LANES), lambda p, i: (i, p)),
                  pl.BlockSpec((t_len, LANES), lambda p, i: (0, npair + p)),
                  pl.BlockSpec((t_len, LANES), lambda p, i: (0, 2 * npair + p)),
                  pl.BlockSpec((Q_BLOCK, LANES), lambda p, i: (i, p)),
                  pl.BlockSpec((Q_BLOCK, LANES), lambda p, i: (i, p)),
                  pl.BlockSpec((Q_BLOCK, LANES), lambda p, i: (i, npair + p)),
                  pl.BlockSpec((1, LANES), lambda p, i: (0, p)),
                  pl.BlockSpec((2 * LANES, 2 * LANES), lambda p, i: (0, 0)),
                  pl.BlockSpec((2 * LANES, 2 * LANES), lambda p, i: (0, 0))],
        out_specs=[pl.BlockSpec((Q_BLOCK, LANES), lambda p, i: (i, p)), whole, whole,
                   pl.BlockSpec((8, LANES), lambda p, i: (0, p))],
        out_shape=[jax.ShapeDtypeStruct((t_len, width), BF16)] * 3 + [jax.ShapeDtypeStruct((8, width), F32)],
        scratch_shapes=[pltpu.VMEM((t_len, LANES), F32), pltpu.VMEM((t_len, LANES), F32)],
        compiler_params=_params(("parallel", "arbitrary")),
    )(qkv, qkv, qkv, o, o_fine, dymix, g_attn, _cumsum_matrix(False), _cumsum_matrix(True))


def kernel(x, w_in, conv_w, g_conv, g_attn, w_out, ln1_g, ln1_b, w_up, w_down, ln2_g, ln2_b, loss_target, m_w_in, m_conv_w, m_g_conv, m_g_attn, m_w_out, m_ln1_g, m_ln1_b, m_w_up, m_w_down, m_ln2_g, m_ln2_b, v_w_in, v_conv_w, v_g_conv, v_g_attn, v_w_out, v_ln1_g, v_ln1_b, v_w_up, v_w_down, v_ln2_g, v_ln2_b):
    x2, target = x[0], loss_target[0]
    t_len, d_model = x2.shape
    width = g_conv.shape[1]
    in_shard = w_in.shape[2]
    up_shard = w_up.shape[2]
    d_ff = up_shard * N_DEV
    cw = conv_w.shape[2]
    xi, yi, ci = _place()
    dev = 4 * xi + 2 * yi + ci
    tm = 256

    win_g, wout_g, wup_g, wdown_g = _all_gather(
        "gather_weights", [w_in[0], w_out[0], w_up[0], w_down[0]], BF16)
    (convw_g,) = _all_gather("gather_conv_w", [jnp.pad(conv_w[0], ((0, 5), (0, LANES - cw)))], F32)
    conv_full = jnp.transpose(convw_g[:, :3, :cw], (1, 0, 2)).reshape(3, width)
    wout_full = wout_g.reshape(d_model, d_model)
    wdown_full = wdown_g.reshape(d_ff, d_model)

    n_gate = (3 * width) // in_shard

    def cast_epilogue(acc, extra, outs, i):
        outs[0][...] = acc.astype(outs[0].dtype)

    def proj_in(name, first_shard, dtype):
        return _matmul(
            name, x2, win_g,
            pl.BlockSpec((tm, d_model), lambda i, j, k: (i, 0)),
            pl.BlockSpec((None, d_model, in_shard), lambda i, j, k: (first_shard + j, 0, 0)),
            _NN, (t_len // tm, n_gate, 1), None,
            [(jax.ShapeDtypeStruct((t_len, 3 * width), dtype), pl.BlockSpec((tm, in_shard), lambda i, j, k: (i, j)))],
            cast_epilogue)[0]

    gates = proj_in("proj_in_gates", 0, F32)
    qkv = proj_in("proj_in_qkv", n_gate, BF16)

    ymix = _conv_fwd(gates, conv_full, g_conv, width)
    o_attn, o_fine, ymix = _attn_fwd(qkv, g_attn, ymix, width)

    row_tile = pl.BlockSpec((tm, d_model), lambda i, j, k: (i, 0))
    vec = pl.BlockSpec((1, d_model), lambda i, j, k: (0, 0))
    full_sq = pl.BlockSpec((d_model, d_model), lambda i, j, k: (0, 0))

    def out_ln1_epilogue(acc, extra, outs, i):
        x_ref, g_ref, b_ref = extra
        r1 = ALPHA * x_ref[...] + acc
        x1, _, _ = _ln_fwd(r1, g_ref[...], b_ref[...])
        outs[0][...] = r1
        outs[1][...] = x1
        outs[2][...] = x1.astype(BF16)

    r1, x1, x1_bf = _matmul(
        "proj_out_ln1", ymix, wout_full, row_tile, full_sq, _NN, (t_len // tm, 1, 1), None,
        [(jax.ShapeDtypeStruct((t_len, d_model), F32), row_tile),
         (jax.ShapeDtypeStruct((t_len, d_model), F32), row_tile),
         (jax.ShapeDtypeStruct((t_len, d_model), BF16), row_tile)],
        out_ln1_epilogue, extras=[(x2, row_tile), (ln1_g, vec), (ln1_b, vec)])

    def up_epilogue(acc, extra, outs, i):
        outs[0][...] = acc
        relu = jnp.maximum(acc, 0.0)
        outs[1][...] = (relu * relu).astype(BF16)

    up_tile = pl.BlockSpec((tm, up_shard), lambda i, j, k: (i, j))
    up, hid = _matmul(
        "ffn_up", x1_bf, wup_g, row_tile,
        pl.BlockSpec((None, d_model, up_shard), lambda i, j, k: (j, 0, 0)),
        _NN, (t_len // tm, N_DEV, 1), None,
        [(jax.ShapeDtypeStruct((t_len, d_ff), F32), up_tile),
         (jax.ShapeDtypeStruct((t_len, d_ff), BF16), up_tile)],
        up_epilogue)

    n_row_tiles = t_len // tm
    acc8 = pl.BlockSpec((8, d_model), lambda i, j, k: (0, 0))

    def down_ln2_epilogue(acc, extra, outs, i):
        x1_ref, t_ref, g_ref, b_ref = extra
        dr2_ref, dr2_bf_ref, loss_ref, gg_ref, gb_ref = outs
        gain = g_ref[...]
        r2 = ALPHA * x1_ref[...] + acc
        y, xhat, rstd = _ln_fwd(r2, gain, b_ref[...])
        diff = y - t_ref[...]
        d_y = diff * (1.0 / d_model)
        dr2 = _ln_bwd(d_y, xhat, rstd, gain)
        dr2_ref[...] = dr2
        dr2_bf_ref[...] = dr2.astype(BF16)
        _accumulate(gg_ref, i, n_row_tiles, _rows8(d_y * xhat))
        _accumulate(gb_ref, i, n_row_tiles, _rows8(d_y))
        _accumulate(loss_ref, i, n_row_tiles, _rows8(diff * diff), all_lanes=True)

    dr2, dr2_bf, loss_cols, g_ln2_g, g_ln2_b = _matmul(
        "ffn_down_ln2_loss", hid, wdown_full,
        pl.BlockSpec((tm, d_ff), lambda i, j, k: (i, 0)),
        pl.BlockSpec((d_ff, d_model), lambda i, j, k: (0, 0)),
        _NN, (n_row_tiles, 1, 1), None,
        [(jax.ShapeDtypeStruct((t_len, d_model), F32), row_tile),
         (jax.ShapeDtypeStruct((t_len, d_model), BF16), row_tile),
         (jax.ShapeDtypeStruct((8, d_model), F32), acc8),
         (jax.ShapeDtypeStruct((8, d_model), F32), acc8),
         (jax.ShapeDtypeStruct((8, d_model), F32), acc8)],
        down_ln2_epilogue, extras=[(x1, row_tile), (target, row_tile), (ln2_g, vec), (ln2_b, vec)])

    def dup_epilogue(acc, extra, outs, i):
        outs[0][...] = (acc * (2.0 * jnp.maximum(extra[0][...], 0.0))).astype(BF16)

    ff_tile = pl.BlockSpec((tm, 1024), lambda i, j, k: (i, j))
    (dup,) = _matmul(
        "ffn_dhid", dr2_bf, wdown_full, row_tile,
        pl.BlockSpec((1024, d_model), lambda i, j, k: (j, 0)),
        _NT, (t_len // tm, d_ff // 1024, 1), None,
        [(jax.ShapeDtypeStruct((t_len, d_ff), BF16), ff_tile)],
        dup_epilogue, extras=[(up, ff_tile)])

    def grad_epilogue(acc, extra, outs, i):
        outs[0][...] = acc
        outs[1][...] = acc.astype(BF16)

    tk = 512

    def weight_grad(name, a, b, rows, cols, tmw, shard_rows, shard_cols):
        by_rows = shard_cols == cols
        n_i = rows // tmw
        if by_rows:
            per = shard_rows // tmw
            out_spec = pl.BlockSpec((None, tmw, cols), lambda i, j, k: (i // per, i % per, 0))
            b_spec = pl.BlockSpec((tk, cols), lambda i, j, k: (k, 0))
            grid = (n_i, 1, t_len // tk)
            acc_shape = (tmw, cols)
        else:
            out_spec = pl.BlockSpec((None, tmw, shard_cols), lambda i, j, k: (j, i, 0))
            b_spec = pl.BlockSpec((tk, shard_cols), lambda i, j, k: (k, j))
            grid = (n_i, N_DEV, t_len // tk)
            acc_shape = (tmw, shard_cols)
        shape = (N_DEV, shard_rows, shard_cols)
        return _matmul(
            name, a, b, pl.BlockSpec((tk, tmw), lambda i, j, k: (k, i)), b_spec, _TN, grid, acc_shape,
            [(jax.ShapeDtypeStruct(shape, F32), out_spec), (jax.ShapeDtypeStruct(shape, BF16), out_spec)],
            grad_epilogue)

    gw_down, gw_down_bf = weight_grad("grad_w_down", hid, dr2_bf, d_ff, d_model, 512, d_ff // N_DEV, d_model)
    gw_up, gw_up_bf = weight_grad("grad_w_up", x1_bf, dup, d_model, d_ff, 512, d_model, up_shard)

    def dx1_ln1_epilogue(acc, extra, outs, i):
        dr2_ref, r1_ref, g_ref, b_ref = extra
        dr1_ref, dr1_bf_ref, gg_ref, gb_ref = outs
        gain = g_ref[...]
        dx1 = ALPHA * dr2_ref[...] + acc
        _, xhat, rstd = _ln_fwd(r1_ref[...], gain, b_ref[...])
        dr1 = _ln_bwd(dx1, xhat, rstd, gain)
        dr1_ref[...] = dr1
        dr1_bf_ref[...] = dr1.astype(BF16)
        _accumulate(gg_ref, i, n_row_tiles, _rows8(dx1 * xhat))
        _accumulate(gb_ref, i, n_row_tiles, _rows8(dx1))

    dr1, dr1_bf, g_ln1_g, g_ln1_b = _matmul(
        "ffn_dx1_ln1", dup, wup_g,
        pl.BlockSpec((tm, up_shard), lambda i, j, k: (i, k)),
        pl.BlockSpec((None, d_model, up_shard), lambda i, j, k: (k, 0, 0)),
        _NT, (n_row_tiles, 1, N_DEV), (tm, d_model),
        [(jax.ShapeDtypeStruct((t_len, d_model), F32), row_tile),
         (jax.ShapeDtypeStruct((t_len, d_model), BF16), row_tile),
         (jax.ShapeDtypeStruct((8, d_model), F32), acc8),
         (jax.ShapeDtypeStruct((8, d_model), F32), acc8)],
        dx1_ln1_epilogue, extras=[(dr2, row_tile), (r1, row_tile), (ln1_g, vec), (ln1_b, vec)])

    gw_out, gw_out_bf = weight_grad("grad_w_out", ymix, dr1_bf, d_model, d_model, 128, d_model // N_DEV, d_model)

    def plain_epilogue(acc, extra, outs, i):
        outs[0][...] = acc

    (dymix,) = _matmul(
        "proj_out_bwd", dr1_bf, wout_full, row_tile, full_sq, _NT, (n_row_tiles, 1, 1), None,
        [(jax.ShapeDtypeStruct((t_len, d_model), F32), row_tile)], plain_epilogue)

    d_bg, d_cg, d_h, g_gconv, g_convw = _conv_bwd(gates, dymix, conv_full, g_conv, width)
    d_q, d_k, d_v, g_gattn = _attn_bwd(qkv, o_attn, o_fine, dymix, g_attn, width)
    dproj = jnp.concatenate([d_bg, d_cg, d_h, d_q, d_k, d_v], axis=1)

    gw_in, gw_in_bf = weight_grad("grad_w_in", x2, dproj, d_model, 3 * width * 2, 512, d_model, in_shard)

    def dx_epilogue(acc, extra, outs, i):
        outs[0][...] = ALPHA * extra[0][...] + acc

    (grad_x,) = _matmul(
        "proj_in_bwd", dproj, win_g,
        pl.BlockSpec((tm, in_shard), lambda i, j, k: (i, k)),
        pl.BlockSpec((None, d_model, in_shard), lambda i, j, k: (k, 0, 0)),
        _NT, (n_row_tiles, 1, N_DEV), (tm, d_model),
        [(jax.ShapeDtypeStruct((t_len, d_model), F32), row_tile)],
        dx_epilogue, extras=[(dr1, row_tile)])

    c_idx = jnp.reshape(ci, (1,)).astype(jnp.int32)
    q_me = jnp.reshape(2 * xi + yi, (1,)).astype(jnp.int32)
    grads_f32 = [gw_in, gw_out, gw_up, gw_down]
    landed = _pair_exchange("reduce_pair", [gw_in_bf, gw_out_bf, gw_up_bf, gw_down_bf])
    names = ["w_in", "w_out", "w_up", "w_down"]
    wire, own = [], []
    for nm, g_f32, land in zip(names, grads_f32, landed):
        wr, ow = _pair_sum("pair_sum_" + nm, g_f32, land, c_idx, q_me)
        wire.append(wr)
        own.append(ow)
    landed2 = _chip_exchange("reduce_chips", wire)
    big = {}
    for nm, ow, land, w, m, v in zip(names, own, landed2, (w_in, w_out, w_up, w_down),
                                     (m_w_in, m_w_out, m_w_up, m_w_down), (v_w_in, v_w_out, v_w_up, v_w_down)):
        big[nm] = [r[None] for r in _shard_update("update_" + nm, ow, land, w[0], m[0], v[0])]

    def pack(l1g, l1b, l2g, l2b, gc, ga, taps):
        return jnp.concatenate([l1g, l1b, l2g, l2b, jnp.concatenate([gc, ga], axis=1),
                                jnp.concatenate([taps, jnp.zeros((3, d_model - width), F32)], axis=1)], axis=0)

    local = pack(g_ln1_g[:1], g_ln1_b[:1], g_ln2_g[:1], g_ln2_b[:1], g_gconv[:1], g_gattn[:1], g_convw[:3])
    (parts,) = _all_gather("gather_small_grads", [local], F32)

    def spread(a):
        return lax.dynamic_update_slice(jnp.zeros((3, width), F32), a, (0, dev * cw))

    w_small = pack(ln1_g, ln1_b, ln2_g, ln2_b, g_conv, g_attn, spread(conv_w[0]))
    m_small = pack(m_ln1_g, m_ln1_b, m_ln2_g, m_ln2_b, m_g_conv, m_g_attn, spread(m_conv_w[0]))
    v_small = pack(v_ln1_g, v_ln1_b, v_ln2_g, v_ln2_b, v_g_conv, v_g_attn, spread(v_conv_w[0]))
    small = _small_update(parts, w_small, m_small, v_small)

    def unpack(a):
        taps = lax.dynamic_slice(a[5:8, :width], (0, dev * cw), (3, cw))[None]
        return {"ln1_g": a[0:1], "ln1_b": a[1:2], "ln2_g": a[2:3], "ln2_b": a[3:4],
                "g_conv": a[4:5, :width], "g_attn": a[4:5, width:], "conv_w": taps}

    small = [unpack(a) for a in small]
    loss = lax.psum((0.5 / d_model) * loss_cols[0, 0], ("x", "y", "c"))

    order = ["w_in", "conv_w", "g_conv", "g_attn", "w_out", "ln1_g", "ln1_b", "w_up", "w_down", "ln2_g", "ln2_b"]
    result = [loss, grad_x[None]]
    for kind in range(4):
        for nm in order:
            result.append(big[nm][kind] if nm in big else small[kind][nm])
    return tuple(result)
```

```python
import functools

import jax
import jax.numpy as jnp
from jax import lax
from jax.experimental import pallas as pl
from jax.experimental.pallas import tpu as pltpu

F32 = jnp.float32
BF16 = jnp.bfloat16
MESH = pl.DeviceIdType.MESH

N_DEV = 8
HEAD_DIM = 64
LANES = 128
Q_BLOCK = 128
ALPHA = 2.0 ** 0.25
LN_EPS = 1e-5
RMS_EPS = 1e-6
ADAM_LR, ADAM_B1, ADAM_B2, ADAM_EPS, ADAM_WD, ADAM_STEP = 0.001, 0.9, 0.999, 1e-08, 0.01, 10
VMEM_LIMIT_BYTES = 48 * 1024 * 1024

_NT = ((1,), (1,))
_NN = ((1,), (0,))
_TN = ((0,), (0,))


def _params(sem=None):
    return pltpu.CompilerParams(dimension_semantics=sem, vmem_limit_bytes=VMEM_LIMIT_BYTES)


def _dot(a, b, contract):
    return lax.dot_general(a.astype(BF16), b.astype(BF16), (contract, ((), ())), preferred_element_type=F32)


def _whole(contract):
    return lambda a_ref, b_ref: _dot(a_ref[...], b_ref[...], contract)


def _shards_wide(a_ref, b_ref):
    a = a_ref[...].astype(BF16)
    return jnp.concatenate([_dot(a, b_ref[s], _NN) for s in range(b_ref.shape[0])], axis=1)


def _shards_deep(a_ref, b_ref):
    n = b_ref.shape[2]
    acc = _dot(a_ref[:, 0:n], b_ref[0], _NT)
    for s in range(1, b_ref.shape[0]):
        acc = acc + _dot(a_ref[:, s * n:(s + 1) * n], b_ref[s], _NT)
    return acc


def _matmul(name, a, b, a_spec, b_spec, product, n_steps, outs, epilogue, extras=()):
    n_extra = len(extras)

    def body(a_ref, b_ref, *rest):
        epilogue(product(a_ref, b_ref), rest[:n_extra], rest[n_extra:], pl.program_id(0))

    return pl.pallas_call(
        body, name=name, grid=(n_steps,),
        in_specs=[a_spec, b_spec] + [s for _, s in extras],
        out_specs=[s for _, s in outs],
        out_shape=[o for o, _ in outs],
        compiler_params=_params(("arbitrary",)),
    )(a, b, *[e for e, _ in extras])


def _ln_fwd(r, g, b):
    mu = jnp.mean(r, axis=-1, keepdims=True)
    xc = r - mu
    rstd = lax.rsqrt(jnp.mean(xc * xc, axis=-1, keepdims=True) + LN_EPS)
    xhat = xc * rstd
    return xhat * g + b, xhat, rstd


def _ln_bwd(dy, xhat, rstd, g):
    dxh = dy * g
    m1 = jnp.mean(dxh, axis=-1, keepdims=True)
    m2 = jnp.mean(dxh * xhat, axis=-1, keepdims=True)
    return rstd * (dxh - m1 - xhat * m2)


def _rows8(v):
    n, c = v.shape
    return jnp.sum(v.reshape(n // 8, 8, c), axis=0)


def _accumulate(ref, i, n_steps, part8, all_lanes=False):
    @pl.when(i == 0)
    def _():
        ref[...] = part8

    @pl.when(i > 0)
    def _():
        ref[...] += part8

    @pl.when(i == n_steps - 1)
    def _():
        tot = jnp.sum(ref[...], axis=0, keepdims=True)
        if all_lanes:
            tot = jnp.sum(tot, axis=1, keepdims=True)
        ref[...] = jnp.broadcast_to(tot, ref.shape)


def _group_sums(v, lane_lo):
    s0 = jnp.sum(jnp.where(lane_lo, v, 0.0), axis=-1, keepdims=True)
    s1 = jnp.sum(jnp.where(lane_lo, 0.0, v), axis=-1, keepdims=True)
    return jnp.where(lane_lo, s0, s1)


def _place():
    return lax.axis_index("x"), lax.axis_index("y"), lax.axis_index("c")


def _all_gather(name, shards, out_dtype):
    n = len(shards)

    def body(*refs):
        in_refs, out_refs, stage = refs[:n], refs[n:2 * n], refs[2 * n:3 * n]
        send_sems, recv_sems, local_sems = refs[3 * n:]
        x, y, c = _place()
        me, sibling = (x, y, c), (x, y, 1 - c)
        chips = [(1 - x, y), (x, 1 - y), (1 - x, 1 - y)]

        def copy(a, k, block, to, src=None):
            slot = out_refs[a].at[4 * block[0] + 2 * block[1] + block[2]]
            return pltpu.make_async_remote_copy(
                src_ref=slot if src is None else src, dst_ref=slot,
                send_sem=send_sems.at[a, k], recv_sem=recv_sems.at[a, k],
                device_id=to, device_id_type=MESH)

        mine, first, passed = [], [], []
        for a in range(n):
            stage[a][...] = in_refs[a][...].astype(out_dtype)
            mine.append(pltpu.make_async_copy(stage[a], out_refs[a].at[4 * x + 2 * y + c], local_sems.at[a]))
            mine[a].start()
            first.append([copy(a, 0, me, sibling, src=stage[a])]
                         + [copy(a, 1 + j, me, (*chip, c), src=stage[a]) for j, chip in enumerate(chips)])
            for cp in first[a]:
                cp.start()
            passed.append([copy(a, 4 + j, (*chip, c), sibling) for j, chip in enumerate(chips)])
        for j, chip in enumerate(chips):
            for a in range(n):
                copy(a, 1 + j, (*chip, c), me).wait_recv()
                passed[a][j].start()
        for a in range(n):
            copy(a, 0, sibling, me).wait_recv()
            for j, chip in enumerate(chips):
                copy(a, 4 + j, (*chip, 1 - c), me).wait_recv()
        for a in range(n):
            for cp in first[a] + passed[a]:
                cp.wait_send()
            mine[a].wait()

    return pl.pallas_call(
        body, name=name,
        in_specs=[pl.BlockSpec(memory_space=pltpu.VMEM)] * n,
        out_specs=[pl.BlockSpec(memory_space=pl.ANY)] * n,
        out_shape=[jax.ShapeDtypeStruct((N_DEV,) + s.shape, out_dtype) for s in shards],
        scratch_shapes=[pltpu.VMEM(s.shape, out_dtype) for s in shards]
        + [pltpu.SemaphoreType.DMA((n, 7)), pltpu.SemaphoreType.DMA((n, 7)), pltpu.SemaphoreType.DMA((n,))],
        compiler_params=pltpu.CompilerParams(vmem_limit_bytes=VMEM_LIMIT_BYTES),
    )(*shards)


def _pair_exchange(name, grads):
    n = len(grads)

    def body(*refs):
        g_refs, land_refs = refs[:n], refs[n:2 * n]
        send_sems, recv_sems = refs[2 * n:]
        x, y, c = _place()
        copies = []
        for a in range(n):
            for q in range(4):
                cp = pltpu.make_async_remote_copy(
                    src_ref=g_refs[a].at[2 * q + (1 - c)], dst_ref=land_refs[a].at[q],
                    send_sem=send_sems.at[a, q], recv_sem=recv_sems.at[a, q],
                    device_id=(x, y, 1 - c), device_id_type=MESH)
                cp.start()
                copies.append(cp)
        for cp in copies:
            cp.wait()

    return pl.pallas_call(
        body, name=name,
        in_specs=[pl.BlockSpec(memory_space=pl.ANY)] * n,
        out_specs=[pl.BlockSpec(memory_space=pl.ANY)] * n,
        out_shape=[jax.ShapeDtypeStruct((4,) + g.shape[1:], g.dtype) for g in grads],
        scratch_shapes=[pltpu.SemaphoreType.DMA((n, 4)), pltpu.SemaphoreType.DMA((n, 4))],
    )(*grads)


def _chip_exchange(name, pair_sums):
    n = len(pair_sums)

    def body(*refs):
        p_refs, land_refs = refs[:n], refs[n:2 * n]
        send_sems, recv_sems = refs[2 * n:]
        x, y, c = _place()
        chips = [(1 - x, y), (x, 1 - y), (1 - x, 1 - y)]
        copies = []
        for a in range(n):
            for k, (px, py) in enumerate(chips):
                cp = pltpu.make_async_remote_copy(
                    src_ref=p_refs[a].at[2 * px + py], dst_ref=land_refs[a].at[k],
                    send_sem=send_sems.at[a, k], recv_sem=recv_sems.at[a, k],
                    device_id=(px, py, c), device_id_type=MESH)
                cp.start()
                copies.append(cp)
        for cp in copies:
            cp.wait()

    return pl.pallas_call(
        body, name=name,
        in_specs=[pl.BlockSpec(memory_space=pl.ANY)] * n,
        out_specs=[pl.BlockSpec(memory_space=pl.ANY)] * n,
        out_shape=[jax.ShapeDtypeStruct((3,) + p.shape[1:], p.dtype) for p in pair_sums],
        scratch_shapes=[pltpu.SemaphoreType.DMA((n, 3)), pltpu.SemaphoreType.DMA((n, 3))],
    )(*pair_sums)


def _pair_sum(name, g_f32, landed, c_idx, q_me):
    _, rows, cols = g_f32.shape
    tr = min(rows, 256)

    def body(c_ref, q_ref, g_ref, l_ref, wire_ref, own_ref):
        q = pl.program_id(1)
        s = g_ref[...] + l_ref[...].astype(F32)
        wire_ref[...] = s.astype(BF16)

        @pl.when(q == q_ref[0])
        def _():
            own_ref[...] = s

    return pl.pallas_call(
        body, name=name,
        grid_spec=pltpu.PrefetchScalarGridSpec(
            num_scalar_prefetch=2, grid=(rows // tr, 4),
            in_specs=[pl.BlockSpec((None, tr, cols), lambda i, q, c_ref, q_ref: (2 * q + c_ref[0], i, 0)),
                      pl.BlockSpec((None, tr, cols), lambda i, q, c_ref, q_ref: (q, i, 0))],
            out_specs=[pl.BlockSpec((None, tr, cols), lambda i, q, c_ref, q_ref: (q, i, 0)),
                       pl.BlockSpec((tr, cols), lambda i, q, c_ref, q_ref: (i, 0))]),
        out_shape=[jax.ShapeDtypeStruct((4, rows, cols), BF16), jax.ShapeDtypeStruct((rows, cols), F32)],
        compiler_params=_params(("parallel", "arbitrary")),
    )(c_idx, q_me, g_f32, landed)


def _adamw(w, g, m, v):
    m = ADAM_B1 * m + (1.0 - ADAM_B1) * g
    v = ADAM_B2 * v + (1.0 - ADAM_B2) * (g * g)
    m_hat = m / (1.0 - ADAM_B1 ** ADAM_STEP)
    v_hat = v / (1.0 - ADAM_B2 ** ADAM_STEP)
    delta = -ADAM_LR * (m_hat / (jnp.sqrt(v_hat) + ADAM_EPS) + ADAM_WD * w)
    return delta, m, v


def _shard_update(name, own, landed, w, m, v):
    rows, cols = own.shape
    tr = min(rows, 256)

    def body(own_ref, l_ref, w_ref, m_ref, v_ref, g_out, d_out, m_out, v_out):
        g = own_ref[...] + l_ref[0].astype(F32) + l_ref[1].astype(F32) + l_ref[2].astype(F32)
        d, m_new, v_new = _adamw(w_ref[...], g, m_ref[...], v_ref[...])
        g_out[...] = g
        d_out[...] = d
        m_out[...] = m_new
        v_out[...] = v_new

    tile = pl.BlockSpec((tr, cols), lambda i: (i, 0))
    return pl.pallas_call(
        body, name=name, grid=(rows // tr,),
        in_specs=[tile, pl.BlockSpec((3, tr, cols), lambda i: (0, i, 0)), tile, tile, tile],
        out_specs=[tile] * 4,
        out_shape=[jax.ShapeDtypeStruct((rows, cols), F32)] * 4,
        compiler_params=_params(("parallel",)),
    )(own, landed, w, m, v)


def _small_update(parts, w, m, v):
    def body(p_ref, w_ref, m_ref, v_ref, g_out, d_out, m_out, v_out):
        g = p_ref[0]
        for d in range(1, N_DEV):
            g = g + p_ref[d]
        dl, m_new, v_new = _adamw(w_ref[...], g, m_ref[...], v_ref[...])
        g_out[...] = g
        d_out[...] = dl
        m_out[...] = m_new
        v_out[...] = v_new

    return pl.pallas_call(
        body, name="small_update",
        out_shape=[jax.ShapeDtypeStruct(w.shape, F32)] * 4,
        compiler_params=pltpu.CompilerParams(vmem_limit_bytes=VMEM_LIMIT_BYTES),
    )(parts, w, m, v)


CONV_CHUNK = 512


def _conv_taps(u_s, lo, chunk):
    ext = u_s[pl.ds(lo, chunk + 8), :]
    return ext[8:], pltpu.roll(ext, 1, 0)[8:], pltpu.roll(ext, 2, 0)[8:]


def _conv_fwd(gates, conv_w, g_conv, width):
    t_len = gates.shape[0]
    nblk = width // LANES
    n_chunks = t_len // CONV_CHUNK

    def body(bg_ref, cg_ref, h_ref, w_ref, g_ref, y_ref, u_s):
        lane_lo = lax.broadcasted_iota(jnp.int32, (CONV_CHUNK, LANES), 1) < HEAD_DIM
        u_s[0:8, :] = jnp.zeros((8, LANES), F32)

        def fill(r, _):
            lo = pl.multiple_of(r * CONV_CHUNK, CONV_CHUNK)
            u_s[pl.ds(pl.multiple_of(lo + 8, 8), CONV_CHUNK), :] = cg_ref[pl.ds(lo, CONV_CHUNK), :] * h_ref[pl.ds(lo, CONV_CHUNK), :]
            return 0

        lax.fori_loop(0, n_chunks, fill, 0)
        w = w_ref[...]
        gain = g_ref[...]

        def step(r, _):
            lo = pl.multiple_of(r * CONV_CHUNK, CONV_CHUNK)
            u0, u1, u2 = _conv_taps(u_s, lo, CONV_CHUNK)
            y = bg_ref[pl.ds(lo, CONV_CHUNK), :] * (w[2:3] * u0 + w[1:2] * u1 + w[0:1] * u2)
            rs = lax.rsqrt(_group_sums(y * y, lane_lo) * (1.0 / HEAD_DIM) + RMS_EPS)
            y_ref[pl.ds(lo, CONV_CHUNK), :] = (y * rs * gain).astype(BF16)
            return 0

        lax.fori_loop(0, n_chunks, step, 0)

    col = lambda off: pl.BlockSpec((t_len, LANES), lambda p: (0, off + p))
    return pl.pallas_call(
        body, name="conv_fwd", grid=(nblk,),
        in_specs=[col(0), col(nblk), col(2 * nblk),
                  pl.BlockSpec((3, LANES), lambda p: (0, p)), pl.BlockSpec((1, LANES), lambda p: (0, p))],
        out_specs=col(0),
        out_shape=jax.ShapeDtypeStruct((t_len, 2 * width), BF16),
        scratch_shapes=[pltpu.VMEM((t_len + 8, LANES), F32)],
        compiler_params=_params(("parallel",)),
    )(gates, gates, gates, conv_w, g_conv)


def _conv_bwd(gates, dymix, conv_w, g_conv, width):
    t_len = gates.shape[0]
    nblk = width // LANES
    n_chunks = t_len // CONV_CHUNK

    def body(bg_ref, cg_ref, h_ref, dy_ref, w_ref, g_ref, dbg_ref, dcg_ref, dh_ref, gg_ref, gw_ref, u_s, dc_s):
        lane_lo = lax.broadcasted_iota(jnp.int32, (CONV_CHUNK, LANES), 1) < HEAD_DIM
        u_s[0:8, :] = jnp.zeros((8, LANES), F32)
        dc_s[t_len:t_len + 8, :] = jnp.zeros((8, LANES), F32)

        def fill(r, _):
            lo = pl.multiple_of(r * CONV_CHUNK, CONV_CHUNK)
            u_s[pl.ds(pl.multiple_of(lo + 8, 8), CONV_CHUNK), :] = cg_ref[pl.ds(lo, CONV_CHUNK), :] * h_ref[pl.ds(lo, CONV_CHUNK), :]
            return 0

        lax.fori_loop(0, n_chunks, fill, 0)
        w = w_ref[...]
        gain = g_ref[...]

        def step(r, acc):
            gg, gw0, gw1, gw2 = acc
            lo = pl.multiple_of(r * CONV_CHUNK, CONV_CHUNK)
            u0, u1, u2 = _conv_taps(u_s, lo, CONV_CHUNK)
            bg = bg_ref[pl.ds(lo, CONV_CHUNK), :]
            conv = w[2:3] * u0 + w[1:2] * u1 + w[0:1] * u2
            y = bg * conv
            rs = lax.rsqrt(_group_sums(y * y, lane_lo) * (1.0 / HEAD_DIM) + RMS_EPS)
            yn = y * rs
            dyn_raw = dy_ref[pl.ds(lo, CONV_CHUNK), :]
            dyn = dyn_raw * gain
            d_y = rs * (dyn - yn * (_group_sums(dyn * yn, lane_lo) * (1.0 / HEAD_DIM)))
            dbg_ref[pl.ds(lo, CONV_CHUNK), :] = (d_y * conv).astype(BF16)
            dconv = d_y * bg
            dc_s[pl.ds(lo, CONV_CHUNK), :] = dconv
            return (gg + _rows8(dyn_raw * yn), gw0 + _rows8(dconv * u2), gw1 + _rows8(dconv * u1),
                    gw2 + _rows8(dconv * u0))

        zero = jnp.zeros((8, LANES), F32)
        gg, gw0, gw1, gw2 = lax.fori_loop(0, n_chunks, step, (zero, zero, zero, zero))
        gg_ref[...] = jnp.broadcast_to(jnp.sum(gg, axis=0, keepdims=True), (8, LANES))
        row = lax.broadcasted_iota(jnp.int32, (8, LANES), 0)
        tot = lambda v: jnp.broadcast_to(jnp.sum(v, axis=0, keepdims=True), (8, LANES))
        gw_ref[...] = jnp.where(row == 0, tot(gw0), jnp.where(row == 1, tot(gw1), jnp.where(row == 2, tot(gw2), 0.0)))

        def back(r, _):
            lo = pl.multiple_of(r * CONV_CHUNK, CONV_CHUNK)
            ext = dc_s[pl.ds(lo, CONV_CHUNK + 8), :]
            d0 = ext[:CONV_CHUNK]
            d1 = pltpu.roll(ext, CONV_CHUNK + 7, 0)[:CONV_CHUNK]
            d2 = pltpu.roll(ext, CONV_CHUNK + 6, 0)[:CONV_CHUNK]
            du = w[2:3] * d0 + w[1:2] * d1 + w[0:1] * d2
            dcg_ref[pl.ds(lo, CONV_CHUNK), :] = (du * h_ref[pl.ds(lo, CONV_CHUNK), :]).astype(BF16)
            dh_ref[pl.ds(lo, CONV_CHUNK), :] = (du * cg_ref[pl.ds(lo, CONV_CHUNK), :]).astype(BF16)
            return 0

        lax.fori_loop(0, n_chunks, back, 0)

    col = lambda off: pl.BlockSpec((t_len, LANES), lambda p: (0, off + p))
    small = pl.BlockSpec((8, LANES), lambda p: (0, p))
    return pl.pallas_call(
        body, name="conv_bwd", grid=(nblk,),
        in_specs=[col(0), col(nblk), col(2 * nblk), col(0),
                  pl.BlockSpec((3, LANES), lambda p: (0, p)), pl.BlockSpec((1, LANES), lambda p: (0, p))],
        out_specs=[col(0), col(0), col(0), small, small],
        out_shape=[jax.ShapeDtypeStruct((t_len, width), BF16)] * 3 + [jax.ShapeDtypeStruct((8, width), F32)] * 2,
        scratch_shapes=[pltpu.VMEM((t_len + 8, LANES), F32), pltpu.VMEM((t_len + 8, LANES), F32)],
        compiler_params=_params(("parallel",)),
    )(gates, gates, gates, dymix, conv_w, g_conv)


def _cumsum_matrix(inclusive):
    j = lax.broadcasted_iota(jnp.int32, (2 * LANES, 2 * LANES), 0) % LANES
    s = lax.broadcasted_iota(jnp.int32, (2 * LANES, 2 * LANES), 1)
    keep = (s >= LANES) | ((j >= s) if inclusive else (j > s))
    return keep.astype(BF16)


def _suffix_sums(val, mat):
    hi = val.astype(BF16)
    lo = (val - hi.astype(F32)).astype(BF16)
    both = jnp.dot(jnp.concatenate([hi, lo], axis=1), mat, preferred_element_type=F32)
    return both[:, :LANES], both[:, LANES:]


KEY_CHUNK = 512
SUB_TILES = KEY_CHUNK // Q_BLOCK


def _chunked_suffix(val, carry, mat):
    parts = []
    for b in reversed(range(SUB_TILES)):
        within, total = _suffix_sums(val[:, b * Q_BLOCK:(b + 1) * Q_BLOCK], mat)
        parts.append(within + carry)
        carry = carry + total
    return jnp.concatenate(parts[::-1], axis=1), carry


def _stick_chunk(qm, kc, carry, umat, mask):
    z = lax.dot_general(qm, kc, (_NT, ((), ())), preferred_element_type=F32)
    e = jnp.exp(-jnp.abs(z))
    lk = -(jnp.maximum(z, 0.0) + jnp.log(1.0 + e))
    lb = z + lk
    if mask is not None:
        lk = jnp.where(mask, lk, 0.0)
    suffix, carry = _chunked_suffix(lk, carry, umat)
    a = jnp.exp(lb + suffix)
    if mask is not None:
        a = jnp.where(mask, a, 0.0)
    return z, e, a, carry


DEAD_BELOW = -105.0


def _sweep(i, step, state, carries_of):
    top = (i + 1) * Q_BLOCK
    n_chunks = (top + KEY_CHUNK - 1) // KEY_CHUNK
    col = lax.broadcasted_iota(jnp.int32, (Q_BLOCK, KEY_CHUNK), 1)
    q_row = i * Q_BLOCK + lax.broadcasted_iota(jnp.int32, (Q_BLOCK, KEY_CHUNK), 0)

    def body(loop):
        n, _, st = loop
        end = top - n * KEY_CHUNK
        start = pl.multiple_of(jnp.maximum(end - KEY_CHUNK, 0), Q_BLOCK)
        st = step(start, (col + start) < jnp.minimum(q_row, end), st)
        c0, c1 = carries_of(st)
        alive = (jnp.max(jnp.maximum(c0, c1)) >= DEAD_BELOW).astype(jnp.int32)
        return n + 1, alive, st

    return lax.while_loop(lambda loop: (loop[0] < n_chunks) & (loop[1] > 0), body,
                          (jnp.int32(0), jnp.int32(1), state))[2]


def _head_rows(block, lane_lo_rows):
    zero = jnp.zeros_like(block)
    return jnp.concatenate([jnp.where(lane_lo_rows, block, zero), jnp.where(lane_lo_rows, zero, block)], axis=0)


def _attn_fwd(qkv, g_attn, ymix_in, width):
    t_len = qkv.shape[0]
    npair = width // LANES
    nqb = t_len // Q_BLOCK

    def body(q_ref, k_ref, v_ref, g_ref, umat_ref, ymix_in_ref, o_ref, o_fine_ref, ymix_ref):
        i = pl.program_id(1)
        lane_lo = lax.broadcasted_iota(jnp.int32, (Q_BLOCK, LANES), 1) < HEAD_DIM
        lane_lo_keys = lax.broadcasted_iota(jnp.int32, (KEY_CHUNK, LANES), 1) < HEAD_DIM
        umat = umat_ref[...]
        q = q_ref[...] * jnp.asarray(HEAD_DIM ** -0.5, BF16)
        q_heads = (jnp.where(lane_lo, q, jnp.zeros_like(q)), jnp.where(lane_lo, jnp.zeros_like(q), q))

        def step(rows, mask, state):
            carries, o_acc, o_rest = state
            kc = k_ref[pl.ds(rows, KEY_CHUNK), :]
            v_heads = _head_rows(v_ref[pl.ds(rows, KEY_CHUNK), :], lane_lo_keys)
            new_carries, his, los = [], [], []
            for h in range(2):
                _, _, a, carry = _stick_chunk(q_heads[h], kc, carries[h], umat, mask)
                a_hi = a.astype(BF16)
                his.append(a_hi)
                los.append((a - a_hi.astype(F32)).astype(BF16))
                new_carries.append(carry)
            o_acc = o_acc + jnp.dot(jnp.concatenate(his, axis=1), v_heads, preferred_element_type=F32)
            o_rest = o_rest + jnp.dot(jnp.concatenate(los, axis=1), v_heads, preferred_element_type=F32)
            return tuple(new_carries), o_acc, o_rest

        zero = jnp.zeros((Q_BLOCK, LANES), F32)
        _, o, o_rest = _sweep(i, step, ((zero, zero), zero, zero), lambda st: st[0])
        o_ref[...] = o
        o_fine_ref[...] = o + o_rest
        rs = lax.rsqrt(_group_sums(o * o, lane_lo) * (1.0 / HEAD_DIM) + RMS_EPS)
        ymix_ref[...] = (o * rs * g_ref[...]).astype(BF16)

    return pl.pallas_call(
        body, name="attn_fwd", grid=(npair, nqb),
        in_specs=[pl.BlockSpec((Q_BLOCK, LANES), lambda p, i: (i, p)),
                  pl.BlockSpec((t_len, LANES), lambda p, i: (0, npair + p)),
                  pl.BlockSpec((t_len, LANES), lambda p, i: (0, 2 * npair + p)),
                  pl.BlockSpec((1, LANES), lambda p, i: (0, p)),
                  pl.BlockSpec((2 * LANES, 2 * LANES), lambda p, i: (0, 0)),
                  pl.BlockSpec(memory_space=pl.ANY)],
        out_specs=[pl.BlockSpec((Q_BLOCK, LANES), lambda p, i: (i, p)),
                   pl.BlockSpec((Q_BLOCK, LANES), lambda p, i: (i, p)),
                   pl.BlockSpec((Q_BLOCK, LANES), lambda p, i: (i, npair + p))],
        out_shape=[jax.ShapeDtypeStruct((t_len, width), F32), jax.ShapeDtypeStruct((t_len, width), F32),
                   jax.ShapeDtypeStruct(ymix_in.shape, BF16)],
        input_output_aliases={5: 2},
        compiler_params=_params(("parallel", "arbitrary")),
    )(qkv, qkv, qkv, g_attn, _cumsum_matrix(False), ymix_in)


def _attn_bwd(qkv, o, o_fine, dymix, g_attn, width):
    t_len = qkv.shape[0]
    npair = width // LANES
    nqb = t_len // Q_BLOCK
    scale = HEAD_DIM ** -0.5

    def body(q_ref, k_ref, v_ref, o_ref, o_fine_ref, dy_ref, g_ref, umat_ref, gmat_ref,
             dq_ref, dk_ref, dv_ref, gg_ref, dk_s, dv_s):
        i = pl.program_id(1)
        lane_lo = lax.broadcasted_iota(jnp.int32, (Q_BLOCK, LANES), 1) < HEAD_DIM
        lane_lo_keys = lax.broadcasted_iota(jnp.int32, (KEY_CHUNK, LANES), 1) < HEAD_DIM
        umat, gmat = umat_ref[...], gmat_ref[...]

        @pl.when(i == 0)
        def _():
            dk_s[...] = jnp.zeros_like(dk_s)
            dv_s[...] = jnp.zeros_like(dv_s)

        o_blk = o_ref[...]
        rs = lax.rsqrt(_group_sums(o_blk * o_blk, lane_lo) * (1.0 / HEAD_DIM) + RMS_EPS)
        yn = o_blk * rs
        dy_raw = dy_ref[...]
        dyn = dy_raw * g_ref[...]
        d_o = rs * (dyn - yn * (_group_sums(dyn * yn, lane_lo) * (1.0 / HEAD_DIM)))
        _accumulate(gg_ref, i, nqb, _rows8(dy_raw * yn))
        do_bf = d_o.astype(BF16)
        do_o = do_bf.astype(F32) * o_fine_ref[...]
        totals = (jnp.sum(jnp.where(lane_lo, do_o, 0.0), axis=-1, keepdims=True),
                  jnp.sum(jnp.where(lane_lo, 0.0, do_o), axis=-1, keepdims=True))
        q = q_ref[...]
        q_rows = _head_rows(q, lane_lo)
        do_rows = _head_rows(do_bf, lane_lo)
        q_scaled = q_rows * jnp.asarray(scale, BF16)

        def step(rows, mask, state):
            carries, carries_g, dq_acc = state
            kc = k_ref[pl.ds(rows, KEY_CHUNK), :]
            vc = v_ref[pl.ds(rows, KEY_CHUNK), :]
            new_carries, new_carries_g, dzs, a_bf = [], [], [], []
            for h in range(2):
                head = slice(h * Q_BLOCK, (h + 1) * Q_BLOCK)
                z, e, a, carry = _stick_chunk(q_scaled[head], kc, carries[h], umat, mask)
                da = lax.dot_general(do_rows[head], vc, (_NT, ((), ())), preferred_element_type=F32)
                g = a * da
                suffix_g, carry_g = _chunked_suffix(g, carries_g[h], gmat)
                before = totals[h] - suffix_g
                inv = 1.0 / (1.0 + e)
                sig = jnp.where(z >= 0.0, inv, e * inv)
                dz = g * (1.0 - sig) - before * sig
                if mask is not None:
                    dz = jnp.where(mask, dz, 0.0)
                dzs.append((dz * scale).astype(BF16))
                a_bf.append(a.astype(BF16))
                new_carries.append(carry)
                new_carries_g.append(carry_g)
            dq_acc = dq_acc + jnp.dot(jnp.concatenate(dzs, axis=1), _head_rows(kc, lane_lo_keys),
                                      preferred_element_type=F32)
            dk_s[pl.ds(rows, KEY_CHUNK), :] += lax.dot_general(
                jnp.concatenate(dzs, axis=0), q_rows, (_TN, ((), ())), preferred_element_type=F32)
            dv_s[pl.ds(rows, KEY_CHUNK), :] += lax.dot_general(
                jnp.concatenate(a_bf, axis=0), do_rows, (_TN, ((), ())), preferred_element_type=F32)
            return tuple(new_carries), tuple(new_carries_g), dq_acc

        zero = jnp.zeros((Q_BLOCK, LANES), F32)
        state = _sweep(i, step, ((zero, zero), (zero, zero), zero), lambda st: st[0])
        dq_ref[...] = state[2].astype(BF16)

        @pl.when(i == nqb - 1)
        def _():
            dk_ref[...] = dk_s[...].astype(BF16)
            dv_ref[...] = dv_s[...].astype(BF16)

    whole = pl.BlockSpec((t_len, LANES), lambda p, i: (0, p))
    return pl.pallas_call(
        body, name="attn_bwd", grid=(npair, nqb),
        in_specs=[pl.BlockSpec((Q_BLOCK, LANES), lambda p, i: (i, p)),
                  pl.BlockSpec((t_len, LANES), lambda p, i: (0, npair + p)),
                  pl.BlockSpec((t_len, LANES), lambda p, i: (0, 2 * npair + p)),
                  pl.BlockSpec((Q_BLOCK, LANES), lambda p, i: (i, p)),
                  pl.BlockSpec((Q_BLOCK, LANES), lambda p, i: (i, p)),
                  pl.BlockSpec((Q_BLOCK, LANES), lambda p, i: (i, npair + p)),
                  pl.BlockSpec((1, LANES), lambda p, i: (0, p)),
                  pl.BlockSpec((2 * LANES, 2 * LANES), lambda p, i: (0, 0)),
                  pl.BlockSpec((2 * LANES, 2 * LANES), lambda p, i: (0, 0))],
        out_specs=[pl.BlockSpec((Q_BLOCK, LANES), lambda p, i: (i, p)), whole, whole,
                   pl.BlockSpec((8, LANES), lambda p, i: (0, p))],
        out_shape=[jax.ShapeDtypeStruct((t_len, width), BF16)] * 3 + [jax.ShapeDtypeStruct((8, width), F32)],
        scratch_shapes=[pltpu.VMEM((t_len, LANES), F32), pltpu.VMEM((t_len, LANES), F32)],
        compiler_params=_params(("parallel", "arbitrary")),
    )(qkv, qkv, qkv, o, o_fine, dymix, g_attn, _cumsum_matrix(False), _cumsum_matrix(True))


def kernel(x, w_in, conv_w, g_conv, g_attn, w_out, ln1_g, ln1_b, w_up, w_down, ln2_g, ln2_b, loss_target, m_w_in, m_conv_w, m_g_conv, m_g_attn, m_w_out, m_ln1_g, m_ln1_b, m_w_up, m_w_down, m_ln2_g, m_ln2_b, v_w_in, v_conv_w, v_g_conv, v_g_attn, v_w_out, v_ln1_g, v_ln1_b, v_w_up, v_w_down, v_ln2_g, v_ln2_b):
    x2, target = x[0], loss_target[0]
    t_len, d_model = x2.shape
    width = g_conv.shape[1]
    in_shard = w_in.shape[2]
    up_shard = w_up.shape[2]
    d_ff = up_shard * N_DEV
    cw = conv_w.shape[2]
    xi, yi, ci = _place()
    dev = 4 * xi + 2 * yi + ci
    tm = 256

    win_g, wout_g, wup_g, wdown_g = _all_gather(
        "gather_weights", [w_in[0], w_out[0], w_up[0], w_down[0]], BF16)
    (convw_g,) = _all_gather("gather_conv_w", [jnp.pad(conv_w[0], ((0, 5), (0, LANES - cw)))], F32)
    conv_full = jnp.transpose(convw_g[:, :3, :cw], (1, 0, 2)).reshape(3, width)
    wout_full = wout_g.reshape(d_model, d_model)
    wdown_full = wdown_g.reshape(d_ff, d_model)

    n_row_tiles = t_len // tm
    rows = lambda cols: pl.BlockSpec((tm, cols), lambda i: (i, 0))
    resident = lambda arr: pl.BlockSpec(arr.shape, lambda i: (0,) * arr.ndim)
    vec = pl.BlockSpec((1, d_model), lambda i: (0, 0))
    acc8 = pl.BlockSpec((8, d_model), lambda i: (0, 0))
    tokens = lambda cols, dtype: jax.ShapeDtypeStruct((t_len, cols), dtype)

    def proj_in_epilogue(acc, extra, outs, i):
        outs[0][...] = acc[:, :3 * width]
        outs[1][...] = acc[:, 3 * width:].astype(BF16)
        outs[2][...] = extra[0][...].astype(BF16)

    gates, qkv, x_bf = _matmul(
        "proj_in", x2, win_g, rows(d_model), resident(win_g), _shards_wide, n_row_tiles,
        [(tokens(3 * width, F32), rows(3 * width)), (tokens(3 * width, BF16), rows(3 * width)),
         (tokens(d_model, BF16), rows(d_model))],
        proj_in_epilogue, extras=[(x2, rows(d_model))])

    ymix = _conv_fwd(gates, conv_full, g_conv, width)
    o_attn, o_fine, ymix = _attn_fwd(qkv, g_attn, ymix, width)

    def out_ln1_epilogue(acc, extra, outs, i):
        x_ref, g_ref, b_ref = extra
        r1 = ALPHA * x_ref[...] + acc
        x1, _, _ = _ln_fwd(r1, g_ref[...], b_ref[...])
        outs[0][...] = r1
        outs[1][...] = x1
        outs[2][...] = x1.astype(BF16)

    r1, x1, x1_bf = _matmul(
        "proj_out_ln1", ymix, wout_full, rows(d_model), resident(wout_full), _whole(_NN), n_row_tiles,
        [(tokens(d_model, F32), rows(d_model)), (tokens(d_model, F32), rows(d_model)),
         (tokens(d_model, BF16), rows(d_model))],
        out_ln1_epilogue, extras=[(x2, rows(d_model)), (ln1_g, vec), (ln1_b, vec)])

    def up_epilogue(acc, extra, outs, i):
        outs[0][...] = acc
        relu = jnp.maximum(acc, 0.0)
        outs[1][...] = (relu * relu).astype(BF16)

    up, hid = _matmul(
        "ffn_up", x1_bf, wup_g, rows(d_model), resident(wup_g), _shards_wide, n_row_tiles,
        [(tokens(d_ff, F32), rows(d_ff)), (tokens(d_ff, BF16), rows(d_ff))], up_epilogue)

    def down_ln2_epilogue(acc, extra, outs, i):
        x1_ref, t_ref, g_ref, b_ref = extra
        dr2_ref, dr2_bf_ref, loss_ref, gg_ref, gb_ref = outs
        gain = g_ref[...]
        r2 = ALPHA * x1_ref[...] + acc
        y, xhat, rstd = _ln_fwd(r2, gain, b_ref[...])
        diff = y - t_ref[...]
        d_y = diff * (1.0 / d_model)
        dr2 = _ln_bwd(d_y, xhat, rstd, gain)
        dr2_ref[...] = dr2
        dr2_bf_ref[...] = dr2.astype(BF16)
        _accumulate(gg_ref, i, n_row_tiles, _rows8(d_y * xhat))
        _accumulate(gb_ref, i, n_row_tiles, _rows8(d_y))
        _accumulate(loss_ref, i, n_row_tiles, _rows8(diff * diff), all_lanes=True)

    small_acc = (jax.ShapeDtypeStruct((8, d_model), F32), acc8)
    dr2, dr2_bf, loss_cols, g_ln2_g, g_ln2_b = _matmul(
        "ffn_down_ln2_loss", hid, wdown_full, rows(d_ff), resident(wdown_full), _whole(_NN), n_row_tiles,
        [(tokens(d_model, F32), rows(d_model)), (tokens(d_model, BF16), rows(d_model)),
         small_acc, small_acc, small_acc],
        down_ln2_epilogue, extras=[(x1, rows(d_model)), (target, rows(d_model)), (ln2_g, vec), (ln2_b, vec)])

    def dup_epilogue(acc, extra, outs, i):
        outs[0][...] = (acc * (2.0 * jnp.maximum(extra[0][...], 0.0))).astype(BF16)

    (dup,) = _matmul(
        "ffn_dhid", dr2_bf, wdown_full, rows(d_model), resident(wdown_full), _whole(_NT), n_row_tiles,
        [(tokens(d_ff, BF16), rows(d_ff))], dup_epilogue, extras=[(up, rows(d_ff))])

    def weight_grad(name, acts, cots, shard_shape, acts_by_shard):
        sr, sc = shard_shape
        per = max(1, 256 // sr) if acts_by_shard else 1
        tall = lambda cols: pl.BlockSpec((t_len, cols), lambda i: (0, i))
        if acts_by_shard:
            a_spec, b_spec = tall(per * sr), resident(cots)
        else:
            a_spec, b_spec = resident(acts), tall(sc)
        out_spec = pl.BlockSpec((per, sr, sc), lambda i: (i, 0, 0))

        def epilogue(acc, extra, outs, i):
            acc = acc.reshape(per, sr, sc)
            outs[0][...] = acc
            outs[1][...] = acc.astype(BF16)

        shape = (N_DEV, sr, sc)
        return _matmul(name, acts, cots, a_spec, b_spec, _whole(_TN), N_DEV // per,
                       [(jax.ShapeDtypeStruct(shape, F32), out_spec), (jax.ShapeDtypeStruct(shape, BF16), out_spec)],
                       epilogue)

    gw_down, gw_down_bf = weight_grad("grad_w_down", hid, dr2_bf, (d_ff // N_DEV, d_model), True)
    gw_up, gw_up_bf = weight_grad("grad_w_up", x1_bf, dup, (d_model, up_shard), False)

    def dx1_ln1_epilogue(acc, extra, outs, i):
        dr2_ref, r1_ref, g_ref, b_ref = extra
        dr1_ref, dr1_bf_ref, gg_ref, gb_ref = outs
        gain = g_ref[...]
        dx1 = ALPHA * dr2_ref[...] + acc
        _, xhat, rstd = _ln_fwd(r1_ref[...], gain, b_ref[...])
        dr1 = _ln_bwd(dx1, xhat, rstd, gain)
        dr1_ref[...] = dr1
        dr1_bf_ref[...] = dr1.astype(BF16)
        _accumulate(gg_ref, i, n_row_tiles, _rows8(dx1 * xhat))
        _accumulate(gb_ref, i, n_row_tiles, _rows8(dx1))

    dr1, dr1_bf, g_ln1_g, g_ln1_b = _matmul(
        "ffn_dx1_ln1", dup, wup_g, rows(d_ff), resident(wup_g), _shards_deep, n_row_tiles,
        [(tokens(d_model, F32), rows(d_model)), (tokens(d_model, BF16), rows(d_model)), small_acc, small_acc],
        dx1_ln1_epilogue, extras=[(dr2, rows(d_model)), (r1, rows(d_model)), (ln1_g, vec), (ln1_b, vec)])

    gw_out, gw_out_bf = weight_grad("grad_w_out", ymix, dr1_bf, (d_model // N_DEV, d_model), True)

    def plain_epilogue(acc, extra, outs, i):
        outs[0][...] = acc

    (dymix,) = _matmul(
        "proj_out_bwd", dr1_bf, wout_full, rows(d_model), resident(wout_full), _whole(_NT), n_row_tiles,
        [(tokens(d_model, F32), rows(d_model))], plain_epilogue)

    d_bg, d_cg, d_h, g_gconv, g_convw = _conv_bwd(gates, dymix, conv_full, g_conv, width)
    d_q, d_k, d_v, g_gattn = _attn_bwd(qkv, o_attn, o_fine, dymix, g_attn, width)
    dproj = jnp.concatenate([d_bg, d_cg, d_h, d_q, d_k, d_v], axis=1)

    gw_in, gw_in_bf = weight_grad("grad_w_in", x_bf, dproj, (d_model, in_shard), False)

    def dx_epilogue(acc, extra, outs, i):
        outs[0][...] = ALPHA * extra[0][...] + acc

    (grad_x,) = _matmul(
        "proj_in_bwd", dproj, win_g, rows(6 * width), resident(win_g), _shards_deep, n_row_tiles,
        [(tokens(d_model, F32), rows(d_model))], dx_epilogue, extras=[(dr1, rows(d_model))])

    c_idx = jnp.reshape(ci, (1,)).astype(jnp.int32)
    q_me = jnp.reshape(2 * xi + yi, (1,)).astype(jnp.int32)
    grads_f32 = [gw_in, gw_out, gw_up, gw_down]
    landed = _pair_exchange("reduce_pair", [gw_in_bf, gw_out_bf, gw_up_bf, gw_down_bf])
    names = ["w_in", "w_out", "w_up", "w_down"]
    wire, own = [], []
    for nm, g_f32, land in zip(names, grads_f32, landed):
        wr, ow = _pair_sum("pair_sum_" + nm, g_f32, land, c_idx, q_me)
        wire.append(wr)
        own.append(ow)
    landed2 = _chip_exchange("reduce_chips", wire)
    big = {}
    for nm, ow, land, w, m, v in zip(names, own, landed2, (w_in, w_out, w_up, w_down),
                                     (m_w_in, m_w_out, m_w_up, m_w_down), (v_w_in, v_w_out, v_w_up, v_w_down)):
        big[nm] = [r[None] for r in _shard_update("update_" + nm, ow, land, w[0], m[0], v[0])]

    def pack(l1g, l1b, l2g, l2b, gc, ga, taps):
        return jnp.concatenate([l1g, l1b, l2g, l2b, jnp.concatenate([gc, ga], axis=1),
                                jnp.concatenate([taps, jnp.zeros((3, d_model - width), F32)], axis=1)], axis=0)

    local = pack(g_ln1_g[:1], g_ln1_b[:1], g_ln2_g[:1], g_ln2_b[:1], g_gconv[:1], g_gattn[:1], g_convw[:3])
    (parts,) = _all_gather("gather_small_grads", [local], F32)

    def spread(a):
        return lax.dynamic_update_slice(jnp.zeros((3, width), F32), a, (0, dev * cw))

    w_small = pack(ln1_g, ln1_b, ln2_g, ln2_b, g_conv, g_attn, spread(conv_w[0]))
    m_small = pack(m_ln1_g, m_ln1_b, m_ln2_g, m_ln2_b, m_g_conv, m_g_attn, spread(m_conv_w[0]))
    v_small = pack(v_ln1_g, v_ln1_b, v_ln2_g, v_ln2_b, v_g_conv, v_g_attn, spread(v_conv_w[0]))
    small = _small_update(parts, w_small, m_small, v_small)

    def unpack(a):
        taps = lax.dynamic_slice(a[5:8, :width], (0, dev * cw), (3, cw))[None]
        return {"ln1_g": a[0:1], "ln1_b": a[1:2], "ln2_g": a[2:3], "ln2_b": a[3:4],
                "g_conv": a[4:5, :width], "g_attn": a[4:5, width:], "conv_w": taps}

    small = [unpack(a) for a in small]
    loss = lax.psum((0.5 / d_model) * loss_cols[0, 0], ("x", "y", "c"))

    order = ["w_in", "conv_w", "g_conv", "g_attn", "w_out", "ln1_g", "ln1_b", "w_up", "w_down", "ln2_g", "ln2_b"]
    result = [loss, grad_x[None]]
    for kind in range(4):
        for nm in order:
            result.append(big[nm][kind] if nm in big else small[kind][nm])
    return tuple(result)
```

```python
import functools

import jax
import jax.numpy as jnp
from jax import lax
from jax.experimental import pallas as pl
from jax.experimental.pallas import tpu as pltpu

F32 = jnp.float32
BF16 = jnp.bfloat16
MESH = pl.DeviceIdType.MESH

N_DEV = 8
HEAD_DIM = 64
LANES = 128
Q_BLOCK = 128
ALPHA = 2.0 ** 0.25
LN_EPS = 1e-5
RMS_EPS = 1e-6
ADAM_LR, ADAM_B1, ADAM_B2, ADAM_EPS, ADAM_WD, ADAM_STEP = 0.001, 0.9, 0.999, 1e-08, 0.01, 10
VMEM_LIMIT_BYTES = 48 * 1024 * 1024

_NT = ((1,), (1,))
_NN = ((1,), (0,))
_TN = ((0,), (0,))


def _params(sem=None):
    return pltpu.CompilerParams(dimension_semantics=sem, vmem_limit_bytes=VMEM_LIMIT_BYTES)


def _dot(a, b, contract):
    return lax.dot_general(a.astype(BF16), b.astype(BF16), (contract, ((), ())), preferred_element_type=F32)


def _whole(contract):
    return lambda a_ref, b_ref: _dot(a_ref[...], b_ref[...], contract)


def _shards_wide(a_ref, b_ref):
    a = a_ref[...].astype(BF16)
    return jnp.concatenate([_dot(a, b_ref[s], _NN) for s in range(b_ref.shape[0])], axis=1)


def _shards_deep(a_ref, b_ref):
    n = b_ref.shape[2]
    acc = _dot(a_ref[:, 0:n], b_ref[0], _NT)
    for s in range(1, b_ref.shape[0]):
        acc = acc + _dot(a_ref[:, s * n:(s + 1) * n], b_ref[s], _NT)
    return acc


def _matmul(name, a, b, a_spec, b_spec, product, n_steps, outs, epilogue, extras=(), exchange=()):
    n_extra, n_out, n_ex = len(extras), len(outs), len(exchange)

    def body(a_ref, b_ref, *rest):
        i = pl.program_id(0)
        extra_refs, rest = rest[:n_extra], rest[n_extra:]
        p_refs, rest = rest[:n_ex], rest[n_ex:]
        out_refs, rest = rest[:n_out], rest[n_out:]
        land_refs, sems = rest[:n_ex], rest[n_ex:]
        copies = _chip_copies(p_refs, land_refs, *sems) if n_ex else []

        @pl.when(i == 0)
        def _():
            for cp in copies:
                cp.start()

        epilogue(product(a_ref, b_ref), extra_refs, out_refs, i)

        @pl.when(i == n_steps - 1)
        def _():
            for cp in copies:
                cp.wait()

    hbm = pl.BlockSpec(memory_space=pl.ANY)
    return pl.pallas_call(
        body, name=name, grid=(n_steps,),
        in_specs=[a_spec, b_spec] + [s for _, s in extras] + [hbm] * n_ex,
        out_specs=[s for _, s in outs] + [hbm] * n_ex,
        out_shape=[o for o, _ in outs] + _chip_landing(exchange),
        scratch_shapes=_chip_scratch(n_ex) if n_ex else [],
        compiler_params=_params(("arbitrary",)),
    )(a, b, *[e for e, _ in extras], *exchange)


def _ln_fwd(r, g, b):
    mu = jnp.mean(r, axis=-1, keepdims=True)
    xc = r - mu
    rstd = lax.rsqrt(jnp.mean(xc * xc, axis=-1, keepdims=True) + LN_EPS)
    xhat = xc * rstd
    return xhat * g + b, xhat, rstd


def _ln_bwd(dy, xhat, rstd, g):
    dxh = dy * g
    m1 = jnp.mean(dxh, axis=-1, keepdims=True)
    m2 = jnp.mean(dxh * xhat, axis=-1, keepdims=True)
    return rstd * (dxh - m1 - xhat * m2)


def _rows8(v):
    n, c = v.shape
    return jnp.sum(v.reshape(n // 8, 8, c), axis=0)


def _accumulate(ref, i, n_steps, part8, all_lanes=False):
    @pl.when(i == 0)
    def _():
        ref[...] = part8

    @pl.when(i > 0)
    def _():
        ref[...] += part8

    @pl.when(i == n_steps - 1)
    def _():
        tot = jnp.sum(ref[...], axis=0, keepdims=True)
        if all_lanes:
            tot = jnp.sum(tot, axis=1, keepdims=True)
        ref[...] = jnp.broadcast_to(tot, ref.shape)


def _group_sums(v, lane_lo):
    s0 = jnp.sum(jnp.where(lane_lo, v, 0.0), axis=-1, keepdims=True)
    s1 = jnp.sum(jnp.where(lane_lo, 0.0, v), axis=-1, keepdims=True)
    return jnp.where(lane_lo, s0, s1)


def _place():
    return lax.axis_index("x"), lax.axis_index("y"), lax.axis_index("c")


def _gather_plan(src_refs, out_refs, send_sems, recv_sems, local_sems):
    n = len(src_refs)
    x, y, c = _place()
    me, sibling = (x, y, c), (x, y, 1 - c)
    chips = [(1 - x, y), (x, 1 - y), (1 - x, 1 - y)]

    def copy(a, k, block, to, src=None):
        slot = out_refs[a].at[4 * block[0] + 2 * block[1] + block[2]]
        return pltpu.make_async_remote_copy(
            src_ref=slot if src is None else src, dst_ref=slot,
            send_sem=send_sems.at[a, k], recv_sem=recv_sems.at[a, k],
            device_id=to, device_id_type=MESH)

    mine = [pltpu.make_async_copy(src_refs[a], out_refs[a].at[4 * x + 2 * y + c], local_sems.at[a]) for a in range(n)]
    first = [[copy(a, 0, me, sibling, src=src_refs[a])]
             + [copy(a, 1 + j, me, (*chip, c), src=src_refs[a]) for j, chip in enumerate(chips)] for a in range(n)]
    passed = [[copy(a, 4 + j, (*chip, c), sibling) for j, chip in enumerate(chips)] for a in range(n)]

    def start():
        for a in range(n):
            mine[a].start()
            for cp in first[a]:
                cp.start()

    def forward():
        for j, chip in enumerate(chips):
            for a in range(n):
                copy(a, 1 + j, (*chip, c), me).wait_recv()
                passed[a][j].start()

    def finish():
        for a in range(n):
            copy(a, 0, sibling, me).wait_recv()
            for j, chip in enumerate(chips):
                copy(a, 4 + j, (*chip, 1 - c), me).wait_recv()
        for a in range(n):
            for cp in first[a] + passed[a]:
                cp.wait_send()
            mine[a].wait()

    return start, forward, finish


def _gather_scratch(n):
    return [pltpu.SemaphoreType.DMA((n, 7)), pltpu.SemaphoreType.DMA((n, 7)), pltpu.SemaphoreType.DMA((n,))]


def _all_gather(name, shards, out_dtype, cast_only=()):
    n, m = len(shards), len(cast_only)

    def body(*refs):
        in_refs, cast_in = refs[:n], refs[n:n + m]
        out_refs, cast_out = refs[n + m:2 * n + m], refs[2 * n + m:2 * (n + m)]
        stage = refs[2 * (n + m):3 * n + 2 * m]
        for a in range(n):
            stage[a][...] = in_refs[a][...].astype(out_dtype)
        start, forward, finish = _gather_plan(stage, out_refs, *refs[3 * n + 2 * m:])
        start()
        for a in range(m):
            cast_out[a][...] = cast_in[a][...].astype(out_dtype)
        forward()
        finish()

    return pl.pallas_call(
        body, name=name,
        in_specs=[pl.BlockSpec(memory_space=pltpu.VMEM)] * (n + m),
        out_specs=[pl.BlockSpec(memory_space=pl.ANY)] * n + [pl.BlockSpec(memory_space=pltpu.VMEM)] * m,
        out_shape=[jax.ShapeDtypeStruct((N_DEV,) + s.shape, out_dtype) for s in shards]
        + [jax.ShapeDtypeStruct(s.shape, out_dtype) for s in cast_only],
        scratch_shapes=[pltpu.VMEM(s.shape, out_dtype) for s in shards] + _gather_scratch(n),
        compiler_params=pltpu.CompilerParams(vmem_limit_bytes=VMEM_LIMIT_BYTES),
    )(*shards, *cast_only)


def _pair_exchange(name, grads):
    n = len(grads)

    def body(*refs):
        g_refs, land_refs = refs[:n], refs[n:2 * n]
        send_sems, recv_sems = refs[2 * n:]
        x, y, c = _place()
        copies = []
        for a in range(n):
            for q in range(4):
                cp = pltpu.make_async_remote_copy(
                    src_ref=g_refs[a].at[2 * q + (1 - c)], dst_ref=land_refs[a].at[q],
                    send_sem=send_sems.at[a, q], recv_sem=recv_sems.at[a, q],
                    device_id=(x, y, 1 - c), device_id_type=MESH)
                cp.start()
                copies.append(cp)
        for cp in copies:
            cp.wait()

    return pl.pallas_call(
        body, name=name,
        in_specs=[pl.BlockSpec(memory_space=pl.ANY)] * n,
        out_specs=[pl.BlockSpec(memory_space=pl.ANY)] * n,
        out_shape=[jax.ShapeDtypeStruct((4,) + g.shape[1:], g.dtype) for g in grads],
        scratch_shapes=[pltpu.SemaphoreType.DMA((n, 4)), pltpu.SemaphoreType.DMA((n, 4))],
    )(*grads)


def _chip_copies(p_refs, land_refs, send_sems, recv_sems):
    x, y, c = _place()
    chips = [(1 - x, y), (x, 1 - y), (1 - x, 1 - y)]
    return [pltpu.make_async_remote_copy(
        src_ref=p_refs[a].at[2 * px + py], dst_ref=land_refs[a].at[k],
        send_sem=send_sems.at[a, k], recv_sem=recv_sems.at[a, k],
        device_id=(px, py, c), device_id_type=MESH)
        for a in range(len(p_refs)) for k, (px, py) in enumerate(chips)]


def _chip_scratch(n):
    return [pltpu.SemaphoreType.DMA((n, 3)), pltpu.SemaphoreType.DMA((n, 3))]


def _chip_landing(pair_sums):
    return [jax.ShapeDtypeStruct((3,) + p.shape[1:], p.dtype) for p in pair_sums]


def _pair_sum(name, g_f32, landed, c_idx, q_me):
    _, rows, cols = g_f32.shape
    tr = min(rows, 256)

    def body(c_ref, q_ref, g_ref, l_ref, wire_ref, own_ref):
        q = pl.program_id(1)
        s = g_ref[...] + l_ref[...].astype(F32)
        wire_ref[...] = s.astype(BF16)

        @pl.when(q == q_ref[0])
        def _():
            own_ref[...] = s

    return pl.pallas_call(
        body, name=name,
        grid_spec=pltpu.PrefetchScalarGridSpec(
            num_scalar_prefetch=2, grid=(rows // tr, 4),
            in_specs=[pl.BlockSpec((None, tr, cols), lambda i, q, c_ref, q_ref: (2 * q + c_ref[0], i, 0)),
                      pl.BlockSpec((None, tr, cols), lambda i, q, c_ref, q_ref: (q, i, 0))],
            out_specs=[pl.BlockSpec((None, tr, cols), lambda i, q, c_ref, q_ref: (q, i, 0)),
                       pl.BlockSpec((tr, cols), lambda i, q, c_ref, q_ref: (i, 0))]),
        out_shape=[jax.ShapeDtypeStruct((4, rows, cols), BF16), jax.ShapeDtypeStruct((rows, cols), F32)],
        compiler_params=_params(("parallel", "arbitrary")),
    )(c_idx, q_me, g_f32, landed)


def _adamw(w, g, m, v):
    m = ADAM_B1 * m + (1.0 - ADAM_B1) * g
    v = ADAM_B2 * v + (1.0 - ADAM_B2) * (g * g)
    m_hat = m / (1.0 - ADAM_B1 ** ADAM_STEP)
    v_hat = v / (1.0 - ADAM_B2 ** ADAM_STEP)
    delta = -ADAM_LR * (m_hat / (jnp.sqrt(v_hat) + ADAM_EPS) + ADAM_WD * w)
    return delta, m, v


def _shard_update(name, own, landed, w, m, v):
    rows, cols = own.shape
    tr = min(rows, 256)

    def body(own_ref, l_ref, w_ref, m_ref, v_ref, g_out, d_out, m_out, v_out):
        g = own_ref[...] + l_ref[0].astype(F32) + l_ref[1].astype(F32) + l_ref[2].astype(F32)
        d, m_new, v_new = _adamw(w_ref[...], g, m_ref[...], v_ref[...])
        g_out[...] = g
        d_out[...] = d
        m_out[...] = m_new
        v_out[...] = v_new

    tile = pl.BlockSpec((tr, cols), lambda i: (i, 0))
    return pl.pallas_call(
        body, name=name, grid=(rows // tr,),
        in_specs=[tile, pl.BlockSpec((3, tr, cols), lambda i: (0, i, 0)), tile, tile, tile],
        out_specs=[tile] * 4,
        out_shape=[jax.ShapeDtypeStruct((rows, cols), F32)] * 4,
        compiler_params=_params(("parallel",)),
    )(own, landed, w, m, v)


def _small_update(parts, w, m, v):
    def body(p_ref, w_ref, m_ref, v_ref, g_out, d_out, m_out, v_out):
        g = p_ref[0]
        for d in range(1, N_DEV):
            g = g + p_ref[d]
        dl, m_new, v_new = _adamw(w_ref[...], g, m_ref[...], v_ref[...])
        g_out[...] = g
        d_out[...] = dl
        m_out[...] = m_new
        v_out[...] = v_new

    return pl.pallas_call(
        body, name="small_update",
        out_shape=[jax.ShapeDtypeStruct(w.shape, F32)] * 4,
        compiler_params=pltpu.CompilerParams(vmem_limit_bytes=VMEM_LIMIT_BYTES),
    )(parts, w, m, v)


CONV_CHUNK = 512


def _conv_taps(u_s, lo, chunk):
    ext = u_s[pl.ds(lo, chunk + 8), :]
    return ext[8:], pltpu.roll(ext, 1, 0)[8:], pltpu.roll(ext, 2, 0)[8:]


def _conv_fwd(gates, conv_w, g_conv, width):
    t_len = gates.shape[0]
    nblk = width // LANES
    n_chunks = t_len // CONV_CHUNK

    def body(bg_ref, cg_ref, h_ref, w_ref, g_ref, y_ref, u_s):
        lane_lo = lax.broadcasted_iota(jnp.int32, (CONV_CHUNK, LANES), 1) < HEAD_DIM
        u_s[0:8, :] = jnp.zeros((8, LANES), F32)

        def fill(r, _):
            lo = pl.multiple_of(r * CONV_CHUNK, CONV_CHUNK)
            u_s[pl.ds(pl.multiple_of(lo + 8, 8), CONV_CHUNK), :] = cg_ref[pl.ds(lo, CONV_CHUNK), :] * h_ref[pl.ds(lo, CONV_CHUNK), :]
            return 0

        lax.fori_loop(0, n_chunks, fill, 0)
        w = w_ref[...]
        gain = g_ref[...]

        def step(r, _):
            lo = pl.multiple_of(r * CONV_CHUNK, CONV_CHUNK)
            u0, u1, u2 = _conv_taps(u_s, lo, CONV_CHUNK)
            y = bg_ref[pl.ds(lo, CONV_CHUNK), :] * (w[2:3] * u0 + w[1:2] * u1 + w[0:1] * u2)
            rs = lax.rsqrt(_group_sums(y * y, lane_lo) * (1.0 / HEAD_DIM) + RMS_EPS)
            y_ref[pl.ds(lo, CONV_CHUNK), :] = (y * rs * gain).astype(BF16)
            return 0

        lax.fori_loop(0, n_chunks, step, 0)

    col = lambda off: pl.BlockSpec((t_len, LANES), lambda p: (0, off + p))
    return pl.pallas_call(
        body, name="conv_fwd", grid=(nblk,),
        in_specs=[col(0), col(nblk), col(2 * nblk),
                  pl.BlockSpec((3, LANES), lambda p: (0, p)), pl.BlockSpec((1, LANES), lambda p: (0, p))],
        out_specs=col(0),
        out_shape=jax.ShapeDtypeStruct((t_len, 2 * width), BF16),
        scratch_shapes=[pltpu.VMEM((t_len + 8, LANES), F32)],
        compiler_params=_params(("parallel",)),
    )(gates, gates, gates, conv_w, g_conv)


def _conv_bwd(gates, dymix, conv_w, g_conv, width):
    t_len = gates.shape[0]
    nblk = width // LANES
    n_chunks = t_len // CONV_CHUNK

    def body(bg_ref, cg_ref, h_ref, dy_ref, w_ref, g_ref, dbg_ref, dcg_ref, dh_ref, gg_ref, gw_ref, u_s, dc_s):
        lane_lo = lax.broadcasted_iota(jnp.int32, (CONV_CHUNK, LANES), 1) < HEAD_DIM
        u_s[0:8, :] = jnp.zeros((8, LANES), F32)
        dc_s[t_len:t_len + 8, :] = jnp.zeros((8, LANES), F32)

        def fill(r, _):
            lo = pl.multiple_of(r * CONV_CHUNK, CONV_CHUNK)
            u_s[pl.ds(pl.multiple_of(lo + 8, 8), CONV_CHUNK), :] = cg_ref[pl.ds(lo, CONV_CHUNK), :] * h_ref[pl.ds(lo, CONV_CHUNK), :]
            return 0

        lax.fori_loop(0, n_chunks, fill, 0)
        w = w_ref[...]
        gain = g_ref[...]

        def step(r, acc):
            gg, gw0, gw1, gw2 = acc
            lo = pl.multiple_of(r * CONV_CHUNK, CONV_CHUNK)
            u0, u1, u2 = _conv_taps(u_s, lo, CONV_CHUNK)
            bg = bg_ref[pl.ds(lo, CONV_CHUNK), :]
            conv = w[2:3] * u0 + w[1:2] * u1 + w[0:1] * u2
            y = bg * conv
            rs = lax.rsqrt(_group_sums(y * y, lane_lo) * (1.0 / HEAD_DIM) + RMS_EPS)
            yn = y * rs
            dyn_raw = dy_ref[pl.ds(lo, CONV_CHUNK), :]
            dyn = dyn_raw * gain
            d_y = rs * (dyn - yn * (_group_sums(dyn * yn, lane_lo) * (1.0 / HEAD_DIM)))
            dbg_ref[pl.ds(lo, CONV_CHUNK), :] = (d_y * conv).astype(BF16)
            dconv = d_y * bg
            dc_s[pl.ds(lo, CONV_CHUNK), :] = dconv
            return (gg + _rows8(dyn_raw * yn), gw0 + _rows8(dconv * u2), gw1 + _rows8(dconv * u1),
                    gw2 + _rows8(dconv * u0))

        zero = jnp.zeros((8, LANES), F32)
        gg, gw0, gw1, gw2 = lax.fori_loop(0, n_chunks, step, (zero, zero, zero, zero))
        gg_ref[...] = jnp.broadcast_to(jnp.sum(gg, axis=0, keepdims=True), (8, LANES))
        row = lax.broadcasted_iota(jnp.int32, (8, LANES), 0)
        tot = lambda v: jnp.broadcast_to(jnp.sum(v, axis=0, keepdims=True), (8, LANES))
        gw_ref[...] = jnp.where(row == 0, tot(gw0), jnp.where(row == 1, tot(gw1), jnp.where(row == 2, tot(gw2), 0.0)))

        def back(r, _):
            lo = pl.multiple_of(r * CONV_CHUNK, CONV_CHUNK)
            ext = dc_s[pl.ds(lo, CONV_CHUNK + 8), :]
            d0 = ext[:CONV_CHUNK]
            d1 = pltpu.roll(ext, CONV_CHUNK + 7, 0)[:CONV_CHUNK]
            d2 = pltpu.roll(ext, CONV_CHUNK + 6, 0)[:CONV_CHUNK]
            du = w[2:3] * d0 + w[1:2] * d1 + w[0:1] * d2
            dcg_ref[pl.ds(lo, CONV_CHUNK), :] = (du * h_ref[pl.ds(lo, CONV_CHUNK), :]).astype(BF16)
            dh_ref[pl.ds(lo, CONV_CHUNK), :] = (du * cg_ref[pl.ds(lo, CONV_CHUNK), :]).astype(BF16)
            return 0

        lax.fori_loop(0, n_chunks, back, 0)

    col = lambda off: pl.BlockSpec((t_len, LANES), lambda p: (0, off + p))
    small = pl.BlockSpec((8, LANES), lambda p: (0, p))
    return pl.pallas_call(
        body, name="conv_bwd", grid=(nblk,),
        in_specs=[col(0), col(nblk), col(2 * nblk), col(0),
                  pl.BlockSpec((3, LANES), lambda p: (0, p)), pl.BlockSpec((1, LANES), lambda p: (0, p))],
        out_specs=[col(0), col(0), col(0), small, small],
        out_shape=[jax.ShapeDtypeStruct((t_len, width), BF16)] * 3 + [jax.ShapeDtypeStruct((8, width), F32)] * 2,
        scratch_shapes=[pltpu.VMEM((t_len + 8, LANES), F32), pltpu.VMEM((t_len + 8, LANES), F32)],
        compiler_params=_params(("parallel",)),
    )(gates, gates, gates, dymix, conv_w, g_conv)


def _cumsum_matrix(inclusive):
    j = lax.broadcasted_iota(jnp.int32, (2 * LANES, 2 * LANES), 0) % LANES
    s = lax.broadcasted_iota(jnp.int32, (2 * LANES, 2 * LANES), 1)
    keep = (s >= LANES) | ((j >= s) if inclusive else (j > s))
    return keep.astype(BF16)


def _suffix_sums(val, mat):
    hi = val.astype(BF16)
    lo = (val - hi.astype(F32)).astype(BF16)
    both = jnp.dot(jnp.concatenate([hi, lo], axis=1), mat, preferred_element_type=F32)
    return both[:, :LANES], both[:, LANES:]


KEY_CHUNK = 384
SUB_TILES = KEY_CHUNK // Q_BLOCK


def _chunked_suffix(val, carry, mat):
    parts = []
    for b in reversed(range(SUB_TILES)):
        within, total = _suffix_sums(val[:, b * Q_BLOCK:(b + 1) * Q_BLOCK], mat)
        parts.append(within + carry)
        carry = carry + total
    return jnp.concatenate(parts[::-1], axis=1), carry


def _stick_chunk(qm, kc, carry, umat, mask):
    z = lax.dot_general(qm, kc, (_NT, ((), ())), preferred_element_type=F32)
    e = jnp.exp(-jnp.abs(z))
    lk = -(jnp.maximum(z, 0.0) + jnp.log(1.0 + e))
    lb = z + lk
    if mask is not None:
        lk = jnp.where(mask, lk, 0.0)
    suffix, carry = _chunked_suffix(lk, carry, umat)
    a = jnp.exp(lb + suffix)
    if mask is not None:
        a = jnp.where(mask, a, 0.0)
    return z, e, a, carry


DEAD_BELOW = -105.0


def _sweep(i, step, state, carries_of):
    top = (i + 1) * Q_BLOCK
    n_chunks = (top + KEY_CHUNK - 1) // KEY_CHUNK
    col = lax.broadcasted_iota(jnp.int32, (Q_BLOCK, KEY_CHUNK), 1)
    q_row = i * Q_BLOCK + lax.broadcasted_iota(jnp.int32, (Q_BLOCK, KEY_CHUNK), 0)

    def body(loop):
        n, _, st = loop
        end = top - n * KEY_CHUNK
        start = pl.multiple_of(jnp.maximum(end - KEY_CHUNK, 0), Q_BLOCK)
        st = step(start, (col + start) < jnp.minimum(q_row, end), st)
        c0, c1 = carries_of(st)
        alive = (jnp.max(jnp.maximum(c0, c1)) >= DEAD_BELOW).astype(jnp.int32)
        return n + 1, alive, st

    return lax.while_loop(lambda loop: (loop[0] < n_chunks) & (loop[1] > 0), body,
                          (jnp.int32(0), jnp.int32(1), state))[2]


def _head_rows(block, lane_lo_rows):
    zero = jnp.zeros_like(block)
    return jnp.concatenate([jnp.where(lane_lo_rows, block, zero), jnp.where(lane_lo_rows, zero, block)], axis=0)


def _attn_fwd(qkv, g_attn, ymix_in, width, shards):
    t_len = qkv.shape[0]
    npair = width // LANES
    nqb = t_len // Q_BLOCK
    n_sh = len(shards)

    def body(q_ref, k_ref, v_ref, g_ref, umat_ref, ymix_in_ref, *rest):
        shard_refs, rest = rest[:n_sh], rest[n_sh:]
        o_ref, o_fine_ref, ymix_ref = rest[:3]
        gathered_refs, sems = rest[3:3 + n_sh], rest[3 + n_sh:]
        p, i = pl.program_id(0), pl.program_id(1)
        start, forward, finish = _gather_plan(shard_refs, gathered_refs, *sems)
        pl.when((p == 0) & (i == 0))(start)
        pl.when((p == npair - 1) & (i == 0))(forward)
        lane_lo = lax.broadcasted_iota(jnp.int32, (Q_BLOCK, LANES), 1) < HEAD_DIM
        lane_lo_keys = lax.broadcasted_iota(jnp.int32, (KEY_CHUNK, LANES), 1) < HEAD_DIM
        umat = umat_ref[...]
        q = q_ref[...] * jnp.asarray(HEAD_DIM ** -0.5, BF16)
        q_heads = (jnp.where(lane_lo, q, jnp.zeros_like(q)), jnp.where(lane_lo, jnp.zeros_like(q), q))

        def step(rows, mask, state):
            carries, o_acc, o_rest = state
            kc = k_ref[pl.ds(rows, KEY_CHUNK), :]
            v_heads = _head_rows(v_ref[pl.ds(rows, KEY_CHUNK), :], lane_lo_keys)
            new_carries, his, los = [], [], []
            for h in range(2):
                _, _, a, carry = _stick_chunk(q_heads[h], kc, carries[h], umat, mask)
                a_hi = a.astype(BF16)
                his.append(a_hi)
                los.append((a - a_hi.astype(F32)).astype(BF16))
                new_carries.append(carry)
            o_acc = o_acc + jnp.dot(jnp.concatenate(his, axis=1), v_heads, preferred_element_type=F32)
            o_rest = o_rest + jnp.dot(jnp.concatenate(los, axis=1), v_heads, preferred_element_type=F32)
            return tuple(new_carries), o_acc, o_rest

        zero = jnp.zeros((Q_BLOCK, LANES), F32)
        _, o, o_rest = _sweep(i, step, ((zero, zero), zero, zero), lambda st: st[0])
        o_ref[...] = o
        o_fine_ref[...] = o + o_rest
        rs = lax.rsqrt(_group_sums(o * o, lane_lo) * (1.0 / HEAD_DIM) + RMS_EPS)
        ymix_ref[...] = (o * rs * g_ref[...]).astype(BF16)
        pl.when((p == npair - 1) & (i == nqb - 1))(finish)

    hbm = pl.BlockSpec(memory_space=pl.ANY)
    outs = pl.pallas_call(
        body, name="attn_fwd", grid=(npair, nqb),
        in_specs=[pl.BlockSpec((Q_BLOCK, LANES), lambda p, i: (i, p)),
                  pl.BlockSpec((t_len, LANES), lambda p, i: (0, npair + p)),
                  pl.BlockSpec((t_len, LANES), lambda p, i: (0, 2 * npair + p)),
                  pl.BlockSpec((1, LANES), lambda p, i: (0, p)),
                  pl.BlockSpec((2 * LANES, 2 * LANES), lambda p, i: (0, 0)),
                  hbm] + [hbm] * n_sh,
        out_specs=[pl.BlockSpec((Q_BLOCK, LANES), lambda p, i: (i, p)),
                   pl.BlockSpec((Q_BLOCK, LANES), lambda p, i: (i, p)),
                   pl.BlockSpec((Q_BLOCK, LANES), lambda p, i: (i, npair + p))] + [hbm] * n_sh,
        out_shape=[jax.ShapeDtypeStruct((t_len, width), F32), jax.ShapeDtypeStruct((t_len, width), F32),
                   jax.ShapeDtypeStruct(ymix_in.shape, BF16)]
        + [jax.ShapeDtypeStruct((N_DEV,) + s.shape, s.dtype) for s in shards],
        input_output_aliases={5: 2},
        scratch_shapes=_gather_scratch(n_sh),
        compiler_params=_params(("arbitrary", "arbitrary")),
    )(qkv, qkv, qkv, g_attn, _cumsum_matrix(False), ymix_in, *shards)
    return outs[0], outs[1], outs[2], outs[3:]


def _attn_bwd(qkv, o, o_fine, dymix, g_attn, width, exchange):
    t_len = qkv.shape[0]
    npair = width // LANES
    nqb = t_len // Q_BLOCK
    scale = HEAD_DIM ** -0.5
    n_ex = len(exchange)

    def body(q_ref, k_ref, v_ref, o_ref, o_fine_ref, dy_ref, g_ref, umat_ref, gmat_ref, *rest):
        p_refs, rest = rest[:n_ex], rest[n_ex:]
        dq_ref, dk_ref, dv_ref, gg_ref = rest[:4]
        land_refs, rest = rest[4:4 + n_ex], rest[4 + n_ex:]
        dk_s, dv_s = rest[:2]
        copies = _chip_copies(p_refs, land_refs, *rest[2:])
        p, i = pl.program_id(0), pl.program_id(1)

        @pl.when((p == 0) & (i == 0))
        def _():
            for cp in copies:
                cp.start()

        lane_lo = lax.broadcasted_iota(jnp.int32, (Q_BLOCK, LANES), 1) < HEAD_DIM
        lane_lo_keys = lax.broadcasted_iota(jnp.int32, (KEY_CHUNK, LANES), 1) < HEAD_DIM
        umat, gmat = umat_ref[...], gmat_ref[...]

        @pl.when(i == 0)
        def _():
            dk_s[...] = jnp.zeros_like(dk_s)
            dv_s[...] = jnp.zeros_like(dv_s)

        o_blk = o_ref[...]
        rs = lax.rsqrt(_group_sums(o_blk * o_blk, lane_lo) * (1.0 / HEAD_DIM) + RMS_EPS)
        yn = o_blk * rs
        dy_raw = dy_ref[...]
        dyn = dy_raw * g_ref[...]
        d_o = rs * (dyn - yn * (_group_sums(dyn * yn, lane_lo) * (1.0 / HEAD_DIM)))
        _accumulate(gg_ref, i, nqb, _rows8(dy_raw * yn))
        do_bf = d_o.astype(BF16)
        do_o = do_bf.astype(F32) * o_fine_ref[...]
        totals = (jnp.sum(jnp.where(lane_lo, do_o, 0.0), axis=-1, keepdims=True),
                  jnp.sum(jnp.where(lane_lo, 0.0, do_o), axis=-1, keepdims=True))
        q = q_ref[...]
        q_rows = _head_rows(q, lane_lo)
        do_rows = _head_rows(do_bf, lane_lo)
        q_scaled = q_rows * jnp.asarray(scale, BF16)

        def step(rows, mask, state):
            carries, carries_g, dq_acc = state
            kc = k_ref[pl.ds(rows, KEY_CHUNK), :]
            vc = v_ref[pl.ds(rows, KEY_CHUNK), :]
            new_carries, new_carries_g, dzs, a_bf = [], [], [], []
            for h in range(2):
                head = slice(h * Q_BLOCK, (h + 1) * Q_BLOCK)
                z, e, a, carry = _stick_chunk(q_scaled[head], kc, carries[h], umat, mask)
                da = lax.dot_general(do_rows[head], vc, (_NT, ((), ())), preferred_element_type=F32)
                g = a * da
                suffix_g, carry_g = _chunked_suffix(g, carries_g[h], gmat)
                before = totals[h] - suffix_g
                inv = 1.0 / (1.0 + e)
                sig = jnp.where(z >= 0.0, inv, e * inv)
                dz = g * (1.0 - sig) - before * sig
                if mask is not None:
                    dz = jnp.where(mask, dz, 0.0)
                dzs.append((dz * scale).astype(BF16))
                a_bf.append(a.astype(BF16))
                new_carries.append(carry)
                new_carries_g.append(carry_g)
            dq_acc = dq_acc + jnp.dot(jnp.concatenate(dzs, axis=1), _head_rows(kc, lane_lo_keys),
                                      preferred_element_type=F32)
            dk_s[pl.ds(rows, KEY_CHUNK), :] += lax.dot_general(
                jnp.concatenate(dzs, axis=0), q_rows, (_TN, ((), ())), preferred_element_type=F32)
            dv_s[pl.ds(rows, KEY_CHUNK), :] += lax.dot_general(
                jnp.concatenate(a_bf, axis=0), do_rows, (_TN, ((), ())), preferred_element_type=F32)
            return tuple(new_carries), tuple(new_carries_g), dq_acc

        zero = jnp.zeros((Q_BLOCK, LANES), F32)
        state = _sweep(i, step, ((zero, zero), (zero, zero), zero), lambda st: st[0])
        dq_ref[...] = state[2].astype(BF16)

        @pl.when(i == nqb - 1)
        def _():
            dk_ref[...] = dk_s[...].astype(BF16)
            dv_ref[...] = dv_s[...].astype(BF16)

        @pl.when((p == npair - 1) & (i == nqb - 1))
        def _():
            for cp in copies:
                cp.wait()

    whole = pl.BlockSpec((t_len, LANES), lambda p, i: (0, p))
    hbm = pl.BlockSpec(memory_space=pl.ANY)
    outs = pl.pallas_call(
        body, name="attn_bwd", grid=(npair, nqb),
        in_specs=[pl.BlockSpec((Q_BLOCK, LANES), lambda p, i: (i, p)),
                  pl.BlockSpec((t_len, LANES), lambda p, i: (0, npair + p)),
                  pl.BlockSpec((t_len, LANES), lambda p, i: (0, 2 * npair + p)),
                  pl.BlockSpec((Q_BLOCK, LANES), lambda p, i: (i, p)),
                  pl.BlockSpec((Q_BLOCK, LANES), lambda p, i: (i, p)),
                  pl.BlockSpec((Q_BLOCK, LANES), lambda p, i: (i, npair + p)),
                  pl.BlockSpec((1, LANES), lambda p, i: (0, p)),
                  pl.BlockSpec((2 * LANES, 2 * LANES), lambda p, i: (0, 0)),
                  pl.BlockSpec((2 * LANES, 2 * LANES), lambda p, i: (0, 0))] + [hbm] * n_ex,
        out_specs=[pl.BlockSpec((Q_BLOCK, LANES), lambda p, i: (i, p)), whole, whole,
                   pl.BlockSpec((8, LANES), lambda p, i: (0, p))] + [hbm] * n_ex,
        out_shape=[jax.ShapeDtypeStruct((t_len, width), BF16)] * 3 + [jax.ShapeDtypeStruct((8, width), F32)]
        + _chip_landing(exchange),
        scratch_shapes=[pltpu.VMEM((t_len, LANES), F32), pltpu.VMEM((t_len, LANES), F32)] + _chip_scratch(n_ex),
        compiler_params=_params(("arbitrary", "arbitrary")),
    )(qkv, qkv, qkv, o, o_fine, dymix, g_attn, _cumsum_matrix(False), _cumsum_matrix(True), *exchange)
    return outs[0], outs[1], outs[2], outs[3], outs[4:]


def kernel(x, w_in, conv_w, g_conv, g_attn, w_out, ln1_g, ln1_b, w_up, w_down, ln2_g, ln2_b, loss_target, m_w_in, m_conv_w, m_g_conv, m_g_attn, m_w_out, m_ln1_g, m_ln1_b, m_w_up, m_w_down, m_ln2_g, m_ln2_b, v_w_in, v_conv_w, v_g_conv, v_g_attn, v_w_out, v_ln1_g, v_ln1_b, v_w_up, v_w_down, v_ln2_g, v_ln2_b):
    x2, target = x[0], loss_target[0]
    t_len, d_model = x2.shape
    width = g_conv.shape[1]
    in_shard = w_in.shape[2]
    up_shard = w_up.shape[2]
    d_ff = up_shard * N_DEV
    cw = conv_w.shape[2]
    xi, yi, ci = _place()
    dev = 4 * xi + 2 * yi + ci
    tm = 256

    win_g, wout_bf, wup_bf, wdown_bf = _all_gather(
        "gather_w_in", [w_in[0]], BF16, cast_only=[w_out[0], w_up[0], w_down[0]])
    (convw_g,) = _all_gather("gather_conv_w", [jnp.pad(conv_w[0], ((0, 5), (0, LANES - cw)))], F32)
    conv_full = jnp.transpose(convw_g[:, :3, :cw], (1, 0, 2)).reshape(3, width)

    n_row_tiles = t_len // tm
    rows = lambda cols: pl.BlockSpec((tm, cols), lambda i: (i, 0))
    resident = lambda arr: pl.BlockSpec(arr.shape, lambda i: (0,) * arr.ndim)
    vec = pl.BlockSpec((1, d_model), lambda i: (0, 0))
    acc8 = pl.BlockSpec((8, d_model), lambda i: (0, 0))
    tokens = lambda cols, dtype: jax.ShapeDtypeStruct((t_len, cols), dtype)

    def proj_in_epilogue(acc, extra, outs, i):
        outs[0][...] = acc[:, :3 * width]
        outs[1][...] = acc[:, 3 * width:].astype(BF16)
        outs[2][...] = extra[0][...].astype(BF16)

    gates, qkv, x_bf = _matmul(
        "proj_in", x2, win_g, rows(d_model), resident(win_g), _shards_wide, n_row_tiles,
        [(tokens(3 * width, F32), rows(3 * width)), (tokens(3 * width, BF16), rows(3 * width)),
         (tokens(d_model, BF16), rows(d_model))],
        proj_in_epilogue, extras=[(x2, rows(d_model))])

    ymix = _conv_fwd(gates, conv_full, g_conv, width)
    o_attn, o_fine, ymix, (wout_g, wup_g, wdown_g) = _attn_fwd(qkv, g_attn, ymix, width, [wout_bf, wup_bf, wdown_bf])
    wout_full = wout_g.reshape(d_model, d_model)
    wdown_full = wdown_g.reshape(d_ff, d_model)

    def out_ln1_epilogue(acc, extra, outs, i):
        x_ref, g_ref, b_ref = extra
        r1 = ALPHA * x_ref[...] + acc
        x1, _, _ = _ln_fwd(r1, g_ref[...], b_ref[...])
        outs[0][...] = r1
        outs[1][...] = x1
        outs[2][...] = x1.astype(BF16)

    r1, x1, x1_bf = _matmul(
        "proj_out_ln1", ymix, wout_full, rows(d_model), resident(wout_full), _whole(_NN), n_row_tiles,
        [(tokens(d_model, F32), rows(d_model)), (tokens(d_model, F32), rows(d_model)),
         (tokens(d_model, BF16), rows(d_model))],
        out_ln1_epilogue, extras=[(x2, rows(d_model)), (ln1_g, vec), (ln1_b, vec)])

    def up_epilogue(acc, extra, outs, i):
        outs[0][...] = acc
        relu = jnp.maximum(acc, 0.0)
        outs[1][...] = (relu * relu).astype(BF16)

    up, hid = _matmul(
        "ffn_up", x1_bf, wup_g, rows(d_model), resident(wup_g), _shards_wide, n_row_tiles,
        [(tokens(d_ff, F32), rows(d_ff)), (tokens(d_ff, BF16), rows(d_ff))], up_epilogue)

    def down_ln2_epilogue(acc, extra, outs, i):
        x1_ref, t_ref, g_ref, b_ref = extra
        dr2_ref, dr2_bf_ref, loss_ref, gg_ref, gb_ref = outs
        gain = g_ref[...]
        r2 = ALPHA * x1_ref[...] + acc
        y, xhat, rstd = _ln_fwd(r2, gain, b_ref[...])
        diff = y - t_ref[...]
        d_y = diff * (1.0 / d_model)
        dr2 = _ln_bwd(d_y, xhat, rstd, gain)
        dr2_ref[...] = dr2
        dr2_bf_ref[...] = dr2.astype(BF16)
        _accumulate(gg_ref, i, n_row_tiles, _rows8(d_y * xhat))
        _accumulate(gb_ref, i, n_row_tiles, _rows8(d_y))
        _accumulate(loss_ref, i, n_row_tiles, _rows8(diff * diff), all_lanes=True)

    small_acc = (jax.ShapeDtypeStruct((8, d_model), F32), acc8)
    dr2, dr2_bf, loss_cols, g_ln2_g, g_ln2_b = _matmul(
        "ffn_down_ln2_loss", hid, wdown_full, rows(d_ff), resident(wdown_full), _whole(_NN), n_row_tiles,
        [(tokens(d_model, F32), rows(d_model)), (tokens(d_model, BF16), rows(d_model)),
         small_acc, small_acc, small_acc],
        down_ln2_epilogue, extras=[(x1, rows(d_model)), (target, rows(d_model)), (ln2_g, vec), (ln2_b, vec)])

    def dup_epilogue(acc, extra, outs, i):
        outs[0][...] = (acc * (2.0 * jnp.maximum(extra[0][...], 0.0))).astype(BF16)

    (dup,) = _matmul(
        "ffn_dhid", dr2_bf, wdown_full, rows(d_model), resident(wdown_full), _whole(_NT), n_row_tiles,
        [(tokens(d_ff, BF16), rows(d_ff))], dup_epilogue, extras=[(up, rows(d_ff))])

    def weight_grad(name, acts, cots, shard_shape, acts_by_shard):
        sr, sc = shard_shape
        per = max(1, 256 // sr) if acts_by_shard else 1
        tall = lambda cols: pl.BlockSpec((t_len, cols), lambda i: (0, i))
        if acts_by_shard:
            a_spec, b_spec = tall(per * sr), resident(cots)
        else:
            a_spec, b_spec = resident(acts), tall(sc)
        out_spec = pl.BlockSpec((per, sr, sc), lambda i: (i, 0, 0))

        def epilogue(acc, extra, outs, i):
            acc = acc.reshape(per, sr, sc)
            outs[0][...] = acc
            outs[1][...] = acc.astype(BF16)

        shape = (N_DEV, sr, sc)
        return _matmul(name, acts, cots, a_spec, b_spec, _whole(_TN), N_DEV // per,
                       [(jax.ShapeDtypeStruct(shape, F32), out_spec), (jax.ShapeDtypeStruct(shape, BF16), out_spec)],
                       epilogue)

    gw_down, gw_down_bf = weight_grad("grad_w_down", hid, dr2_bf, (d_ff // N_DEV, d_model), True)
    gw_up, gw_up_bf = weight_grad("grad_w_up", x1_bf, dup, (d_model, up_shard), False)

    def dx1_ln1_epilogue(acc, extra, outs, i):
        dr2_ref, r1_ref, g_ref, b_ref = extra
        dr1_ref, dr1_bf_ref, gg_ref, gb_ref = outs
        gain = g_ref[...]
        dx1 = ALPHA * dr2_ref[...] + acc
        _, xhat, rstd = _ln_fwd(r1_ref[...], gain, b_ref[...])
        dr1 = _ln_bwd(dx1, xhat, rstd, gain)
        dr1_ref[...] = dr1
        dr1_bf_ref[...] = dr1.astype(BF16)
        _accumulate(gg_ref, i, n_row_tiles, _rows8(dx1 * xhat))
        _accumulate(gb_ref, i, n_row_tiles, _rows8(dx1))

    dr1, dr1_bf, g_ln1_g, g_ln1_b = _matmul(
        "ffn_dx1_ln1", dup, wup_g, rows(d_ff), resident(wup_g), _shards_deep, n_row_tiles,
        [(tokens(d_model, F32), rows(d_model)), (tokens(d_model, BF16), rows(d_model)), small_acc, small_acc],
        dx1_ln1_epilogue, extras=[(dr2, rows(d_model)), (r1, rows(d_model)), (ln1_g, vec), (ln1_b, vec)])

    gw_out, gw_out_bf = weight_grad("grad_w_out", ymix, dr1_bf, (d_model // N_DEV, d_model), True)

    c_idx = jnp.reshape(ci, (1,)).astype(jnp.int32)
    q_me = jnp.reshape(2 * xi + yi, (1,)).astype(jnp.int32)

    def pair_stage(tag, names, grads_f32, grads_bf):
        landed = _pair_exchange("reduce_pair_" + tag, grads_bf)
        sums = [_pair_sum("pair_sum_" + nm, g, land, c_idx, q_me) for nm, g, land in zip(names, grads_f32, landed)]
        return [s[0] for s in sums], [s[1] for s in sums]

    wire_ffn, own_ffn = pair_stage("ffn", ["w_out", "w_up", "w_down"], [gw_out, gw_up, gw_down],
                                   [gw_out_bf, gw_up_bf, gw_down_bf])

    def plain_epilogue(acc, extra, outs, i):
        outs[0][...] = acc

    (dymix,) = _matmul(
        "proj_out_bwd", dr1_bf, wout_full, rows(d_model), resident(wout_full), _whole(_NT), n_row_tiles,
        [(tokens(d_model, F32), rows(d_model))], plain_epilogue)

    d_bg, d_cg, d_h, g_gconv, g_convw = _conv_bwd(gates, dymix, conv_full, g_conv, width)
    d_q, d_k, d_v, g_gattn, landed_ffn = _attn_bwd(qkv, o_attn, o_fine, dymix, g_attn, width, wire_ffn)
    dproj = jnp.concatenate([d_bg, d_cg, d_h, d_q, d_k, d_v], axis=1)

    gw_in, gw_in_bf = weight_grad("grad_w_in", x_bf, dproj, (d_model, in_shard), False)
    wire_in, own_in = pair_stage("in", ["w_in"], [gw_in], [gw_in_bf])

    def dx_epilogue(acc, extra, outs, i):
        outs[0][...] = ALPHA * extra[0][...] + acc

    grad_x, landed_in = _matmul(
        "proj_in_bwd", dproj, win_g, rows(6 * width), resident(win_g), _shards_deep, n_row_tiles,
        [(tokens(d_model, F32), rows(d_model))], dx_epilogue, extras=[(dr1, rows(d_model))], exchange=wire_in)

    names = ["w_in", "w_out", "w_up", "w_down"]
    big = {}
    for nm, ow, land, w, m, v in zip(names, own_in + own_ffn, [landed_in] + list(landed_ffn),
                                     (w_in, w_out, w_up, w_down),
                                     (m_w_in, m_w_out, m_w_up, m_w_down), (v_w_in, v_w_out, v_w_up, v_w_down)):
        big[nm] = [r[None] for r in _shard_update("update_" + nm, ow, land, w[0], m[0], v[0])]

    def pack(l1g, l1b, l2g, l2b, gc, ga, taps):
        return jnp.concatenate([l1g, l1b, l2g, l2b, jnp.concatenate([gc, ga], axis=1),
                                jnp.concatenate([taps, jnp.zeros((3, d_model - width), F32)], axis=1)], axis=0)

    local = pack(g_ln1_g[:1], g_ln1_b[:1], g_ln2_g[:1], g_ln2_b[:1], g_gconv[:1], g_gattn[:1], g_convw[:3])
    (parts,) = _all_gather("gather_small_grads", [local], F32)

    def spread(a):
        return lax.dynamic_update_slice(jnp.zeros((3, width), F32), a, (0, dev * cw))

    w_small = pack(ln1_g, ln1_b, ln2_g, ln2_b, g_conv, g_attn, spread(conv_w[0]))
    m_small = pack(m_ln1_g, m_ln1_b, m_ln2_g, m_ln2_b, m_g_conv, m_g_attn, spread(m_conv_w[0]))
    v_small = pack(v_ln1_g, v_ln1_b, v_ln2_g, v_ln2_b, v_g_conv, v_g_attn, spread(v_conv_w[0]))
    small = _small_update(parts, w_small, m_small, v_small)

    def unpack(a):
        taps = lax.dynamic_slice(a[5:8, :width], (0, dev * cw), (3, cw))[None]
        return {"ln1_g": a[0:1], "ln1_b": a[1:2], "ln2_g": a[2:3], "ln2_b": a[3:4],
                "g_conv": a[4:5, :width], "g_attn": a[4:5, width:], "conv_w": taps}

    small = [unpack(a) for a in small]
    loss = lax.psum((0.5 / d_model) * loss_cols[0, 0], ("x", "y", "c"))

    order = ["w_in", "conv_w", "g_conv", "g_attn", "w_out", "ln1_g", "ln1_b", "w_up", "w_down", "ln2_g", "ln2_b"]
    result = [loss, grad_x[None]]
    for kind in range(4):
        for nm in order:
            result.append(big[nm][kind] if nm in big else small[kind][nm])
    return tuple(result)
```

```python
import functools

import jax
import jax.numpy as jnp
from jax import lax
from jax.experimental import pallas as pl
from jax.experimental.pallas import tpu as pltpu

F32 = jnp.float32
BF16 = jnp.bfloat16
MESH = pl.DeviceIdType.MESH

N_DEV = 8
HEAD_DIM = 64
LANES = 128
Q_BLOCK = 128
ALPHA = 2.0 ** 0.25
LN_EPS = 1e-5
RMS_EPS = 1e-6
ADAM_LR, ADAM_B1, ADAM_B2, ADAM_EPS, ADAM_WD, ADAM_STEP = 0.001, 0.9, 0.999, 1e-08, 0.01, 10
VMEM_LIMIT_BYTES = 48 * 1024 * 1024

_NT = ((1,), (1,))
_NN = ((1,), (0,))
_TN = ((0,), (0,))


def _params(sem=None):
    return pltpu.CompilerParams(dimension_semantics=sem, vmem_limit_bytes=VMEM_LIMIT_BYTES)


def _dot(a, b, contract):
    return lax.dot_general(a.astype(BF16), b.astype(BF16), (contract, ((), ())), preferred_element_type=F32)


def _whole(contract):
    return lambda a_ref, b_ref: _dot(a_ref[...], b_ref[...], contract)


def _shards_wide(a_ref, b_ref):
    a = a_ref[...].astype(BF16)
    return jnp.concatenate([_dot(a, b_ref[s], _NN) for s in range(b_ref.shape[0])], axis=1)


def _shards_deep(a_ref, b_ref):
    n = b_ref.shape[2]
    acc = _dot(a_ref[:, 0:n], b_ref[0], _NT)
    for s in range(1, b_ref.shape[0]):
        acc = acc + _dot(a_ref[:, s * n:(s + 1) * n], b_ref[s], _NT)
    return acc


def _matmul(name, a, b, a_spec, b_spec, product, n_steps, outs, epilogue, extras=(), exchange=()):
    n_extra, n_out, n_ex = len(extras), len(outs), len(exchange)

    def body(a_ref, b_ref, *rest):
        i = pl.program_id(0)
        extra_refs, rest = rest[:n_extra], rest[n_extra:]
        p_refs, rest = rest[:n_ex], rest[n_ex:]
        out_refs, rest = rest[:n_out], rest[n_out:]
        land_refs, sems = rest[:n_ex], rest[n_ex:]
        copies = _chip_copies(p_refs, land_refs, *sems) if n_ex else []

        @pl.when(i == 0)
        def _():
            for cp in copies:
                cp.start()

        epilogue(product(a_ref, b_ref), extra_refs, out_refs, i)

        @pl.when(i == n_steps - 1)
        def _():
            for cp in copies:
                cp.wait()

    hbm = pl.BlockSpec(memory_space=pl.ANY)
    return pl.pallas_call(
        body, name=name, grid=(n_steps,),
        in_specs=[a_spec, b_spec] + [s for _, s in extras] + [hbm] * n_ex,
        out_specs=[s for _, s in outs] + [hbm] * n_ex,
        out_shape=[o for o, _ in outs] + _chip_landing(exchange),
        scratch_shapes=_chip_scratch(n_ex) if n_ex else [],
        compiler_params=_params(("arbitrary",)),
    )(a, b, *[e for e, _ in extras], *exchange)


def _ln_fwd(r, g, b):
    mu = jnp.mean(r, axis=-1, keepdims=True)
    xc = r - mu
    rstd = lax.rsqrt(jnp.mean(xc * xc, axis=-1, keepdims=True) + LN_EPS)
    xhat = xc * rstd
    return xhat * g + b, xhat, rstd


def _ln_bwd(dy, xhat, rstd, g):
    dxh = dy * g
    m1 = jnp.mean(dxh, axis=-1, keepdims=True)
    m2 = jnp.mean(dxh * xhat, axis=-1, keepdims=True)
    return rstd * (dxh - m1 - xhat * m2)


def _rows8(v):
    n, c = v.shape
    return jnp.sum(v.reshape(n // 8, 8, c), axis=0)


def _accumulate(ref, i, n_steps, part8, all_lanes=False):
    @pl.when(i == 0)
    def _():
        ref[...] = part8

    @pl.when(i > 0)
    def _():
        ref[...] += part8

    @pl.when(i == n_steps - 1)
    def _():
        tot = jnp.sum(ref[...], axis=0, keepdims=True)
        if all_lanes:
            tot = jnp.sum(tot, axis=1, keepdims=True)
        ref[...] = jnp.broadcast_to(tot, ref.shape)


def _group_sums(v, lane_lo):
    s0 = jnp.sum(jnp.where(lane_lo, v, 0.0), axis=-1, keepdims=True)
    s1 = jnp.sum(jnp.where(lane_lo, 0.0, v), axis=-1, keepdims=True)
    return jnp.where(lane_lo, s0, s1)


def _place():
    return lax.axis_index("x"), lax.axis_index("y"), lax.axis_index("c")


def _gather_plan(src_refs, out_refs, send_sems, recv_sems, local_sems):
    n = len(src_refs)
    x, y, c = _place()
    me, sibling = (x, y, c), (x, y, 1 - c)
    chips = [(1 - x, y), (x, 1 - y), (1 - x, 1 - y)]

    def copy(a, k, block, to, src=None):
        slot = out_refs[a].at[4 * block[0] + 2 * block[1] + block[2]]
        return pltpu.make_async_remote_copy(
            src_ref=slot if src is None else src, dst_ref=slot,
            send_sem=send_sems.at[a, k], recv_sem=recv_sems.at[a, k],
            device_id=to, device_id_type=MESH)

    mine = [pltpu.make_async_copy(src_refs[a], out_refs[a].at[4 * x + 2 * y + c], local_sems.at[a]) for a in range(n)]
    first = [[copy(a, 0, me, sibling, src=src_refs[a])]
             + [copy(a, 1 + j, me, (*chip, c), src=src_refs[a]) for j, chip in enumerate(chips)] for a in range(n)]
    passed = [[copy(a, 4 + j, (*chip, c), sibling) for j, chip in enumerate(chips)] for a in range(n)]

    def start():
        for a in range(n):
            mine[a].start()
            for cp in first[a]:
                cp.start()

    def forward():
        for j, chip in enumerate(chips):
            for a in range(n):
                copy(a, 1 + j, (*chip, c), me).wait_recv()
                passed[a][j].start()

    def finish():
        for a in range(n):
            copy(a, 0, sibling, me).wait_recv()
            for j, chip in enumerate(chips):
                copy(a, 4 + j, (*chip, 1 - c), me).wait_recv()
        for a in range(n):
            for cp in first[a] + passed[a]:
                cp.wait_send()
            mine[a].wait()

    return start, forward, finish


def _gather_scratch(n):
    return [pltpu.SemaphoreType.DMA((n, 7)), pltpu.SemaphoreType.DMA((n, 7)), pltpu.SemaphoreType.DMA((n,))]


def _all_gather(name, shards, out_dtype, cast_only=()):
    n, m = len(shards), len(cast_only)

    def body(*refs):
        in_refs, cast_in = refs[:n], refs[n:n + m]
        out_refs, cast_out = refs[n + m:2 * n + m], refs[2 * n + m:2 * (n + m)]
        stage = refs[2 * (n + m):3 * n + 2 * m]
        for a in range(n):
            stage[a][...] = in_refs[a][...].astype(out_dtype)
        start, forward, finish = _gather_plan(stage, out_refs, *refs[3 * n + 2 * m:])
        start()
        for a in range(m):
            cast_out[a][...] = cast_in[a][...].astype(out_dtype)
        forward()
        finish()

    return pl.pallas_call(
        body, name=name,
        in_specs=[pl.BlockSpec(memory_space=pltpu.VMEM)] * (n + m),
        out_specs=[pl.BlockSpec(memory_space=pl.ANY)] * n + [pl.BlockSpec(memory_space=pltpu.VMEM)] * m,
        out_shape=[jax.ShapeDtypeStruct((N_DEV,) + s.shape, out_dtype) for s in shards]
        + [jax.ShapeDtypeStruct(s.shape, out_dtype) for s in cast_only],
        scratch_shapes=[pltpu.VMEM(s.shape, out_dtype) for s in shards] + _gather_scratch(n),
        compiler_params=pltpu.CompilerParams(vmem_limit_bytes=VMEM_LIMIT_BYTES),
    )(*shards, *cast_only)


def _pair_exchange(name, grads):
    n = len(grads)

    def body(*refs):
        g_refs, land_refs = refs[:n], refs[n:2 * n]
        send_sems, recv_sems = refs[2 * n:]
        x, y, c = _place()
        copies = []
        for a in range(n):
            for q in range(4):
                cp = pltpu.make_async_remote_copy(
                    src_ref=g_refs[a].at[2 * q + (1 - c)], dst_ref=land_refs[a].at[q],
                    send_sem=send_sems.at[a, q], recv_sem=recv_sems.at[a, q],
                    device_id=(x, y, 1 - c), device_id_type=MESH)
                cp.start()
                copies.append(cp)
        for cp in copies:
            cp.wait()

    return pl.pallas_call(
        body, name=name,
        in_specs=[pl.BlockSpec(memory_space=pl.ANY)] * n,
        out_specs=[pl.BlockSpec(memory_space=pl.ANY)] * n,
        out_shape=[jax.ShapeDtypeStruct((4,) + g.shape[1:], g.dtype) for g in grads],
        scratch_shapes=[pltpu.SemaphoreType.DMA((n, 4)), pltpu.SemaphoreType.DMA((n, 4))],
    )(*grads)


def _chip_copies(p_refs, land_refs, send_sems, recv_sems):
    x, y, c = _place()
    chips = [(1 - x, y), (x, 1 - y), (1 - x, 1 - y)]
    return [pltpu.make_async_remote_copy(
        src_ref=p_refs[a].at[2 * px + py], dst_ref=land_refs[a].at[k],
        send_sem=send_sems.at[a, k], recv_sem=recv_sems.at[a, k],
        device_id=(px, py, c), device_id_type=MESH)
        for a in range(len(p_refs)) for k, (px, py) in enumerate(chips)]


def _chip_scratch(n):
    return [pltpu.SemaphoreType.DMA((n, 3)), pltpu.SemaphoreType.DMA((n, 3))]


def _chip_landing(pair_sums):
    return [jax.ShapeDtypeStruct((3,) + p.shape[1:], p.dtype) for p in pair_sums]


def _pair_sum(name, g_f32, landed, c_idx, q_me):
    _, rows, cols = g_f32.shape
    tr = min(rows, 256)

    def body(c_ref, q_ref, g_ref, l_ref, wire_ref, own_ref):
        q = pl.program_id(1)
        s = g_ref[...] + l_ref[...].astype(F32)
        wire_ref[...] = s.astype(BF16)

        @pl.when(q == q_ref[0])
        def _():
            own_ref[...] = s

    return pl.pallas_call(
        body, name=name,
        grid_spec=pltpu.PrefetchScalarGridSpec(
            num_scalar_prefetch=2, grid=(rows // tr, 4),
            in_specs=[pl.BlockSpec((None, tr, cols), lambda i, q, c_ref, q_ref: (2 * q + c_ref[0], i, 0)),
                      pl.BlockSpec((None, tr, cols), lambda i, q, c_ref, q_ref: (q, i, 0))],
            out_specs=[pl.BlockSpec((None, tr, cols), lambda i, q, c_ref, q_ref: (q, i, 0)),
                       pl.BlockSpec((tr, cols), lambda i, q, c_ref, q_ref: (i, 0))]),
        out_shape=[jax.ShapeDtypeStruct((4, rows, cols), BF16), jax.ShapeDtypeStruct((rows, cols), F32)],
        compiler_params=_params(("parallel", "arbitrary")),
    )(c_idx, q_me, g_f32, landed)


def _adamw(w, g, m, v):
    m = ADAM_B1 * m + (1.0 - ADAM_B1) * g
    v = ADAM_B2 * v + (1.0 - ADAM_B2) * (g * g)
    m_hat = m / (1.0 - ADAM_B1 ** ADAM_STEP)
    v_hat = v / (1.0 - ADAM_B2 ** ADAM_STEP)
    delta = -ADAM_LR * (m_hat / (jnp.sqrt(v_hat) + ADAM_EPS) + ADAM_WD * w)
    return delta, m, v


def _shard_update(name, own, landed, w, m, v):
    rows, cols = own.shape
    tr = min(rows, 256)

    def body(own_ref, l_ref, w_ref, m_ref, v_ref, g_out, d_out, m_out, v_out):
        g = own_ref[...] + l_ref[0].astype(F32) + l_ref[1].astype(F32) + l_ref[2].astype(F32)
        d, m_new, v_new = _adamw(w_ref[...], g, m_ref[...], v_ref[...])
        g_out[...] = g
        d_out[...] = d
        m_out[...] = m_new
        v_out[...] = v_new

    tile = pl.BlockSpec((tr, cols), lambda i: (i, 0))
    return pl.pallas_call(
        body, name=name, grid=(rows // tr,),
        in_specs=[tile, pl.BlockSpec((3, tr, cols), lambda i: (0, i, 0)), tile, tile, tile],
        out_specs=[tile] * 4,
        out_shape=[jax.ShapeDtypeStruct((rows, cols), F32)] * 4,
        compiler_params=_params(("parallel",)),
    )(own, landed, w, m, v)


def _small_update(parts, w, m, v):
    def body(p_ref, w_ref, m_ref, v_ref, g_out, d_out, m_out, v_out):
        g = p_ref[0]
        for d in range(1, N_DEV):
            g = g + p_ref[d]
        dl, m_new, v_new = _adamw(w_ref[...], g, m_ref[...], v_ref[...])
        g_out[...] = g
        d_out[...] = dl
        m_out[...] = m_new
        v_out[...] = v_new

    return pl.pallas_call(
        body, name="small_update",
        out_shape=[jax.ShapeDtypeStruct(w.shape, F32)] * 4,
        compiler_params=pltpu.CompilerParams(vmem_limit_bytes=VMEM_LIMIT_BYTES),
    )(parts, w, m, v)


CONV_CHUNK = 512


def _conv_taps(u_s, lo, chunk):
    ext = u_s[pl.ds(lo, chunk + 8), :]
    return ext[8:], pltpu.roll(ext, 1, 0)[8:], pltpu.roll(ext, 2, 0)[8:]


def _conv_fwd(gates, conv_w, g_conv, width):
    t_len = gates.shape[0]
    nblk = width // LANES
    n_chunks = t_len // CONV_CHUNK

    def body(bg_ref, cg_ref, h_ref, w_ref, g_ref, y_ref, u_s):
        lane_lo = lax.broadcasted_iota(jnp.int32, (CONV_CHUNK, LANES), 1) < HEAD_DIM
        u_s[0:8, :] = jnp.zeros((8, LANES), F32)

        def fill(r, _):
            lo = pl.multiple_of(r * CONV_CHUNK, CONV_CHUNK)
            u_s[pl.ds(pl.multiple_of(lo + 8, 8), CONV_CHUNK), :] = cg_ref[pl.ds(lo, CONV_CHUNK), :] * h_ref[pl.ds(lo, CONV_CHUNK), :]
            return 0

        lax.fori_loop(0, n_chunks, fill, 0)
        w = w_ref[...]
        gain = g_ref[...]

        def step(r, _):
            lo = pl.multiple_of(r * CONV_CHUNK, CONV_CHUNK)
            u0, u1, u2 = _conv_taps(u_s, lo, CONV_CHUNK)
            y = bg_ref[pl.ds(lo, CONV_CHUNK), :] * (w[2:3] * u0 + w[1:2] * u1 + w[0:1] * u2)
            rs = lax.rsqrt(_group_sums(y * y, lane_lo) * (1.0 / HEAD_DIM) + RMS_EPS)
            y_ref[pl.ds(lo, CONV_CHUNK), :] = (y * rs * gain).astype(BF16)
            return 0

        lax.fori_loop(0, n_chunks, step, 0)

    col = lambda off: pl.BlockSpec((t_len, LANES), lambda p: (0, off + p))
    return pl.pallas_call(
        body, name="conv_fwd", grid=(nblk,),
        in_specs=[col(0), col(nblk), col(2 * nblk),
                  pl.BlockSpec((3, LANES), lambda p: (0, p)), pl.BlockSpec((1, LANES), lambda p: (0, p))],
        out_specs=col(0),
        out_shape=jax.ShapeDtypeStruct((t_len, 2 * width), BF16),
        scratch_shapes=[pltpu.VMEM((t_len + 8, LANES), F32)],
        compiler_params=_params(("parallel",)),
    )(gates, gates, gates, conv_w, g_conv)


def _conv_bwd(gates, dymix, conv_w, g_conv, width):
    t_len = gates.shape[0]
    nblk = width // LANES
    n_chunks = t_len // CONV_CHUNK

    def body(bg_ref, cg_ref, h_ref, dy_ref, w_ref, g_ref, dbg_ref, dcg_ref, dh_ref, gg_ref, gw_ref, u_s, dc_s):
        lane_lo = lax.broadcasted_iota(jnp.int32, (CONV_CHUNK, LANES), 1) < HEAD_DIM
        u_s[0:8, :] = jnp.zeros((8, LANES), F32)
        dc_s[t_len:t_len + 8, :] = jnp.zeros((8, LANES), F32)

        def fill(r, _):
            lo = pl.multiple_of(r * CONV_CHUNK, CONV_CHUNK)
            u_s[pl.ds(pl.multiple_of(lo + 8, 8), CONV_CHUNK), :] = cg_ref[pl.ds(lo, CONV_CHUNK), :] * h_ref[pl.ds(lo, CONV_CHUNK), :]
            return 0

        lax.fori_loop(0, n_chunks, fill, 0)
        w = w_ref[...]
        gain = g_ref[...]

        def step(r, acc):
            gg, gw0, gw1, gw2 = acc
            lo = pl.multiple_of(r * CONV_CHUNK, CONV_CHUNK)
            u0, u1, u2 = _conv_taps(u_s, lo, CONV_CHUNK)
            bg = bg_ref[pl.ds(lo, CONV_CHUNK), :]
            conv = w[2:3] * u0 + w[1:2] * u1 + w[0:1] * u2
            y = bg * conv
            rs = lax.rsqrt(_group_sums(y * y, lane_lo) * (1.0 / HEAD_DIM) + RMS_EPS)
            yn = y * rs
            dyn_raw = dy_ref[pl.ds(lo, CONV_CHUNK), :]
            dyn = dyn_raw * gain
            d_y = rs * (dyn - yn * (_group_sums(dyn * yn, lane_lo) * (1.0 / HEAD_DIM)))
            dbg_ref[pl.ds(lo, CONV_CHUNK), :] = (d_y * conv).astype(BF16)
            dconv = d_y * bg
            dc_s[pl.ds(lo, CONV_CHUNK), :] = dconv
            return (gg + _rows8(dyn_raw * yn), gw0 + _rows8(dconv * u2), gw1 + _rows8(dconv * u1),
                    gw2 + _rows8(dconv * u0))

        zero = jnp.zeros((8, LANES), F32)
        gg, gw0, gw1, gw2 = lax.fori_loop(0, n_chunks, step, (zero, zero, zero, zero))
        gg_ref[...] = jnp.broadcast_to(jnp.sum(gg, axis=0, keepdims=True), (8, LANES))
        row = lax.broadcasted_iota(jnp.int32, (8, LANES), 0)
        tot = lambda v: jnp.broadcast_to(jnp.sum(v, axis=0, keepdims=True), (8, LANES))
        gw_ref[...] = jnp.where(row == 0, tot(gw0), jnp.where(row == 1, tot(gw1), jnp.where(row == 2, tot(gw2), 0.0)))

        def back(r, _):
            lo = pl.multiple_of(r * CONV_CHUNK, CONV_CHUNK)
            ext = dc_s[pl.ds(lo, CONV_CHUNK + 8), :]
            d0 = ext[:CONV_CHUNK]
            d1 = pltpu.roll(ext, CONV_CHUNK + 7, 0)[:CONV_CHUNK]
            d2 = pltpu.roll(ext, CONV_CHUNK + 6, 0)[:CONV_CHUNK]
            du = w[2:3] * d0 + w[1:2] * d1 + w[0:1] * d2
            dcg_ref[pl.ds(lo, CONV_CHUNK), :] = (du * h_ref[pl.ds(lo, CONV_CHUNK), :]).astype(BF16)
            dh_ref[pl.ds(lo, CONV_CHUNK), :] = (du * cg_ref[pl.ds(lo, CONV_CHUNK), :]).astype(BF16)
            return 0

        lax.fori_loop(0, n_chunks, back, 0)

    col = lambda off: pl.BlockSpec((t_len, LANES), lambda p: (0, off + p))
    small = pl.BlockSpec((8, LANES), lambda p: (0, p))
    return pl.pallas_call(
        body, name="conv_bwd", grid=(nblk,),
        in_specs=[col(0), col(nblk), col(2 * nblk), col(0),
                  pl.BlockSpec((3, LANES), lambda p: (0, p)), pl.BlockSpec((1, LANES), lambda p: (0, p))],
        out_specs=[col(0), col(0), col(0), small, small],
        out_shape=[jax.ShapeDtypeStruct((t_len, width), BF16)] * 3 + [jax.ShapeDtypeStruct((8, width), F32)] * 2,
        scratch_shapes=[pltpu.VMEM((t_len + 8, LANES), F32), pltpu.VMEM((t_len + 8, LANES), F32)],
        compiler_params=_params(("parallel",)),
    )(gates, gates, gates, dymix, conv_w, g_conv)


def _cumsum_matrix(inclusive):
    j = lax.broadcasted_iota(jnp.int32, (2 * LANES, 2 * LANES), 0) % LANES
    s = lax.broadcasted_iota(jnp.int32, (2 * LANES, 2 * LANES), 1)
    keep = (s >= LANES) | ((j >= s) if inclusive else (j > s))
    return keep.astype(BF16)


def _suffix_sums(val, mat):
    hi = val.astype(BF16)
    lo = (val - hi.astype(F32)).astype(BF16)
    both = jnp.dot(jnp.concatenate([hi, lo], axis=1), mat, preferred_element_type=F32)
    return both[:, :LANES], both[:, LANES:]


KEY_CHUNK = 384
SUB_TILES = KEY_CHUNK // Q_BLOCK


def _chunked_suffix(val, carry, mat):
    parts = []
    for b in reversed(range(SUB_TILES)):
        within, total = _suffix_sums(val[:, b * Q_BLOCK:(b + 1) * Q_BLOCK], mat)
        parts.append(within + carry)
        carry = carry + total
    return jnp.concatenate(parts[::-1], axis=1), carry


def _stick_chunk(qm, kc, carry, umat, mask):
    z = lax.dot_general(qm, kc, (_NT, ((), ())), preferred_element_type=F32)
    z = jnp.where(mask, z, MASKED_Z)
    e = jnp.exp(-jnp.abs(z))
    lk = -(jnp.maximum(z, 0.0) + jnp.log(1.0 + e))
    lb = z + lk
    suffix, carry = _chunked_suffix(lk, carry, umat)
    return z, e, jnp.exp(lb + suffix), carry


DEAD_BELOW = -105.0
MASKED_Z = -1e30


Q_GROUP = 2


def _sweep(first_block, step, states, carries_of):
    n_chunks = ((first_block + Q_GROUP) * Q_BLOCK + KEY_CHUNK - 1) // KEY_CHUNK
    col = lax.broadcasted_iota(jnp.int32, (Q_BLOCK, KEY_CHUNK), 1)
    row = lax.broadcasted_iota(jnp.int32, (Q_BLOCK, 1), 0)

    def body(loop):
        n, _, sts = loop
        new, lowest = [], None
        for b in range(Q_GROUP):
            i = first_block + b
            end = (i + 1) * Q_BLOCK - n * KEY_CHUNK
            start = pl.multiple_of(jnp.maximum(end - KEY_CHUNK, 0), Q_BLOCK)
            st = step(b, start, col < jnp.minimum(i * Q_BLOCK + row, end) - start, sts[b])
            c0, c1 = carries_of(st)
            top = jnp.maximum(c0, c1)
            lowest = top if lowest is None else jnp.maximum(lowest, top)
            new.append(st)
        alive = (jnp.max(lowest) >= DEAD_BELOW).astype(jnp.int32)
        return n + 1, alive, tuple(new)

    return lax.while_loop(lambda loop: (loop[0] < n_chunks) & (loop[1] > 0), body,
                          (jnp.int32(0), jnp.int32(1), tuple(states)))[2]


def _head_rows(block, lane_lo_rows):
    zero = jnp.zeros_like(block)
    return jnp.concatenate([jnp.where(lane_lo_rows, block, zero), jnp.where(lane_lo_rows, zero, block)], axis=0)


def _attn_fwd(qkv, g_attn, ymix_in, width, shards):
    t_len = qkv.shape[0]
    npair = width // LANES
    rows_per_step = Q_GROUP * Q_BLOCK
    n_steps = t_len // rows_per_step
    n_sh = len(shards)

    def body(q_ref, k_ref, v_ref, g_ref, umat_ref, ymix_in_ref, *rest):
        shard_refs, rest = rest[:n_sh], rest[n_sh:]
        o_ref, ymix_ref = rest[:2]
        gathered_refs, sems = rest[2:2 + n_sh], rest[2 + n_sh:]
        p, i = pl.program_id(0), pl.program_id(1)
        start, forward, finish = _gather_plan(shard_refs, gathered_refs, *sems)
        pl.when((p == 0) & (i == 0))(start)
        pl.when((p == npair - 1) & (i == 0))(forward)
        lane_lo = lax.broadcasted_iota(jnp.int32, (rows_per_step, LANES), 1) < HEAD_DIM
        lane_lo_keys = lax.broadcasted_iota(jnp.int32, (KEY_CHUNK, LANES), 1) < HEAD_DIM
        umat = umat_ref[...]
        q = q_ref[...] * jnp.asarray(HEAD_DIM ** -0.5, BF16)
        q_heads = (jnp.where(lane_lo, q, jnp.zeros_like(q)), jnp.where(lane_lo, jnp.zeros_like(q), q))

        def step(b, rows, mask, state):
            carries, o_acc = state
            kc = k_ref[pl.ds(rows, KEY_CHUNK), :]
            v_heads = _head_rows(v_ref[pl.ds(rows, KEY_CHUNK), :], lane_lo_keys)
            new_carries, weights = [], []
            for h in range(2):
                qm = q_heads[h][b * Q_BLOCK:(b + 1) * Q_BLOCK]
                _, _, a, carry = _stick_chunk(qm, kc, carries[h], umat, mask)
                weights.append(a.astype(BF16))
                new_carries.append(carry)
            o_acc = o_acc + jnp.dot(jnp.concatenate(weights, axis=1), v_heads, preferred_element_type=F32)
            return tuple(new_carries), o_acc

        zero = jnp.zeros((Q_BLOCK, LANES), F32)
        states = _sweep(i * Q_GROUP, step, [((zero, zero), zero)] * Q_GROUP, lambda st: st[0])
        o = jnp.concatenate([st[1] for st in states], axis=0)
        o_ref[...] = o
        rs = lax.rsqrt(_group_sums(o * o, lane_lo) * (1.0 / HEAD_DIM) + RMS_EPS)
        ymix_ref[...] = (o * rs * g_ref[...]).astype(BF16)
        pl.when((p == npair - 1) & (i == n_steps - 1))(finish)

    hbm = pl.BlockSpec(memory_space=pl.ANY)
    outs = pl.pallas_call(
        body, name="attn_fwd", grid=(npair, n_steps),
        in_specs=[pl.BlockSpec((rows_per_step, LANES), lambda p, i: (i, p)),
                  pl.BlockSpec((t_len, LANES), lambda p, i: (0, npair + p)),
                  pl.BlockSpec((t_len, LANES), lambda p, i: (0, 2 * npair + p)),
                  pl.BlockSpec((1, LANES), lambda p, i: (0, p)),
                  pl.BlockSpec((2 * LANES, 2 * LANES), lambda p, i: (0, 0)),
                  hbm] + [hbm] * n_sh,
        out_specs=[pl.BlockSpec((rows_per_step, LANES), lambda p, i: (i, p)),
                   pl.BlockSpec((rows_per_step, LANES), lambda p, i: (i, npair + p))] + [hbm] * n_sh,
        out_shape=[jax.ShapeDtypeStruct((t_len, width), F32), jax.ShapeDtypeStruct(ymix_in.shape, BF16)]
        + [jax.ShapeDtypeStruct((N_DEV,) + s.shape, s.dtype) for s in shards],
        input_output_aliases={5: 1},
        scratch_shapes=_gather_scratch(n_sh),
        compiler_params=_params(("arbitrary", "arbitrary")),
    )(qkv, qkv, qkv, g_attn, _cumsum_matrix(False), ymix_in, *shards)
    return outs[0], outs[1], outs[2:]


def _attn_bwd(qkv, o, dymix, g_attn, width, exchange):
    t_len = qkv.shape[0]
    npair = width // LANES
    rows_per_step = Q_GROUP * Q_BLOCK
    n_steps = t_len // rows_per_step
    scale = HEAD_DIM ** -0.5
    n_ex = len(exchange)

    def body(q_ref, k_ref, v_ref, o_ref, dy_ref, g_ref, umat_ref, gmat_ref, *rest):
        p_refs, rest = rest[:n_ex], rest[n_ex:]
        dq_ref, dk_ref, dv_ref, gg_ref = rest[:4]
        land_refs, rest = rest[4:4 + n_ex], rest[4 + n_ex:]
        dk_s, dv_s = rest[:2]
        copies = _chip_copies(p_refs, land_refs, *rest[2:])
        p, i = pl.program_id(0), pl.program_id(1)

        @pl.when((p == 0) & (i == 0))
        def _():
            for cp in copies:
                cp.start()

        lane_lo = lax.broadcasted_iota(jnp.int32, (rows_per_step, LANES), 1) < HEAD_DIM
        lane_lo_keys = lax.broadcasted_iota(jnp.int32, (KEY_CHUNK, LANES), 1) < HEAD_DIM
        umat, gmat = umat_ref[...], gmat_ref[...]

        @pl.when(i == 0)
        def _():
            dk_s[...] = jnp.zeros_like(dk_s)
            dv_s[...] = jnp.zeros_like(dv_s)

        o_blk = o_ref[...]
        rs = lax.rsqrt(_group_sums(o_blk * o_blk, lane_lo) * (1.0 / HEAD_DIM) + RMS_EPS)
        yn = o_blk * rs
        dy_raw = dy_ref[...]
        dyn = dy_raw * g_ref[...]
        d_o = rs * (dyn - yn * (_group_sums(dyn * yn, lane_lo) * (1.0 / HEAD_DIM)))
        _accumulate(gg_ref, i, n_steps, _rows8(dy_raw * yn))
        do_bf = d_o.astype(BF16)
        do_o = do_bf.astype(F32) * o_blk
        q = q_ref[...]
        blocks = [slice(b * Q_BLOCK, (b + 1) * Q_BLOCK) for b in range(Q_GROUP)]
        lane_lo_q = lane_lo[:Q_BLOCK]
        totals = [(jnp.sum(jnp.where(lane_lo_q, do_o[r], 0.0), axis=-1, keepdims=True),
                   jnp.sum(jnp.where(lane_lo_q, 0.0, do_o[r]), axis=-1, keepdims=True)) for r in blocks]
        q_rows = [_head_rows(q[r], lane_lo_q) for r in blocks]
        do_rows = [_head_rows(do_bf[r], lane_lo_q) for r in blocks]
        q_scaled = [qr * jnp.asarray(scale, BF16) for qr in q_rows]

        def step(b, rows, mask, state):
            carries, carries_g, dq_acc = state
            kc = k_ref[pl.ds(rows, KEY_CHUNK), :]
            vc = v_ref[pl.ds(rows, KEY_CHUNK), :]
            new_carries, new_carries_g, dzs, a_bf = [], [], [], []
            for h in range(2):
                head = slice(h * Q_BLOCK, (h + 1) * Q_BLOCK)
                z, e, a, carry = _stick_chunk(q_scaled[b][head], kc, carries[h], umat, mask)
                da = lax.dot_general(do_rows[b][head], vc, (_NT, ((), ())), preferred_element_type=F32)
                a = a.astype(BF16)
                g = a.astype(F32) * da
                suffix_g, carry_g = _chunked_suffix(g, carries_g[h], gmat)
                before = totals[b][h] - suffix_g
                inv = 1.0 / (1.0 + e)
                sig = jnp.where(z >= 0.0, inv, e * inv)
                dz = g * (1.0 - sig) - before * sig
                dzs.append((dz * scale).astype(BF16))
                a_bf.append(a)
                new_carries.append(carry)
                new_carries_g.append(carry_g)
            dq_acc = dq_acc + jnp.dot(jnp.concatenate(dzs, axis=1), _head_rows(kc, lane_lo_keys),
                                      preferred_element_type=F32)
            dk_s[pl.ds(rows, KEY_CHUNK), :] += lax.dot_general(
                jnp.concatenate(dzs, axis=0), q_rows[b], (_TN, ((), ())), preferred_element_type=F32)
            dv_s[pl.ds(rows, KEY_CHUNK), :] += lax.dot_general(
                jnp.concatenate(a_bf, axis=0), do_rows[b], (_TN, ((), ())), preferred_element_type=F32)
            return tuple(new_carries), tuple(new_carries_g), dq_acc

        zero = jnp.zeros((Q_BLOCK, LANES), F32)
        states = _sweep(i * Q_GROUP, step, [((zero, zero), (zero, zero), zero)] * Q_GROUP, lambda st: st[0])
        dq_ref[...] = jnp.concatenate([st[2] for st in states], axis=0).astype(BF16)

        @pl.when(i == n_steps - 1)
        def _():
            dk_ref[...] = dk_s[...].astype(BF16)
            dv_ref[...] = dv_s[...].astype(BF16)

        @pl.when((p == npair - 1) & (i == n_steps - 1))
        def _():
            for cp in copies:
                cp.wait()

    whole = pl.BlockSpec((t_len, LANES), lambda p, i: (0, p))
    hbm = pl.BlockSpec(memory_space=pl.ANY)
    outs = pl.pallas_call(
        body, name="attn_bwd", grid=(npair, n_steps),
        in_specs=[pl.BlockSpec((rows_per_step, LANES), lambda p, i: (i, p)),
                  pl.BlockSpec((t_len, LANES), lambda p, i: (0, npair + p)),
                  pl.BlockSpec((t_len, LANES), lambda p, i: (0, 2 * npair + p)),
                  pl.BlockSpec((rows_per_step, LANES), lambda p, i: (i, p)),
                  pl.BlockSpec((rows_per_step, LANES), lambda p, i: (i, npair + p)),
                  pl.BlockSpec((1, LANES), lambda p, i: (0, p)),
                  pl.BlockSpec((2 * LANES, 2 * LANES), lambda p, i: (0, 0)),
                  pl.BlockSpec((2 * LANES, 2 * LANES), lambda p, i: (0, 0))] + [hbm] * n_ex,
        out_specs=[pl.BlockSpec((rows_per_step, LANES), lambda p, i: (i, p)), whole, whole,
                   pl.BlockSpec((8, LANES), lambda p, i: (0, p))] + [hbm] * n_ex,
        out_shape=[jax.ShapeDtypeStruct((t_len, width), BF16)] * 3 + [jax.ShapeDtypeStruct((8, width), F32)]
        + _chip_landing(exchange),
        scratch_shapes=[pltpu.VMEM((t_len, LANES), F32), pltpu.VMEM((t_len, LANES), F32)] + _chip_scratch(n_ex),
        compiler_params=_params(("arbitrary", "arbitrary")),
    )(qkv, qkv, qkv, o, dymix, g_attn, _cumsum_matrix(False), _cumsum_matrix(True), *exchange)
    return outs[0], outs[1], outs[2], outs[3], outs[4:]


def kernel(x, w_in, conv_w, g_conv, g_attn, w_out, ln1_g, ln1_b, w_up, w_down, ln2_g, ln2_b, loss_target, m_w_in, m_conv_w, m_g_conv, m_g_attn, m_w_out, m_ln1_g, m_ln1_b, m_w_up, m_w_down, m_ln2_g, m_ln2_b, v_w_in, v_conv_w, v_g_conv, v_g_attn, v_w_out, v_ln1_g, v_ln1_b, v_w_up, v_w_down, v_ln2_g, v_ln2_b):
    x2, target = x[0], loss_target[0]
    t_len, d_model = x2.shape
    width = g_conv.shape[1]
    in_shard = w_in.shape[2]
    up_shard = w_up.shape[2]
    d_ff = up_shard * N_DEV
    cw = conv_w.shape[2]
    xi, yi, ci = _place()
    dev = 4 * xi + 2 * yi + ci
    tm = 256

    win_g, wout_bf, wup_bf, wdown_bf = _all_gather(
        "gather_w_in", [w_in[0]], BF16, cast_only=[w_out[0], w_up[0], w_down[0]])
    (convw_g,) = _all_gather("gather_conv_w", [jnp.pad(conv_w[0], ((0, 5), (0, LANES - cw)))], F32)
    conv_full = jnp.transpose(convw_g[:, :3, :cw], (1, 0, 2)).reshape(3, width)

    n_row_tiles = t_len // tm
    rows = lambda cols: pl.BlockSpec((tm, cols), lambda i: (i, 0))
    resident = lambda arr: pl.BlockSpec(arr.shape, lambda i: (0,) * arr.ndim)
    vec = pl.BlockSpec((1, d_model), lambda i: (0, 0))
    acc8 = pl.BlockSpec((8, d_model), lambda i: (0, 0))
    tokens = lambda cols, dtype: jax.ShapeDtypeStruct((t_len, cols), dtype)

    def proj_in_epilogue(acc, extra, outs, i):
        outs[0][...] = acc[:, :3 * width]
        outs[1][...] = acc[:, 3 * width:].astype(BF16)
        outs[2][...] = extra[0][...].astype(BF16)

    gates, qkv, x_bf = _matmul(
        "proj_in", x2, win_g, rows(d_model), resident(win_g), _shards_wide, n_row_tiles,
        [(tokens(3 * width, F32), rows(3 * width)), (tokens(3 * width, BF16), rows(3 * width)),
         (tokens(d_model, BF16), rows(d_model))],
        proj_in_epilogue, extras=[(x2, rows(d_model))])

    ymix = _conv_fwd(gates, conv_full, g_conv, width)
    o_attn, ymix, (wout_g, wup_g, wdown_g) = _attn_fwd(qkv, g_attn, ymix, width, [wout_bf, wup_bf, wdown_bf])
    wout_full = wout_g.reshape(d_model, d_model)
    wdown_full = wdown_g.reshape(d_ff, d_model)

    def out_ln1_epilogue(acc, extra, outs, i):
        x_ref, g_ref, b_ref = extra
        r1 = ALPHA * x_ref[...] + acc
        x1, _, _ = _ln_fwd(r1, g_ref[...], b_ref[...])
        outs[0][...] = r1
        outs[1][...] = x1
        outs[2][...] = x1.astype(BF16)

    r1, x1, x1_bf = _matmul(
        "proj_out_ln1", ymix, wout_full, rows(d_model), resident(wout_full), _whole(_NN), n_row_tiles,
        [(tokens(d_model, F32), rows(d_model)), (tokens(d_model, F32), rows(d_model)),
         (tokens(d_model, BF16), rows(d_model))],
        out_ln1_epilogue, extras=[(x2, rows(d_model)), (ln1_g, vec), (ln1_b, vec)])

    def up_epilogue(acc, extra, outs, i):
        outs[0][...] = acc
        relu = jnp.maximum(acc, 0.0)
        outs[1][...] = (relu * relu).astype(BF16)

    up, hid = _matmul(
        "ffn_up", x1_bf, wup_g, rows(d_model), resident(wup_g), _shards_wide, n_row_tiles,
        [(tokens(d_ff, F32), rows(d_ff)), (tokens(d_ff, BF16), rows(d_ff))], up_epilogue)

    def down_ln2_epilogue(acc, extra, outs, i):
        x1_ref, t_ref, g_ref, b_ref = extra
        dr2_ref, dr2_bf_ref, loss_ref, gg_ref, gb_ref = outs
        gain = g_ref[...]
        r2 = ALPHA * x1_ref[...] + acc
        y, xhat, rstd = _ln_fwd(r2, gain, b_ref[...])
        diff = y - t_ref[...]
        d_y = diff * (1.0 / d_model)
        dr2 = _ln_bwd(d_y, xhat, rstd, gain)
        dr2_ref[...] = dr2
        dr2_bf_ref[...] = dr2.astype(BF16)
        _accumulate(gg_ref, i, n_row_tiles, _rows8(d_y * xhat))
        _accumulate(gb_ref, i, n_row_tiles, _rows8(d_y))
        _accumulate(loss_ref, i, n_row_tiles, _rows8(diff * diff), all_lanes=True)

    small_acc = (jax.ShapeDtypeStruct((8, d_model), F32), acc8)
    dr2, dr2_bf, loss_cols, g_ln2_g, g_ln2_b = _matmul(
        "ffn_down_ln2_loss", hid, wdown_full, rows(d_ff), resident(wdown_full), _whole(_NN), n_row_tiles,
        [(tokens(d_model, F32), rows(d_model)), (tokens(d_model, BF16), rows(d_model)),
         small_acc, small_acc, small_acc],
        down_ln2_epilogue, extras=[(x1, rows(d_model)), (target, rows(d_model)), (ln2_g, vec), (ln2_b, vec)])

    def dup_epilogue(acc, extra, outs, i):
        outs[0][...] = (acc * (2.0 * jnp.maximum(extra[0][...], 0.0))).astype(BF16)

    (dup,) = _matmul(
        "ffn_dhid", dr2_bf, wdown_full, rows(d_model), resident(wdown_full), _whole(_NT), n_row_tiles,
        [(tokens(d_ff, BF16), rows(d_ff))], dup_epilogue, extras=[(up, rows(d_ff))])

    def weight_grad(name, acts, cots, shard_shape, acts_by_shard):
        sr, sc = shard_shape
        per = max(1, 256 // sr) if acts_by_shard else 1
        tall = lambda cols: pl.BlockSpec((t_len, cols), lambda i: (0, i))
        if acts_by_shard:
            a_spec, b_spec = tall(per * sr), resident(cots)
        else:
            a_spec, b_spec = resident(acts), tall(sc)
        out_spec = pl.BlockSpec((per, sr, sc), lambda i: (i, 0, 0))

        def epilogue(acc, extra, outs, i):
            acc = acc.reshape(per, sr, sc)
            outs[0][...] = acc
            outs[1][...] = acc.astype(BF16)

        shape = (N_DEV, sr, sc)
        return _matmul(name, acts, cots, a_spec, b_spec, _whole(_TN), N_DEV // per,
                       [(jax.ShapeDtypeStruct(shape, F32), out_spec), (jax.ShapeDtypeStruct(shape, BF16), out_spec)],
                       epilogue)

    gw_down, gw_down_bf = weight_grad("grad_w_down", hid, dr2_bf, (d_ff // N_DEV, d_model), True)
    gw_up, gw_up_bf = weight_grad("grad_w_up", x1_bf, dup, (d_model, up_shard), False)

    def dx1_ln1_epilogue(acc, extra, outs, i):
        dr2_ref, r1_ref, g_ref, b_ref = extra
        dr1_ref, dr1_bf_ref, gg_ref, gb_ref = outs
        gain = g_ref[...]
        dx1 = ALPHA * dr2_ref[...] + acc
        _, xhat, rstd = _ln_fwd(r1_ref[...], gain, b_ref[...])
        dr1 = _ln_bwd(dx1, xhat, rstd, gain)
        dr1_ref[...] = dr1
        dr1_bf_ref[...] = dr1.astype(BF16)
        _accumulate(gg_ref, i, n_row_tiles, _rows8(dx1 * xhat))
        _accumulate(gb_ref, i, n_row_tiles, _rows8(dx1))

    dr1, dr1_bf, g_ln1_g, g_ln1_b = _matmul(
        "ffn_dx1_ln1", dup, wup_g, rows(d_ff), resident(wup_g), _shards_deep, n_row_tiles,
        [(tokens(d_model, F32), rows(d_model)), (tokens(d_model, BF16), rows(d_model)), small_acc, small_acc],
        dx1_ln1_epilogue, extras=[(dr2, rows(d_model)), (r1, rows(d_model)), (ln1_g, vec), (ln1_b, vec)])

    gw_out, gw_out_bf = weight_grad("grad_w_out", ymix, dr1_bf, (d_model // N_DEV, d_model), True)

    c_idx = jnp.reshape(ci, (1,)).astype(jnp.int32)
    q_me = jnp.reshape(2 * xi + yi, (1,)).astype(jnp.int32)

    def pair_stage(tag, names, grads_f32, grads_bf):
        landed = _pair_exchange("reduce_pair_" + tag, grads_bf)
        sums = [_pair_sum("pair_sum_" + nm, g, land, c_idx, q_me) for nm, g, land in zip(names, grads_f32, landed)]
        return [s[0] for s in sums], [s[1] for s in sums]

    wire_ffn, own_ffn = pair_stage("ffn", ["w_out", "w_up", "w_down"], [gw_out, gw_up, gw_down],
                                   [gw_out_bf, gw_up_bf, gw_down_bf])

    def plain_epilogue(acc, extra, outs, i):
        outs[0][...] = acc

    (dymix,) = _matmul(
        "proj_out_bwd", dr1_bf, wout_full, rows(d_model), resident(wout_full), _whole(_NT), n_row_tiles,
        [(tokens(d_model, F32), rows(d_model))], plain_epilogue)

    d_bg, d_cg, d_h, g_gconv, g_convw = _conv_bwd(gates, dymix, conv_full, g_conv, width)
    d_q, d_k, d_v, g_gattn, landed_ffn = _attn_bwd(qkv, o_attn, dymix, g_attn, width, wire_ffn)
    dproj = jnp.concatenate([d_bg, d_cg, d_h, d_q, d_k, d_v], axis=1)

    gw_in, gw_in_bf = weight_grad("grad_w_in", x_bf, dproj, (d_model, in_shard), False)
    wire_in, own_in = pair_stage("in", ["w_in"], [gw_in], [gw_in_bf])

    def dx_epilogue(acc, extra, outs, i):
        outs[0][...] = ALPHA * extra[0][...] + acc

    grad_x, landed_in = _matmul(
        "proj_in_bwd", dproj, win_g, rows(6 * width), resident(win_g), _shards_deep, n_row_tiles,
        [(tokens(d_model, F32), rows(d_model))], dx_epilogue, extras=[(dr1, rows(d_model))], exchange=wire_in)

    names = ["w_in", "w_out", "w_up", "w_down"]
    big = {}
    for nm, ow, land, w, m, v in zip(names, own_in + own_ffn, [landed_in] + list(landed_ffn),
                                     (w_in, w_out, w_up, w_down),
                                     (m_w_in, m_w_out, m_w_up, m_w_down), (v_w_in, v_w_out, v_w_up, v_w_down)):
        big[nm] = [r[None] for r in _shard_update("update_" + nm, ow, land, w[0], m[0], v[0])]

    def pack(l1g, l1b, l2g, l2b, gc, ga, taps):
        return jnp.concatenate([l1g, l1b, l2g, l2b, jnp.concatenate([gc, ga], axis=1),
                                jnp.concatenate([taps, jnp.zeros((3, d_model - width), F32)], axis=1)], axis=0)

    local = pack(g_ln1_g[:1], g_ln1_b[:1], g_ln2_g[:1], g_ln2_b[:1], g_gconv[:1], g_gattn[:1], g_convw[:3])
    (parts,) = _all_gather("gather_small_grads", [local], F32)

    def spread(a):
        return lax.dynamic_update_slice(jnp.zeros((3, width), F32), a, (0, dev * cw))

    w_small = pack(ln1_g, ln1_b, ln2_g, ln2_b, g_conv, g_attn, spread(conv_w[0]))
    m_small = pack(m_ln1_g, m_ln1_b, m_ln2_g, m_ln2_b, m_g_conv, m_g_attn, spread(m_conv_w[0]))
    v_small = pack(v_ln1_g, v_ln1_b, v_ln2_g, v_ln2_b, v_g_conv, v_g_attn, spread(v_conv_w[0]))
    small = _small_update(parts, w_small, m_small, v_small)

    def unpack(a):
        taps = lax.dynamic_slice(a[5:8, :width], (0, dev * cw), (3, cw))[None]
        return {"ln1_g": a[0:1], "ln1_b": a[1:2], "ln2_g": a[2:3], "ln2_b": a[3:4],
                "g_conv": a[4:5, :width], "g_attn": a[4:5, width:], "conv_w": taps}

    small = [unpack(a) for a in small]
    loss = lax.psum((0.5 / d_model) * loss_cols[0, 0], ("x", "y", "c"))

    order = ["w_in", "conv_w", "g_conv", "g_attn", "w_out", "ln1_g", "ln1_b", "w_up", "w_down", "ln2_g", "ln2_b"]
    result = [loss, grad_x[None]]
    for kind in range(4):
        for nm in order:
            result.append(big[nm][kind] if nm in big else small[kind][nm])
    return tuple(result)
```

```python
import functools

import jax
import jax.numpy as jnp
from jax import lax
from jax.experimental import pallas as pl
from jax.experimental.pallas import tpu as pltpu

F32 = jnp.float32
BF16 = jnp.bfloat16
MESH = pl.DeviceIdType.MESH

N_DEV = 8
HEAD_DIM = 64
LANES = 128
Q_BLOCK = 128
ALPHA = 2.0 ** 0.25
LN_EPS = 1e-5
RMS_EPS = 1e-6
ADAM_LR, ADAM_B1, ADAM_B2, ADAM_EPS, ADAM_WD, ADAM_STEP = 0.001, 0.9, 0.999, 1e-08, 0.01, 10
VMEM_LIMIT_BYTES = 48 * 1024 * 1024

_NT = ((1,), (1,))
_NN = ((1,), (0,))
_TN = ((0,), (0,))


def _params(sem=None):
    return pltpu.CompilerParams(dimension_semantics=sem, vmem_limit_bytes=VMEM_LIMIT_BYTES)


def _dot(a, b, contract):
    return lax.dot_general(a.astype(BF16), b.astype(BF16), (contract, ((), ())), preferred_element_type=F32)


def _whole(contract):
    return lambda a_ref, b_ref: _dot(a_ref[...], b_ref[...], contract)


def _shards_wide(a_ref, b_ref):
    a = a_ref[...].astype(BF16)
    return jnp.concatenate([_dot(a, b_ref[s], _NN) for s in range(b_ref.shape[0])], axis=1)


def _shards_deep(a_ref, b_ref):
    n = b_ref.shape[2]
    acc = _dot(a_ref[:, 0:n], b_ref[0], _NT)
    for s in range(1, b_ref.shape[0]):
        acc = acc + _dot(a_ref[:, s * n:(s + 1) * n], b_ref[s], _NT)
    return acc


def _matmul(name, a, b, a_spec, b_spec, product, n_steps, outs, epilogue, extras=(), exchange=()):
    n_extra, n_out, n_ex = len(extras), len(outs), len(exchange)

    def body(a_ref, b_ref, *rest):
        i = pl.program_id(0)
        extra_refs, rest = rest[:n_extra], rest[n_extra:]
        p_refs, rest = rest[:n_ex], rest[n_ex:]
        out_refs, rest = rest[:n_out], rest[n_out:]
        land_refs, sems = rest[:n_ex], rest[n_ex:]
        copies = _chip_copies(p_refs, land_refs, *sems) if n_ex else []

        @pl.when(i == 0)
        def _():
            for cp in copies:
                cp.start()

        epilogue(product(a_ref, b_ref), extra_refs, out_refs, i)

        @pl.when(i == n_steps - 1)
        def _():
            for cp in copies:
                cp.wait()

    hbm = pl.BlockSpec(memory_space=pl.ANY)
    return pl.pallas_call(
        body, name=name, grid=(n_steps,),
        in_specs=[a_spec, b_spec] + [s for _, s in extras] + [hbm] * n_ex,
        out_specs=[s for _, s in outs] + [hbm] * n_ex,
        out_shape=[o for o, _ in outs] + _chip_landing(exchange),
        scratch_shapes=_chip_scratch(n_ex) if n_ex else [],
        compiler_params=_params(("arbitrary",)),
    )(a, b, *[e for e, _ in extras], *exchange)


def _ln_fwd(r, g, b):
    mu = jnp.mean(r, axis=-1, keepdims=True)
    xc = r - mu
    rstd = lax.rsqrt(jnp.mean(xc * xc, axis=-1, keepdims=True) + LN_EPS)
    xhat = xc * rstd
    return xhat * g + b, xhat, rstd


def _ln_bwd(dy, xhat, rstd, g):
    dxh = dy * g
    m1 = jnp.mean(dxh, axis=-1, keepdims=True)
    m2 = jnp.mean(dxh * xhat, axis=-1, keepdims=True)
    return rstd * (dxh - m1 - xhat * m2)


def _rows8(v):
    n, c = v.shape
    return jnp.sum(v.reshape(n // 8, 8, c), axis=0)


def _accumulate(ref, i, n_steps, part8, all_lanes=False):
    @pl.when(i == 0)
    def _():
        ref[...] = part8

    @pl.when(i > 0)
    def _():
        ref[...] += part8

    @pl.when(i == n_steps - 1)
    def _():
        tot = jnp.sum(ref[...], axis=0, keepdims=True)
        if all_lanes:
            tot = jnp.sum(tot, axis=1, keepdims=True)
        ref[...] = jnp.broadcast_to(tot, ref.shape)


def _group_sums(v, lane_lo):
    s0 = jnp.sum(jnp.where(lane_lo, v, 0.0), axis=-1, keepdims=True)
    s1 = jnp.sum(jnp.where(lane_lo, 0.0, v), axis=-1, keepdims=True)
    return jnp.where(lane_lo, s0, s1)


def _place():
    return lax.axis_index("x"), lax.axis_index("y"), lax.axis_index("c")


def _gather_plan(src_refs, out_refs, send_sems, recv_sems, local_sems):
    n = len(src_refs)
    x, y, c = _place()
    me, sibling = (x, y, c), (x, y, 1 - c)
    chips = [(1 - x, y), (x, 1 - y), (1 - x, 1 - y)]

    def copy(a, k, block, to, src=None):
        slot = out_refs[a].at[4 * block[0] + 2 * block[1] + block[2]]
        return pltpu.make_async_remote_copy(
            src_ref=slot if src is None else src, dst_ref=slot,
            send_sem=send_sems.at[a, k], recv_sem=recv_sems.at[a, k],
            device_id=to, device_id_type=MESH)

    mine = [pltpu.make_async_copy(src_refs[a], out_refs[a].at[4 * x + 2 * y + c], local_sems.at[a]) for a in range(n)]
    first = [[copy(a, 0, me, sibling, src=src_refs[a])]
             + [copy(a, 1 + j, me, (*chip, c), src=src_refs[a]) for j, chip in enumerate(chips)] for a in range(n)]
    passed = [[copy(a, 4 + j, (*chip, c), sibling) for j, chip in enumerate(chips)] for a in range(n)]

    def start():
        for a in range(n):
            mine[a].start()
            for cp in first[a]:
                cp.start()

    def forward():
        for j, chip in enumerate(chips):
            for a in range(n):
                copy(a, 1 + j, (*chip, c), me).wait_recv()
                passed[a][j].start()

    def finish():
        for a in range(n):
            copy(a, 0, sibling, me).wait_recv()
            for j, chip in enumerate(chips):
                copy(a, 4 + j, (*chip, 1 - c), me).wait_recv()
        for a in range(n):
            for cp in first[a] + passed[a]:
                cp.wait_send()
            mine[a].wait()

    return start, forward, finish


def _gather_scratch(n):
    return [pltpu.SemaphoreType.DMA((n, 7)), pltpu.SemaphoreType.DMA((n, 7)), pltpu.SemaphoreType.DMA((n,))]


def _all_gather(name, shards, out_dtype, cast_only=()):
    n, m = len(shards), len(cast_only)

    def body(*refs):
        in_refs, cast_in = refs[:n], refs[n:n + m]
        out_refs, cast_out = refs[n + m:2 * n + m], refs[2 * n + m:2 * (n + m)]
        stage = refs[2 * (n + m):3 * n + 2 * m]
        for a in range(n):
            stage[a][...] = in_refs[a][...].astype(out_dtype)
        start, forward, finish = _gather_plan(stage, out_refs, *refs[3 * n + 2 * m:])
        start()
        for a in range(m):
            cast_out[a][...] = cast_in[a][...].astype(out_dtype)
        forward()
        finish()

    return pl.pallas_call(
        body, name=name,
        in_specs=[pl.BlockSpec(memory_space=pltpu.VMEM)] * (n + m),
        out_specs=[pl.BlockSpec(memory_space=pl.ANY)] * n + [pl.BlockSpec(memory_space=pltpu.VMEM)] * m,
        out_shape=[jax.ShapeDtypeStruct((N_DEV,) + s.shape, out_dtype) for s in shards]
        + [jax.ShapeDtypeStruct(s.shape, out_dtype) for s in cast_only],
        scratch_shapes=[pltpu.VMEM(s.shape, out_dtype) for s in shards] + _gather_scratch(n),
        compiler_params=pltpu.CompilerParams(vmem_limit_bytes=VMEM_LIMIT_BYTES),
    )(*shards, *cast_only)


def _pair_exchange(name, grads):
    n = len(grads)

    def body(*refs):
        g_refs, land_refs = refs[:n], refs[n:2 * n]
        send_sems, recv_sems = refs[2 * n:]
        x, y, c = _place()
        copies = []
        for a in range(n):
            for q in range(4):
                cp = pltpu.make_async_remote_copy(
                    src_ref=g_refs[a].at[2 * q + (1 - c)], dst_ref=land_refs[a].at[q],
                    send_sem=send_sems.at[a, q], recv_sem=recv_sems.at[a, q],
                    device_id=(x, y, 1 - c), device_id_type=MESH)
                cp.start()
                copies.append(cp)
        for cp in copies:
            cp.wait()

    return pl.pallas_call(
        body, name=name,
        in_specs=[pl.BlockSpec(memory_space=pl.ANY)] * n,
        out_specs=[pl.BlockSpec(memory_space=pl.ANY)] * n,
        out_shape=[jax.ShapeDtypeStruct((4,) + g.shape[1:], g.dtype) for g in grads],
        scratch_shapes=[pltpu.SemaphoreType.DMA((n, 4)), pltpu.SemaphoreType.DMA((n, 4))],
    )(*grads)


def _chip_copies(p_refs, land_refs, send_sems, recv_sems):
    x, y, c = _place()
    chips = [(1 - x, y), (x, 1 - y), (1 - x, 1 - y)]
    return [pltpu.make_async_remote_copy(
        src_ref=p_refs[a].at[2 * px + py], dst_ref=land_refs[a].at[k],
        send_sem=send_sems.at[a, k], recv_sem=recv_sems.at[a, k],
        device_id=(px, py, c), device_id_type=MESH)
        for a in range(len(p_refs)) for k, (px, py) in enumerate(chips)]


def _chip_scratch(n):
    return [pltpu.SemaphoreType.DMA((n, 3)), pltpu.SemaphoreType.DMA((n, 3))]


def _chip_landing(pair_sums):
    return [jax.ShapeDtypeStruct((3,) + p.shape[1:], p.dtype) for p in pair_sums]


def _pair_sum(name, g_f32, landed, c_idx, q_me):
    _, rows, cols = g_f32.shape
    tr = min(rows, 256)

    def body(c_ref, q_ref, g_ref, l_ref, wire_ref, own_ref):
        q = pl.program_id(1)
        s = g_ref[...] + l_ref[...].astype(F32)
        wire_ref[...] = s.astype(BF16)

        @pl.when(q == q_ref[0])
        def _():
            own_ref[...] = s

    return pl.pallas_call(
        body, name=name,
        grid_spec=pltpu.PrefetchScalarGridSpec(
            num_scalar_prefetch=2, grid=(rows // tr, 4),
            in_specs=[pl.BlockSpec((None, tr, cols), lambda i, q, c_ref, q_ref: (2 * q + c_ref[0], i, 0)),
                      pl.BlockSpec((None, tr, cols), lambda i, q, c_ref, q_ref: (q, i, 0))],
            out_specs=[pl.BlockSpec((None, tr, cols), lambda i, q, c_ref, q_ref: (q, i, 0)),
                       pl.BlockSpec((tr, cols), lambda i, q, c_ref, q_ref: (i, 0))]),
        out_shape=[jax.ShapeDtypeStruct((4, rows, cols), BF16), jax.ShapeDtypeStruct((rows, cols), F32)],
        compiler_params=_params(("parallel", "arbitrary")),
    )(c_idx, q_me, g_f32, landed)


def _adamw(w, g, m, v):
    m = ADAM_B1 * m + (1.0 - ADAM_B1) * g
    v = ADAM_B2 * v + (1.0 - ADAM_B2) * (g * g)
    m_hat = m / (1.0 - ADAM_B1 ** ADAM_STEP)
    v_hat = v / (1.0 - ADAM_B2 ** ADAM_STEP)
    delta = -ADAM_LR * (m_hat / (jnp.sqrt(v_hat) + ADAM_EPS) + ADAM_WD * w)
    return delta, m, v


def _shard_update(name, own, landed, w, m, v):
    rows, cols = own.shape
    tr = min(rows, 256)

    def body(own_ref, l_ref, w_ref, m_ref, v_ref, g_out, d_out, m_out, v_out):
        g = own_ref[...] + l_ref[0].astype(F32) + l_ref[1].astype(F32) + l_ref[2].astype(F32)
        d, m_new, v_new = _adamw(w_ref[...], g, m_ref[...], v_ref[...])
        g_out[...] = g
        d_out[...] = d
        m_out[...] = m_new
        v_out[...] = v_new

    tile = pl.BlockSpec((tr, cols), lambda i: (i, 0))
    return pl.pallas_call(
        body, name=name, grid=(rows // tr,),
        in_specs=[tile, pl.BlockSpec((3, tr, cols), lambda i: (0, i, 0)), tile, tile, tile],
        out_specs=[tile] * 4,
        out_shape=[jax.ShapeDtypeStruct((rows, cols), F32)] * 4,
        compiler_params=_params(("parallel",)),
    )(own, landed, w, m, v)


def _small_update(parts, w, m, v):
    def body(p_ref, w_ref, m_ref, v_ref, g_out, d_out, m_out, v_out):
        g = p_ref[0]
        for d in range(1, N_DEV):
            g = g + p_ref[d]
        dl, m_new, v_new = _adamw(w_ref[...], g, m_ref[...], v_ref[...])
        g_out[...] = g
        d_out[...] = dl
        m_out[...] = m_new
        v_out[...] = v_new

    return pl.pallas_call(
        body, name="small_update",
        out_shape=[jax.ShapeDtypeStruct(w.shape, F32)] * 4,
        compiler_params=pltpu.CompilerParams(vmem_limit_bytes=VMEM_LIMIT_BYTES),
    )(parts, w, m, v)


CONV_CHUNK = 512


def _conv_taps(u_s, lo, chunk):
    ext = u_s[pl.ds(lo, chunk + 8), :]
    return ext[8:], pltpu.roll(ext, 1, 0)[8:], pltpu.roll(ext, 2, 0)[8:]


def _conv_fwd(gates, conv_w, g_conv, width):
    t_len = gates.shape[0]
    nblk = width // LANES
    n_chunks = t_len // CONV_CHUNK

    def body(bg_ref, cg_ref, h_ref, w_ref, g_ref, y_ref, u_s):
        lane_lo = lax.broadcasted_iota(jnp.int32, (CONV_CHUNK, LANES), 1) < HEAD_DIM
        u_s[0:8, :] = jnp.zeros((8, LANES), F32)

        def fill(r, _):
            lo = pl.multiple_of(r * CONV_CHUNK, CONV_CHUNK)
            u_s[pl.ds(pl.multiple_of(lo + 8, 8), CONV_CHUNK), :] = cg_ref[pl.ds(lo, CONV_CHUNK), :] * h_ref[pl.ds(lo, CONV_CHUNK), :]
            return 0

        lax.fori_loop(0, n_chunks, fill, 0)
        w = w_ref[...]
        gain = g_ref[...]

        def step(r, _):
            lo = pl.multiple_of(r * CONV_CHUNK, CONV_CHUNK)
            u0, u1, u2 = _conv_taps(u_s, lo, CONV_CHUNK)
            y = bg_ref[pl.ds(lo, CONV_CHUNK), :] * (w[2:3] * u0 + w[1:2] * u1 + w[0:1] * u2)
            rs = lax.rsqrt(_group_sums(y * y, lane_lo) * (1.0 / HEAD_DIM) + RMS_EPS)
            y_ref[pl.ds(lo, CONV_CHUNK), :] = (y * rs * gain).astype(BF16)
            return 0

        lax.fori_loop(0, n_chunks, step, 0)

    col = lambda off: pl.BlockSpec((t_len, LANES), lambda p: (0, off + p))
    return pl.pallas_call(
        body, name="conv_fwd", grid=(nblk,),
        in_specs=[col(0), col(nblk), col(2 * nblk),
                  pl.BlockSpec((3, LANES), lambda p: (0, p)), pl.BlockSpec((1, LANES), lambda p: (0, p))],
        out_specs=col(0),
        out_shape=jax.ShapeDtypeStruct((t_len, 2 * width), BF16),
        scratch_shapes=[pltpu.VMEM((t_len + 8, LANES), F32)],
        compiler_params=_params(("parallel",)),
    )(gates, gates, gates, conv_w, g_conv)


def _conv_bwd(gates, dymix, conv_w, g_conv, width):
    t_len = gates.shape[0]
    nblk = width // LANES
    n_chunks = t_len // CONV_CHUNK

    def body(bg_ref, cg_ref, h_ref, dy_ref, w_ref, g_ref, dbg_ref, dcg_ref, dh_ref, gg_ref, gw_ref, u_s, dc_s):
        lane_lo = lax.broadcasted_iota(jnp.int32, (CONV_CHUNK, LANES), 1) < HEAD_DIM
        u_s[0:8, :] = jnp.zeros((8, LANES), F32)
        dc_s[t_len:t_len + 8, :] = jnp.zeros((8, LANES), F32)

        def fill(r, _):
            lo = pl.multiple_of(r * CONV_CHUNK, CONV_CHUNK)
            u_s[pl.ds(pl.multiple_of(lo + 8, 8), CONV_CHUNK), :] = cg_ref[pl.ds(lo, CONV_CHUNK), :] * h_ref[pl.ds(lo, CONV_CHUNK), :]
            return 0

        lax.fori_loop(0, n_chunks, fill, 0)
        w = w_ref[...]
        gain = g_ref[...]

        def step(r, acc):
            gg, gw0, gw1, gw2 = acc
            lo = pl.multiple_of(r * CONV_CHUNK, CONV_CHUNK)
            u0, u1, u2 = _conv_taps(u_s, lo, CONV_CHUNK)
            bg = bg_ref[pl.ds(lo, CONV_CHUNK), :]
            conv = w[2:3] * u0 + w[1:2] * u1 + w[0:1] * u2
            y = bg * conv
            rs = lax.rsqrt(_group_sums(y * y, lane_lo) * (1.0 / HEAD_DIM) + RMS_EPS)
            yn = y * rs
            dyn_raw = dy_ref[pl.ds(lo, CONV_CHUNK), :]
            dyn = dyn_raw * gain
            d_y = rs * (dyn - yn * (_group_sums(dyn * yn, lane_lo) * (1.0 / HEAD_DIM)))
            dbg_ref[pl.ds(lo, CONV_CHUNK), :] = (d_y * conv).astype(BF16)
            dconv = d_y * bg
            dc_s[pl.ds(lo, CONV_CHUNK), :] = dconv
            return (gg + _rows8(dyn_raw * yn), gw0 + _rows8(dconv * u2), gw1 + _rows8(dconv * u1),
                    gw2 + _rows8(dconv * u0))

        zero = jnp.zeros((8, LANES), F32)
        gg, gw0, gw1, gw2 = lax.fori_loop(0, n_chunks, step, (zero, zero, zero, zero))
        gg_ref[...] = jnp.broadcast_to(jnp.sum(gg, axis=0, keepdims=True), (8, LANES))
        row = lax.broadcasted_iota(jnp.int32, (8, LANES), 0)
        tot = lambda v: jnp.broadcast_to(jnp.sum(v, axis=0, keepdims=True), (8, LANES))
        gw_ref[...] = jnp.where(row == 0, tot(gw0), jnp.where(row == 1, tot(gw1), jnp.where(row == 2, tot(gw2), 0.0)))

        def back(r, _):
            lo = pl.multiple_of(r * CONV_CHUNK, CONV_CHUNK)
            ext = dc_s[pl.ds(lo, CONV_CHUNK + 8), :]
            d0 = ext[:CONV_CHUNK]
            d1 = pltpu.roll(ext, CONV_CHUNK + 7, 0)[:CONV_CHUNK]
            d2 = pltpu.roll(ext, CONV_CHUNK + 6, 0)[:CONV_CHUNK]
            du = w[2:3] * d0 + w[1:2] * d1 + w[0:1] * d2
            dcg_ref[pl.ds(lo, CONV_CHUNK), :] = (du * h_ref[pl.ds(lo, CONV_CHUNK), :]).astype(BF16)
            dh_ref[pl.ds(lo, CONV_CHUNK), :] = (du * cg_ref[pl.ds(lo, CONV_CHUNK), :]).astype(BF16)
            return 0

        lax.fori_loop(0, n_chunks, back, 0)

    col = lambda off: pl.BlockSpec((t_len, LANES), lambda p: (0, off + p))
    small = pl.BlockSpec((8, LANES), lambda p: (0, p))
    return pl.pallas_call(
        body, name="conv_bwd", grid=(nblk,),
        in_specs=[col(0), col(nblk), col(2 * nblk), col(0),
                  pl.BlockSpec((3, LANES), lambda p: (0, p)), pl.BlockSpec((1, LANES), lambda p: (0, p))],
        out_specs=[col(0), col(0), col(0), small, small],
        out_shape=[jax.ShapeDtypeStruct((t_len, width), BF16)] * 3 + [jax.ShapeDtypeStruct((8, width), F32)] * 2,
        scratch_shapes=[pltpu.VMEM((t_len + 8, LANES), F32), pltpu.VMEM((t_len + 8, LANES), F32)],
        compiler_params=_params(("parallel",)),
    )(gates, gates, gates, dymix, conv_w, g_conv)


def _cumsum_matrix(inclusive):
    j = lax.broadcasted_iota(jnp.int32, (2 * LANES, 2 * LANES), 0) % LANES
    s = lax.broadcasted_iota(jnp.int32, (2 * LANES, 2 * LANES), 1)
    keep = (s >= LANES) | ((j >= s) if inclusive else (j > s))
    return keep.astype(BF16)


KEY_CHUNK = 384
SUB_TILES = KEY_CHUNK // Q_BLOCK
Q_GROUP = 2
DEAD_BELOW = -105.0
MASKED_Z = -1e30


def _scores_phase(qms, kcs, masks):
    zs = [lax.dot_general(q, k, (_NT, ((), ())), preferred_element_type=F32) for q, k in zip(qms, kcs)]
    return [jnp.where(m, z, MASKED_Z) for z, m in zip(zs, masks)]


def _logs_phase(zs):
    es = [jnp.exp(-jnp.abs(z)) for z in zs]
    lks = [-(jnp.maximum(z, 0.0) + jnp.log(1.0 + e)) for z, e in zip(zs, es)]
    return es, lks, [z + lk for z, lk in zip(zs, lks)]


def _suffix_phase(vals, carries, mat):
    def hi_lo(tile):
        hi = tile.astype(BF16)
        return jnp.concatenate([hi, (tile - hi.astype(F32)).astype(BF16)], axis=1)

    packed = [[hi_lo(v[:, t * Q_BLOCK:(t + 1) * Q_BLOCK]) for t in range(SUB_TILES)] for v in vals]
    prods = [[jnp.dot(p, mat, preferred_element_type=F32) for p in chain] for chain in packed]
    sums, new_carries = [], []
    for chain, carry in zip(prods, carries):
        parts = []
        for t in reversed(range(SUB_TILES)):
            parts.append(chain[t][:, :LANES] + carry)
            carry = carry + chain[t][:, LANES:]
        sums.append(jnp.concatenate(parts[::-1], axis=1))
        new_carries.append(carry)
    return sums, new_carries


def _sweep(first_block, step, states, carries_of):
    n_chunks = ((first_block + Q_GROUP) * Q_BLOCK + KEY_CHUNK - 1) // KEY_CHUNK
    col = lax.broadcasted_iota(jnp.int32, (Q_BLOCK, KEY_CHUNK), 1)
    row = lax.broadcasted_iota(jnp.int32, (Q_BLOCK, 1), 0)

    def body(loop):
        n, _, sts = loop
        starts, masks = [], []
        for b in range(Q_GROUP):
            i = first_block + b
            end = (i + 1) * Q_BLOCK - n * KEY_CHUNK
            starts.append(pl.multiple_of(jnp.maximum(end - KEY_CHUNK, 0), Q_BLOCK))
            masks.append(col < jnp.minimum(i * Q_BLOCK + row, end) - starts[b])
        sts = step(starts, masks, sts)
        lowest = None
        for st in sts:
            for carry in carries_of(st):
                lowest = carry if lowest is None else jnp.maximum(lowest, carry)
        alive = (jnp.max(lowest) >= DEAD_BELOW).astype(jnp.int32)
        return n + 1, alive, tuple(sts)

    return lax.while_loop(lambda loop: (loop[0] < n_chunks) & (loop[1] > 0), body,
                          (jnp.int32(0), jnp.int32(1), tuple(states)))[2]


CHAINS = [(b, h) for b in range(Q_GROUP) for h in range(2)]


def _head_rows(block, lane_lo_rows):
    zero = jnp.zeros_like(block)
    return jnp.concatenate([jnp.where(lane_lo_rows, block, zero), jnp.where(lane_lo_rows, zero, block)], axis=0)


def _attn_fwd(qkv, g_attn, ymix_in, width, shards):
    t_len = qkv.shape[0]
    npair = width // LANES
    rows_per_step = Q_GROUP * Q_BLOCK
    n_steps = t_len // rows_per_step
    n_sh = len(shards)

    def body(q_ref, k_ref, v_ref, g_ref, umat_ref, ymix_in_ref, *rest):
        shard_refs, rest = rest[:n_sh], rest[n_sh:]
        o_ref, ymix_ref = rest[:2]
        gathered_refs, sems = rest[2:2 + n_sh], rest[2 + n_sh:]
        p, i = pl.program_id(0), pl.program_id(1)
        start, forward, finish = _gather_plan(shard_refs, gathered_refs, *sems)
        pl.when((p == 0) & (i == 0))(start)
        pl.when((p == npair - 1) & (i == 0))(forward)
        lane_lo = lax.broadcasted_iota(jnp.int32, (rows_per_step, LANES), 1) < HEAD_DIM
        lane_lo_keys = lax.broadcasted_iota(jnp.int32, (KEY_CHUNK, LANES), 1) < HEAD_DIM
        umat = umat_ref[...]
        q = q_ref[...] * jnp.asarray(HEAD_DIM ** -0.5, BF16)
        q_heads = (jnp.where(lane_lo, q, jnp.zeros_like(q)), jnp.where(lane_lo, jnp.zeros_like(q), q))

        def step(starts, masks, states):
            kcs = [k_ref[pl.ds(r, KEY_CHUNK), :] for r in starts]
            zs = _scores_phase([q_heads[h][b * Q_BLOCK:(b + 1) * Q_BLOCK] for b, h in CHAINS],
                               [kcs[b] for b, _ in CHAINS], [masks[b] for b, _ in CHAINS])
            _, lks, lbs = _logs_phase(zs)
            suffixes, carries = _suffix_phase(lks, [states[b][0][h] for b, h in CHAINS], umat)
            weights = [jnp.exp(lb + sfx).astype(BF16) for lb, sfx in zip(lbs, suffixes)]
            new = []
            for b in range(Q_GROUP):
                v_heads = _head_rows(v_ref[pl.ds(starts[b], KEY_CHUNK), :], lane_lo_keys)
                o_acc = states[b][1] + jnp.dot(jnp.concatenate(weights[2 * b:2 * b + 2], axis=1), v_heads,
                                               preferred_element_type=F32)
                new.append(((carries[2 * b], carries[2 * b + 1]), o_acc))
            return new

        zero = jnp.zeros((Q_BLOCK, LANES), F32)
        states = _sweep(i * Q_GROUP, step, [((zero, zero), zero)] * Q_GROUP, lambda st: st[0])
        o = jnp.concatenate([st[1] for st in states], axis=0)
        o_ref[...] = o
        rs = lax.rsqrt(_group_sums(o * o, lane_lo) * (1.0 / HEAD_DIM) + RMS_EPS)
        ymix_ref[...] = (o * rs * g_ref[...]).astype(BF16)
        pl.when((p == npair - 1) & (i == n_steps - 1))(finish)

    hbm = pl.BlockSpec(memory_space=pl.ANY)
    outs = pl.pallas_call(
        body, name="attn_fwd", grid=(npair, n_steps),
        in_specs=[pl.BlockSpec((rows_per_step, LANES), lambda p, i: (i, p)),
                  pl.BlockSpec((t_len, LANES), lambda p, i: (0, npair + p)),
                  pl.BlockSpec((t_len, LANES), lambda p, i: (0, 2 * npair + p)),
                  pl.BlockSpec((1, LANES), lambda p, i: (0, p)),
                  pl.BlockSpec((2 * LANES, 2 * LANES), lambda p, i: (0, 0)),
                  hbm] + [hbm] * n_sh,
        out_specs=[pl.BlockSpec((rows_per_step, LANES), lambda p, i: (i, p)),
                   pl.BlockSpec((rows_per_step, LANES), lambda p, i: (i, npair + p))] + [hbm] * n_sh,
        out_shape=[jax.ShapeDtypeStruct((t_len, width), F32), jax.ShapeDtypeStruct(ymix_in.shape, BF16)]
        + [jax.ShapeDtypeStruct((N_DEV,) + s.shape, s.dtype) for s in shards],
        input_output_aliases={5: 1},
        scratch_shapes=_gather_scratch(n_sh),
        compiler_params=_params(("arbitrary", "arbitrary")),
    )(qkv, qkv, qkv, g_attn, _cumsum_matrix(False), ymix_in, *shards)
    return outs[0], outs[1], outs[2:]


def _attn_bwd(qkv, o, dymix, g_attn, width, exchange):
    t_len = qkv.shape[0]
    npair = width // LANES
    rows_per_step = Q_GROUP * Q_BLOCK
    n_steps = t_len // rows_per_step
    scale = HEAD_DIM ** -0.5
    n_ex = len(exchange)

    def body(q_ref, k_ref, v_ref, o_ref, dy_ref, g_ref, umat_ref, gmat_ref, *rest):
        p_refs, rest = rest[:n_ex], rest[n_ex:]
        dq_ref, dk_ref, dv_ref, gg_ref = rest[:4]
        land_refs, rest = rest[4:4 + n_ex], rest[4 + n_ex:]
        dk_s, dv_s = rest[:2]
        copies = _chip_copies(p_refs, land_refs, *rest[2:])
        p, i = pl.program_id(0), pl.program_id(1)

        @pl.when((p == 0) & (i == 0))
        def _():
            for cp in copies:
                cp.start()

        lane_lo = lax.broadcasted_iota(jnp.int32, (rows_per_step, LANES), 1) < HEAD_DIM
        lane_lo_keys = lax.broadcasted_iota(jnp.int32, (KEY_CHUNK, LANES), 1) < HEAD_DIM
        umat, gmat = umat_ref[...], gmat_ref[...]

        @pl.when(i == 0)
        def _():
            dk_s[...] = jnp.zeros_like(dk_s)
            dv_s[...] = jnp.zeros_like(dv_s)

        o_blk = o_ref[...]
        rs = lax.rsqrt(_group_sums(o_blk * o_blk, lane_lo) * (1.0 / HEAD_DIM) + RMS_EPS)
        yn = o_blk * rs
        dy_raw = dy_ref[...]
        dyn = dy_raw * g_ref[...]
        d_o = rs * (dyn - yn * (_group_sums(dyn * yn, lane_lo) * (1.0 / HEAD_DIM)))
        _accumulate(gg_ref, i, n_steps, _rows8(dy_raw * yn))
        do_bf = d_o.astype(BF16)
        do_o = do_bf.astype(F32) * o_blk
        q = q_ref[...]
        blocks = [slice(b * Q_BLOCK, (b + 1) * Q_BLOCK) for b in range(Q_GROUP)]
        lane_lo_q = lane_lo[:Q_BLOCK]
        totals = [(jnp.sum(jnp.where(lane_lo_q, do_o[r], 0.0), axis=-1, keepdims=True),
                   jnp.sum(jnp.where(lane_lo_q, 0.0, do_o[r]), axis=-1, keepdims=True)) for r in blocks]
        q_rows = [_head_rows(q[r], lane_lo_q) for r in blocks]
        do_rows = [_head_rows(do_bf[r], lane_lo_q) for r in blocks]
        q_scaled = [qr * jnp.asarray(scale, BF16) for qr in q_rows]

        def step(starts, masks, states):
            kcs = [k_ref[pl.ds(r, KEY_CHUNK), :] for r in starts]
            vcs = [v_ref[pl.ds(r, KEY_CHUNK), :] for r in starts]
            head = lambda h: slice(h * Q_BLOCK, (h + 1) * Q_BLOCK)
            zs = _scores_phase([q_scaled[b][head(h)] for b, h in CHAINS], [kcs[b] for b, _ in CHAINS],
                               [masks[b] for b, _ in CHAINS])
            das = [lax.dot_general(do_rows[b][head(h)], vcs[b], (_NT, ((), ())), preferred_element_type=F32)
                   for b, h in CHAINS]
            es, lks, lbs = _logs_phase(zs)
            suffixes, carries = _suffix_phase(lks, [states[b][0][h] for b, h in CHAINS], umat)
            a_bf = [jnp.exp(lb + sfx).astype(BF16) for lb, sfx in zip(lbs, suffixes)]
            gs = [a.astype(F32) * da for a, da in zip(a_bf, das)]
            suffixes_g, carries_g = _suffix_phase(gs, [states[b][1][h] for b, h in CHAINS], gmat)
            dzs = []
            for (b, h), z, e, g, sfx_g in zip(CHAINS, zs, es, gs, suffixes_g):
                before = totals[b][h] - sfx_g
                inv = 1.0 / (1.0 + e)
                sig = jnp.where(z >= 0.0, inv, e * inv)
                dzs.append(((g * (1.0 - sig) - before * sig) * scale).astype(BF16))
            new = []
            for b in range(Q_GROUP):
                pair = slice(2 * b, 2 * b + 2)
                dq_acc = states[b][2] + jnp.dot(jnp.concatenate(dzs[pair], axis=1), _head_rows(kcs[b], lane_lo_keys),
                                                preferred_element_type=F32)
                dk_s[pl.ds(starts[b], KEY_CHUNK), :] += lax.dot_general(
                    jnp.concatenate(dzs[pair], axis=0), q_rows[b], (_TN, ((), ())), preferred_element_type=F32)
                dv_s[pl.ds(starts[b], KEY_CHUNK), :] += lax.dot_general(
                    jnp.concatenate(a_bf[pair], axis=0), do_rows[b], (_TN, ((), ())), preferred_element_type=F32)
                new.append(((carries[2 * b], carries[2 * b + 1]), (carries_g[2 * b], carries_g[2 * b + 1]), dq_acc))
            return new

        zero = jnp.zeros((Q_BLOCK, LANES), F32)
        states = _sweep(i * Q_GROUP, step, [((zero, zero), (zero, zero), zero)] * Q_GROUP, lambda st: st[0])
        dq_ref[...] = jnp.concatenate([st[2] for st in states], axis=0).astype(BF16)

        @pl.when(i == n_steps - 1)
        def _():
            dk_ref[...] = dk_s[...].astype(BF16)
            dv_ref[...] = dv_s[...].astype(BF16)

        @pl.when((p == npair - 1) & (i == n_steps - 1))
        def _():
            for cp in copies:
                cp.wait()

    whole = pl.BlockSpec((t_len, LANES), lambda p, i: (0, p))
    hbm = pl.BlockSpec(memory_space=pl.ANY)
    outs = pl.pallas_call(
        body, name="attn_bwd", grid=(npair, n_steps),
        in_specs=[pl.BlockSpec((rows_per_step, LANES), lambda p, i: (i, p)),
                  pl.BlockSpec((t_len, LANES), lambda p, i: (0, npair + p)),
                  pl.BlockSpec((t_len, LANES), lambda p, i: (0, 2 * npair + p)),
                  pl.BlockSpec((rows_per_step, LANES), lambda p, i: (i, p)),
                  pl.BlockSpec((rows_per_step, LANES), lambda p, i: (i, npair + p)),
                  pl.BlockSpec((1, LANES), lambda p, i: (0, p)),
                  pl.BlockSpec((2 * LANES, 2 * LANES), lambda p, i: (0, 0)),
                  pl.BlockSpec((2 * LANES, 2 * LANES), lambda p, i: (0, 0))] + [hbm] * n_ex,
        out_specs=[pl.BlockSpec((rows_per_step, LANES), lambda p, i: (i, p)), whole, whole,
                   pl.BlockSpec((8, LANES), lambda p, i: (0, p))] + [hbm] * n_ex,
        out_shape=[jax.ShapeDtypeStruct((t_len, width), BF16)] * 3 + [jax.ShapeDtypeStruct((8, width), F32)]
        + _chip_landing(exchange),
        scratch_shapes=[pltpu.VMEM((t_len, LANES), F32), pltpu.VMEM((t_len, LANES), F32)] + _chip_scratch(n_ex),
        compiler_params=_params(("arbitrary", "arbitrary")),
    )(qkv, qkv, qkv, o, dymix, g_attn, _cumsum_matrix(False), _cumsum_matrix(True), *exchange)
    return outs[0], outs[1], outs[2], outs[3], outs[4:]


def kernel(x, w_in, conv_w, g_conv, g_attn, w_out, ln1_g, ln1_b, w_up, w_down, ln2_g, ln2_b, loss_target, m_w_in, m_conv_w, m_g_conv, m_g_attn, m_w_out, m_ln1_g, m_ln1_b, m_w_up, m_w_down, m_ln2_g, m_ln2_b, v_w_in, v_conv_w, v_g_conv, v_g_attn, v_w_out, v_ln1_g, v_ln1_b, v_w_up, v_w_down, v_ln2_g, v_ln2_b):
    x2, target = x[0], loss_target[0]
    t_len, d_model = x2.shape
    width = g_conv.shape[1]
    in_shard = w_in.shape[2]
    up_shard = w_up.shape[2]
    d_ff = up_shard * N_DEV
    cw = conv_w.shape[2]
    xi, yi, ci = _place()
    dev = 4 * xi + 2 * yi + ci
    tm = 256

    win_g, wout_bf, wup_bf, wdown_bf = _all_gather(
        "gather_w_in", [w_in[0]], BF16, cast_only=[w_out[0], w_up[0], w_down[0]])
    (convw_g,) = _all_gather("gather_conv_w", [jnp.pad(conv_w[0], ((0, 5), (0, LANES - cw)))], F32)
    conv_full = jnp.transpose(convw_g[:, :3, :cw], (1, 0, 2)).reshape(3, width)

    n_row_tiles = t_len // tm
    rows = lambda cols: pl.BlockSpec((tm, cols), lambda i: (i, 0))
    resident = lambda arr: pl.BlockSpec(arr.shape, lambda i: (0,) * arr.ndim)
    vec = pl.BlockSpec((1, d_model), lambda i: (0, 0))
    acc8 = pl.BlockSpec((8, d_model), lambda i: (0, 0))
    tokens = lambda cols, dtype: jax.ShapeDtypeStruct((t_len, cols), dtype)

    def proj_in_epilogue(acc, extra, outs, i):
        outs[0][...] = acc[:, :3 * width]
        outs[1][...] = acc[:, 3 * width:].astype(BF16)
        outs[2][...] = extra[0][...].astype(BF16)

    gates, qkv, x_bf = _matmul(
        "proj_in", x2, win_g, rows(d_model), resident(win_g), _shards_wide, n_row_tiles,
        [(tokens(3 * width, F32), rows(3 * width)), (tokens(3 * width, BF16), rows(3 * width)),
         (tokens(d_model, BF16), rows(d_model))],
        proj_in_epilogue, extras=[(x2, rows(d_model))])

    ymix = _conv_fwd(gates, conv_full, g_conv, width)
    o_attn, ymix, (wout_g, wup_g, wdown_g) = _attn_fwd(qkv, g_attn, ymix, width, [wout_bf, wup_bf, wdown_bf])
    wout_full = wout_g.reshape(d_model, d_model)
    wdown_full = wdown_g.reshape(d_ff, d_model)

    def out_ln1_epilogue(acc, extra, outs, i):
        x_ref, g_ref, b_ref = extra
        r1 = ALPHA * x_ref[...] + acc
        x1, _, _ = _ln_fwd(r1, g_ref[...], b_ref[...])
        outs[0][...] = r1
        outs[1][...] = x1
        outs[2][...] = x1.astype(BF16)

    r1, x1, x1_bf = _matmul(
        "proj_out_ln1", ymix, wout_full, rows(d_model), resident(wout_full), _whole(_NN), n_row_tiles,
        [(tokens(d_model, F32), rows(d_model)), (tokens(d_model, F32), rows(d_model)),
         (tokens(d_model, BF16), rows(d_model))],
        out_ln1_epilogue, extras=[(x2, rows(d_model)), (ln1_g, vec), (ln1_b, vec)])

    def up_epilogue(acc, extra, outs, i):
        outs[0][...] = acc
        relu = jnp.maximum(acc, 0.0)
        outs[1][...] = (relu * relu).astype(BF16)

    up, hid = _matmul(
        "ffn_up", x1_bf, wup_g, rows(d_model), resident(wup_g), _shards_wide, n_row_tiles,
        [(tokens(d_ff, F32), rows(d_ff)), (tokens(d_ff, BF16), rows(d_ff))], up_epilogue)

    def down_ln2_epilogue(acc, extra, outs, i):
        x1_ref, t_ref, g_ref, b_ref = extra
        dr2_ref, dr2_bf_ref, loss_ref, gg_ref, gb_ref = outs
        gain = g_ref[...]
        r2 = ALPHA * x1_ref[...] + acc
        y, xhat, rstd = _ln_fwd(r2, gain, b_ref[...])
        diff = y - t_ref[...]
        d_y = diff * (1.0 / d_model)
        dr2 = _ln_bwd(d_y, xhat, rstd, gain)
        dr2_ref[...] = dr2
        dr2_bf_ref[...] = dr2.astype(BF16)
        _accumulate(gg_ref, i, n_row_tiles, _rows8(d_y * xhat))
        _accumulate(gb_ref, i, n_row_tiles, _rows8(d_y))
        _accumulate(loss_ref, i, n_row_tiles, _rows8(diff * diff), all_lanes=True)

    small_acc = (jax.ShapeDtypeStruct((8, d_model), F32), acc8)
    dr2, dr2_bf, loss_cols, g_ln2_g, g_ln2_b = _matmul(
        "ffn_down_ln2_loss", hid, wdown_full, rows(d_ff), resident(wdown_full), _whole(_NN), n_row_tiles,
        [(tokens(d_model, F32), rows(d_model)), (tokens(d_model, BF16), rows(d_model)),
         small_acc, small_acc, small_acc],
        down_ln2_epilogue, extras=[(x1, rows(d_model)), (target, rows(d_model)), (ln2_g, vec), (ln2_b, vec)])

    def dup_epilogue(acc, extra, outs, i):
        outs[0][...] = (acc * (2.0 * jnp.maximum(extra[0][...], 0.0))).astype(BF16)

    (dup,) = _matmul(
        "ffn_dhid", dr2_bf, wdown_full, rows(d_model), resident(wdown_full), _whole(_NT), n_row_tiles,
        [(tokens(d_ff, BF16), rows(d_ff))], dup_epilogue, extras=[(up, rows(d_ff))])

    def weight_grad(name, acts, cots, shard_shape, acts_by_shard):
        sr, sc = shard_shape
        per = max(1, 256 // sr) if acts_by_shard else 1
        tall = lambda cols: pl.BlockSpec((t_len, cols), lambda i: (0, i))
        if acts_by_shard:
            a_spec, b_spec = tall(per * sr), resident(cots)
        else:
            a_spec, b_spec = resident(acts), tall(sc)
        out_spec = pl.BlockSpec((per, sr, sc), lambda i: (i, 0, 0))

        def epilogue(acc, extra, outs, i):
            acc = acc.reshape(per, sr, sc)
            outs[0][...] = acc
            outs[1][...] = acc.astype(BF16)

        shape = (N_DEV, sr, sc)
        return _matmul(name, acts, cots, a_spec, b_spec, _whole(_TN), N_DEV // per,
                       [(jax.ShapeDtypeStruct(shape, F32), out_spec), (jax.ShapeDtypeStruct(shape, BF16), out_spec)],
                       epilogue)

    gw_down, gw_down_bf = weight_grad("grad_w_down", hid, dr2_bf, (d_ff // N_DEV, d_model), True)
    gw_up, gw_up_bf = weight_grad("grad_w_up", x1_bf, dup, (d_model, up_shard), False)

    def dx1_ln1_epilogue(acc, extra, outs, i):
        dr2_ref, r1_ref, g_ref, b_ref = extra
        dr1_ref, dr1_bf_ref, gg_ref, gb_ref = outs
        gain = g_ref[...]
        dx1 = ALPHA * dr2_ref[...] + acc
        _, xhat, rstd = _ln_fwd(r1_ref[...], gain, b_ref[...])
        dr1 = _ln_bwd(dx1, xhat, rstd, gain)
        dr1_ref[...] = dr1
        dr1_bf_ref[...] = dr1.astype(BF16)
        _accumulate(gg_ref, i, n_row_tiles, _rows8(dx1 * xhat))
        _accumulate(gb_ref, i, n_row_tiles, _rows8(dx1))

    dr1, dr1_bf, g_ln1_g, g_ln1_b = _matmul(
        "ffn_dx1_ln1", dup, wup_g, rows(d_ff), resident(wup_g), _shards_deep, n_row_tiles,
        [(tokens(d_model, F32), rows(d_model)), (tokens(d_model, BF16), rows(d_model)), small_acc, small_acc],
        dx1_ln1_epilogue, extras=[(dr2, rows(d_model)), (r1, rows(d_model)), (ln1_g, vec), (ln1_b, vec)])

    gw_out, gw_out_bf = weight_grad("grad_w_out", ymix, dr1_bf, (d_model // N_DEV, d_model), True)

    c_idx = jnp.reshape(ci, (1,)).astype(jnp.int32)
    q_me = jnp.reshape(2 * xi + yi, (1,)).astype(jnp.int32)

    def pair_stage(tag, names, grads_f32, grads_bf):
        landed = _pair_exchange("reduce_pair_" + tag, grads_bf)
        sums = [_pair_sum("pair_sum_" + nm, g, land, c_idx, q_me) for nm, g, land in zip(names, grads_f32, landed)]
        return [s[0] for s in sums], [s[1] for s in sums]

    wire_ffn, own_ffn = pair_stage("ffn", ["w_out", "w_up", "w_down"], [gw_out, gw_up, gw_down],
                                   [gw_out_bf, gw_up_bf, gw_down_bf])

    def plain_epilogue(acc, extra, outs, i):
        outs[0][...] = acc

    (dymix,) = _matmul(
        "proj_out_bwd", dr1_bf, wout_full, rows(d_model), resident(wout_full), _whole(_NT), n_row_tiles,
        [(tokens(d_model, F32), rows(d_model))], plain_epilogue)

    d_bg, d_cg, d_h, g_gconv, g_convw = _conv_bwd(gates, dymix, conv_full, g_conv, width)
    d_q, d_k, d_v, g_gattn, landed_ffn = _attn_bwd(qkv, o_attn, dymix, g_attn, width, wire_ffn)
    dproj = jnp.concatenate([d_bg, d_cg, d_h, d_q, d_k, d_v], axis=1)

    gw_in, gw_in_bf = weight_grad("grad_w_in", x_bf, dproj, (d_model, in_shard), False)
    wire_in, own_in = pair_stage("in", ["w_in"], [gw_in], [gw_in_bf])

    def dx_epilogue(acc, extra, outs, i):
        outs[0][...] = ALPHA * extra[0][...] + acc

    grad_x, landed_in = _matmul(
        "proj_in_bwd", dproj, win_g, rows(6 * width), resident(win_g), _shards_deep, n_row_tiles,
        [(tokens(d_model, F32), rows(d_model))], dx_epilogue, extras=[(dr1, rows(d_model))], exchange=wire_in)

    names = ["w_in", "w_out", "w_up", "w_down"]
    big = {}
    for nm, ow, land, w, m, v in zip(names, own_in + own_ffn, [landed_in] + list(landed_ffn),
                                     (w_in, w_out, w_up, w_down),
                                     (m_w_in, m_w_out, m_w_up, m_w_down), (v_w_in, v_w_out, v_w_up, v_w_down)):
        big[nm] = [r[None] for r in _shard_update("update_" + nm, ow, land, w[0], m[0], v[0])]

    spare = jnp.zeros((3, d_model - width), F32)

    def pack(l1g, l1b, l2g, l2b, gc, ga, taps, tail=spare):
        return jnp.concatenate([l1g, l1b, l2g, l2b, jnp.concatenate([gc, ga], axis=1),
                                jnp.concatenate([taps, tail], axis=1)], axis=0)

    local = pack(g_ln1_g[:1], g_ln1_b[:1], g_ln2_g[:1], g_ln2_b[:1], g_gconv[:1], g_gattn[:1], g_convw[:3],
                 spare + (0.5 / d_model) * loss_cols[0, 0])
    (parts,) = _all_gather("gather_small_grads", [local], F32)

    def spread(a):
        return lax.dynamic_update_slice(jnp.zeros((3, width), F32), a, (0, dev * cw))

    w_small = pack(ln1_g, ln1_b, ln2_g, ln2_b, g_conv, g_attn, spread(conv_w[0]))
    m_small = pack(m_ln1_g, m_ln1_b, m_ln2_g, m_ln2_b, m_g_conv, m_g_attn, spread(m_conv_w[0]))
    v_small = pack(v_ln1_g, v_ln1_b, v_ln2_g, v_ln2_b, v_g_conv, v_g_attn, spread(v_conv_w[0]))
    small = _small_update(parts, w_small, m_small, v_small)

    def unpack(a):
        taps = lax.dynamic_slice(a[5:8, :width], (0, dev * cw), (3, cw))[None]
        return {"ln1_g": a[0:1], "ln1_b": a[1:2], "ln2_g": a[2:3], "ln2_b": a[3:4],
                "g_conv": a[4:5, :width], "g_attn": a[4:5, width:], "conv_w": taps}

    loss = small[0][7, d_model - 1]
    small = [unpack(a) for a in small]

    order = ["w_in", "conv_w", "g_conv", "g_attn", "w_out", "ln1_g", "ln1_b", "w_up", "w_down", "ln2_g", "ln2_b"]
    result = [loss, grad_x[None]]
    for kind in range(4):
        for nm in order:
            result.append(big[nm][kind] if nm in big else small[kind][nm])
    return tuple(result)
```

```python
import functools

import jax
import jax.numpy as jnp
from jax import lax
from jax.experimental import pallas as pl
from jax.experimental.pallas import tpu as pltpu

F32 = jnp.float32
BF16 = jnp.bfloat16
MESH = pl.DeviceIdType.MESH

N_DEV = 8
HEAD_DIM = 64
LANES = 128
Q_BLOCK = 128
ALPHA = 2.0 ** 0.25
LN_EPS = 1e-5
RMS_EPS = 1e-6
ADAM_LR, ADAM_B1, ADAM_B2, ADAM_EPS, ADAM_WD, ADAM_STEP = 0.001, 0.9, 0.999, 1e-08, 0.01, 10
VMEM_LIMIT_BYTES = 48 * 1024 * 1024

_NT = ((1,), (1,))
_NN = ((1,), (0,))
_TN = ((0,), (0,))


def _params(sem=None):
    return pltpu.CompilerParams(dimension_semantics=sem, vmem_limit_bytes=VMEM_LIMIT_BYTES)


def _dot(a, b, contract):
    return lax.dot_general(a.astype(BF16), b.astype(BF16), (contract, ((), ())), preferred_element_type=F32)


def _whole(contract):
    return lambda a_ref, b_ref: _dot(a_ref[...], b_ref[...], contract)


def _shards_wide(a_ref, b_ref):
    a = a_ref[...].astype(BF16)
    return jnp.concatenate([_dot(a, b_ref[s], _NN) for s in range(b_ref.shape[0])], axis=1)


def _shards_deep(a_ref, b_ref):
    n = b_ref.shape[2]
    acc = _dot(a_ref[:, 0:n], b_ref[0], _NT)
    for s in range(1, b_ref.shape[0]):
        acc = acc + _dot(a_ref[:, s * n:(s + 1) * n], b_ref[s], _NT)
    return acc


def _matmul(name, a, b, a_spec, b_spec, product, n_steps, outs, epilogue, extras=(), exchange=(), gather=()):
    assert not (exchange and gather)
    behind = list(exchange) + list(gather)
    n_extra, n_out, n_b = len(extras), len(outs), len(behind)

    def body(a_ref, b_ref, *rest):
        i = pl.program_id(0)
        extra_refs, rest = rest[:n_extra], rest[n_extra:]
        src_refs, rest = rest[:n_b], rest[n_b:]
        out_refs, rest = rest[:n_out], rest[n_out:]
        land_refs, sems = rest[:n_b], rest[n_b:]
        if exchange:
            copies = _chip_copies(src_refs, land_refs, *sems)

            def start():
                for cp in copies:
                    cp.start()

            def finish():
                for cp in copies:
                    cp.wait()
        elif gather:
            start, forward, last = _gather_plan(src_refs, land_refs, *sems)

            def finish():
                forward()
                last()
        if behind:
            pl.when(i == 0)(start)
        epilogue(product(a_ref, b_ref), extra_refs, out_refs, i)
        if behind:
            pl.when(i == n_steps - 1)(finish)

    hbm = pl.BlockSpec(memory_space=pl.ANY)
    landing = _chip_landing(exchange) + [jax.ShapeDtypeStruct((N_DEV,) + s.shape, s.dtype) for s in gather]
    scratch = (_chip_scratch(len(exchange)) if exchange else []) + (_gather_scratch(len(gather)) if gather else [])
    return pl.pallas_call(
        body, name=name, grid=(n_steps,),
        in_specs=[a_spec, b_spec] + [s for _, s in extras] + [hbm] * n_b,
        out_specs=[s for _, s in outs] + [hbm] * n_b,
        out_shape=[o for o, _ in outs] + landing,
        scratch_shapes=scratch,
        compiler_params=_params(("arbitrary",)),
    )(a, b, *[e for e, _ in extras], *behind)


def _ln_fwd(r, g, b):
    mu = jnp.mean(r, axis=-1, keepdims=True)
    xc = r - mu
    rstd = lax.rsqrt(jnp.mean(xc * xc, axis=-1, keepdims=True) + LN_EPS)
    xhat = xc * rstd
    return xhat * g + b, xhat, rstd


def _ln_bwd(dy, xhat, rstd, g):
    dxh = dy * g
    m1 = jnp.mean(dxh, axis=-1, keepdims=True)
    m2 = jnp.mean(dxh * xhat, axis=-1, keepdims=True)
    return rstd * (dxh - m1 - xhat * m2)


def _rows8(v):
    n, c = v.shape
    return jnp.sum(v.reshape(n // 8, 8, c), axis=0)


def _accumulate(ref, i, n_steps, part8, all_lanes=False):
    @pl.when(i == 0)
    def _():
        ref[...] = part8

    @pl.when(i > 0)
    def _():
        ref[...] += part8

    @pl.when(i == n_steps - 1)
    def _():
        tot = jnp.sum(ref[...], axis=0, keepdims=True)
        if all_lanes:
            tot = jnp.sum(tot, axis=1, keepdims=True)
        ref[...] = jnp.broadcast_to(tot, ref.shape)


def _group_sums(v, lane_lo):
    s0 = jnp.sum(jnp.where(lane_lo, v, 0.0), axis=-1, keepdims=True)
    s1 = jnp.sum(jnp.where(lane_lo, 0.0, v), axis=-1, keepdims=True)
    return jnp.where(lane_lo, s0, s1)


def _place():
    return lax.axis_index("x"), lax.axis_index("y"), lax.axis_index("c")


def _gather_plan(src_refs, out_refs, send_sems, recv_sems, local_sems):
    n = len(src_refs)
    x, y, c = _place()
    me, sibling = (x, y, c), (x, y, 1 - c)
    chips = [(1 - x, y), (x, 1 - y), (1 - x, 1 - y)]

    def copy(a, k, block, to, src=None):
        slot = out_refs[a].at[4 * block[0] + 2 * block[1] + block[2]]
        return pltpu.make_async_remote_copy(
            src_ref=slot if src is None else src, dst_ref=slot,
            send_sem=send_sems.at[a, k], recv_sem=recv_sems.at[a, k],
            device_id=to, device_id_type=MESH)

    mine = [pltpu.make_async_copy(src_refs[a], out_refs[a].at[4 * x + 2 * y + c], local_sems.at[a]) for a in range(n)]
    first = [[copy(a, 0, me, sibling, src=src_refs[a])]
             + [copy(a, 1 + j, me, (*chip, c), src=src_refs[a]) for j, chip in enumerate(chips)] for a in range(n)]
    passed = [[copy(a, 4 + j, (*chip, c), sibling) for j, chip in enumerate(chips)] for a in range(n)]

    def start():
        for a in range(n):
            mine[a].start()
            for cp in first[a]:
                cp.start()

    def forward():
        for j, chip in enumerate(chips):
            for a in range(n):
                copy(a, 1 + j, (*chip, c), me).wait_recv()
                passed[a][j].start()

    def finish():
        for a in range(n):
            copy(a, 0, sibling, me).wait_recv()
            for j, chip in enumerate(chips):
                copy(a, 4 + j, (*chip, 1 - c), me).wait_recv()
        for a in range(n):
            for cp in first[a] + passed[a]:
                cp.wait_send()
            mine[a].wait()

    return start, forward, finish


def _gather_scratch(n):
    return [pltpu.SemaphoreType.DMA((n, 7)), pltpu.SemaphoreType.DMA((n, 7)), pltpu.SemaphoreType.DMA((n,))]


def _all_gather(name, shards, out_dtype, cast_only=()):
    n, m = len(shards), len(cast_only)

    def body(*refs):
        in_refs, cast_in = refs[:n], refs[n:n + m]
        out_refs, cast_out = refs[n + m:2 * n + m], refs[2 * n + m:2 * (n + m)]
        stage = refs[2 * (n + m):3 * n + 2 * m]
        for a in range(n):
            stage[a][...] = in_refs[a][...].astype(out_dtype)
        start, forward, finish = _gather_plan(stage, out_refs, *refs[3 * n + 2 * m:])
        start()
        for a in range(m):
            cast_out[a][...] = cast_in[a][...].astype(out_dtype)
        forward()
        finish()

    return pl.pallas_call(
        body, name=name,
        in_specs=[pl.BlockSpec(memory_space=pltpu.VMEM)] * (n + m),
        out_specs=[pl.BlockSpec(memory_space=pl.ANY)] * n + [pl.BlockSpec(memory_space=pltpu.VMEM)] * m,
        out_shape=[jax.ShapeDtypeStruct((N_DEV,) + s.shape, out_dtype) for s in shards]
        + [jax.ShapeDtypeStruct(s.shape, out_dtype) for s in cast_only],
        scratch_shapes=[pltpu.VMEM(s.shape, out_dtype) for s in shards] + _gather_scratch(n),
        compiler_params=pltpu.CompilerParams(vmem_limit_bytes=VMEM_LIMIT_BYTES),
    )(*shards, *cast_only)


def _pair_exchange(name, grads):
    n = len(grads)

    def body(*refs):
        g_refs, land_refs = refs[:n], refs[n:2 * n]
        send_sems, recv_sems = refs[2 * n:]
        x, y, c = _place()
        copies = []
        for a in range(n):
            for q in range(4):
                cp = pltpu.make_async_remote_copy(
                    src_ref=g_refs[a].at[2 * q + (1 - c)], dst_ref=land_refs[a].at[q],
                    send_sem=send_sems.at[a, q], recv_sem=recv_sems.at[a, q],
                    device_id=(x, y, 1 - c), device_id_type=MESH)
                cp.start()
                copies.append(cp)
        for cp in copies:
            cp.wait()

    return pl.pallas_call(
        body, name=name,
        in_specs=[pl.BlockSpec(memory_space=pl.ANY)] * n,
        out_specs=[pl.BlockSpec(memory_space=pl.ANY)] * n,
        out_shape=[jax.ShapeDtypeStruct((4,) + g.shape[1:], g.dtype) for g in grads],
        scratch_shapes=[pltpu.SemaphoreType.DMA((n, 4)), pltpu.SemaphoreType.DMA((n, 4))],
    )(*grads)


def _chip_copies(p_refs, land_refs, send_sems, recv_sems):
    x, y, c = _place()
    chips = [(1 - x, y), (x, 1 - y), (1 - x, 1 - y)]
    return [pltpu.make_async_remote_copy(
        src_ref=p_refs[a].at[2 * px + py], dst_ref=land_refs[a].at[k],
        send_sem=send_sems.at[a, k], recv_sem=recv_sems.at[a, k],
        device_id=(px, py, c), device_id_type=MESH)
        for a in range(len(p_refs)) for k, (px, py) in enumerate(chips)]


def _chip_scratch(n):
    return [pltpu.SemaphoreType.DMA((n, 3)), pltpu.SemaphoreType.DMA((n, 3))]


def _chip_landing(pair_sums):
    return [jax.ShapeDtypeStruct((3,) + p.shape[1:], p.dtype) for p in pair_sums]


def _pair_sum(name, g_f32, landed, c_idx, q_me):
    _, rows, cols = g_f32.shape
    tr = min(rows, 256)

    def body(c_ref, q_ref, g_ref, l_ref, wire_ref, own_ref):
        q = pl.program_id(1)
        s = g_ref[...] + l_ref[...].astype(F32)
        wire_ref[...] = s.astype(BF16)

        @pl.when(q == q_ref[0])
        def _():
            own_ref[...] = s

    return pl.pallas_call(
        body, name=name,
        grid_spec=pltpu.PrefetchScalarGridSpec(
            num_scalar_prefetch=2, grid=(rows // tr, 4),
            in_specs=[pl.BlockSpec((None, tr, cols), lambda i, q, c_ref, q_ref: (2 * q + c_ref[0], i, 0)),
                      pl.BlockSpec((None, tr, cols), lambda i, q, c_ref, q_ref: (q, i, 0))],
            out_specs=[pl.BlockSpec((None, tr, cols), lambda i, q, c_ref, q_ref: (q, i, 0)),
                       pl.BlockSpec((tr, cols), lambda i, q, c_ref, q_ref: (i, 0))]),
        out_shape=[jax.ShapeDtypeStruct((4, rows, cols), BF16), jax.ShapeDtypeStruct((rows, cols), F32)],
        compiler_params=_params(("parallel", "arbitrary")),
    )(c_idx, q_me, g_f32, landed)


def _adamw(w, g, m, v):
    m = ADAM_B1 * m + (1.0 - ADAM_B1) * g
    v = ADAM_B2 * v + (1.0 - ADAM_B2) * (g * g)
    m_hat = m / (1.0 - ADAM_B1 ** ADAM_STEP)
    v_hat = v / (1.0 - ADAM_B2 ** ADAM_STEP)
    delta = -ADAM_LR * (m_hat / (jnp.sqrt(v_hat) + ADAM_EPS) + ADAM_WD * w)
    return delta, m, v


def _shard_update(name, own, landed, w, m, v):
    rows, cols = own.shape
    tr = min(rows, 256)

    def body(own_ref, l_ref, w_ref, m_ref, v_ref, g_out, d_out, m_out, v_out):
        g = own_ref[...] + l_ref[0].astype(F32) + l_ref[1].astype(F32) + l_ref[2].astype(F32)
        d, m_new, v_new = _adamw(w_ref[...], g, m_ref[...], v_ref[...])
        g_out[...] = g
        d_out[...] = d
        m_out[...] = m_new
        v_out[...] = v_new

    tile = pl.BlockSpec((tr, cols), lambda i: (i, 0))
    return pl.pallas_call(
        body, name=name, grid=(rows // tr,),
        in_specs=[tile, pl.BlockSpec((3, tr, cols), lambda i: (0, i, 0)), tile, tile, tile],
        out_specs=[tile] * 4,
        out_shape=[jax.ShapeDtypeStruct((rows, cols), F32)] * 4,
        compiler_params=_params(("parallel",)),
    )(own, landed, w, m, v)


def _small_update(parts, w, m, v):
    def body(p_ref, w_ref, m_ref, v_ref, g_out, d_out, m_out, v_out):
        g = p_ref[0]
        for d in range(1, N_DEV):
            g = g + p_ref[d]
        dl, m_new, v_new = _adamw(w_ref[...], g, m_ref[...], v_ref[...])
        g_out[...] = g
        d_out[...] = dl
        m_out[...] = m_new
        v_out[...] = v_new

    return pl.pallas_call(
        body, name="small_update",
        out_shape=[jax.ShapeDtypeStruct(w.shape, F32)] * 4,
        compiler_params=pltpu.CompilerParams(vmem_limit_bytes=VMEM_LIMIT_BYTES),
    )(parts, w, m, v)


CONV_CHUNK = 512


def _conv_taps(u_s, lo, chunk):
    ext = u_s[pl.ds(lo, chunk + 8), :]
    return ext[8:], pltpu.roll(ext, 1, 0)[8:], pltpu.roll(ext, 2, 0)[8:]


def _conv_fwd(gates, conv_w, g_conv, width):
    t_len = gates.shape[0]
    nblk = width // LANES
    n_chunks = t_len // CONV_CHUNK

    def body(bg_ref, cg_ref, h_ref, w_ref, g_ref, y_ref, u_s):
        lane_lo = lax.broadcasted_iota(jnp.int32, (CONV_CHUNK, LANES), 1) < HEAD_DIM
        u_s[0:8, :] = jnp.zeros((8, LANES), F32)

        def fill(r, _):
            lo = pl.multiple_of(r * CONV_CHUNK, CONV_CHUNK)
            u_s[pl.ds(pl.multiple_of(lo + 8, 8), CONV_CHUNK), :] = cg_ref[pl.ds(lo, CONV_CHUNK), :] * h_ref[pl.ds(lo, CONV_CHUNK), :]
            return 0

        lax.fori_loop(0, n_chunks, fill, 0)
        w = w_ref[...]
        gain = g_ref[...]

        def step(r, _):
            lo = pl.multiple_of(r * CONV_CHUNK, CONV_CHUNK)
            u0, u1, u2 = _conv_taps(u_s, lo, CONV_CHUNK)
            y = bg_ref[pl.ds(lo, CONV_CHUNK), :] * (w[2:3] * u0 + w[1:2] * u1 + w[0:1] * u2)
            rs = lax.rsqrt(_group_sums(y * y, lane_lo) * (1.0 / HEAD_DIM) + RMS_EPS)
            y_ref[pl.ds(lo, CONV_CHUNK), :] = (y * rs * gain).astype(BF16)
            return 0

        lax.fori_loop(0, n_chunks, step, 0)

    col = lambda off: pl.BlockSpec((t_len, LANES), lambda p: (0, off + p))
    return pl.pallas_call(
        body, name="conv_fwd", grid=(nblk,),
        in_specs=[col(0), col(nblk), col(2 * nblk),
                  pl.BlockSpec((3, LANES), lambda p: (0, p)), pl.BlockSpec((1, LANES), lambda p: (0, p))],
        out_specs=col(0),
        out_shape=jax.ShapeDtypeStruct((t_len, 2 * width), BF16),
        scratch_shapes=[pltpu.VMEM((t_len + 8, LANES), F32)],
        compiler_params=_params(("parallel",)),
    )(gates, gates, gates, conv_w, g_conv)


def _conv_bwd(gates, dymix, conv_w, g_conv, width):
    t_len = gates.shape[0]
    nblk = width // LANES
    n_chunks = t_len // CONV_CHUNK

    def body(bg_ref, cg_ref, h_ref, dy_ref, w_ref, g_ref, dbg_ref, dcg_ref, dh_ref, gg_ref, gw_ref, u_s, dc_s):
        lane_lo = lax.broadcasted_iota(jnp.int32, (CONV_CHUNK, LANES), 1) < HEAD_DIM
        u_s[0:8, :] = jnp.zeros((8, LANES), F32)
        dc_s[t_len:t_len + 8, :] = jnp.zeros((8, LANES), F32)

        def fill(r, _):
            lo = pl.multiple_of(r * CONV_CHUNK, CONV_CHUNK)
            u_s[pl.ds(pl.multiple_of(lo + 8, 8), CONV_CHUNK), :] = cg_ref[pl.ds(lo, CONV_CHUNK), :] * h_ref[pl.ds(lo, CONV_CHUNK), :]
            return 0

        lax.fori_loop(0, n_chunks, fill, 0)
        w = w_ref[...]
        gain = g_ref[...]

        def step(r, acc):
            gg, gw0, gw1, gw2 = acc
            lo = pl.multiple_of(r * CONV_CHUNK, CONV_CHUNK)
            u0, u1, u2 = _conv_taps(u_s, lo, CONV_CHUNK)
            bg = bg_ref[pl.ds(lo, CONV_CHUNK), :]
            conv = w[2:3] * u0 + w[1:2] * u1 + w[0:1] * u2
            y = bg * conv
            rs = lax.rsqrt(_group_sums(y * y, lane_lo) * (1.0 / HEAD_DIM) + RMS_EPS)
            yn = y * rs
            dyn_raw = dy_ref[pl.ds(lo, CONV_CHUNK), :]
            dyn = dyn_raw * gain
            d_y = rs * (dyn - yn * (_group_sums(dyn * yn, lane_lo) * (1.0 / HEAD_DIM)))
            dbg_ref[pl.ds(lo, CONV_CHUNK), :] = (d_y * conv).astype(BF16)
            dconv = d_y * bg
            dc_s[pl.ds(lo, CONV_CHUNK), :] = dconv
            return (gg + _rows8(dyn_raw * yn), gw0 + _rows8(dconv * u2), gw1 + _rows8(dconv * u1),
                    gw2 + _rows8(dconv * u0))

        zero = jnp.zeros((8, LANES), F32)
        gg, gw0, gw1, gw2 = lax.fori_loop(0, n_chunks, step, (zero, zero, zero, zero))
        gg_ref[...] = jnp.broadcast_to(jnp.sum(gg, axis=0, keepdims=True), (8, LANES))
        row = lax.broadcasted_iota(jnp.int32, (8, LANES), 0)
        tot = lambda v: jnp.broadcast_to(jnp.sum(v, axis=0, keepdims=True), (8, LANES))
        gw_ref[...] = jnp.where(row == 0, tot(gw0), jnp.where(row == 1, tot(gw1), jnp.where(row == 2, tot(gw2), 0.0)))

        def back(r, _):
            lo = pl.multiple_of(r * CONV_CHUNK, CONV_CHUNK)
            ext = dc_s[pl.ds(lo, CONV_CHUNK + 8), :]
            d0 = ext[:CONV_CHUNK]
            d1 = pltpu.roll(ext, CONV_CHUNK + 7, 0)[:CONV_CHUNK]
            d2 = pltpu.roll(ext, CONV_CHUNK + 6, 0)[:CONV_CHUNK]
            du = w[2:3] * d0 + w[1:2] * d1 + w[0:1] * d2
            dcg_ref[pl.ds(lo, CONV_CHUNK), :] = (du * h_ref[pl.ds(lo, CONV_CHUNK), :]).astype(BF16)
            dh_ref[pl.ds(lo, CONV_CHUNK), :] = (du * cg_ref[pl.ds(lo, CONV_CHUNK), :]).astype(BF16)
            return 0

        lax.fori_loop(0, n_chunks, back, 0)

    col = lambda off: pl.BlockSpec((t_len, LANES), lambda p: (0, off + p))
    small = pl.BlockSpec((8, LANES), lambda p: (0, p))
    return pl.pallas_call(
        body, name="conv_bwd", grid=(nblk,),
        in_specs=[col(0), col(nblk), col(2 * nblk), col(0),
                  pl.BlockSpec((3, LANES), lambda p: (0, p)), pl.BlockSpec((1, LANES), lambda p: (0, p))],
        out_specs=[col(0), col(0), col(0), small, small],
        out_shape=[jax.ShapeDtypeStruct((t_len, width), BF16)] * 3 + [jax.ShapeDtypeStruct((8, width), F32)] * 2,
        scratch_shapes=[pltpu.VMEM((t_len + 8, LANES), F32), pltpu.VMEM((t_len + 8, LANES), F32)],
        compiler_params=_params(("parallel",)),
    )(gates, gates, gates, dymix, conv_w, g_conv)


def _cumsum_matrix(inclusive):
    j = lax.broadcasted_iota(jnp.int32, (2 * LANES, 2 * LANES), 0) % LANES
    s = lax.broadcasted_iota(jnp.int32, (2 * LANES, 2 * LANES), 1)
    keep = (s >= LANES) | ((j >= s) if inclusive else (j > s))
    return keep.astype(BF16)


KEY_CHUNK = 384
SUB_TILES = KEY_CHUNK // Q_BLOCK
Q_GROUP = 2
DEAD_BELOW = -105.0
MASKED_Z = -1e30


def _scores_phase(qms, kcs, masks):
    zs = [lax.dot_general(q, k, (_NT, ((), ())), preferred_element_type=F32) for q, k in zip(qms, kcs)]
    return [jnp.where(m, z, MASKED_Z) for z, m in zip(zs, masks)]


def _logs_phase(zs):
    es = [jnp.exp(-jnp.abs(z)) for z in zs]
    lks = [-(jnp.maximum(z, 0.0) + jnp.log(1.0 + e)) for z, e in zip(zs, es)]
    return es, lks, [z + lk for z, lk in zip(zs, lks)]


def _suffix_phase(vals, carries, mat):
    def hi_lo(tile):
        hi = tile.astype(BF16)
        return jnp.concatenate([hi, (tile - hi.astype(F32)).astype(BF16)], axis=1)

    packed = [[hi_lo(v[:, t * Q_BLOCK:(t + 1) * Q_BLOCK]) for t in range(SUB_TILES)] for v in vals]
    prods = [[jnp.dot(p, mat, preferred_element_type=F32) for p in chain] for chain in packed]
    sums, new_carries = [], []
    for chain, carry in zip(prods, carries):
        parts = []
        for t in reversed(range(SUB_TILES)):
            parts.append(chain[t][:, :LANES] + carry)
            carry = carry + chain[t][:, LANES:]
        sums.append(jnp.concatenate(parts[::-1], axis=1))
        new_carries.append(carry)
    return sums, new_carries


def _sweep(first_block, step, states, carries_of):
    n_chunks = ((first_block + Q_GROUP) * Q_BLOCK + KEY_CHUNK - 1) // KEY_CHUNK
    col = lax.broadcasted_iota(jnp.int32, (Q_BLOCK, KEY_CHUNK), 1)
    row = lax.broadcasted_iota(jnp.int32, (Q_BLOCK, 1), 0)

    def body(loop):
        n, _, sts = loop
        starts, masks = [], []
        for b in range(Q_GROUP):
            i = first_block + b
            end = (i + 1) * Q_BLOCK - n * KEY_CHUNK
            starts.append(pl.multiple_of(jnp.maximum(end - KEY_CHUNK, 0), Q_BLOCK))
            masks.append(col < jnp.minimum(i * Q_BLOCK + row, end) - starts[b])
        sts = step(starts, masks, sts)
        lowest = None
        for st in sts:
            for carry in carries_of(st):
                lowest = carry if lowest is None else jnp.maximum(lowest, carry)
        alive = (jnp.max(lowest) >= DEAD_BELOW).astype(jnp.int32)
        return n + 1, alive, tuple(sts)

    return lax.while_loop(lambda loop: (loop[0] < n_chunks) & (loop[1] > 0), body,
                          (jnp.int32(0), jnp.int32(1), tuple(states)))[2]


CHAINS = [(b, h) for b in range(Q_GROUP) for h in range(2)]


def _head_rows(block, lane_lo_rows):
    zero = jnp.zeros_like(block)
    return jnp.concatenate([jnp.where(lane_lo_rows, block, zero), jnp.where(lane_lo_rows, zero, block)], axis=0)


def _attn_fwd(qkv, g_attn, ymix_in, width, shards):
    t_len = qkv.shape[0]
    npair = width // LANES
    rows_per_step = Q_GROUP * Q_BLOCK
    n_steps = t_len // rows_per_step
    n_sh = len(shards)

    def body(q_ref, k_ref, v_ref, g_ref, umat_ref, ymix_in_ref, *rest):
        shard_refs, rest = rest[:n_sh], rest[n_sh:]
        o_ref, ymix_ref = rest[:2]
        gathered_refs, sems = rest[2:2 + n_sh], rest[2 + n_sh:]
        p, i = pl.program_id(0), pl.program_id(1)
        start, forward, finish = _gather_plan(shard_refs, gathered_refs, *sems)
        pl.when((p == 0) & (i == 0))(start)
        pl.when((p == npair - 1) & (i == n_steps // 2))(forward)
        lane_lo = lax.broadcasted_iota(jnp.int32, (rows_per_step, LANES), 1) < HEAD_DIM
        lane_lo_keys = lax.broadcasted_iota(jnp.int32, (KEY_CHUNK, LANES), 1) < HEAD_DIM
        umat = umat_ref[...]
        q = q_ref[...] * jnp.asarray(HEAD_DIM ** -0.5, BF16)
        q_heads = (jnp.where(lane_lo, q, jnp.zeros_like(q)), jnp.where(lane_lo, jnp.zeros_like(q), q))

        def step(starts, masks, states):
            kcs = [k_ref[pl.ds(r, KEY_CHUNK), :] for r in starts]
            zs = _scores_phase([q_heads[h][b * Q_BLOCK:(b + 1) * Q_BLOCK] for b, h in CHAINS],
                               [kcs[b] for b, _ in CHAINS], [masks[b] for b, _ in CHAINS])
            _, lks, lbs = _logs_phase(zs)
            suffixes, carries = _suffix_phase(lks, [states[b][0][h] for b, h in CHAINS], umat)
            weights = [jnp.exp(lb + sfx).astype(BF16) for lb, sfx in zip(lbs, suffixes)]
            new = []
            for b in range(Q_GROUP):
                v_heads = _head_rows(v_ref[pl.ds(starts[b], KEY_CHUNK), :], lane_lo_keys)
                o_acc = states[b][1] + jnp.dot(jnp.concatenate(weights[2 * b:2 * b + 2], axis=1), v_heads,
                                               preferred_element_type=F32)
                new.append(((carries[2 * b], carries[2 * b + 1]), o_acc))
            return new

        zero = jnp.zeros((Q_BLOCK, LANES), F32)
        states = _sweep(i * Q_GROUP, step, [((zero, zero), zero)] * Q_GROUP, lambda st: st[0])
        o = jnp.concatenate([st[1] for st in states], axis=0)
        o_ref[...] = o
        rs = lax.rsqrt(_group_sums(o * o, lane_lo) * (1.0 / HEAD_DIM) + RMS_EPS)
        ymix_ref[...] = (o * rs * g_ref[...]).astype(BF16)
        pl.when((p == npair - 1) & (i == n_steps - 1))(finish)

    hbm = pl.BlockSpec(memory_space=pl.ANY)
    outs = pl.pallas_call(
        body, name="attn_fwd", grid=(npair, n_steps),
        in_specs=[pl.BlockSpec((rows_per_step, LANES), lambda p, i: (i, p)),
                  pl.BlockSpec((t_len, LANES), lambda p, i: (0, npair + p)),
                  pl.BlockSpec((t_len, LANES), lambda p, i: (0, 2 * npair + p)),
                  pl.BlockSpec((1, LANES), lambda p, i: (0, p)),
                  pl.BlockSpec((2 * LANES, 2 * LANES), lambda p, i: (0, 0)),
                  hbm] + [hbm] * n_sh,
        out_specs=[pl.BlockSpec((rows_per_step, LANES), lambda p, i: (i, p)),
                   pl.BlockSpec((rows_per_step, LANES), lambda p, i: (i, npair + p))] + [hbm] * n_sh,
        out_shape=[jax.ShapeDtypeStruct((t_len, width), F32), jax.ShapeDtypeStruct(ymix_in.shape, BF16)]
        + [jax.ShapeDtypeStruct((N_DEV,) + s.shape, s.dtype) for s in shards],
        input_output_aliases={5: 1},
        scratch_shapes=_gather_scratch(n_sh),
        compiler_params=_params(("arbitrary", "arbitrary")),
    )(qkv, qkv, qkv, g_attn, _cumsum_matrix(False), ymix_in, *shards)
    return outs[0], outs[1], outs[2:]


def _attn_bwd(qkv, o, dymix, g_attn, width, exchange):
    t_len = qkv.shape[0]
    npair = width // LANES
    rows_per_step = Q_GROUP * Q_BLOCK
    n_steps = t_len // rows_per_step
    scale = HEAD_DIM ** -0.5
    n_ex = len(exchange)

    def body(q_ref, k_ref, v_ref, o_ref, dy_ref, g_ref, umat_ref, gmat_ref, *rest):
        p_refs, rest = rest[:n_ex], rest[n_ex:]
        dq_ref, dk_ref, dv_ref, gg_ref = rest[:4]
        land_refs, rest = rest[4:4 + n_ex], rest[4 + n_ex:]
        dk_s, dv_s = rest[:2]
        copies = _chip_copies(p_refs, land_refs, *rest[2:])
        p, i = pl.program_id(0), pl.program_id(1)

        @pl.when((p == 0) & (i == 0))
        def _():
            for cp in copies:
                cp.start()

        lane_lo = lax.broadcasted_iota(jnp.int32, (rows_per_step, LANES), 1) < HEAD_DIM
        lane_lo_keys = lax.broadcasted_iota(jnp.int32, (KEY_CHUNK, LANES), 1) < HEAD_DIM
        umat, gmat = umat_ref[...], gmat_ref[...]

        @pl.when(i == 0)
        def _():
            dk_s[...] = jnp.zeros_like(dk_s)
            dv_s[...] = jnp.zeros_like(dv_s)

        o_blk = o_ref[...]
        rs = lax.rsqrt(_group_sums(o_blk * o_blk, lane_lo) * (1.0 / HEAD_DIM) + RMS_EPS)
        yn = o_blk * rs
        dy_raw = dy_ref[...]
        dyn = dy_raw * g_ref[...]
        d_o = rs * (dyn - yn * (_group_sums(dyn * yn, lane_lo) * (1.0 / HEAD_DIM)))
        _accumulate(gg_ref, i, n_steps, _rows8(dy_raw * yn))
        do_bf = d_o.astype(BF16)
        do_o = do_bf.astype(F32) * o_blk
        q = q_ref[...]
        blocks = [slice(b * Q_BLOCK, (b + 1) * Q_BLOCK) for b in range(Q_GROUP)]
        lane_lo_q = lane_lo[:Q_BLOCK]
        totals = [(jnp.sum(jnp.where(lane_lo_q, do_o[r], 0.0), axis=-1, keepdims=True),
                   jnp.sum(jnp.where(lane_lo_q, 0.0, do_o[r]), axis=-1, keepdims=True)) for r in blocks]
        q_rows = [_head_rows(q[r], lane_lo_q) for r in blocks]
        do_rows = [_head_rows(do_bf[r], lane_lo_q) for r in blocks]
        q_scaled = [qr * jnp.asarray(scale, BF16) for qr in q_rows]

        def step(starts, masks, states):
            kcs = [k_ref[pl.ds(r, KEY_CHUNK), :] for r in starts]
            vcs = [v_ref[pl.ds(r, KEY_CHUNK), :] for r in starts]
            head = lambda h: slice(h * Q_BLOCK, (h + 1) * Q_BLOCK)
            zs = _scores_phase([q_scaled[b][head(h)] for b, h in CHAINS], [kcs[b] for b, _ in CHAINS],
                               [masks[b] for b, _ in CHAINS])
            das = [lax.dot_general(do_rows[b][head(h)], vcs[b], (_NT, ((), ())), preferred_element_type=F32)
                   for b, h in CHAINS]
            es, lks, lbs = _logs_phase(zs)
            suffixes, carries = _suffix_phase(lks, [states[b][0][h] for b, h in CHAINS], umat)
            a_bf = [jnp.exp(lb + sfx).astype(BF16) for lb, sfx in zip(lbs, suffixes)]
            gs = [a.astype(F32) * da for a, da in zip(a_bf, das)]
            suffixes_g, carries_g = _suffix_phase(gs, [states[b][1][h] for b, h in CHAINS], gmat)
            dzs = []
            for (b, h), z, e, g, sfx_g in zip(CHAINS, zs, es, gs, suffixes_g):
                before = totals[b][h] - sfx_g
                inv = 1.0 / (1.0 + e)
                sig = jnp.where(z >= 0.0, inv, e * inv)
                dzs.append(((g * (1.0 - sig) - before * sig) * scale).astype(BF16))
            new = []
            for b in range(Q_GROUP):
                pair = slice(2 * b, 2 * b + 2)
                dq_acc = states[b][2] + jnp.dot(jnp.concatenate(dzs[pair], axis=1), _head_rows(kcs[b], lane_lo_keys),
                                                preferred_element_type=F32)
                dk_s[pl.ds(starts[b], KEY_CHUNK), :] += lax.dot_general(
                    jnp.concatenate(dzs[pair], axis=0), q_rows[b], (_TN, ((), ())), preferred_element_type=F32)
                dv_s[pl.ds(starts[b], KEY_CHUNK), :] += lax.dot_general(
                    jnp.concatenate(a_bf[pair], axis=0), do_rows[b], (_TN, ((), ())), preferred_element_type=F32)
                new.append(((carries[2 * b], carries[2 * b + 1]), (carries_g[2 * b], carries_g[2 * b + 1]), dq_acc))
            return new

        zero = jnp.zeros((Q_BLOCK, LANES), F32)
        states = _sweep(i * Q_GROUP, step, [((zero, zero), (zero, zero), zero)] * Q_GROUP, lambda st: st[0])
        dq_ref[...] = jnp.concatenate([st[2] for st in states], axis=0).astype(BF16)

        @pl.when(i == n_steps - 1)
        def _():
            dk_ref[...] = dk_s[...].astype(BF16)
            dv_ref[...] = dv_s[...].astype(BF16)

        @pl.when((p == npair - 1) & (i == n_steps - 1))
        def _():
            for cp in copies:
                cp.wait()

    whole = pl.BlockSpec((t_len, LANES), lambda p, i: (0, p))
    hbm = pl.BlockSpec(memory_space=pl.ANY)
    outs = pl.pallas_call(
        body, name="attn_bwd", grid=(npair, n_steps),
        in_specs=[pl.BlockSpec((rows_per_step, LANES), lambda p, i: (i, p)),
                  pl.BlockSpec((t_len, LANES), lambda p, i: (0, npair + p)),
                  pl.BlockSpec((t_len, LANES), lambda p, i: (0, 2 * npair + p)),
                  pl.BlockSpec((rows_per_step, LANES), lambda p, i: (i, p)),
                  pl.BlockSpec((rows_per_step, LANES), lambda p, i: (i, npair + p)),
                  pl.BlockSpec((1, LANES), lambda p, i: (0, p)),
                  pl.BlockSpec((2 * LANES, 2 * LANES), lambda p, i: (0, 0)),
                  pl.BlockSpec((2 * LANES, 2 * LANES), lambda p, i: (0, 0))] + [hbm] * n_ex,
        out_specs=[pl.BlockSpec((rows_per_step, LANES), lambda p, i: (i, p)), whole, whole,
                   pl.BlockSpec((8, LANES), lambda p, i: (0, p))] + [hbm] * n_ex,
        out_shape=[jax.ShapeDtypeStruct((t_len, width), BF16)] * 3 + [jax.ShapeDtypeStruct((8, width), F32)]
        + _chip_landing(exchange),
        scratch_shapes=[pltpu.VMEM((t_len, LANES), F32), pltpu.VMEM((t_len, LANES), F32)] + _chip_scratch(n_ex),
        compiler_params=_params(("arbitrary", "arbitrary")),
    )(qkv, qkv, qkv, o, dymix, g_attn, _cumsum_matrix(False), _cumsum_matrix(True), *exchange)
    return outs[0], outs[1], outs[2], outs[3], outs[4:]


def kernel(x, w_in, conv_w, g_conv, g_attn, w_out, ln1_g, ln1_b, w_up, w_down, ln2_g, ln2_b, loss_target, m_w_in, m_conv_w, m_g_conv, m_g_attn, m_w_out, m_ln1_g, m_ln1_b, m_w_up, m_w_down, m_ln2_g, m_ln2_b, v_w_in, v_conv_w, v_g_conv, v_g_attn, v_w_out, v_ln1_g, v_ln1_b, v_w_up, v_w_down, v_ln2_g, v_ln2_b):
    x2, target = x[0], loss_target[0]
    t_len, d_model = x2.shape
    width = g_conv.shape[1]
    in_shard = w_in.shape[2]
    up_shard = w_up.shape[2]
    d_ff = up_shard * N_DEV
    cw = conv_w.shape[2]
    xi, yi, ci = _place()
    dev = 4 * xi + 2 * yi + ci
    tm = 256

    win_g, wout_bf, wup_bf, wdown_bf = _all_gather(
        "gather_w_in", [w_in[0]], BF16, cast_only=[w_out[0], w_up[0], w_down[0]])
    (convw_g,) = _all_gather("gather_conv_w", [jnp.pad(conv_w[0], ((0, 5), (0, LANES - cw)))], F32)
    conv_full = jnp.transpose(convw_g[:, :3, :cw], (1, 0, 2)).reshape(3, width)

    n_row_tiles = t_len // tm
    rows = lambda cols: pl.BlockSpec((tm, cols), lambda i: (i, 0))
    resident = lambda arr: pl.BlockSpec(arr.shape, lambda i: (0,) * arr.ndim)
    vec = pl.BlockSpec((1, d_model), lambda i: (0, 0))
    acc8 = pl.BlockSpec((8, d_model), lambda i: (0, 0))
    tokens = lambda cols, dtype: jax.ShapeDtypeStruct((t_len, cols), dtype)

    def proj_in_epilogue(acc, extra, outs, i):
        outs[0][...] = acc[:, :3 * width]
        outs[1][...] = acc[:, 3 * width:].astype(BF16)
        outs[2][...] = extra[0][...].astype(BF16)

    gates, qkv, x_bf, wdown_g = _matmul(
        "proj_in", x2, win_g, rows(d_model), resident(win_g), _shards_wide, n_row_tiles,
        [(tokens(3 * width, F32), rows(3 * width)), (tokens(3 * width, BF16), rows(3 * width)),
         (tokens(d_model, BF16), rows(d_model))],
        proj_in_epilogue, extras=[(x2, rows(d_model))], gather=[wdown_bf])

    ymix = _conv_fwd(gates, conv_full, g_conv, width)
    o_attn, ymix, (wout_g, wup_g) = _attn_fwd(qkv, g_attn, ymix, width, [wout_bf, wup_bf])
    wout_full = wout_g.reshape(d_model, d_model)
    wdown_full = wdown_g.reshape(d_ff, d_model)

    def out_ln1_epilogue(acc, extra, outs, i):
        x_ref, g_ref, b_ref = extra
        r1 = ALPHA * x_ref[...] + acc
        x1, _, _ = _ln_fwd(r1, g_ref[...], b_ref[...])
        outs[0][...] = r1
        outs[1][...] = x1
        outs[2][...] = x1.astype(BF16)

    r1, x1, x1_bf = _matmul(
        "proj_out_ln1", ymix, wout_full, rows(d_model), resident(wout_full), _whole(_NN), n_row_tiles,
        [(tokens(d_model, F32), rows(d_model)), (tokens(d_model, F32), rows(d_model)),
         (tokens(d_model, BF16), rows(d_model))],
        out_ln1_epilogue, extras=[(x2, rows(d_model)), (ln1_g, vec), (ln1_b, vec)])

    def up_epilogue(acc, extra, outs, i):
        outs[0][...] = acc
        relu = jnp.maximum(acc, 0.0)
        outs[1][...] = (relu * relu).astype(BF16)

    up, hid = _matmul(
        "ffn_up", x1_bf, wup_g, rows(d_model), resident(wup_g), _shards_wide, n_row_tiles,
        [(tokens(d_ff, F32), rows(d_ff)), (tokens(d_ff, BF16), rows(d_ff))], up_epilogue)

    def down_ln2_epilogue(acc, extra, outs, i):
        x1_ref, t_ref, g_ref, b_ref = extra
        dr2_ref, dr2_bf_ref, loss_ref, gg_ref, gb_ref = outs
        gain = g_ref[...]
        r2 = ALPHA * x1_ref[...] + acc
        y, xhat, rstd = _ln_fwd(r2, gain, b_ref[...])
        diff = y - t_ref[...]
        d_y = diff * (1.0 / d_model)
        dr2 = _ln_bwd(d_y, xhat, rstd, gain)
        dr2_ref[...] = dr2
        dr2_bf_ref[...] = dr2.astype(BF16)
        _accumulate(gg_ref, i, n_row_tiles, _rows8(d_y * xhat))
        _accumulate(gb_ref, i, n_row_tiles, _rows8(d_y))
        _accumulate(loss_ref, i, n_row_tiles, _rows8(diff * diff), all_lanes=True)

    small_acc = (jax.ShapeDtypeStruct((8, d_model), F32), acc8)
    dr2, dr2_bf, loss_cols, g_ln2_g, g_ln2_b = _matmul(
        "ffn_down_ln2_loss", hid, wdown_full, rows(d_ff), resident(wdown_full), _whole(_NN), n_row_tiles,
        [(tokens(d_model, F32), rows(d_model)), (tokens(d_model, BF16), rows(d_model)),
         small_acc, small_acc, small_acc],
        down_ln2_epilogue, extras=[(x1, rows(d_model)), (target, rows(d_model)), (ln2_g, vec), (ln2_b, vec)])

    def dup_epilogue(acc, extra, outs, i):
        outs[0][...] = (acc * (2.0 * jnp.maximum(extra[0][...], 0.0))).astype(BF16)

    (dup,) = _matmul(
        "ffn_dhid", dr2_bf, wdown_full, rows(d_model), resident(wdown_full), _whole(_NT), n_row_tiles,
        [(tokens(d_ff, BF16), rows(d_ff))], dup_epilogue, extras=[(up, rows(d_ff))])

    def weight_grad(name, acts, cots, shard_shape, acts_by_shard):
        sr, sc = shard_shape
        per = max(1, 256 // sr) if acts_by_shard else 1
        tall = lambda cols: pl.BlockSpec((t_len, cols), lambda i: (0, i))
        if acts_by_shard:
            a_spec, b_spec = tall(per * sr), resident(cots)
        else:
            a_spec, b_spec = resident(acts), tall(sc)
        out_spec = pl.BlockSpec((per, sr, sc), lambda i: (i, 0, 0))

        def epilogue(acc, extra, outs, i):
            acc = acc.reshape(per, sr, sc)
            outs[0][...] = acc
            outs[1][...] = acc.astype(BF16)

        shape = (N_DEV, sr, sc)
        return _matmul(name, acts, cots, a_spec, b_spec, _whole(_TN), N_DEV // per,
                       [(jax.ShapeDtypeStruct(shape, F32), out_spec), (jax.ShapeDtypeStruct(shape, BF16), out_spec)],
                       epilogue)

    gw_down, gw_down_bf = weight_grad("grad_w_down", hid, dr2_bf, (d_ff // N_DEV, d_model), True)
    gw_up, gw_up_bf = weight_grad("grad_w_up", x1_bf, dup, (d_model, up_shard), False)

    def dx1_ln1_epilogue(acc, extra, outs, i):
        dr2_ref, r1_ref, g_ref, b_ref = extra
        dr1_ref, dr1_bf_ref, gg_ref, gb_ref = outs
        gain = g_ref[...]
        dx1 = ALPHA * dr2_ref[...] + acc
        _, xhat, rstd = _ln_fwd(r1_ref[...], gain, b_ref[...])
        dr1 = _ln_bwd(dx1, xhat, rstd, gain)
        dr1_ref[...] = dr1
        dr1_bf_ref[...] = dr1.astype(BF16)
        _accumulate(gg_ref, i, n_row_tiles, _rows8(dx1 * xhat))
        _accumulate(gb_ref, i, n_row_tiles, _rows8(dx1))

    dr1, dr1_bf, g_ln1_g, g_ln1_b = _matmul(
        "ffn_dx1_ln1", dup, wup_g, rows(d_ff), resident(wup_g), _shards_deep, n_row_tiles,
        [(tokens(d_model, F32), rows(d_model)), (tokens(d_model, BF16), rows(d_model)), small_acc, small_acc],
        dx1_ln1_epilogue, extras=[(dr2, rows(d_model)), (r1, rows(d_model)), (ln1_g, vec), (ln1_b, vec)])

    gw_out, gw_out_bf = weight_grad("grad_w_out", ymix, dr1_bf, (d_model // N_DEV, d_model), True)

    c_idx = jnp.reshape(ci, (1,)).astype(jnp.int32)
    q_me = jnp.reshape(2 * xi + yi, (1,)).astype(jnp.int32)

    def pair_stage(tag, names, grads_f32, grads_bf):
        landed = _pair_exchange("reduce_pair_" + tag, grads_bf)
        sums = [_pair_sum("pair_sum_" + nm, g, land, c_idx, q_me) for nm, g, land in zip(names, grads_f32, landed)]
        return [s[0] for s in sums], [s[1] for s in sums]

    wire_ffn, own_ffn = pair_stage("ffn", ["w_out", "w_up", "w_down"], [gw_out, gw_up, gw_down],
                                   [gw_out_bf, gw_up_bf, gw_down_bf])

    def plain_epilogue(acc, extra, outs, i):
        outs[0][...] = acc

    (dymix,) = _matmul(
        "proj_out_bwd", dr1_bf, wout_full, rows(d_model), resident(wout_full), _whole(_NT), n_row_tiles,
        [(tokens(d_model, F32), rows(d_model))], plain_epilogue)

    d_bg, d_cg, d_h, g_gconv, g_convw = _conv_bwd(gates, dymix, conv_full, g_conv, width)
    d_q, d_k, d_v, g_gattn, landed_ffn = _attn_bwd(qkv, o_attn, dymix, g_attn, width, wire_ffn)
    dproj = jnp.concatenate([d_bg, d_cg, d_h, d_q, d_k, d_v], axis=1)

    gw_in, gw_in_bf = weight_grad("grad_w_in", x_bf, dproj, (d_model, in_shard), False)
    wire_in, own_in = pair_stage("in", ["w_in"], [gw_in], [gw_in_bf])

    def dx_epilogue(acc, extra, outs, i):
        outs[0][...] = ALPHA * extra[0][...] + acc

    grad_x, landed_in = _matmul(
        "proj_in_bwd", dproj, win_g, rows(6 * width), resident(win_g), _shards_deep, n_row_tiles,
        [(tokens(d_model, F32), rows(d_model))], dx_epilogue, extras=[(dr1, rows(d_model))], exchange=wire_in)

    names = ["w_in", "w_out", "w_up", "w_down"]
    big = {}
    for nm, ow, land, w, m, v in zip(names, own_in + own_ffn, [landed_in] + list(landed_ffn),
                                     (w_in, w_out, w_up, w_down),
                                     (m_w_in, m_w_out, m_w_up, m_w_down), (v_w_in, v_w_out, v_w_up, v_w_down)):
        big[nm] = [r[None] for r in _shard_update("update_" + nm, ow, land, w[0], m[0], v[0])]

    spare = jnp.zeros((3, d_model - width), F32)

    def pack(l1g, l1b, l2g, l2b, gc, ga, taps, tail=spare):
        return jnp.concatenate([l1g, l1b, l2g, l2b, jnp.concatenate([gc, ga], axis=1),
                                jnp.concatenate([taps, tail], axis=1)], axis=0)

    local = pack(g_ln1_g[:1], g_ln1_b[:1], g_ln2_g[:1], g_ln2_b[:1], g_gconv[:1], g_gattn[:1], g_convw[:3],
                 spare + (0.5 / d_model) * loss_cols[0, 0])
    (parts,) = _all_gather("gather_small_grads", [local], F32)

    def spread(a):
        return lax.dynamic_update_slice(jnp.zeros((3, width), F32), a, (0, dev * cw))

    w_small = pack(ln1_g, ln1_b, ln2_g, ln2_b, g_conv, g_attn, spread(conv_w[0]))
    m_small = pack(m_ln1_g, m_ln1_b, m_ln2_g, m_ln2_b, m_g_conv, m_g_attn, spread(m_conv_w[0]))
    v_small = pack(v_ln1_g, v_ln1_b, v_ln2_g, v_ln2_b, v_g_conv, v_g_attn, spread(v_conv_w[0]))
    small = _small_update(parts, w_small, m_small, v_small)

    def unpack(a):
        taps = lax.dynamic_slice(a[5:8, :width], (0, dev * cw), (3, cw))[None]
        return {"ln1_g": a[0:1], "ln1_b": a[1:2], "ln2_g": a[2:3], "ln2_b": a[3:4],
                "g_conv": a[4:5, :width], "g_attn": a[4:5, width:], "conv_w": taps}

    loss = small[0][7, d_model - 1]
    small = [unpack(a) for a in small]

    order = ["w_in", "conv_w", "g_conv", "g_attn", "w_out", "ln1_g", "ln1_b", "w_up", "w_down", "ln2_g", "ln2_b"]
    result = [loss, grad_x[None]]
    for kind in range(4):
        for nm in order:
            result.append(big[nm][kind] if nm in big else small[kind][nm])
    return tuple(result)
```

```python
import functools

import jax
import jax.numpy as jnp
from jax import lax
from jax.experimental import pallas as pl
from jax.experimental.pallas import tpu as pltpu

F32 = jnp.float32
BF16 = jnp.bfloat16
MESH = pl.DeviceIdType.MESH

N_DEV = 8
HEAD_DIM = 64
LANES = 128
Q_BLOCK = 128
ALPHA = 2.0 ** 0.25
LN_EPS = 1e-5
RMS_EPS = 1e-6
ADAM_LR, ADAM_B1, ADAM_B2, ADAM_EPS, ADAM_WD, ADAM_STEP = 0.001, 0.9, 0.999, 1e-08, 0.01, 10
VMEM_LIMIT_BYTES = 48 * 1024 * 1024

_NT = ((1,), (1,))
_NN = ((1,), (0,))
_TN = ((0,), (0,))


def _params(sem=None):
    return pltpu.CompilerParams(dimension_semantics=sem, vmem_limit_bytes=VMEM_LIMIT_BYTES)


def _dot(a, b, contract):
    return lax.dot_general(a.astype(BF16), b.astype(BF16), (contract, ((), ())), preferred_element_type=F32)


def _whole(contract):
    return lambda a_ref, b_ref: _dot(a_ref[...], b_ref[...], contract)


def _shards_wide(a_ref, b_ref):
    a = a_ref[...].astype(BF16)
    return jnp.concatenate([_dot(a, b_ref[s], _NN) for s in range(b_ref.shape[0])], axis=1)


def _shards_deep(a_ref, b_ref):
    n = b_ref.shape[2]
    acc = _dot(a_ref[:, 0:n], b_ref[0], _NT)
    for s in range(1, b_ref.shape[0]):
        acc = acc + _dot(a_ref[:, s * n:(s + 1) * n], b_ref[s], _NT)
    return acc


def _matmul(name, a, b, a_spec, b_spec, product, n_steps, outs, epilogue, extras=(), exchange=(), gather=()):
    behind = list(exchange) + list(gather)
    n_extra, n_out, n_ex, n_b = len(extras), len(outs), len(exchange), len(behind)

    def body(a_ref, b_ref, *rest):
        i = pl.program_id(0)
        extra_refs, rest = rest[:n_extra], rest[n_extra:]
        src_refs, rest = rest[:n_b], rest[n_b:]
        out_refs, rest = rest[:n_out], rest[n_out:]
        land_refs, sems = rest[:n_b], rest[n_b:]
        copies = _chip_copies(src_refs[:n_ex], land_refs[:n_ex], *sems[:2]) if exchange else []
        gather_sems = sems[2:] if exchange else sems
        plan = _gather_plan(src_refs[n_ex:], land_refs[n_ex:], *gather_sems) if gather else None

        def start():
            for cp in copies:
                cp.start()
            if plan:
                plan[0]()

        def finish():
            if plan:
                plan[1]()
                plan[2]()
            for cp in copies:
                cp.wait()

        if behind:
            pl.when(i == 0)(start)
        epilogue(product(a_ref, b_ref), extra_refs, out_refs, i)
        if behind:
            pl.when(i == n_steps - 1)(finish)

    hbm = pl.BlockSpec(memory_space=pl.ANY)
    landing = _chip_landing(exchange) + [jax.ShapeDtypeStruct((N_DEV,) + s.shape, s.dtype) for s in gather]
    scratch = (_chip_scratch(len(exchange)) if exchange else []) + (_gather_scratch(len(gather)) if gather else [])
    return pl.pallas_call(
        body, name=name, grid=(n_steps,),
        in_specs=[a_spec, b_spec] + [s for _, s in extras] + [hbm] * n_b,
        out_specs=[s for _, s in outs] + [hbm] * n_b,
        out_shape=[o for o, _ in outs] + landing,
        scratch_shapes=scratch,
        compiler_params=_params(("arbitrary",)),
    )(a, b, *[e for e, _ in extras], *behind)


def _ln_fwd(r, g, b):
    mu = jnp.mean(r, axis=-1, keepdims=True)
    xc = r - mu
    rstd = lax.rsqrt(jnp.mean(xc * xc, axis=-1, keepdims=True) + LN_EPS)
    xhat = xc * rstd
    return xhat * g + b, xhat, rstd


def _ln_bwd(dy, xhat, rstd, g):
    dxh = dy * g
    m1 = jnp.mean(dxh, axis=-1, keepdims=True)
    m2 = jnp.mean(dxh * xhat, axis=-1, keepdims=True)
    return rstd * (dxh - m1 - xhat * m2)


def _rows8(v):
    n, c = v.shape
    return jnp.sum(v.reshape(n // 8, 8, c), axis=0)


def _accumulate(ref, i, n_steps, part8, all_lanes=False):
    @pl.when(i == 0)
    def _():
        ref[...] = part8

    @pl.when(i > 0)
    def _():
        ref[...] += part8

    @pl.when(i == n_steps - 1)
    def _():
        tot = jnp.sum(ref[...], axis=0, keepdims=True)
        if all_lanes:
            tot = jnp.sum(tot, axis=1, keepdims=True)
        ref[...] = jnp.broadcast_to(tot, ref.shape)


def _group_sums(v, lane_lo):
    s0 = jnp.sum(jnp.where(lane_lo, v, 0.0), axis=-1, keepdims=True)
    s1 = jnp.sum(jnp.where(lane_lo, 0.0, v), axis=-1, keepdims=True)
    return jnp.where(lane_lo, s0, s1)


def _place():
    return lax.axis_index("x"), lax.axis_index("y"), lax.axis_index("c")


def _gather_plan(src_refs, out_refs, send_sems, recv_sems, local_sems):
    n = len(src_refs)
    x, y, c = _place()
    me, sibling = (x, y, c), (x, y, 1 - c)
    chips = [(1 - x, y), (x, 1 - y), (1 - x, 1 - y)]

    def copy(a, k, block, to, src=None):
        slot = out_refs[a].at[4 * block[0] + 2 * block[1] + block[2]]
        return pltpu.make_async_remote_copy(
            src_ref=slot if src is None else src, dst_ref=slot,
            send_sem=send_sems.at[a, k], recv_sem=recv_sems.at[a, k],
            device_id=to, device_id_type=MESH)

    mine = [pltpu.make_async_copy(src_refs[a], out_refs[a].at[4 * x + 2 * y + c], local_sems.at[a]) for a in range(n)]
    first = [[copy(a, 0, me, sibling, src=src_refs[a])]
             + [copy(a, 1 + j, me, (*chip, c), src=src_refs[a]) for j, chip in enumerate(chips)] for a in range(n)]
    passed = [[copy(a, 4 + j, (*chip, c), sibling) for j, chip in enumerate(chips)] for a in range(n)]

    def start():
        for a in range(n):
            mine[a].start()
            for cp in first[a]:
                cp.start()

    def forward():
        for j, chip in enumerate(chips):
            for a in range(n):
                copy(a, 1 + j, (*chip, c), me).wait_recv()
                passed[a][j].start()

    def finish():
        for a in range(n):
            copy(a, 0, sibling, me).wait_recv()
            for j, chip in enumerate(chips):
                copy(a, 4 + j, (*chip, 1 - c), me).wait_recv()
        for a in range(n):
            for cp in first[a] + passed[a]:
                cp.wait_send()
            mine[a].wait()

    return start, forward, finish


def _gather_scratch(n):
    return [pltpu.SemaphoreType.DMA((n, 7)), pltpu.SemaphoreType.DMA((n, 7)), pltpu.SemaphoreType.DMA((n,))]


def _all_gather(name, shards, out_dtypes, cast_only=()):
    n, m = len(shards), len(cast_only)

    def body(*refs):
        in_refs, cast_in = refs[:n], refs[n:n + m]
        out_refs, cast_out = refs[n + m:2 * n + m], refs[2 * n + m:2 * (n + m)]
        stage = refs[2 * (n + m):3 * n + 2 * m]
        for a in range(n):
            stage[a][...] = in_refs[a][...].astype(out_dtypes[a])
        start, forward, finish = _gather_plan(stage, out_refs, *refs[3 * n + 2 * m:])
        start()
        for a in range(m):
            cast_out[a][...] = cast_in[a][...].astype(BF16)
        forward()
        finish()

    return pl.pallas_call(
        body, name=name,
        in_specs=[pl.BlockSpec(memory_space=pltpu.VMEM)] * (n + m),
        out_specs=[pl.BlockSpec(memory_space=pl.ANY)] * n + [pl.BlockSpec(memory_space=pltpu.VMEM)] * m,
        out_shape=[jax.ShapeDtypeStruct((N_DEV,) + s.shape, dt) for s, dt in zip(shards, out_dtypes)]
        + [jax.ShapeDtypeStruct(s.shape, BF16) for s in cast_only],
        scratch_shapes=[pltpu.VMEM(s.shape, dt) for s, dt in zip(shards, out_dtypes)] + _gather_scratch(n),
        compiler_params=pltpu.CompilerParams(vmem_limit_bytes=VMEM_LIMIT_BYTES),
    )(*shards, *cast_only)


def _weight_grad(name, acts, cots, shard_shape, acts_by_shard):
    t_len = acts.shape[0]
    sr, sc = shard_shape
    per = max(1, 256 // sr) if acts_by_shard else 1
    assert per in (1, 2)
    n_steps = N_DEV // per

    def body(a_ref, b_ref, g_ref, land_ref, stage, send_sems, recv_sems):
        i = pl.program_id(0)
        x, y, c = _place()

        def copy(q):
            return pltpu.make_async_remote_copy(
                src_ref=stage.at[q % 2], dst_ref=land_ref.at[q], send_sem=send_sems.at[q], recv_sem=recv_sems.at[q],
                device_id=(x, y, 1 - c), device_id_type=MESH)

        acc = _dot(a_ref[...], b_ref[...], _TN).reshape(per, sr, sc)
        g_ref[...] = acc
        if per == 2:
            payload, due, q = jnp.where(c == 0, acc[1], acc[0]), i >= 0, i
        else:
            payload, due, q = acc[0], (i % 2) == (1 - c), i // 2

        @pl.when(due)
        def _():
            @pl.when(q >= 2)
            def _():
                copy(q - 2).wait_send()

            stage[q % 2] = payload.astype(BF16)
            copy(q).start()

        @pl.when(i == n_steps - 1)
        def _():
            copy(2).wait_send()
            copy(3).wait_send()
            for k in range(4):
                copy(k).wait_recv()

    tall = lambda cols: pl.BlockSpec((t_len, cols), lambda i: (0, i))
    whole = lambda arr: pl.BlockSpec(arr.shape, lambda i: (0, 0))
    a_spec, b_spec = (tall(per * sr), whole(cots)) if acts_by_shard else (whole(acts), tall(sc))
    return pl.pallas_call(
        body, name=name, grid=(n_steps,),
        in_specs=[a_spec, b_spec],
        out_specs=[pl.BlockSpec((per, sr, sc), lambda i: (i, 0, 0)), pl.BlockSpec(memory_space=pl.ANY)],
        out_shape=[jax.ShapeDtypeStruct((N_DEV, sr, sc), F32), jax.ShapeDtypeStruct((4, sr, sc), BF16)],
        scratch_shapes=[pltpu.VMEM((2, sr, sc), BF16), pltpu.SemaphoreType.DMA((4,)), pltpu.SemaphoreType.DMA((4,))],
        compiler_params=_params(("arbitrary",)),
    )(acts, cots)


def _chip_copies(p_refs, land_refs, send_sems, recv_sems):
    x, y, c = _place()
    chips = [(1 - x, y), (x, 1 - y), (1 - x, 1 - y)]
    return [pltpu.make_async_remote_copy(
        src_ref=p_refs[a].at[2 * px + py], dst_ref=land_refs[a].at[k],
        send_sem=send_sems.at[a, k], recv_sem=recv_sems.at[a, k],
        device_id=(px, py, c), device_id_type=MESH)
        for a in range(len(p_refs)) for k, (px, py) in enumerate(chips)]


def _chip_scratch(n):
    return [pltpu.SemaphoreType.DMA((n, 3)), pltpu.SemaphoreType.DMA((n, 3))]


def _chip_landing(pair_sums):
    return [jax.ShapeDtypeStruct((3,) + p.shape[1:], p.dtype) for p in pair_sums]


def _pair_sum(name, g_f32, landed, c_idx, q_me):
    _, rows, cols = g_f32.shape
    tr = min(rows, 256)

    def body(c_ref, q_ref, g_ref, l_ref, wire_ref, own_ref):
        q = pl.program_id(1)
        s = g_ref[...] + l_ref[...].astype(F32)
        wire_ref[...] = s.astype(BF16)

        @pl.when(q == q_ref[0])
        def _():
            own_ref[...] = s

    return pl.pallas_call(
        body, name=name,
        grid_spec=pltpu.PrefetchScalarGridSpec(
            num_scalar_prefetch=2, grid=(rows // tr, 4),
            in_specs=[pl.BlockSpec((None, tr, cols), lambda i, q, c_ref, q_ref: (2 * q + c_ref[0], i, 0)),
                      pl.BlockSpec((None, tr, cols), lambda i, q, c_ref, q_ref: (q, i, 0))],
            out_specs=[pl.BlockSpec((None, tr, cols), lambda i, q, c_ref, q_ref: (q, i, 0)),
                       pl.BlockSpec((tr, cols), lambda i, q, c_ref, q_ref: (i, 0))]),
        out_shape=[jax.ShapeDtypeStruct((4, rows, cols), BF16), jax.ShapeDtypeStruct((rows, cols), F32)],
        compiler_params=_params(("parallel", "arbitrary")),
    )(c_idx, q_me, g_f32, landed)


def _adamw(w, g, m, v):
    m = ADAM_B1 * m + (1.0 - ADAM_B1) * g
    v = ADAM_B2 * v + (1.0 - ADAM_B2) * (g * g)
    m_hat = m / (1.0 - ADAM_B1 ** ADAM_STEP)
    v_hat = v / (1.0 - ADAM_B2 ** ADAM_STEP)
    delta = -ADAM_LR * (m_hat / (jnp.sqrt(v_hat) + ADAM_EPS) + ADAM_WD * w)
    return delta, m, v


def _shard_update(name, own, landed, w, m, v):
    rows, cols = own.shape
    tr = min(rows, 256)

    def body(own_ref, l_ref, w_ref, m_ref, v_ref, g_out, d_out, m_out, v_out):
        g = own_ref[...] + l_ref[0].astype(F32) + l_ref[1].astype(F32) + l_ref[2].astype(F32)
        d, m_new, v_new = _adamw(w_ref[...], g, m_ref[...], v_ref[...])
        g_out[...] = g
        d_out[...] = d
        m_out[...] = m_new
        v_out[...] = v_new

    tile = pl.BlockSpec((tr, cols), lambda i: (i, 0))
    return pl.pallas_call(
        body, name=name, grid=(rows // tr,),
        in_specs=[tile, pl.BlockSpec((3, tr, cols), lambda i: (0, i, 0)), tile, tile, tile],
        out_specs=[tile] * 4,
        out_shape=[jax.ShapeDtypeStruct((rows, cols), F32)] * 4,
        compiler_params=_params(("parallel",)),
    )(own, landed, w, m, v)


def _small_update(parts, w, m, v):
    def body(p_ref, w_ref, m_ref, v_ref, g_out, d_out, m_out, v_out):
        g = p_ref[0]
        for d in range(1, N_DEV):
            g = g + p_ref[d]
        dl, m_new, v_new = _adamw(w_ref[...], g, m_ref[...], v_ref[...])
        g_out[...] = g
        d_out[...] = dl
        m_out[...] = m_new
        v_out[...] = v_new

    return pl.pallas_call(
        body, name="small_update",
        out_shape=[jax.ShapeDtypeStruct(w.shape, F32)] * 4,
        compiler_params=pltpu.CompilerParams(vmem_limit_bytes=VMEM_LIMIT_BYTES),
    )(parts, w, m, v)


CONV_CHUNK = 512


def _conv_taps(u_s, lo, chunk):
    ext = u_s[pl.ds(lo, chunk + 8), :]
    return ext[8:], pltpu.roll(ext, 1, 0)[8:], pltpu.roll(ext, 2, 0)[8:]


def _conv_fwd(gates, conv_w, g_conv, width):
    t_len = gates.shape[0]
    nblk = width // LANES
    n_chunks = t_len // CONV_CHUNK

    def body(bg_ref, cg_ref, h_ref, w_ref, g_ref, y_ref, u_s):
        lane_lo = lax.broadcasted_iota(jnp.int32, (CONV_CHUNK, LANES), 1) < HEAD_DIM
        u_s[0:8, :] = jnp.zeros((8, LANES), F32)

        def fill(r, _):
            lo = pl.multiple_of(r * CONV_CHUNK, CONV_CHUNK)
            u_s[pl.ds(pl.multiple_of(lo + 8, 8), CONV_CHUNK), :] = cg_ref[pl.ds(lo, CONV_CHUNK), :] * h_ref[pl.ds(lo, CONV_CHUNK), :]
            return 0

        lax.fori_loop(0, n_chunks, fill, 0)
        w = w_ref[...]
        gain = g_ref[...]

        def step(r, _):
            lo = pl.multiple_of(r * CONV_CHUNK, CONV_CHUNK)
            u0, u1, u2 = _conv_taps(u_s, lo, CONV_CHUNK)
            y = bg_ref[pl.ds(lo, CONV_CHUNK), :] * (w[2:3] * u0 + w[1:2] * u1 + w[0:1] * u2)
            rs = lax.rsqrt(_group_sums(y * y, lane_lo) * (1.0 / HEAD_DIM) + RMS_EPS)
            y_ref[pl.ds(lo, CONV_CHUNK), :] = (y * rs * gain).astype(BF16)
            return 0

        lax.fori_loop(0, n_chunks, step, 0)

    col = lambda off: pl.BlockSpec((t_len, LANES), lambda p: (0, off + p))
    return pl.pallas_call(
        body, name="conv_fwd", grid=(nblk,),
        in_specs=[col(0), col(nblk), col(2 * nblk),
                  pl.BlockSpec((3, LANES), lambda p: (0, p)), pl.BlockSpec((1, LANES), lambda p: (0, p))],
        out_specs=col(0),
        out_shape=jax.ShapeDtypeStruct((t_len, 2 * width), BF16),
        scratch_shapes=[pltpu.VMEM((t_len + 8, LANES), F32)],
        compiler_params=_params(("parallel",)),
    )(gates, gates, gates, conv_w, g_conv)


def _conv_bwd(gates, dymix, conv_w, g_conv, width):
    t_len = gates.shape[0]
    nblk = width // LANES
    n_chunks = t_len // CONV_CHUNK

    def body(bg_ref, cg_ref, h_ref, dy_ref, w_ref, g_ref, dbg_ref, dcg_ref, dh_ref, gg_ref, gw_ref, u_s, dc_s):
        lane_lo = lax.broadcasted_iota(jnp.int32, (CONV_CHUNK, LANES), 1) < HEAD_DIM
        u_s[0:8, :] = jnp.zeros((8, LANES), F32)
        dc_s[t_len:t_len + 8, :] = jnp.zeros((8, LANES), F32)

        def fill(r, _):
            lo = pl.multiple_of(r * CONV_CHUNK, CONV_CHUNK)
            u_s[pl.ds(pl.multiple_of(lo + 8, 8), CONV_CHUNK), :] = cg_ref[pl.ds(lo, CONV_CHUNK), :] * h_ref[pl.ds(lo, CONV_CHUNK), :]
            return 0

        lax.fori_loop(0, n_chunks, fill, 0)
        w = w_ref[...]
        gain = g_ref[...]

        def step(r, acc):
            gg, gw0, gw1, gw2 = acc
            lo = pl.multiple_of(r * CONV_CHUNK, CONV_CHUNK)
            u0, u1, u2 = _conv_taps(u_s, lo, CONV_CHUNK)
            bg = bg_ref[pl.ds(lo, CONV_CHUNK), :]
            conv = w[2:3] * u0 + w[1:2] * u1 + w[0:1] * u2
            y = bg * conv
            rs = lax.rsqrt(_group_sums(y * y, lane_lo) * (1.0 / HEAD_DIM) + RMS_EPS)
            yn = y * rs
            dyn_raw = dy_ref[pl.ds(lo, CONV_CHUNK), :]
            dyn = dyn_raw * gain
            d_y = rs * (dyn - yn * (_group_sums(dyn * yn, lane_lo) * (1.0 / HEAD_DIM)))
            dbg_ref[pl.ds(lo, CONV_CHUNK), :] = (d_y * conv).astype(BF16)
            dconv = d_y * bg
            dc_s[pl.ds(lo, CONV_CHUNK), :] = dconv
            return (gg + _rows8(dyn_raw * yn), gw0 + _rows8(dconv * u2), gw1 + _rows8(dconv * u1),
                    gw2 + _rows8(dconv * u0))

        zero = jnp.zeros((8, LANES), F32)
        gg, gw0, gw1, gw2 = lax.fori_loop(0, n_chunks, step, (zero, zero, zero, zero))
        gg_ref[...] = jnp.broadcast_to(jnp.sum(gg, axis=0, keepdims=True), (8, LANES))
        row = lax.broadcasted_iota(jnp.int32, (8, LANES), 0)
        tot = lambda v: jnp.broadcast_to(jnp.sum(v, axis=0, keepdims=True), (8, LANES))
        gw_ref[...] = jnp.where(row == 0, tot(gw0), jnp.where(row == 1, tot(gw1), jnp.where(row == 2, tot(gw2), 0.0)))

        def back(r, _):
            lo = pl.multiple_of(r * CONV_CHUNK, CONV_CHUNK)
            ext = dc_s[pl.ds(lo, CONV_CHUNK + 8), :]
            d0 = ext[:CONV_CHUNK]
            d1 = pltpu.roll(ext, CONV_CHUNK + 7, 0)[:CONV_CHUNK]
            d2 = pltpu.roll(ext, CONV_CHUNK + 6, 0)[:CONV_CHUNK]
            du = w[2:3] * d0 + w[1:2] * d1 + w[0:1] * d2
            dcg_ref[pl.ds(lo, CONV_CHUNK), :] = (du * h_ref[pl.ds(lo, CONV_CHUNK), :]).astype(BF16)
            dh_ref[pl.ds(lo, CONV_CHUNK), :] = (du * cg_ref[pl.ds(lo, CONV_CHUNK), :]).astype(BF16)
            return 0

        lax.fori_loop(0, n_chunks, back, 0)

    col = lambda off: pl.BlockSpec((t_len, LANES), lambda p: (0, off + p))
    small = pl.BlockSpec((8, LANES), lambda p: (0, p))
    return pl.pallas_call(
        body, name="conv_bwd", grid=(nblk,),
        in_specs=[col(0), col(nblk), col(2 * nblk), col(0),
                  pl.BlockSpec((3, LANES), lambda p: (0, p)), pl.BlockSpec((1, LANES), lambda p: (0, p))],
        out_specs=[col(0), col(0), col(0), small, small],
        out_shape=[jax.ShapeDtypeStruct((t_len, width), BF16)] * 3 + [jax.ShapeDtypeStruct((8, width), F32)] * 2,
        scratch_shapes=[pltpu.VMEM((t_len + 8, LANES), F32), pltpu.VMEM((t_len + 8, LANES), F32)],
        compiler_params=_params(("parallel",)),
    )(gates, gates, gates, dymix, conv_w, g_conv)


def _cumsum_matrix(inclusive):
    j = lax.broadcasted_iota(jnp.int32, (2 * LANES, 2 * LANES), 0) % LANES
    s = lax.broadcasted_iota(jnp.int32, (2 * LANES, 2 * LANES), 1)
    keep = (s >= LANES) | ((j >= s) if inclusive else (j > s))
    return keep.astype(BF16)


KEY_CHUNK = 384
SUB_TILES = KEY_CHUNK // Q_BLOCK
Q_GROUP = 2
DEAD_BELOW = -105.0
MASKED_Z = -1e30


def _scores_phase(qms, kcs, masks):
    zs = [lax.dot_general(q, k, (_NT, ((), ())), preferred_element_type=F32) for q, k in zip(qms, kcs)]
    return [jnp.where(m, z, MASKED_Z) for z, m in zip(zs, masks)]


def _logs_phase(zs):
    es = [jnp.exp(-jnp.abs(z)) for z in zs]
    lks = [-(jnp.maximum(z, 0.0) + jnp.log(1.0 + e)) for z, e in zip(zs, es)]
    return es, lks, [z + lk for z, lk in zip(zs, lks)]


def _suffix_phase(vals, carries, mat):
    def hi_lo(tile):
        hi = tile.astype(BF16)
        return jnp.concatenate([hi, (tile - hi.astype(F32)).astype(BF16)], axis=1)

    packed = [[hi_lo(v[:, t * Q_BLOCK:(t + 1) * Q_BLOCK]) for t in range(SUB_TILES)] for v in vals]
    prods = [[jnp.dot(p, mat, preferred_element_type=F32) for p in chain] for chain in packed]
    sums, new_carries = [], []
    for chain, carry in zip(prods, carries):
        parts = []
        for t in reversed(range(SUB_TILES)):
            parts.append(chain[t][:, :LANES] + carry)
            carry = carry + chain[t][:, LANES:]
        sums.append(jnp.concatenate(parts[::-1], axis=1))
        new_carries.append(carry)
    return sums, new_carries


def _sweep(first_block, step, states, carries_of):
    n_chunks = ((first_block + Q_GROUP) * Q_BLOCK + KEY_CHUNK - 1) // KEY_CHUNK
    col = lax.broadcasted_iota(jnp.int32, (Q_BLOCK, KEY_CHUNK), 1)
    row = lax.broadcasted_iota(jnp.int32, (Q_BLOCK, 1), 0)

    def body(loop):
        n, _, sts = loop
        starts, masks = [], []
        for b in range(Q_GROUP):
            i = first_block + b
            end = (i + 1) * Q_BLOCK - n * KEY_CHUNK
            starts.append(pl.multiple_of(jnp.maximum(end - KEY_CHUNK, 0), Q_BLOCK))
            masks.append(col < jnp.minimum(i * Q_BLOCK + row, end) - starts[b])
        sts = step(starts, masks, sts)
        lowest = None
        for st in sts:
            for carry in carries_of(st):
                lowest = carry if lowest is None else jnp.maximum(lowest, carry)
        alive = (jnp.max(lowest) >= DEAD_BELOW).astype(jnp.int32)
        return n + 1, alive, tuple(sts)

    return lax.while_loop(lambda loop: (loop[0] < n_chunks) & (loop[1] > 0), body,
                          (jnp.int32(0), jnp.int32(1), tuple(states)))[2]


CHAINS = [(b, h) for b in range(Q_GROUP) for h in range(2)]


def _head_rows(block, lane_lo_rows):
    zero = jnp.zeros_like(block)
    return jnp.concatenate([jnp.where(lane_lo_rows, block, zero), jnp.where(lane_lo_rows, zero, block)], axis=0)


def _attn_fwd(qkv, g_attn, ymix_in, width, shards):
    t_len = qkv.shape[0]
    npair = width // LANES
    rows_per_step = Q_GROUP * Q_BLOCK
    n_steps = t_len // rows_per_step
    n_sh = len(shards)

    def body(q_ref, k_ref, v_ref, g_ref, umat_ref, ymix_in_ref, *rest):
        shard_refs, rest = rest[:n_sh], rest[n_sh:]
        o_ref, ymix_ref = rest[:2]
        gathered_refs, sems = rest[2:2 + n_sh], rest[2 + n_sh:]
        p, i = pl.program_id(0), pl.program_id(1)
        start, forward, finish = _gather_plan(shard_refs, gathered_refs, *sems)
        pl.when((p == 0) & (i == 0))(start)
        pl.when((p == npair - 1) & (i == n_steps // 2))(forward)
        lane_lo = lax.broadcasted_iota(jnp.int32, (rows_per_step, LANES), 1) < HEAD_DIM
        lane_lo_keys = lax.broadcasted_iota(jnp.int32, (KEY_CHUNK, LANES), 1) < HEAD_DIM
        umat = umat_ref[...]
        q = q_ref[...] * jnp.asarray(HEAD_DIM ** -0.5, BF16)
        q_heads = (jnp.where(lane_lo, q, jnp.zeros_like(q)), jnp.where(lane_lo, jnp.zeros_like(q), q))

        def step(starts, masks, states):
            kcs = [k_ref[pl.ds(r, KEY_CHUNK), :] for r in starts]
            zs = _scores_phase([q_heads[h][b * Q_BLOCK:(b + 1) * Q_BLOCK] for b, h in CHAINS],
                               [kcs[b] for b, _ in CHAINS], [masks[b] for b, _ in CHAINS])
            _, lks, lbs = _logs_phase(zs)
            suffixes, carries = _suffix_phase(lks, [states[b][0][h] for b, h in CHAINS], umat)
            weights = [jnp.exp(lb + sfx).astype(BF16) for lb, sfx in zip(lbs, suffixes)]
            new = []
            for b in range(Q_GROUP):
                v_heads = _head_rows(v_ref[pl.ds(starts[b], KEY_CHUNK), :], lane_lo_keys)
                o_acc = states[b][1] + jnp.dot(jnp.concatenate(weights[2 * b:2 * b + 2], axis=1), v_heads,
                                               preferred_element_type=F32)
                new.append(((carries[2 * b], carries[2 * b + 1]), o_acc))
            return new

        zero = jnp.zeros((Q_BLOCK, LANES), F32)
        states = _sweep(i * Q_GROUP, step, [((zero, zero), zero)] * Q_GROUP, lambda st: st[0])
        o = jnp.concatenate([st[1] for st in states], axis=0)
        o_ref[...] = o
        rs = lax.rsqrt(_group_sums(o * o, lane_lo) * (1.0 / HEAD_DIM) + RMS_EPS)
        ymix_ref[...] = (o * rs * g_ref[...]).astype(BF16)
        pl.when((p == npair - 1) & (i == n_steps - 1))(finish)

    hbm = pl.BlockSpec(memory_space=pl.ANY)
    outs = pl.pallas_call(
        body, name="attn_fwd", grid=(npair, n_steps),
        in_specs=[pl.BlockSpec((rows_per_step, LANES), lambda p, i: (i, p)),
                  pl.BlockSpec((t_len, LANES), lambda p, i: (0, npair + p)),
                  pl.BlockSpec((t_len, LANES), lambda p, i: (0, 2 * npair + p)),
                  pl.BlockSpec((1, LANES), lambda p, i: (0, p)),
                  pl.BlockSpec((2 * LANES, 2 * LANES), lambda p, i: (0, 0)),
                  hbm] + [hbm] * n_sh,
        out_specs=[pl.BlockSpec((rows_per_step, LANES), lambda p, i: (i, p)),
                   pl.BlockSpec((rows_per_step, LANES), lambda p, i: (i, npair + p))] + [hbm] * n_sh,
        out_shape=[jax.ShapeDtypeStruct((t_len, width), F32), jax.ShapeDtypeStruct(ymix_in.shape, BF16)]
        + [jax.ShapeDtypeStruct((N_DEV,) + s.shape, s.dtype) for s in shards],
        input_output_aliases={5: 1},
        scratch_shapes=_gather_scratch(n_sh),
        compiler_params=_params(("arbitrary", "arbitrary")),
    )(qkv, qkv, qkv, g_attn, _cumsum_matrix(False), ymix_in, *shards)
    return outs[0], outs[1], outs[2:]


def _attn_bwd(qkv, o, dymix, g_attn, width, exchange):
    t_len = qkv.shape[0]
    npair = width // LANES
    rows_per_step = Q_GROUP * Q_BLOCK
    n_steps = t_len // rows_per_step
    scale = HEAD_DIM ** -0.5
    n_ex = len(exchange)

    def body(q_ref, k_ref, v_ref, o_ref, dy_ref, g_ref, umat_ref, gmat_ref, *rest):
        p_refs, rest = rest[:n_ex], rest[n_ex:]
        dq_ref, dk_ref, dv_ref, gg_ref = rest[:4]
        land_refs, rest = rest[4:4 + n_ex], rest[4 + n_ex:]
        dk_s, dv_s = rest[:2]
        copies = _chip_copies(p_refs, land_refs, *rest[2:])
        p, i = pl.program_id(0), pl.program_id(1)

        @pl.when((p == 0) & (i == 0))
        def _():
            for cp in copies:
                cp.start()

        lane_lo = lax.broadcasted_iota(jnp.int32, (rows_per_step, LANES), 1) < HEAD_DIM
        lane_lo_keys = lax.broadcasted_iota(jnp.int32, (KEY_CHUNK, LANES), 1) < HEAD_DIM
        umat, gmat = umat_ref[...], gmat_ref[...]

        @pl.when(i == 0)
        def _():
            dk_s[...] = jnp.zeros_like(dk_s)
            dv_s[...] = jnp.zeros_like(dv_s)

        o_blk = o_ref[...]
        rs = lax.rsqrt(_group_sums(o_blk * o_blk, lane_lo) * (1.0 / HEAD_DIM) + RMS_EPS)
        yn = o_blk * rs
        dy_raw = dy_ref[...]
        dyn = dy_raw * g_ref[...]
        d_o = rs * (dyn - yn * (_group_sums(dyn * yn, lane_lo) * (1.0 / HEAD_DIM)))
        _accumulate(gg_ref, i, n_steps, _rows8(dy_raw * yn))
        do_bf = d_o.astype(BF16)
        do_o = do_bf.astype(F32) * o_blk
        q = q_ref[...]
        blocks = [slice(b * Q_BLOCK, (b + 1) * Q_BLOCK) for b in range(Q_GROUP)]
        lane_lo_q = lane_lo[:Q_BLOCK]
        totals = [(jnp.sum(jnp.where(lane_lo_q, do_o[r], 0.0), axis=-1, keepdims=True),
                   jnp.sum(jnp.where(lane_lo_q, 0.0, do_o[r]), axis=-1, keepdims=True)) for r in blocks]
        q_rows = [_head_rows(q[r], lane_lo_q) for r in blocks]
        do_rows = [_head_rows(do_bf[r], lane_lo_q) for r in blocks]
        q_scaled = [qr * jnp.asarray(scale, BF16) for qr in q_rows]

        def step(starts, masks, states):
            kcs = [k_ref[pl.ds(r, KEY_CHUNK), :] for r in starts]
            vcs = [v_ref[pl.ds(r, KEY_CHUNK), :] for r in starts]
            head = lambda h: slice(h * Q_BLOCK, (h + 1) * Q_BLOCK)
            zs = _scores_phase([q_scaled[b][head(h)] for b, h in CHAINS], [kcs[b] for b, _ in CHAINS],
                               [masks[b] for b, _ in CHAINS])
            das = [lax.dot_general(do_rows[b][head(h)], vcs[b], (_NT, ((), ())), preferred_element_type=F32)
                   for b, h in CHAINS]
            es, lks, lbs = _logs_phase(zs)
            suffixes, carries = _suffix_phase(lks, [states[b][0][h] for b, h in CHAINS], umat)
            a_bf = [jnp.exp(lb + sfx).astype(BF16) for lb, sfx in zip(lbs, suffixes)]
            gs = [a.astype(F32) * da for a, da in zip(a_bf, das)]
            suffixes_g, carries_g = _suffix_phase(gs, [states[b][1][h] for b, h in CHAINS], gmat)
            dzs = []
            for (b, h), z, e, g, sfx_g in zip(CHAINS, zs, es, gs, suffixes_g):
                before = totals[b][h] - sfx_g
                inv = pl.reciprocal(1.0 + e, approx=True)
                sig = jnp.where(z >= 0.0, inv, e * inv)
                dzs.append(((g * (1.0 - sig) - before * sig) * scale).astype(BF16))
            new = []
            for b in range(Q_GROUP):
                pair = slice(2 * b, 2 * b + 2)
                dq_acc = states[b][2] + jnp.dot(jnp.concatenate(dzs[pair], axis=1), _head_rows(kcs[b], lane_lo_keys),
                                                preferred_element_type=F32)
                dk_s[pl.ds(starts[b], KEY_CHUNK), :] += lax.dot_general(
                    jnp.concatenate(dzs[pair], axis=0), q_rows[b], (_TN, ((), ())), preferred_element_type=F32)
                dv_s[pl.ds(starts[b], KEY_CHUNK), :] += lax.dot_general(
                    jnp.concatenate(a_bf[pair], axis=0), do_rows[b], (_TN, ((), ())), preferred_element_type=F32)
                new.append(((carries[2 * b], carries[2 * b + 1]), (carries_g[2 * b], carries_g[2 * b + 1]), dq_acc))
            return new

        zero = jnp.zeros((Q_BLOCK, LANES), F32)
        states = _sweep(i * Q_GROUP, step, [((zero, zero), (zero, zero), zero)] * Q_GROUP, lambda st: st[0])
        dq_ref[...] = jnp.concatenate([st[2] for st in states], axis=0).astype(BF16)

        @pl.when(i == n_steps - 1)
        def _():
            dk_ref[...] = dk_s[...].astype(BF16)
            dv_ref[...] = dv_s[...].astype(BF16)

        @pl.when((p == npair - 1) & (i == n_steps - 1))
        def _():
            for cp in copies:
                cp.wait()

    whole = pl.BlockSpec((t_len, LANES), lambda p, i: (0, p))
    hbm = pl.BlockSpec(memory_space=pl.ANY)
    outs = pl.pallas_call(
        body, name="attn_bwd", grid=(npair, n_steps),
        in_specs=[pl.BlockSpec((rows_per_step, LANES), lambda p, i: (i, p)),
                  pl.BlockSpec((t_len, LANES), lambda p, i: (0, npair + p)),
                  pl.BlockSpec((t_len, LANES), lambda p, i: (0, 2 * npair + p)),
                  pl.BlockSpec((rows_per_step, LANES), lambda p, i: (i, p)),
                  pl.BlockSpec((rows_per_step, LANES), lambda p, i: (i, npair + p)),
                  pl.BlockSpec((1, LANES), lambda p, i: (0, p)),
                  pl.BlockSpec((2 * LANES, 2 * LANES), lambda p, i: (0, 0)),
                  pl.BlockSpec((2 * LANES, 2 * LANES), lambda p, i: (0, 0))] + [hbm] * n_ex,
        out_specs=[pl.BlockSpec((rows_per_step, LANES), lambda p, i: (i, p)), whole, whole,
                   pl.BlockSpec((8, LANES), lambda p, i: (0, p))] + [hbm] * n_ex,
        out_shape=[jax.ShapeDtypeStruct((t_len, width), BF16)] * 3 + [jax.ShapeDtypeStruct((8, width), F32)]
        + _chip_landing(exchange),
        scratch_shapes=[pltpu.VMEM((t_len, LANES), F32), pltpu.VMEM((t_len, LANES), F32)] + _chip_scratch(n_ex),
        compiler_params=_params(("arbitrary", "arbitrary")),
    )(qkv, qkv, qkv, o, dymix, g_attn, _cumsum_matrix(False), _cumsum_matrix(True), *exchange)
    return outs[0], outs[1], outs[2], outs[3], outs[4:]


def kernel(x, w_in, conv_w, g_conv, g_attn, w_out, ln1_g, ln1_b, w_up, w_down, ln2_g, ln2_b, loss_target, m_w_in, m_conv_w, m_g_conv, m_g_attn, m_w_out, m_ln1_g, m_ln1_b, m_w_up, m_w_down, m_ln2_g, m_ln2_b, v_w_in, v_conv_w, v_g_conv, v_g_attn, v_w_out, v_ln1_g, v_ln1_b, v_w_up, v_w_down, v_ln2_g, v_ln2_b):
    x2, target = x[0], loss_target[0]
    t_len, d_model = x2.shape
    width = g_conv.shape[1]
    in_shard = w_in.shape[2]
    up_shard = w_up.shape[2]
    d_ff = up_shard * N_DEV
    cw = conv_w.shape[2]
    xi, yi, ci = _place()
    dev = 4 * xi + 2 * yi + ci
    tm = 256

    win_g, convw_g, wout_bf, wup_bf, wdown_bf = _all_gather(
        "gather_w_in", [w_in[0], jnp.pad(conv_w[0], ((0, 5), (0, LANES - cw)))], [BF16, F32],
        cast_only=[w_out[0], w_up[0], w_down[0]])
    conv_full = jnp.transpose(convw_g[:, :3, :cw], (1, 0, 2)).reshape(3, width)

    n_row_tiles = t_len // tm
    rows = lambda cols: pl.BlockSpec((tm, cols), lambda i: (i, 0))
    resident = lambda arr: pl.BlockSpec(arr.shape, lambda i: (0,) * arr.ndim)
    vec = pl.BlockSpec((1, d_model), lambda i: (0, 0))
    acc8 = pl.BlockSpec((8, d_model), lambda i: (0, 0))
    tokens = lambda cols, dtype: jax.ShapeDtypeStruct((t_len, cols), dtype)

    def proj_in_epilogue(acc, extra, outs, i):
        outs[0][...] = acc[:, :3 * width]
        outs[1][...] = acc[:, 3 * width:].astype(BF16)
        outs[2][...] = extra[0][...].astype(BF16)

    gates, qkv, x_bf, wdown_g = _matmul(
        "proj_in", x2, win_g, rows(d_model), resident(win_g), _shards_wide, n_row_tiles,
        [(tokens(3 * width, F32), rows(3 * width)), (tokens(3 * width, BF16), rows(3 * width)),
         (tokens(d_model, BF16), rows(d_model))],
        proj_in_epilogue, extras=[(x2, rows(d_model))], gather=[wdown_bf])

    ymix = _conv_fwd(gates, conv_full, g_conv, width)
    o_attn, ymix, (wout_g, wup_g) = _attn_fwd(qkv, g_attn, ymix, width, [wout_bf, wup_bf])
    wout_full = wout_g.reshape(d_model, d_model)
    wdown_full = wdown_g.reshape(d_ff, d_model)

    def out_ln1_epilogue(acc, extra, outs, i):
        x_ref, g_ref, b_ref = extra
        r1 = ALPHA * x_ref[...] + acc
        x1, _, _ = _ln_fwd(r1, g_ref[...], b_ref[...])
        outs[0][...] = r1
        outs[1][...] = x1
        outs[2][...] = x1.astype(BF16)

    r1, x1, x1_bf = _matmul(
        "proj_out_ln1", ymix, wout_full, rows(d_model), resident(wout_full), _whole(_NN), n_row_tiles,
        [(tokens(d_model, F32), rows(d_model)), (tokens(d_model, F32), rows(d_model)),
         (tokens(d_model, BF16), rows(d_model))],
        out_ln1_epilogue, extras=[(x2, rows(d_model)), (ln1_g, vec), (ln1_b, vec)])

    def up_epilogue(acc, extra, outs, i):
        outs[0][...] = acc
        relu = jnp.maximum(acc, 0.0)
        outs[1][...] = (relu * relu).astype(BF16)

    up, hid = _matmul(
        "ffn_up", x1_bf, wup_g, rows(d_model), resident(wup_g), _shards_wide, n_row_tiles,
        [(tokens(d_ff, F32), rows(d_ff)), (tokens(d_ff, BF16), rows(d_ff))], up_epilogue)

    def down_ln2_epilogue(acc, extra, outs, i):
        x1_ref, t_ref, g_ref, b_ref = extra
        dr2_ref, dr2_bf_ref, loss_ref, gg_ref, gb_ref = outs
        gain = g_ref[...]
        r2 = ALPHA * x1_ref[...] + acc
        y, xhat, rstd = _ln_fwd(r2, gain, b_ref[...])
        diff = y - t_ref[...]
        d_y = diff * (1.0 / d_model)
        dr2 = _ln_bwd(d_y, xhat, rstd, gain)
        dr2_ref[...] = dr2
        dr2_bf_ref[...] = dr2.astype(BF16)
        _accumulate(gg_ref, i, n_row_tiles, _rows8(d_y * xhat))
        _accumulate(gb_ref, i, n_row_tiles, _rows8(d_y))
        _accumulate(loss_ref, i, n_row_tiles, _rows8(diff * diff), all_lanes=True)

    small_acc = (jax.ShapeDtypeStruct((8, d_model), F32), acc8)
    dr2, dr2_bf, loss_cols, g_ln2_g, g_ln2_b = _matmul(
        "ffn_down_ln2_loss", hid, wdown_full, rows(d_ff), resident(wdown_full), _whole(_NN), n_row_tiles,
        [(tokens(d_model, F32), rows(d_model)), (tokens(d_model, BF16), rows(d_model)),
         small_acc, small_acc, small_acc],
        down_ln2_epilogue, extras=[(x1, rows(d_model)), (target, rows(d_model)), (ln2_g, vec), (ln2_b, vec)])

    def dup_epilogue(acc, extra, outs, i):
        outs[0][...] = (acc * (2.0 * jnp.maximum(extra[0][...], 0.0))).astype(BF16)

    (dup,) = _matmul(
        "ffn_dhid", dr2_bf, wdown_full, rows(d_model), resident(wdown_full), _whole(_NT), n_row_tiles,
        [(tokens(d_ff, BF16), rows(d_ff))], dup_epilogue, extras=[(up, rows(d_ff))])

    gw_down, pair_down = _weight_grad("grad_w_down", hid, dr2_bf, (d_ff // N_DEV, d_model), True)
    gw_up, pair_up = _weight_grad("grad_w_up", x1_bf, dup, (d_model, up_shard), False)

    def dx1_ln1_epilogue(acc, extra, outs, i):
        dr2_ref, r1_ref, g_ref, b_ref = extra
        dr1_ref, dr1_bf_ref, gg_ref, gb_ref = outs
        gain = g_ref[...]
        dx1 = ALPHA * dr2_ref[...] + acc
        _, xhat, rstd = _ln_fwd(r1_ref[...], gain, b_ref[...])
        dr1 = _ln_bwd(dx1, xhat, rstd, gain)
        dr1_ref[...] = dr1
        dr1_bf_ref[...] = dr1.astype(BF16)
        _accumulate(gg_ref, i, n_row_tiles, _rows8(dx1 * xhat))
        _accumulate(gb_ref, i, n_row_tiles, _rows8(dx1))

    dr1, dr1_bf, g_ln1_g, g_ln1_b = _matmul(
        "ffn_dx1_ln1", dup, wup_g, rows(d_ff), resident(wup_g), _shards_deep, n_row_tiles,
        [(tokens(d_model, F32), rows(d_model)), (tokens(d_model, BF16), rows(d_model)), small_acc, small_acc],
        dx1_ln1_epilogue, extras=[(dr2, rows(d_model)), (r1, rows(d_model)), (ln1_g, vec), (ln1_b, vec)])

    gw_out, pair_out = _weight_grad("grad_w_out", ymix, dr1_bf, (d_model // N_DEV, d_model), True)

    c_idx = jnp.reshape(ci, (1,)).astype(jnp.int32)
    q_me = jnp.reshape(2 * xi + yi, (1,)).astype(jnp.int32)

    def pair_sums(names, grads, landed):
        sums = [_pair_sum("pair_sum_" + nm, g, land, c_idx, q_me) for nm, g, land in zip(names, grads, landed)]
        return [s[0] for s in sums], [s[1] for s in sums]

    wire_ffn, own_ffn = pair_sums(["w_out", "w_up", "w_down"], [gw_out, gw_up, gw_down],
                                  [pair_out, pair_up, pair_down])

    def plain_epilogue(acc, extra, outs, i):
        outs[0][...] = acc

    (dymix,) = _matmul(
        "proj_out_bwd", dr1_bf, wout_full, rows(d_model), resident(wout_full), _whole(_NT), n_row_tiles,
        [(tokens(d_model, F32), rows(d_model))], plain_epilogue)

    d_bg, d_cg, d_h, g_gconv, g_convw = _conv_bwd(gates, dymix, conv_full, g_conv, width)
    d_q, d_k, d_v, g_gattn, landed_ffn = _attn_bwd(qkv, o_attn, dymix, g_attn, width, wire_ffn)
    dproj = jnp.concatenate([d_bg, d_cg, d_h, d_q, d_k, d_v], axis=1)

    gw_in, pair_in = _weight_grad("grad_w_in", x_bf, dproj, (d_model, in_shard), False)
    wire_in, own_in = pair_sums(["w_in"], [gw_in], [pair_in])

    def dx_epilogue(acc, extra, outs, i):
        outs[0][...] = ALPHA * extra[0][...] + acc

    spare = jnp.zeros((3, d_model - width), F32)

    def pack(l1g, l1b, l2g, l2b, gc, ga, taps, tail=spare):
        return jnp.concatenate([l1g, l1b, l2g, l2b, jnp.concatenate([gc, ga], axis=1),
                                jnp.concatenate([taps, tail], axis=1)], axis=0)

    local = pack(g_ln1_g[:1], g_ln1_b[:1], g_ln2_g[:1], g_ln2_b[:1], g_gconv[:1], g_gattn[:1], g_convw[:3],
                 spare + (0.5 / d_model) * loss_cols[0, 0])

    grad_x, landed_in, parts = _matmul(
        "proj_in_bwd", dproj, win_g, rows(6 * width), resident(win_g), _shards_deep, n_row_tiles,
        [(tokens(d_model, F32), rows(d_model))], dx_epilogue, extras=[(dr1, rows(d_model))],
        exchange=wire_in, gather=[local])

    names = ["w_in", "w_out", "w_up", "w_down"]
    big = {}
    for nm, ow, land, w, m, v in zip(names, own_in + own_ffn, [landed_in] + list(landed_ffn),
                                     (w_in, w_out, w_up, w_down),
                                     (m_w_in, m_w_out, m_w_up, m_w_down), (v_w_in, v_w_out, v_w_up, v_w_down)):
        big[nm] = [r[None] for r in _shard_update("update_" + nm, ow, land, w[0], m[0], v[0])]

    def spread(a):
        return lax.dynamic_update_slice(jnp.zeros((3, width), F32), a, (0, dev * cw))

    w_small = pack(ln1_g, ln1_b, ln2_g, ln2_b, g_conv, g_attn, spread(conv_w[0]))
    m_small = pack(m_ln1_g, m_ln1_b, m_ln2_g, m_ln2_b, m_g_conv, m_g_attn, spread(m_conv_w[0]))
    v_small = pack(v_ln1_g, v_ln1_b, v_ln2_g, v_ln2_b, v_g_conv, v_g_attn, spread(v_conv_w[0]))
    small = _small_update(parts, w_small, m_small, v_small)

    def unpack(a):
        taps = lax.dynamic_slice(a[5:8, :width], (0, dev * cw), (3, cw))[None]
        return {"ln1_g": a[0:1], "ln1_b": a[1:2], "ln2_g": a[2:3], "ln2_b": a[3:4],
                "g_conv": a[4:5, :width], "g_attn": a[4:5, width:], "conv_w": taps}

    loss = small[0][7, d_model - 1]
    small = [unpack(a) for a in small]

    order = ["w_in", "conv_w", "g_conv", "g_attn", "w_out", "ln1_g", "ln1_b", "w_up", "w_down", "ln2_g", "ln2_b"]
    result = [loss, grad_x[None]]
    for kind in range(4):
        for nm in order:
            result.append(big[nm][kind] if nm in big else small[kind][nm])
    return tuple(result)
```

```python
import functools

import jax
import jax.numpy as jnp
from jax import lax
from jax.experimental import pallas as pl
from jax.experimental.pallas import tpu as pltpu

F32 = jnp.float32
BF16 = jnp.bfloat16
MESH = pl.DeviceIdType.MESH

N_DEV = 8
HEAD_DIM = 64
LANES = 128
Q_BLOCK = 128
ALPHA = 2.0 ** 0.25
LN_EPS = 1e-5
RMS_EPS = 1e-6
ADAM_LR, ADAM_B1, ADAM_B2, ADAM_EPS, ADAM_WD, ADAM_STEP = 0.001, 0.9, 0.999, 1e-08, 0.01, 10
VMEM_LIMIT_BYTES = 48 * 1024 * 1024

_NT = ((1,), (1,))
_NN = ((1,), (0,))
_TN = ((0,), (0,))


def _params(sem=None):
    return pltpu.CompilerParams(dimension_semantics=sem, vmem_limit_bytes=VMEM_LIMIT_BYTES)


def _dot(a, b, contract):
    return lax.dot_general(a.astype(BF16), b.astype(BF16), (contract, ((), ())), preferred_element_type=F32)


def _whole(contract):
    return lambda a_ref, b_ref: _dot(a_ref[...], b_ref[...], contract)


def _shards_wide(a_ref, b_ref):
    a = a_ref[...].astype(BF16)
    return jnp.concatenate([_dot(a, b_ref[s], _NN) for s in range(b_ref.shape[0])], axis=1)


def _shards_deep(a_ref, b_ref):
    n = b_ref.shape[2]
    acc = _dot(a_ref[:, 0:n], b_ref[0], _NT)
    for s in range(1, b_ref.shape[0]):
        acc = acc + _dot(a_ref[:, s * n:(s + 1) * n], b_ref[s], _NT)
    return acc


def _matmul(name, a, b, a_spec, b_spec, product, n_steps, outs, epilogue, extras=(), exchange=(), gather=(),
            carry=()):
    behind = list(exchange) + list(gather)
    n_extra, n_out, n_ex, n_b, n_carry = len(extras), len(outs), len(exchange), len(behind), len(carry)

    def body(a_ref, b_ref, *rest):
        i = pl.program_id(0)
        extra_refs, rest = rest[:n_extra], rest[n_extra:]
        src_refs, rest = rest[:n_b], rest[n_b:]
        out_refs, rest = rest[:n_out], rest[n_out:]
        land_refs, rest = rest[:n_b], rest[n_b:]
        extra_refs, sems = extra_refs + rest[:n_carry], rest[n_carry:]
        copies = _chip_copies(src_refs[:n_ex], land_refs[:n_ex], *sems[:2]) if exchange else []
        gather_sems = sems[2:] if exchange else sems
        plan = _gather_plan(src_refs[n_ex:], land_refs[n_ex:], *gather_sems) if gather else None

        def start():
            for cp in copies:
                cp.start()
            if plan:
                plan[0]()

        def finish():
            if plan:
                plan[1]()
                plan[2]()
            for cp in copies:
                cp.wait()

        if behind:
            pl.when(i == 0)(start)
        epilogue(product(a_ref, b_ref), extra_refs, out_refs, i)
        if behind:
            pl.when(i == n_steps - 1)(finish)

    hbm = pl.BlockSpec(memory_space=pl.ANY)
    landing = _chip_landing(exchange) + [jax.ShapeDtypeStruct((N_DEV,) + s.shape, s.dtype) for s in gather]
    scratch = (list(carry) + (_chip_scratch(len(exchange)) if exchange else [])
               + (_gather_scratch(len(gather)) if gather else []))
    return pl.pallas_call(
        body, name=name, grid=(n_steps,),
        in_specs=[a_spec, b_spec] + [s for _, s in extras] + [hbm] * n_b,
        out_specs=[s for _, s in outs] + [hbm] * n_b,
        out_shape=[o for o, _ in outs] + landing,
        scratch_shapes=scratch,
        compiler_params=_params(("arbitrary",)),
    )(a, b, *[e for e, _ in extras], *behind)


def _ln_fwd(r, g, b):
    mu = jnp.mean(r, axis=-1, keepdims=True)
    xc = r - mu
    rstd = lax.rsqrt(jnp.mean(xc * xc, axis=-1, keepdims=True) + LN_EPS)
    xhat = xc * rstd
    return xhat * g + b, xhat, rstd


def _ln_bwd(dy, xhat, rstd, g):
    dxh = dy * g
    m1 = jnp.mean(dxh, axis=-1, keepdims=True)
    m2 = jnp.mean(dxh * xhat, axis=-1, keepdims=True)
    return rstd * (dxh - m1 - xhat * m2)


def _rows8(v):
    n, c = v.shape
    return jnp.sum(v.reshape(n // 8, 8, c), axis=0)


def _accumulate(ref, i, n_steps, part8, all_lanes=False):
    @pl.when(i == 0)
    def _():
        ref[...] = part8

    @pl.when(i > 0)
    def _():
        ref[...] += part8

    @pl.when(i == n_steps - 1)
    def _():
        tot = jnp.sum(ref[...], axis=0, keepdims=True)
        if all_lanes:
            tot = jnp.sum(tot, axis=1, keepdims=True)
        ref[...] = jnp.broadcast_to(tot, ref.shape)


def _group_sums(v, lane_lo):
    s0 = jnp.sum(jnp.where(lane_lo, v, 0.0), axis=-1, keepdims=True)
    s1 = jnp.sum(jnp.where(lane_lo, 0.0, v), axis=-1, keepdims=True)
    return jnp.where(lane_lo, s0, s1)


def _place():
    return lax.axis_index("x"), lax.axis_index("y"), lax.axis_index("c")


def _gather_plan(src_refs, out_refs, send_sems, recv_sems, local_sems):
    n = len(src_refs)
    x, y, c = _place()
    me, sibling = (x, y, c), (x, y, 1 - c)
    chips = [(1 - x, y), (x, 1 - y), (1 - x, 1 - y)]

    def copy(a, k, block, to, src=None):
        slot = out_refs[a].at[4 * block[0] + 2 * block[1] + block[2]]
        return pltpu.make_async_remote_copy(
            src_ref=slot if src is None else src, dst_ref=slot,
            send_sem=send_sems.at[a, k], recv_sem=recv_sems.at[a, k],
            device_id=to, device_id_type=MESH)

    mine = [pltpu.make_async_copy(src_refs[a], out_refs[a].at[4 * x + 2 * y + c], local_sems.at[a]) for a in range(n)]
    first = [[copy(a, 0, me, sibling, src=src_refs[a])]
             + [copy(a, 1 + j, me, (*chip, c), src=src_refs[a]) for j, chip in enumerate(chips)] for a in range(n)]
    passed = [[copy(a, 4 + j, (*chip, c), sibling) for j, chip in enumerate(chips)] for a in range(n)]

    def start():
        for a in range(n):
            mine[a].start()
            for cp in first[a]:
                cp.start()

    def forward():
        for j, chip in enumerate(chips):
            for a in range(n):
                copy(a, 1 + j, (*chip, c), me).wait_recv()
                passed[a][j].start()

    def finish():
        for a in range(n):
            copy(a, 0, sibling, me).wait_recv()
            for j, chip in enumerate(chips):
                copy(a, 4 + j, (*chip, 1 - c), me).wait_recv()
        for a in range(n):
            for cp in first[a] + passed[a]:
                cp.wait_send()
            mine[a].wait()

    return start, forward, finish


def _gather_scratch(n):
    return [pltpu.SemaphoreType.DMA((n, 7)), pltpu.SemaphoreType.DMA((n, 7)), pltpu.SemaphoreType.DMA((n,))]


def _all_gather(name, shards, out_dtypes, cast_only=()):
    n, m = len(shards), len(cast_only)

    def body(*refs):
        in_refs, cast_in = refs[:n], refs[n:n + m]
        out_refs, cast_out = refs[n + m:2 * n + m], refs[2 * n + m:2 * (n + m)]
        stage = refs[2 * (n + m):3 * n + 2 * m]
        for a in range(n):
            stage[a][...] = in_refs[a][...].astype(out_dtypes[a])
        start, forward, finish = _gather_plan(stage, out_refs, *refs[3 * n + 2 * m:])
        start()
        for a in range(m):
            cast_out[a][...] = cast_in[a][...].astype(BF16)
        forward()
        finish()

    return pl.pallas_call(
        body, name=name,
        in_specs=[pl.BlockSpec(memory_space=pltpu.VMEM)] * (n + m),
        out_specs=[pl.BlockSpec(memory_space=pl.ANY)] * n + [pl.BlockSpec(memory_space=pltpu.VMEM)] * m,
        out_shape=[jax.ShapeDtypeStruct((N_DEV,) + s.shape, dt) for s, dt in zip(shards, out_dtypes)]
        + [jax.ShapeDtypeStruct(s.shape, BF16) for s in cast_only],
        scratch_shapes=[pltpu.VMEM(s.shape, dt) for s, dt in zip(shards, out_dtypes)] + _gather_scratch(n),
        compiler_params=pltpu.CompilerParams(vmem_limit_bytes=VMEM_LIMIT_BYTES),
    )(*shards, *cast_only)


def _weight_grad(name, acts, cots, shard_shape, acts_by_shard):
    t_len = acts.shape[0]
    sr, sc = shard_shape
    per = max(1, 256 // sr) if acts_by_shard else 1
    assert per in (1, 2)
    n_steps = N_DEV // per

    def body(a_ref, b_ref, g_ref, land_ref, stage, send_sems, recv_sems):
        i = pl.program_id(0)
        x, y, c = _place()

        def copy(q):
            return pltpu.make_async_remote_copy(
                src_ref=stage.at[q % 2], dst_ref=land_ref.at[q], send_sem=send_sems.at[q], recv_sem=recv_sems.at[q],
                device_id=(x, y, 1 - c), device_id_type=MESH)

        acc = _dot(a_ref[...], b_ref[...], _TN).reshape(per, sr, sc)
        g_ref[...] = acc
        if per == 2:
            payload, due, q = jnp.where(c == 0, acc[1], acc[0]), i >= 0, i
        else:
            payload, due, q = acc[0], (i % 2) == (1 - c), i // 2

        @pl.when(due)
        def _():
            @pl.when(q >= 2)
            def _():
                copy(q - 2).wait_send()

            stage[q % 2] = payload.astype(BF16)
            copy(q).start()

        @pl.when(i == n_steps - 1)
        def _():
            copy(2).wait_send()
            copy(3).wait_send()
            for k in range(4):
                copy(k).wait_recv()

    tall = lambda cols: pl.BlockSpec((t_len, cols), lambda i: (0, i))
    whole = lambda arr: pl.BlockSpec(arr.shape, lambda i: (0, 0))
    a_spec, b_spec = (tall(per * sr), whole(cots)) if acts_by_shard else (whole(acts), tall(sc))
    return pl.pallas_call(
        body, name=name, grid=(n_steps,),
        in_specs=[a_spec, b_spec],
        out_specs=[pl.BlockSpec((per, sr, sc), lambda i: (i, 0, 0)), pl.BlockSpec(memory_space=pl.ANY)],
        out_shape=[jax.ShapeDtypeStruct((N_DEV, sr, sc), F32), jax.ShapeDtypeStruct((4, sr, sc), BF16)],
        scratch_shapes=[pltpu.VMEM((2, sr, sc), BF16), pltpu.SemaphoreType.DMA((4,)), pltpu.SemaphoreType.DMA((4,))],
        compiler_params=_params(("arbitrary",)),
    )(acts, cots)


def _chip_copies(p_refs, land_refs, send_sems, recv_sems):
    x, y, c = _place()
    chips = [(1 - x, y), (x, 1 - y), (1 - x, 1 - y)]
    return [pltpu.make_async_remote_copy(
        src_ref=p_refs[a].at[2 * px + py], dst_ref=land_refs[a].at[k],
        send_sem=send_sems.at[a, k], recv_sem=recv_sems.at[a, k],
        device_id=(px, py, c), device_id_type=MESH)
        for a in range(len(p_refs)) for k, (px, py) in enumerate(chips)]


def _chip_scratch(n):
    return [pltpu.SemaphoreType.DMA((n, 3)), pltpu.SemaphoreType.DMA((n, 3))]


def _chip_landing(pair_sums):
    return [jax.ShapeDtypeStruct((3,) + p.shape[1:], p.dtype) for p in pair_sums]


def _pair_sum(name, g_f32, landed, c_idx, q_me):
    _, rows, cols = g_f32.shape
    tr = min(rows, 1024)

    def body(c_ref, q_ref, g_ref, l_ref, wire_ref, own_ref):
        q = pl.program_id(1)
        s = g_ref[...] + l_ref[...].astype(F32)
        wire_ref[...] = s.astype(BF16)

        @pl.when(q == q_ref[0])
        def _():
            own_ref[...] = s

    return pl.pallas_call(
        body, name=name,
        grid_spec=pltpu.PrefetchScalarGridSpec(
            num_scalar_prefetch=2, grid=(rows // tr, 4),
            in_specs=[pl.BlockSpec((None, tr, cols), lambda i, q, c_ref, q_ref: (2 * q + c_ref[0], i, 0)),
                      pl.BlockSpec((None, tr, cols), lambda i, q, c_ref, q_ref: (q, i, 0))],
            out_specs=[pl.BlockSpec((None, tr, cols), lambda i, q, c_ref, q_ref: (q, i, 0)),
                       pl.BlockSpec((tr, cols), lambda i, q, c_ref, q_ref: (i, 0))]),
        out_shape=[jax.ShapeDtypeStruct((4, rows, cols), BF16), jax.ShapeDtypeStruct((rows, cols), F32)],
        compiler_params=_params(("parallel", "arbitrary")),
    )(c_idx, q_me, g_f32, landed)


def _adamw(w, g, m, v):
    m = ADAM_B1 * m + (1.0 - ADAM_B1) * g
    v = ADAM_B2 * v + (1.0 - ADAM_B2) * (g * g)
    m_hat = m / (1.0 - ADAM_B1 ** ADAM_STEP)
    v_hat = v / (1.0 - ADAM_B2 ** ADAM_STEP)
    delta = -ADAM_LR * (m_hat / (jnp.sqrt(v_hat) + ADAM_EPS) + ADAM_WD * w)
    return delta, m, v


def _shard_update(name, own, landed, w, m, v):
    rows, cols = own.shape
    tr = min(rows, 256)

    def body(own_ref, l_ref, w_ref, m_ref, v_ref, g_out, d_out, m_out, v_out):
        g = own_ref[...] + l_ref[0].astype(F32) + l_ref[1].astype(F32) + l_ref[2].astype(F32)
        d, m_new, v_new = _adamw(w_ref[...], g, m_ref[...], v_ref[...])
        g_out[...] = g
        d_out[...] = d
        m_out[...] = m_new
        v_out[...] = v_new

    tile = pl.BlockSpec((tr, cols), lambda i: (i, 0))
    return pl.pallas_call(
        body, name=name, grid=(rows // tr,),
        in_specs=[tile, pl.BlockSpec((3, tr, cols), lambda i: (0, i, 0)), tile, tile, tile],
        out_specs=[tile] * 4,
        out_shape=[jax.ShapeDtypeStruct((rows, cols), F32)] * 4,
        compiler_params=_params(("parallel",)),
    )(own, landed, w, m, v)


def _small_update(parts, w, m, v):
    def body(p_ref, w_ref, m_ref, v_ref, g_out, d_out, m_out, v_out):
        g = p_ref[0]
        for d in range(1, N_DEV):
            g = g + p_ref[d]
        dl, m_new, v_new = _adamw(w_ref[...], g, m_ref[...], v_ref[...])
        g_out[...] = g
        d_out[...] = dl
        m_out[...] = m_new
        v_out[...] = v_new

    return pl.pallas_call(
        body, name="small_update",
        out_shape=[jax.ShapeDtypeStruct(w.shape, F32)] * 4,
        compiler_params=pltpu.CompilerParams(vmem_limit_bytes=VMEM_LIMIT_BYTES),
    )(parts, w, m, v)


CONV_CHUNK = 512


def _conv_taps(u_s, lo, chunk):
    ext = u_s[pl.ds(lo, chunk + 8), :]
    return ext[8:], pltpu.roll(ext, 1, 0)[8:], pltpu.roll(ext, 2, 0)[8:]


def _conv_bwd(gates, dymix, conv_w, g_conv, width):
    t_len = gates.shape[0]
    nblk = width // LANES
    n_chunks = t_len // CONV_CHUNK

    def body(bg_ref, cg_ref, h_ref, dy_ref, w_ref, g_ref, dbg_ref, dcg_ref, dh_ref, gg_ref, gw_ref, u_s, dc_s):
        lane_lo = lax.broadcasted_iota(jnp.int32, (CONV_CHUNK, LANES), 1) < HEAD_DIM
        u_s[0:8, :] = jnp.zeros((8, LANES), F32)
        dc_s[t_len:t_len + 8, :] = jnp.zeros((8, LANES), F32)

        def fill(r, _):
            lo = pl.multiple_of(r * CONV_CHUNK, CONV_CHUNK)
            u_s[pl.ds(pl.multiple_of(lo + 8, 8), CONV_CHUNK), :] = cg_ref[pl.ds(lo, CONV_CHUNK), :] * h_ref[pl.ds(lo, CONV_CHUNK), :]
            return 0

        lax.fori_loop(0, n_chunks, fill, 0)
        w = w_ref[...]
        gain = g_ref[...]

        def step(r, acc):
            gg, gw0, gw1, gw2 = acc
            lo = pl.multiple_of(r * CONV_CHUNK, CONV_CHUNK)
            u0, u1, u2 = _conv_taps(u_s, lo, CONV_CHUNK)
            bg = bg_ref[pl.ds(lo, CONV_CHUNK), :]
            conv = w[2:3] * u0 + w[1:2] * u1 + w[0:1] * u2
            y = bg * conv
            rs = lax.rsqrt(_group_sums(y * y, lane_lo) * (1.0 / HEAD_DIM) + RMS_EPS)
            yn = y * rs
            dyn_raw = dy_ref[pl.ds(lo, CONV_CHUNK), :]
            dyn = dyn_raw * gain
            d_y = rs * (dyn - yn * (_group_sums(dyn * yn, lane_lo) * (1.0 / HEAD_DIM)))
            dbg_ref[pl.ds(lo, CONV_CHUNK), :] = (d_y * conv).astype(BF16)
            dconv = d_y * bg
            dc_s[pl.ds(lo, CONV_CHUNK), :] = dconv
            return (gg + _rows8(dyn_raw * yn), gw0 + _rows8(dconv * u2), gw1 + _rows8(dconv * u1),
                    gw2 + _rows8(dconv * u0))

        zero = jnp.zeros((8, LANES), F32)
        gg, gw0, gw1, gw2 = lax.fori_loop(0, n_chunks, step, (zero, zero, zero, zero))
        gg_ref[...] = jnp.broadcast_to(jnp.sum(gg, axis=0, keepdims=True), (8, LANES))
        row = lax.broadcasted_iota(jnp.int32, (8, LANES), 0)
        tot = lambda v: jnp.broadcast_to(jnp.sum(v, axis=0, keepdims=True), (8, LANES))
        gw_ref[...] = jnp.where(row == 0, tot(gw0), jnp.where(row == 1, tot(gw1), jnp.where(row == 2, tot(gw2), 0.0)))

        def back(r, _):
            lo = pl.multiple_of(r * CONV_CHUNK, CONV_CHUNK)
            ext = dc_s[pl.ds(lo, CONV_CHUNK + 8), :]
            d0 = ext[:CONV_CHUNK]
            d1 = pltpu.roll(ext, CONV_CHUNK + 7, 0)[:CONV_CHUNK]
            d2 = pltpu.roll(ext, CONV_CHUNK + 6, 0)[:CONV_CHUNK]
            du = w[2:3] * d0 + w[1:2] * d1 + w[0:1] * d2
            dcg_ref[pl.ds(lo, CONV_CHUNK), :] = (du * h_ref[pl.ds(lo, CONV_CHUNK), :]).astype(BF16)
            dh_ref[pl.ds(lo, CONV_CHUNK), :] = (du * cg_ref[pl.ds(lo, CONV_CHUNK), :]).astype(BF16)
            return 0

        lax.fori_loop(0, n_chunks, back, 0)

    col = lambda off: pl.BlockSpec((t_len, LANES), lambda p: (0, off + p))
    small = pl.BlockSpec((8, LANES), lambda p: (0, p))
    return pl.pallas_call(
        body, name="conv_bwd", grid=(nblk,),
        in_specs=[col(0), col(nblk), col(2 * nblk), col(0),
                  pl.BlockSpec((3, LANES), lambda p: (0, p)), pl.BlockSpec((1, LANES), lambda p: (0, p))],
        out_specs=[col(0), col(0), col(0), small, small],
        out_shape=[jax.ShapeDtypeStruct((t_len, width), BF16)] * 3 + [jax.ShapeDtypeStruct((8, width), F32)] * 2,
        scratch_shapes=[pltpu.VMEM((t_len + 8, LANES), F32), pltpu.VMEM((t_len + 8, LANES), F32)],
        compiler_params=_params(("parallel",)),
    )(gates, gates, gates, dymix, conv_w, g_conv)


def _cumsum_matrix(inclusive):
    j = lax.broadcasted_iota(jnp.int32, (2 * LANES, 2 * LANES), 0) % LANES
    s = lax.broadcasted_iota(jnp.int32, (2 * LANES, 2 * LANES), 1)
    keep = (s >= LANES) | ((j >= s) if inclusive else (j > s))
    return keep.astype(BF16)


KEY_CHUNK = 384
SUB_TILES = KEY_CHUNK // Q_BLOCK
Q_GROUP = 2
DEAD_BELOW = -105.0
MASKED_Z = -1e30


def _scores_phase(qms, kcs, masks):
    zs = [lax.dot_general(q, k, (_NT, ((), ())), preferred_element_type=F32) for q, k in zip(qms, kcs)]
    return [jnp.where(m, z, MASKED_Z) for z, m in zip(zs, masks)]


def _logs_phase(zs):
    es = [jnp.exp(-jnp.abs(z)) for z in zs]
    lks = [-(jnp.maximum(z, 0.0) + jnp.log(1.0 + e)) for z, e in zip(zs, es)]
    return es, lks, [z + lk for z, lk in zip(zs, lks)]


def _suffix_phase(vals, carries, mat):
    def hi_lo(tile):
        hi = tile.astype(BF16)
        return jnp.concatenate([hi, (tile - hi.astype(F32)).astype(BF16)], axis=1)

    packed = [[hi_lo(v[:, t * Q_BLOCK:(t + 1) * Q_BLOCK]) for t in range(SUB_TILES)] for v in vals]
    prods = [[jnp.dot(p, mat, preferred_element_type=F32) for p in chain] for chain in packed]
    sums, new_carries = [], []
    for chain, carry in zip(prods, carries):
        parts = []
        for t in reversed(range(SUB_TILES)):
            parts.append(chain[t][:, :LANES] + carry)
            carry = carry + chain[t][:, LANES:]
        sums.append(jnp.concatenate(parts[::-1], axis=1))
        new_carries.append(carry)
    return sums, new_carries


def _sweep(first_block, step, states, carries_of):
    n_chunks = ((first_block + Q_GROUP) * Q_BLOCK + KEY_CHUNK - 1) // KEY_CHUNK
    col = lax.broadcasted_iota(jnp.int32, (Q_BLOCK, KEY_CHUNK), 1)
    row = lax.broadcasted_iota(jnp.int32, (Q_BLOCK, 1), 0)

    def body(loop):
        n, _, sts = loop
        starts, masks = [], []
        for b in range(Q_GROUP):
            i = first_block + b
            end = (i + 1) * Q_BLOCK - n * KEY_CHUNK
            starts.append(pl.multiple_of(jnp.maximum(end - KEY_CHUNK, 0), Q_BLOCK))
            masks.append(col < jnp.minimum(i * Q_BLOCK + row, end) - starts[b])
        sts = step(starts, masks, sts)
        lowest = None
        for st in sts:
            for carry in carries_of(st):
                lowest = carry if lowest is None else jnp.maximum(lowest, carry)
        alive = (jnp.max(lowest) >= DEAD_BELOW).astype(jnp.int32)
        return n + 1, alive, tuple(sts)

    return lax.while_loop(lambda loop: (loop[0] < n_chunks) & (loop[1] > 0), body,
                          (jnp.int32(0), jnp.int32(1), tuple(states)))[2]


CHAINS = [(b, h) for b in range(Q_GROUP) for h in range(2)]


def _head_rows(block, lane_lo_rows):
    zero = jnp.zeros_like(block)
    return jnp.concatenate([jnp.where(lane_lo_rows, block, zero), jnp.where(lane_lo_rows, zero, block)], axis=0)


def _attn_fwd(qkv, g_attn, ymix_in, width, shards):
    t_len = qkv.shape[0]
    npair = width // LANES
    rows_per_step = Q_GROUP * Q_BLOCK
    n_steps = t_len // rows_per_step
    n_sh = len(shards)

    def body(q_ref, k_ref, v_ref, g_ref, umat_ref, ymix_in_ref, *rest):
        shard_refs, rest = rest[:n_sh], rest[n_sh:]
        o_ref, ymix_ref = rest[:2]
        gathered_refs, sems = rest[2:2 + n_sh], rest[2 + n_sh:]
        p, i = pl.program_id(0), pl.program_id(1)
        start, forward, finish = _gather_plan(shard_refs, gathered_refs, *sems)
        pl.when((p == 0) & (i == 0))(start)
        pl.when((p == npair - 1) & (i == n_steps // 2))(forward)
        lane_lo = lax.broadcasted_iota(jnp.int32, (rows_per_step, LANES), 1) < HEAD_DIM
        lane_lo_keys = lax.broadcasted_iota(jnp.int32, (KEY_CHUNK, LANES), 1) < HEAD_DIM
        umat = umat_ref[...]
        q = q_ref[...] * jnp.asarray(HEAD_DIM ** -0.5, BF16)
        q_heads = (jnp.where(lane_lo, q, jnp.zeros_like(q)), jnp.where(lane_lo, jnp.zeros_like(q), q))

        def step(starts, masks, states):
            kcs = [k_ref[pl.ds(r, KEY_CHUNK), :] for r in starts]
            zs = _scores_phase([q_heads[h][b * Q_BLOCK:(b + 1) * Q_BLOCK] for b, h in CHAINS],
                               [kcs[b] for b, _ in CHAINS], [masks[b] for b, _ in CHAINS])
            _, lks, lbs = _logs_phase(zs)
            suffixes, carries = _suffix_phase(lks, [states[b][0][h] for b, h in CHAINS], umat)
            weights = [jnp.exp(lb + sfx).astype(BF16) for lb, sfx in zip(lbs, suffixes)]
            new = []
            for b in range(Q_GROUP):
                v_heads = _head_rows(v_ref[pl.ds(starts[b], KEY_CHUNK), :], lane_lo_keys)
                o_acc = states[b][1] + jnp.dot(jnp.concatenate(weights[2 * b:2 * b + 2], axis=1), v_heads,
                                               preferred_element_type=F32)
                new.append(((carries[2 * b], carries[2 * b + 1]), o_acc))
            return new

        zero = jnp.zeros((Q_BLOCK, LANES), F32)
        states = _sweep(i * Q_GROUP, step, [((zero, zero), zero)] * Q_GROUP, lambda st: st[0])
        o = jnp.concatenate([st[1] for st in states], axis=0)
        o_ref[...] = o
        rs = lax.rsqrt(_group_sums(o * o, lane_lo) * (1.0 / HEAD_DIM) + RMS_EPS)
        ymix_ref[...] = (o * rs * g_ref[...]).astype(BF16)
        pl.when((p == npair - 1) & (i == n_steps - 1))(finish)

    hbm = pl.BlockSpec(memory_space=pl.ANY)
    outs = pl.pallas_call(
        body, name="attn_fwd", grid=(npair, n_steps),
        in_specs=[pl.BlockSpec((rows_per_step, LANES), lambda p, i: (i, p)),
                  pl.BlockSpec((t_len, LANES), lambda p, i: (0, npair + p)),
                  pl.BlockSpec((t_len, LANES), lambda p, i: (0, 2 * npair + p)),
                  pl.BlockSpec((1, LANES), lambda p, i: (0, p)),
                  pl.BlockSpec((2 * LANES, 2 * LANES), lambda p, i: (0, 0)),
                  hbm] + [hbm] * n_sh,
        out_specs=[pl.BlockSpec((rows_per_step, LANES), lambda p, i: (i, p)),
                   pl.BlockSpec((rows_per_step, LANES), lambda p, i: (i, npair + p))] + [hbm] * n_sh,
        out_shape=[jax.ShapeDtypeStruct((t_len, width), F32), jax.ShapeDtypeStruct(ymix_in.shape, BF16)]
        + [jax.ShapeDtypeStruct((N_DEV,) + s.shape, s.dtype) for s in shards],
        input_output_aliases={5: 1},
        scratch_shapes=_gather_scratch(n_sh),
        compiler_params=_params(("arbitrary", "arbitrary")),
    )(qkv, qkv, qkv, g_attn, _cumsum_matrix(False), ymix_in, *shards)
    return outs[0], outs[1], outs[2:]


def _attn_bwd(qkv, o, dymix, g_attn, width, exchange):
    t_len = qkv.shape[0]
    npair = width // LANES
    rows_per_step = Q_GROUP * Q_BLOCK
    n_steps = t_len // rows_per_step
    scale = HEAD_DIM ** -0.5
    n_ex = len(exchange)

    def body(q_ref, k_ref, v_ref, o_ref, dy_ref, g_ref, umat_ref, gmat_ref, *rest):
        p_refs, rest = rest[:n_ex], rest[n_ex:]
        dq_ref, dk_ref, dv_ref, gg_ref = rest[:4]
        land_refs, rest = rest[4:4 + n_ex], rest[4 + n_ex:]
        dk_s, dv_s = rest[:2]
        copies = _chip_copies(p_refs, land_refs, *rest[2:])
        p, i = pl.program_id(0), pl.program_id(1)

        @pl.when((p == 0) & (i == 0))
        def _():
            for cp in copies:
                cp.start()

        lane_lo = lax.broadcasted_iota(jnp.int32, (rows_per_step, LANES), 1) < HEAD_DIM
        lane_lo_keys = lax.broadcasted_iota(jnp.int32, (KEY_CHUNK, LANES), 1) < HEAD_DIM
        umat, gmat = umat_ref[...], gmat_ref[...]

        @pl.when(i == 0)
        def _():
            dk_s[...] = jnp.zeros_like(dk_s)
            dv_s[...] = jnp.zeros_like(dv_s)

        o_blk = o_ref[...]
        rs = lax.rsqrt(_group_sums(o_blk * o_blk, lane_lo) * (1.0 / HEAD_DIM) + RMS_EPS)
        yn = o_blk * rs
        dy_raw = dy_ref[...]
        dyn = dy_raw * g_ref[...]
        d_o = rs * (dyn - yn * (_group_sums(dyn * yn, lane_lo) * (1.0 / HEAD_DIM)))
        _accumulate(gg_ref, i, n_steps, _rows8(dy_raw * yn))
        do_bf = d_o.astype(BF16)
        do_o = do_bf.astype(F32) * o_blk
        q = q_ref[...]
        blocks = [slice(b * Q_BLOCK, (b + 1) * Q_BLOCK) for b in range(Q_GROUP)]
        lane_lo_q = lane_lo[:Q_BLOCK]
        totals = [(jnp.sum(jnp.where(lane_lo_q, do_o[r], 0.0), axis=-1, keepdims=True),
                   jnp.sum(jnp.where(lane_lo_q, 0.0, do_o[r]), axis=-1, keepdims=True)) for r in blocks]
        q_rows = [_head_rows(q[r], lane_lo_q) for r in blocks]
        do_rows = [_head_rows(do_bf[r], lane_lo_q) for r in blocks]
        q_scaled = [qr * jnp.asarray(scale, BF16) for qr in q_rows]

        def step(starts, masks, states):
            kcs = [k_ref[pl.ds(r, KEY_CHUNK), :] for r in starts]
            vcs = [v_ref[pl.ds(r, KEY_CHUNK), :] for r in starts]
            head = lambda h: slice(h * Q_BLOCK, (h + 1) * Q_BLOCK)
            zs = _scores_phase([q_scaled[b][head(h)] for b, h in CHAINS], [kcs[b] for b, _ in CHAINS],
                               [masks[b] for b, _ in CHAINS])
            das = [lax.dot_general(do_rows[b][head(h)], vcs[b], (_NT, ((), ())), preferred_element_type=F32)
                   for b, h in CHAINS]
            es, lks, lbs = _logs_phase(zs)
            suffixes, carries = _suffix_phase(lks, [states[b][0][h] for b, h in CHAINS], umat)
            a_bf = [jnp.exp(lb + sfx).astype(BF16) for lb, sfx in zip(lbs, suffixes)]
            gs = [a.astype(F32) * da for a, da in zip(a_bf, das)]
            suffixes_g, carries_g = _suffix_phase(gs, [states[b][1][h] for b, h in CHAINS], gmat)
            dzs = []
            for (b, h), z, e, g, sfx_g in zip(CHAINS, zs, es, gs, suffixes_g):
                before = totals[b][h] - sfx_g
                inv = pl.reciprocal(1.0 + e, approx=True)
                sig = jnp.where(z >= 0.0, inv, e * inv)
                dzs.append(((g * (1.0 - sig) - before * sig) * scale).astype(BF16))
            new = []
            for b in range(Q_GROUP):
                pair = slice(2 * b, 2 * b + 2)
                dq_acc = states[b][2] + jnp.dot(jnp.concatenate(dzs[pair], axis=1), _head_rows(kcs[b], lane_lo_keys),
                                                preferred_element_type=F32)
                dk_s[pl.ds(starts[b], KEY_CHUNK), :] += lax.dot_general(
                    jnp.concatenate(dzs[pair], axis=0), q_rows[b], (_TN, ((), ())), preferred_element_type=F32)
                dv_s[pl.ds(starts[b], KEY_CHUNK), :] += lax.dot_general(
                    jnp.concatenate(a_bf[pair], axis=0), do_rows[b], (_TN, ((), ())), preferred_element_type=F32)
                new.append(((carries[2 * b], carries[2 * b + 1]), (carries_g[2 * b], carries_g[2 * b + 1]), dq_acc))
            return new

        zero = jnp.zeros((Q_BLOCK, LANES), F32)
        states = _sweep(i * Q_GROUP, step, [((zero, zero), (zero, zero), zero)] * Q_GROUP, lambda st: st[0])
        dq_ref[...] = jnp.concatenate([st[2] for st in states], axis=0).astype(BF16)

        @pl.when(i == n_steps - 1)
        def _():
            dk_ref[...] = dk_s[...].astype(BF16)
            dv_ref[...] = dv_s[...].astype(BF16)

        @pl.when((p == npair - 1) & (i == n_steps - 1))
        def _():
            for cp in copies:
                cp.wait()

    whole = pl.BlockSpec((t_len, LANES), lambda p, i: (0, p))
    hbm = pl.BlockSpec(memory_space=pl.ANY)
    outs = pl.pallas_call(
        body, name="attn_bwd", grid=(npair, n_steps),
        in_specs=[pl.BlockSpec((rows_per_step, LANES), lambda p, i: (i, p)),
                  pl.BlockSpec((t_len, LANES), lambda p, i: (0, npair + p)),
                  pl.BlockSpec((t_len, LANES), lambda p, i: (0, 2 * npair + p)),
                  pl.BlockSpec((rows_per_step, LANES), lambda p, i: (i, p)),
                  pl.BlockSpec((rows_per_step, LANES), lambda p, i: (i, npair + p)),
                  pl.BlockSpec((1, LANES), lambda p, i: (0, p)),
                  pl.BlockSpec((2 * LANES, 2 * LANES), lambda p, i: (0, 0)),
                  pl.BlockSpec((2 * LANES, 2 * LANES), lambda p, i: (0, 0))] + [hbm] * n_ex,
        out_specs=[pl.BlockSpec((rows_per_step, LANES), lambda p, i: (i, p)), whole, whole,
                   pl.BlockSpec((8, LANES), lambda p, i: (0, p))] + [hbm] * n_ex,
        out_shape=[jax.ShapeDtypeStruct((t_len, width), BF16)] * 3 + [jax.ShapeDtypeStruct((8, width), F32)]
        + _chip_landing(exchange),
        scratch_shapes=[pltpu.VMEM((t_len, LANES), F32), pltpu.VMEM((t_len, LANES), F32)] + _chip_scratch(n_ex),
        compiler_params=_params(("arbitrary", "arbitrary")),
    )(qkv, qkv, qkv, o, dymix, g_attn, _cumsum_matrix(False), _cumsum_matrix(True), *exchange)
    return outs[0], outs[1], outs[2], outs[3], outs[4:]


def kernel(x, w_in, conv_w, g_conv, g_attn, w_out, ln1_g, ln1_b, w_up, w_down, ln2_g, ln2_b, loss_target, m_w_in, m_conv_w, m_g_conv, m_g_attn, m_w_out, m_ln1_g, m_ln1_b, m_w_up, m_w_down, m_ln2_g, m_ln2_b, v_w_in, v_conv_w, v_g_conv, v_g_attn, v_w_out, v_ln1_g, v_ln1_b, v_w_up, v_w_down, v_ln2_g, v_ln2_b):
    x2, target = x[0], loss_target[0]
    t_len, d_model = x2.shape
    width = g_conv.shape[1]
    in_shard = w_in.shape[2]
    up_shard = w_up.shape[2]
    d_ff = up_shard * N_DEV
    cw = conv_w.shape[2]
    xi, yi, ci = _place()
    dev = 4 * xi + 2 * yi + ci
    tm = 256

    win_g, convw_g, wout_bf, wup_bf, wdown_bf = _all_gather(
        "gather_w_in", [w_in[0], jnp.pad(conv_w[0], ((0, 5), (0, LANES - cw)))], [BF16, F32],
        cast_only=[w_out[0], w_up[0], w_down[0]])
    conv_full = jnp.transpose(convw_g[:, :3, :cw], (1, 0, 2)).reshape(3, width)

    n_row_tiles = t_len // tm
    rows = lambda cols: pl.BlockSpec((tm, cols), lambda i: (i, 0))
    resident = lambda arr: pl.BlockSpec(arr.shape, lambda i: (0,) * arr.ndim)
    vec = pl.BlockSpec((1, d_model), lambda i: (0, 0))
    acc8 = pl.BlockSpec((8, d_model), lambda i: (0, 0))
    tokens = lambda cols, dtype: jax.ShapeDtypeStruct((t_len, cols), dtype)

    def proj_in_epilogue(acc, extra, outs, i):
        x_ref, w_ref, gain_ref, tail_ref = extra
        gates_ref, qkv_ref, xbf_ref, ymix_ref = outs
        gates_ref[...] = acc[:, :3 * width]
        qkv_ref[...] = acc[:, 3 * width:].astype(BF16)
        xbf_ref[...] = x_ref[...].astype(BF16)

        @pl.when(i == 0)
        def _():
            tail_ref[...] = jnp.zeros_like(tail_ref)

        u = acc[:, width:2 * width] * acc[:, 2 * width:3 * width]
        ext = jnp.concatenate([tail_ref[...], u], axis=0)
        tail_ref[...] = u[tm - 8:]
        w = w_ref[...]
        y = acc[:, :width] * (w[2:3] * u + w[1:2] * pltpu.roll(ext, 1, 0)[8:] + w[0:1] * pltpu.roll(ext, 2, 0)[8:])
        lane_lo = lax.broadcasted_iota(jnp.int32, (tm, LANES), 1) < HEAD_DIM
        gain = gain_ref[...]
        for blk in range(width // LANES):
            lanes = slice(blk * LANES, (blk + 1) * LANES)
            y_blk = y[:, lanes]
            rs = lax.rsqrt(_group_sums(y_blk * y_blk, lane_lo) * (1.0 / HEAD_DIM) + RMS_EPS)
            ymix_ref[:, lanes] = (y_blk * rs * gain[:, lanes]).astype(BF16)

    half_vec = pl.BlockSpec((1, width), lambda i: (0, 0))
    gates, qkv, x_bf, ymix, wdown_g = _matmul(
        "proj_in", x2, win_g, rows(d_model), resident(win_g), _shards_wide, n_row_tiles,
        [(tokens(3 * width, F32), rows(3 * width)), (tokens(3 * width, BF16), rows(3 * width)),
         (tokens(d_model, BF16), rows(d_model)), (tokens(2 * width, BF16), rows(width))],
        proj_in_epilogue,
        extras=[(x2, rows(d_model)), (conv_full, pl.BlockSpec((3, width), lambda i: (0, 0))), (g_conv, half_vec)],
        gather=[wdown_bf], carry=[pltpu.VMEM((8, width), F32)])

    o_attn, ymix, (wout_g, wup_g) = _attn_fwd(qkv, g_attn, ymix, width, [wout_bf, wup_bf])
    wout_full = wout_g.reshape(d_model, d_model)
    wdown_full = wdown_g.reshape(d_ff, d_model)

    def out_ln1_epilogue(acc, extra, outs, i):
        x_ref, g_ref, b_ref = extra
        r1 = ALPHA * x_ref[...] + acc
        x1, _, _ = _ln_fwd(r1, g_ref[...], b_ref[...])
        outs[0][...] = r1
        outs[1][...] = x1
        outs[2][...] = x1.astype(BF16)

    r1, x1, x1_bf = _matmul(
        "proj_out_ln1", ymix, wout_full, rows(d_model), resident(wout_full), _whole(_NN), n_row_tiles,
        [(tokens(d_model, F32), rows(d_model)), (tokens(d_model, F32), rows(d_model)),
         (tokens(d_model, BF16), rows(d_model))],
        out_ln1_epilogue, extras=[(x2, rows(d_model)), (ln1_g, vec), (ln1_b, vec)])

    def up_epilogue(acc, extra, outs, i):
        outs[0][...] = acc
        relu = jnp.maximum(acc, 0.0)
        outs[1][...] = (relu * relu).astype(BF16)

    up, hid = _matmul(
        "ffn_up", x1_bf, wup_g, rows(d_model), resident(wup_g), _shards_wide, n_row_tiles,
        [(tokens(d_ff, F32), rows(d_ff)), (tokens(d_ff, BF16), rows(d_ff))], up_epilogue)

    def down_ln2_epilogue(acc, extra, outs, i):
        x1_ref, t_ref, g_ref, b_ref = extra
        dr2_ref, dr2_bf_ref, loss_ref, gg_ref, gb_ref = outs
        gain = g_ref[...]
        r2 = ALPHA * x1_ref[...] + acc
        y, xhat, rstd = _ln_fwd(r2, gain, b_ref[...])
        diff = y - t_ref[...]
        d_y = diff * (1.0 / d_model)
        dr2 = _ln_bwd(d_y, xhat, rstd, gain)
        dr2_ref[...] = dr2
        dr2_bf_ref[...] = dr2.astype(BF16)
        _accumulate(gg_ref, i, n_row_tiles, _rows8(d_y * xhat))
        _accumulate(gb_ref, i, n_row_tiles, _rows8(d_y))
        _accumulate(loss_ref, i, n_row_tiles, _rows8(diff * diff), all_lanes=True)

    small_acc = (jax.ShapeDtypeStruct((8, d_model), F32), acc8)
    dr2, dr2_bf, loss_cols, g_ln2_g, g_ln2_b = _matmul(
        "ffn_down_ln2_loss", hid, wdown_full, rows(d_ff), resident(wdown_full), _whole(_NN), n_row_tiles,
        [(tokens(d_model, F32), rows(d_model)), (tokens(d_model, BF16), rows(d_model)),
         small_acc, small_acc, small_acc],
        down_ln2_epilogue, extras=[(x1, rows(d_model)), (target, rows(d_model)), (ln2_g, vec), (ln2_b, vec)])

    def dup_epilogue(acc, extra, outs, i):
        outs[0][...] = (acc * (2.0 * jnp.maximum(extra[0][...], 0.0))).astype(BF16)

    (dup,) = _matmul(
        "ffn_dhid", dr2_bf, wdown_full, rows(d_model), resident(wdown_full), _whole(_NT), n_row_tiles,
        [(tokens(d_ff, BF16), rows(d_ff))], dup_epilogue, extras=[(up, rows(d_ff))])

    gw_down, pair_down = _weight_grad("grad_w_down", hid, dr2_bf, (d_ff // N_DEV, d_model), True)
    gw_up, pair_up = _weight_grad("grad_w_up", x1_bf, dup, (d_model, up_shard), False)

    def dx1_ln1_epilogue(acc, extra, outs, i):
        dr2_ref, r1_ref, g_ref, b_ref = extra
        dr1_ref, dr1_bf_ref, gg_ref, gb_ref = outs
        gain = g_ref[...]
        dx1 = ALPHA * dr2_ref[...] + acc
        _, xhat, rstd = _ln_fwd(r1_ref[...], gain, b_ref[...])
        dr1 = _ln_bwd(dx1, xhat, rstd, gain)
        dr1_ref[...] = dr1
        dr1_bf_ref[...] = dr1.astype(BF16)
        _accumulate(gg_ref, i, n_row_tiles, _rows8(dx1 * xhat))
        _accumulate(gb_ref, i, n_row_tiles, _rows8(dx1))

    dr1, dr1_bf, g_ln1_g, g_ln1_b = _matmul(
        "ffn_dx1_ln1", dup, wup_g, rows(d_ff), resident(wup_g), _shards_deep, n_row_tiles,
        [(tokens(d_model, F32), rows(d_model)), (tokens(d_model, BF16), rows(d_model)), small_acc, small_acc],
        dx1_ln1_epilogue, extras=[(dr2, rows(d_model)), (r1, rows(d_model)), (ln1_g, vec), (ln1_b, vec)])

    gw_out, pair_out = _weight_grad("grad_w_out", ymix, dr1_bf, (d_model // N_DEV, d_model), True)

    c_idx = jnp.reshape(ci, (1,)).astype(jnp.int32)
    q_me = jnp.reshape(2 * xi + yi, (1,)).astype(jnp.int32)

    def pair_sums(names, grads, landed):
        sums = [_pair_sum("pair_sum_" + nm, g, land, c_idx, q_me) for nm, g, land in zip(names, grads, landed)]
        return [s[0] for s in sums], [s[1] for s in sums]

    wire_ffn, own_ffn = pair_sums(["w_out", "w_up", "w_down"], [gw_out, gw_up, gw_down],
                                  [pair_out, pair_up, pair_down])

    def plain_epilogue(acc, extra, outs, i):
        outs[0][...] = acc

    (dymix,) = _matmul(
        "proj_out_bwd", dr1_bf, wout_full, rows(d_model), resident(wout_full), _whole(_NT), n_row_tiles,
        [(tokens(d_model, F32), rows(d_model))], plain_epilogue)

    d_bg, d_cg, d_h, g_gconv, g_convw = _conv_bwd(gates, dymix, conv_full, g_conv, width)
    d_q, d_k, d_v, g_gattn, landed_ffn = _attn_bwd(qkv, o_attn, dymix, g_attn, width, wire_ffn)
    dproj = jnp.concatenate([d_bg, d_cg, d_h, d_q, d_k, d_v], axis=1)

    gw_in, pair_in = _weight_grad("grad_w_in", x_bf, dproj, (d_model, in_shard), False)
    wire_in, own_in = pair_sums(["w_in"], [gw_in], [pair_in])

    def dx_epilogue(acc, extra, outs, i):
        outs[0][...] = ALPHA * extra[0][...] + acc

    spare = jnp.zeros((3, d_model - width), F32)

    def pack(l1g, l1b, l2g, l2b, gc, ga, taps, tail=spare):
        return jnp.concatenate([l1g, l1b, l2g, l2b, jnp.concatenate([gc, ga], axis=1),
                                jnp.concatenate([taps, tail], axis=1)], axis=0)

    local = pack(g_ln1_g[:1], g_ln1_b[:1], g_ln2_g[:1], g_ln2_b[:1], g_gconv[:1], g_gattn[:1], g_convw[:3],
                 spare + (0.5 / d_model) * loss_cols[0, 0])

    grad_x, landed_in, parts = _matmul(
        "proj_in_bwd", dproj, win_g, rows(6 * width), resident(win_g), _shards_deep, n_row_tiles,
        [(tokens(d_model, F32), rows(d_model))], dx_epilogue, extras=[(dr1, rows(d_model))],
        exchange=wire_in, gather=[local])

    names = ["w_in", "w_out", "w_up", "w_down"]
    big = {}
    for nm, ow, land, w, m, v in zip(names, own_in + own_ffn, [landed_in] + list(landed_ffn),
                                     (w_in, w_out, w_up, w_down),
                                     (m_w_in, m_w_out, m_w_up, m_w_down), (v_w_in, v_w_out, v_w_up, v_w_down)):
        big[nm] = [r[None] for r in _shard_update("update_" + nm, ow, land, w[0], m[0], v[0])]

    def spread(a):
        return lax.dynamic_update_slice(jnp.zeros((3, width), F32), a, (0, dev * cw))

    w_small = pack(ln1_g, ln1_b, ln2_g, ln2_b, g_conv, g_attn, spread(conv_w[0]))
    m_small = pack(m_ln1_g, m_ln1_b, m_ln2_g, m_ln2_b, m_g_conv, m_g_attn, spread(m_conv_w[0]))
    v_small = pack(v_ln1_g, v_ln1_b, v_ln2_g, v_ln2_b, v_g_conv, v_g_attn, spread(v_conv_w[0]))
    small = _small_update(parts, w_small, m_small, v_small)

    def unpack(a):
        taps = lax.dynamic_slice(a[5:8, :width], (0, dev * cw), (3, cw))[None]
        return {"ln1_g": a[0:1], "ln1_b": a[1:2], "ln2_g": a[2:3], "ln2_b": a[3:4],
                "g_conv": a[4:5, :width], "g_attn": a[4:5, width:], "conv_w": taps}

    loss = small[0][7, d_model - 1]
    small = [unpack(a) for a in small]

    order = ["w_in", "conv_w", "g_conv", "g_attn", "w_out", "ln1_g", "ln1_b", "w_up", "w_down", "ln2_g", "ln2_b"]
    result = [loss, grad_x[None]]
    for kind in range(4):
        for nm in order:
            result.append(big[nm][kind] if nm in big else small[kind][nm])
    return tuple(result)
```

```python
import functools

import jax
import jax.numpy as jnp
from jax import lax
from jax.experimental import pallas as pl
from jax.experimental.pallas import tpu as pltpu

F32 = jnp.float32
BF16 = jnp.bfloat16
MESH = pl.DeviceIdType.MESH

N_DEV = 8
HEAD_DIM = 64
LANES = 128
Q_BLOCK = 128
ALPHA = 2.0 ** 0.25
LN_EPS = 1e-5
RMS_EPS = 1e-6
ADAM_LR, ADAM_B1, ADAM_B2, ADAM_EPS, ADAM_WD, ADAM_STEP = 0.001, 0.9, 0.999, 1e-08, 0.01, 10
VMEM_LIMIT_BYTES = 48 * 1024 * 1024

_NT = ((1,), (1,))
_NN = ((1,), (0,))
_TN = ((0,), (0,))


def _params(sem=None):
    return pltpu.CompilerParams(dimension_semantics=sem, vmem_limit_bytes=VMEM_LIMIT_BYTES)


def _dot(a, b, contract):
    return lax.dot_general(a.astype(BF16), b.astype(BF16), (contract, ((), ())), preferred_element_type=F32)


def _whole(contract):
    return lambda a_ref, b_ref: _dot(a_ref[...], b_ref[...], contract)


def _shards_wide(a_ref, b_ref):
    a = a_ref[...].astype(BF16)
    return jnp.concatenate([_dot(a, b_ref[s], _NN) for s in range(b_ref.shape[0])], axis=1)


def _shards_deep(a_ref, b_ref):
    n = b_ref.shape[2]
    acc = _dot(a_ref[:, 0:n], b_ref[0], _NT)
    for s in range(1, b_ref.shape[0]):
        acc = acc + _dot(a_ref[:, s * n:(s + 1) * n], b_ref[s], _NT)
    return acc


def _matmul(name, a, b, a_spec, b_spec, product, n_steps, outs, epilogue, extras=(), exchange=(), gather=(),
            carry=()):
    behind = list(exchange) + list(gather)
    n_extra, n_out, n_ex, n_b, n_carry = len(extras), len(outs), len(exchange), len(behind), len(carry)

    def body(a_ref, b_ref, *rest):
        i = pl.program_id(0)
        extra_refs, rest = rest[:n_extra], rest[n_extra:]
        src_refs, rest = rest[:n_b], rest[n_b:]
        out_refs, rest = rest[:n_out], rest[n_out:]
        land_refs, rest = rest[:n_b], rest[n_b:]
        extra_refs, sems = extra_refs + rest[:n_carry], rest[n_carry:]
        copies = _chip_copies(src_refs[:n_ex], land_refs[:n_ex], *sems[:2]) if exchange else []
        gather_sems = sems[2:] if exchange else sems
        plan = _gather_plan(src_refs[n_ex:], land_refs[n_ex:], *gather_sems) if gather else None

        def start():
            for cp in copies:
                cp.start()
            if plan:
                plan[0]()

        def finish():
            if plan:
                plan[1]()
                plan[2]()
            for cp in copies:
                cp.wait()

        if behind:
            pl.when(i == 0)(start)
        epilogue(product(a_ref, b_ref), extra_refs, out_refs, i)
        if behind:
            pl.when(i == n_steps - 1)(finish)

    hbm = pl.BlockSpec(memory_space=pl.ANY)
    landing = _chip_landing(exchange) + [jax.ShapeDtypeStruct((N_DEV,) + s.shape, s.dtype) for s in gather]
    scratch = (list(carry) + (_chip_scratch(len(exchange)) if exchange else [])
               + (_gather_scratch(len(gather)) if gather else []))
    return pl.pallas_call(
        body, name=name, grid=(n_steps,),
        in_specs=[a_spec, b_spec] + [s for _, s in extras] + [hbm] * n_b,
        out_specs=[s for _, s in outs] + [hbm] * n_b,
        out_shape=[o for o, _ in outs] + landing,
        scratch_shapes=scratch,
        compiler_params=_params(("arbitrary",)),
    )(a, b, *[e for e, _ in extras], *behind)


def _ln_fwd(r, g, b):
    mu = jnp.mean(r, axis=-1, keepdims=True)
    xc = r - mu
    rstd = lax.rsqrt(jnp.mean(xc * xc, axis=-1, keepdims=True) + LN_EPS)
    xhat = xc * rstd
    return xhat * g + b, xhat, rstd


def _ln_bwd(dy, xhat, rstd, g):
    dxh = dy * g
    m1 = jnp.mean(dxh, axis=-1, keepdims=True)
    m2 = jnp.mean(dxh * xhat, axis=-1, keepdims=True)
    return rstd * (dxh - m1 - xhat * m2)


def _rows8(v):
    n, c = v.shape
    return jnp.sum(v.reshape(n // 8, 8, c), axis=0)


def _accumulate(ref, i, n_steps, part8, all_lanes=False):
    @pl.when(i == 0)
    def _():
        ref[...] = part8

    @pl.when(i > 0)
    def _():
        ref[...] += part8

    @pl.when(i == n_steps - 1)
    def _():
        tot = jnp.sum(ref[...], axis=0, keepdims=True)
        if all_lanes:
            tot = jnp.sum(tot, axis=1, keepdims=True)
        ref[...] = jnp.broadcast_to(tot, ref.shape)


def _group_sums(v, lane_lo):
    s0 = jnp.sum(jnp.where(lane_lo, v, 0.0), axis=-1, keepdims=True)
    s1 = jnp.sum(jnp.where(lane_lo, 0.0, v), axis=-1, keepdims=True)
    return jnp.where(lane_lo, s0, s1)


def _place():
    return lax.axis_index("x"), lax.axis_index("y"), lax.axis_index("c")


def _gather_plan(src_refs, out_refs, send_sems, recv_sems, local_sems):
    n = len(src_refs)
    x, y, c = _place()
    me, sibling = (x, y, c), (x, y, 1 - c)
    chips = [(1 - x, y), (x, 1 - y), (1 - x, 1 - y)]

    def copy(a, k, block, to, src=None):
        slot = out_refs[a].at[4 * block[0] + 2 * block[1] + block[2]]
        return pltpu.make_async_remote_copy(
            src_ref=slot if src is None else src, dst_ref=slot,
            send_sem=send_sems.at[a, k], recv_sem=recv_sems.at[a, k],
            device_id=to, device_id_type=MESH)

    mine = [pltpu.make_async_copy(src_refs[a], out_refs[a].at[4 * x + 2 * y + c], local_sems.at[a]) for a in range(n)]
    first = [[copy(a, 0, me, sibling, src=src_refs[a])]
             + [copy(a, 1 + j, me, (*chip, c), src=src_refs[a]) for j, chip in enumerate(chips)] for a in range(n)]
    passed = [[copy(a, 4 + j, (*chip, c), sibling) for j, chip in enumerate(chips)] for a in range(n)]

    def start():
        for a in range(n):
            mine[a].start()
            for cp in first[a]:
                cp.start()

    def forward():
        for j, chip in enumerate(chips):
            for a in range(n):
                copy(a, 1 + j, (*chip, c), me).wait_recv()
                passed[a][j].start()

    def finish():
        for a in range(n):
            copy(a, 0, sibling, me).wait_recv()
            for j, chip in enumerate(chips):
                copy(a, 4 + j, (*chip, 1 - c), me).wait_recv()
        for a in range(n):
            for cp in first[a] + passed[a]:
                cp.wait_send()
            mine[a].wait()

    return start, forward, finish


def _gather_scratch(n):
    return [pltpu.SemaphoreType.DMA((n, 7)), pltpu.SemaphoreType.DMA((n, 7)), pltpu.SemaphoreType.DMA((n,))]


def _all_gather(name, shards, out_dtypes, cast_only=()):
    n, m = len(shards), len(cast_only)

    def body(*refs):
        in_refs, cast_in = refs[:n], refs[n:n + m]
        out_refs, cast_out = refs[n + m:2 * n + m], refs[2 * n + m:2 * (n + m)]
        stage = refs[2 * (n + m):3 * n + 2 * m]
        for a in range(n):
            stage[a][...] = in_refs[a][...].astype(out_dtypes[a])
        start, forward, finish = _gather_plan(stage, out_refs, *refs[3 * n + 2 * m:])
        start()
        for a in range(m):
            cast_out[a][...] = cast_in[a][...].astype(BF16)
        forward()
        finish()

    return pl.pallas_call(
        body, name=name,
        in_specs=[pl.BlockSpec(memory_space=pltpu.VMEM)] * (n + m),
        out_specs=[pl.BlockSpec(memory_space=pl.ANY)] * n + [pl.BlockSpec(memory_space=pltpu.VMEM)] * m,
        out_shape=[jax.ShapeDtypeStruct((N_DEV,) + s.shape, dt) for s, dt in zip(shards, out_dtypes)]
        + [jax.ShapeDtypeStruct(s.shape, BF16) for s in cast_only],
        scratch_shapes=[pltpu.VMEM(s.shape, dt) for s, dt in zip(shards, out_dtypes)] + _gather_scratch(n),
        compiler_params=pltpu.CompilerParams(vmem_limit_bytes=VMEM_LIMIT_BYTES),
    )(*shards, *cast_only)


def _weight_grad(name, acts, cots, shard_shape, acts_by_shard):
    t_len = acts.shape[0]
    sr, sc = shard_shape
    per = max(1, 256 // sr) if acts_by_shard else 1
    assert per in (1, 2)
    n_steps = N_DEV // per

    def body(a_ref, b_ref, g_ref, land_ref, stage, send_sems, recv_sems):
        i = pl.program_id(0)
        x, y, c = _place()

        def copy(q):
            return pltpu.make_async_remote_copy(
                src_ref=stage.at[q % 2], dst_ref=land_ref.at[q], send_sem=send_sems.at[q], recv_sem=recv_sems.at[q],
                device_id=(x, y, 1 - c), device_id_type=MESH)

        acc = _dot(a_ref[...], b_ref[...], _TN).reshape(per, sr, sc)
        g_ref[...] = acc
        if per == 2:
            payload, due, q = jnp.where(c == 0, acc[1], acc[0]), i >= 0, i
        else:
            payload, due, q = acc[0], (i % 2) == (1 - c), i // 2

        @pl.when(due)
        def _():
            @pl.when(q >= 2)
            def _():
                copy(q - 2).wait_send()

            stage[q % 2] = payload.astype(BF16)
            copy(q).start()

        @pl.when(i == n_steps - 1)
        def _():
            copy(2).wait_send()
            copy(3).wait_send()
            for k in range(4):
                copy(k).wait_recv()

    tall = lambda cols: pl.BlockSpec((t_len, cols), lambda i: (0, i))
    whole = lambda arr: pl.BlockSpec(arr.shape, lambda i: (0, 0))
    a_spec, b_spec = (tall(per * sr), whole(cots)) if acts_by_shard else (whole(acts), tall(sc))
    return pl.pallas_call(
        body, name=name, grid=(n_steps,),
        in_specs=[a_spec, b_spec],
        out_specs=[pl.BlockSpec((per, sr, sc), lambda i: (i, 0, 0)), pl.BlockSpec(memory_space=pl.ANY)],
        out_shape=[jax.ShapeDtypeStruct((N_DEV, sr, sc), F32), jax.ShapeDtypeStruct((4, sr, sc), BF16)],
        scratch_shapes=[pltpu.VMEM((2, sr, sc), BF16), pltpu.SemaphoreType.DMA((4,)), pltpu.SemaphoreType.DMA((4,))],
        compiler_params=_params(("arbitrary",)),
    )(acts, cots)


def _chip_copies(p_refs, land_refs, send_sems, recv_sems):
    x, y, c = _place()
    chips = [(1 - x, y), (x, 1 - y), (1 - x, 1 - y)]
    return [pltpu.make_async_remote_copy(
        src_ref=p_refs[a].at[2 * px + py], dst_ref=land_refs[a].at[k],
        send_sem=send_sems.at[a, k], recv_sem=recv_sems.at[a, k],
        device_id=(px, py, c), device_id_type=MESH)
        for a in range(len(p_refs)) for k, (px, py) in enumerate(chips)]


def _chip_scratch(n):
    return [pltpu.SemaphoreType.DMA((n, 3)), pltpu.SemaphoreType.DMA((n, 3))]


def _chip_landing(pair_sums):
    return [jax.ShapeDtypeStruct((3,) + p.shape[1:], p.dtype) for p in pair_sums]


def _pair_sum(name, g_f32, landed, c_idx, q_me):
    _, rows, cols = g_f32.shape
    tr = min(rows, 1024)

    def body(c_ref, q_ref, g_ref, l_ref, wire_ref, own_ref):
        q = pl.program_id(1)
        s = g_ref[...] + l_ref[...].astype(F32)
        wire_ref[...] = s.astype(BF16)

        @pl.when(q == q_ref[0])
        def _():
            own_ref[...] = s

    return pl.pallas_call(
        body, name=name,
        grid_spec=pltpu.PrefetchScalarGridSpec(
            num_scalar_prefetch=2, grid=(rows // tr, 4),
            in_specs=[pl.BlockSpec((None, tr, cols), lambda i, q, c_ref, q_ref: (2 * q + c_ref[0], i, 0)),
                      pl.BlockSpec((None, tr, cols), lambda i, q, c_ref, q_ref: (q, i, 0))],
            out_specs=[pl.BlockSpec((None, tr, cols), lambda i, q, c_ref, q_ref: (q, i, 0)),
                       pl.BlockSpec((tr, cols), lambda i, q, c_ref, q_ref: (i, 0))]),
        out_shape=[jax.ShapeDtypeStruct((4, rows, cols), BF16), jax.ShapeDtypeStruct((rows, cols), F32)],
        compiler_params=_params(("parallel", "arbitrary")),
    )(c_idx, q_me, g_f32, landed)


def _adamw(w, g, m, v):
    m = ADAM_B1 * m + (1.0 - ADAM_B1) * g
    v = ADAM_B2 * v + (1.0 - ADAM_B2) * (g * g)
    m_hat = m / (1.0 - ADAM_B1 ** ADAM_STEP)
    v_hat = v / (1.0 - ADAM_B2 ** ADAM_STEP)
    delta = -ADAM_LR * (m_hat / (jnp.sqrt(v_hat) + ADAM_EPS) + ADAM_WD * w)
    return delta, m, v


def _shard_update(name, own, landed, w, m, v):
    rows, cols = own.shape
    tr = min(rows, 256)

    def body(own_ref, l_ref, w_ref, m_ref, v_ref, g_out, d_out, m_out, v_out):
        g = own_ref[...] + l_ref[0].astype(F32) + l_ref[1].astype(F32) + l_ref[2].astype(F32)
        d, m_new, v_new = _adamw(w_ref[...], g, m_ref[...], v_ref[...])
        g_out[...] = g
        d_out[...] = d
        m_out[...] = m_new
        v_out[...] = v_new

    tile = pl.BlockSpec((tr, cols), lambda i: (i, 0))
    return pl.pallas_call(
        body, name=name, grid=(rows // tr,),
        in_specs=[tile, pl.BlockSpec((3, tr, cols), lambda i: (0, i, 0)), tile, tile, tile],
        out_specs=[tile] * 4,
        out_shape=[jax.ShapeDtypeStruct((rows, cols), F32)] * 4,
        compiler_params=_params(("parallel",)),
    )(own, landed, w, m, v)


def _small_update(parts, w, m, v):
    def body(p_ref, w_ref, m_ref, v_ref, g_out, d_out, m_out, v_out):
        g = p_ref[0]
        for d in range(1, N_DEV):
            g = g + p_ref[d]
        dl, m_new, v_new = _adamw(w_ref[...], g, m_ref[...], v_ref[...])
        g_out[...] = g
        d_out[...] = dl
        m_out[...] = m_new
        v_out[...] = v_new

    return pl.pallas_call(
        body, name="small_update",
        out_shape=[jax.ShapeDtypeStruct(w.shape, F32)] * 4,
        compiler_params=pltpu.CompilerParams(vmem_limit_bytes=VMEM_LIMIT_BYTES),
    )(parts, w, m, v)


CONV_CHUNK = 512


def _conv_taps(u_s, lo, chunk):
    ext = u_s[pl.ds(lo, chunk + 8), :]
    return ext[8:], pltpu.roll(ext, 1, 0)[8:], pltpu.roll(ext, 2, 0)[8:]


def _conv_bwd(gates, dymix, conv_w, g_conv, width):
    t_len = gates.shape[0]
    nblk = width // LANES
    n_chunks = t_len // CONV_CHUNK

    def body(bg_ref, cg_ref, h_ref, dy_ref, w_ref, g_ref, dbg_ref, dcg_ref, dh_ref, gg_ref, gw_ref, u_s, dc_s):
        lane_lo = lax.broadcasted_iota(jnp.int32, (CONV_CHUNK, LANES), 1) < HEAD_DIM
        u_s[0:8, :] = jnp.zeros((8, LANES), F32)
        dc_s[t_len:t_len + 8, :] = jnp.zeros((8, LANES), F32)

        def fill(r, _):
            lo = pl.multiple_of(r * CONV_CHUNK, CONV_CHUNK)
            u_s[pl.ds(pl.multiple_of(lo + 8, 8), CONV_CHUNK), :] = cg_ref[pl.ds(lo, CONV_CHUNK), :] * h_ref[pl.ds(lo, CONV_CHUNK), :]
            return 0

        lax.fori_loop(0, n_chunks, fill, 0)
        w = w_ref[...]
        gain = g_ref[...]

        def step(r, acc):
            gg, gw0, gw1, gw2 = acc
            lo = pl.multiple_of(r * CONV_CHUNK, CONV_CHUNK)
            u0, u1, u2 = _conv_taps(u_s, lo, CONV_CHUNK)
            bg = bg_ref[pl.ds(lo, CONV_CHUNK), :]
            conv = w[2:3] * u0 + w[1:2] * u1 + w[0:1] * u2
            y = bg * conv
            rs = lax.rsqrt(_group_sums(y * y, lane_lo) * (1.0 / HEAD_DIM) + RMS_EPS)
            yn = y * rs
            dyn_raw = dy_ref[pl.ds(lo, CONV_CHUNK), :]
            dyn = dyn_raw * gain
            d_y = rs * (dyn - yn * (_group_sums(dyn * yn, lane_lo) * (1.0 / HEAD_DIM)))
            dbg_ref[pl.ds(lo, CONV_CHUNK), :] = (d_y * conv).astype(BF16)
            dconv = d_y * bg
            dc_s[pl.ds(lo, CONV_CHUNK), :] = dconv
            return (gg + _rows8(dyn_raw * yn), gw0 + _rows8(dconv * u2), gw1 + _rows8(dconv * u1),
                    gw2 + _rows8(dconv * u0))

        zero = jnp.zeros((8, LANES), F32)
        gg, gw0, gw1, gw2 = lax.fori_loop(0, n_chunks, step, (zero, zero, zero, zero))
        gg_ref[...] = jnp.broadcast_to(jnp.sum(gg, axis=0, keepdims=True), (8, LANES))
        row = lax.broadcasted_iota(jnp.int32, (8, LANES), 0)
        tot = lambda v: jnp.broadcast_to(jnp.sum(v, axis=0, keepdims=True), (8, LANES))
        gw_ref[...] = jnp.where(row == 0, tot(gw0), jnp.where(row == 1, tot(gw1), jnp.where(row == 2, tot(gw2), 0.0)))

        def back(r, _):
            lo = pl.multiple_of(r * CONV_CHUNK, CONV_CHUNK)
            ext = dc_s[pl.ds(lo, CONV_CHUNK + 8), :]
            d0 = ext[:CONV_CHUNK]
            d1 = pltpu.roll(ext, CONV_CHUNK + 7, 0)[:CONV_CHUNK]
            d2 = pltpu.roll(ext, CONV_CHUNK + 6, 0)[:CONV_CHUNK]
            du = w[2:3] * d0 + w[1:2] * d1 + w[0:1] * d2
            dcg_ref[pl.ds(lo, CONV_CHUNK), :] = (du * h_ref[pl.ds(lo, CONV_CHUNK), :]).astype(BF16)
            dh_ref[pl.ds(lo, CONV_CHUNK), :] = (du * cg_ref[pl.ds(lo, CONV_CHUNK), :]).astype(BF16)
            return 0

        lax.fori_loop(0, n_chunks, back, 0)

    col = lambda off: pl.BlockSpec((t_len, LANES), lambda p: (0, off + p))
    small = pl.BlockSpec((8, LANES), lambda p: (0, p))
    return pl.pallas_call(
        body, name="conv_bwd", grid=(nblk,),
        in_specs=[col(0), col(nblk), col(2 * nblk), col(0),
                  pl.BlockSpec((3, LANES), lambda p: (0, p)), pl.BlockSpec((1, LANES), lambda p: (0, p))],
        out_specs=[col(0), col(0), col(0), small, small],
        out_shape=[jax.ShapeDtypeStruct((t_len, width), BF16)] * 3 + [jax.ShapeDtypeStruct((8, width), F32)] * 2,
        scratch_shapes=[pltpu.VMEM((t_len + 8, LANES), F32), pltpu.VMEM((t_len + 8, LANES), F32)],
        compiler_params=_params(("parallel",)),
    )(gates, gates, gates, dymix, conv_w, g_conv)


def _cumsum_matrix(inclusive):
    j = lax.broadcasted_iota(jnp.int32, (2 * LANES, 2 * LANES), 0) % LANES
    s = lax.broadcasted_iota(jnp.int32, (2 * LANES, 2 * LANES), 1)
    keep = (s >= LANES) | ((j >= s) if inclusive else (j > s))
    return keep.astype(BF16)


KEY_CHUNK = 384
SUB_TILES = KEY_CHUNK // Q_BLOCK
Q_GROUP = 4
DEAD_ABOVE = 105.0
MASKED_Z = -1e30


def _scores_phase(qms, kcs, masks):
    zs = [lax.dot_general(q, k, (_NT, ((), ())), preferred_element_type=F32) for q, k in zip(qms, kcs)]
    return [jnp.where(m, z, MASKED_Z) for z, m in zip(zs, masks)]


def _logs_phase(zs):
    sps = [jnp.maximum(z, 0.0) + jnp.log(1.0 + jnp.exp(-jnp.abs(z))) for z in zs]
    return sps, [z - sp for z, sp in zip(zs, sps)]


def _suffix_phase(vals, carries, mat):
    def hi_lo(tile):
        hi = tile.astype(BF16)
        return jnp.concatenate([hi, (tile - hi.astype(F32)).astype(BF16)], axis=1)

    packed = [[hi_lo(v[:, t * Q_BLOCK:(t + 1) * Q_BLOCK]) for t in range(SUB_TILES)] for v in vals]
    prods = [[jnp.dot(p, mat, preferred_element_type=F32) for p in chain] for chain in packed]
    sums, new_carries = [], []
    for chain, carry in zip(prods, carries):
        parts = []
        for t in reversed(range(SUB_TILES)):
            parts.append(chain[t][:, :LANES] + carry)
            carry = carry + chain[t][:, LANES:]
        sums.append(jnp.concatenate(parts[::-1], axis=1))
        new_carries.append(carry)
    return sums, new_carries


def _sweep(first_block, step, states, carries_of):
    n_chunks = ((first_block + Q_GROUP) * Q_BLOCK + KEY_CHUNK - 1) // KEY_CHUNK
    col = lax.broadcasted_iota(jnp.int32, (Q_BLOCK, KEY_CHUNK), 1)
    row = lax.broadcasted_iota(jnp.int32, (Q_BLOCK, 1), 0)

    def body(loop):
        n, _, sts = loop
        starts, masks = [], []
        for b in range(Q_GROUP):
            i = first_block + b
            end = (i + 1) * Q_BLOCK - n * KEY_CHUNK
            starts.append(pl.multiple_of(jnp.maximum(end - KEY_CHUNK, 0), Q_BLOCK))
            masks.append(col < jnp.minimum(i * Q_BLOCK + row, end) - starts[b])
        sts = step(starts, masks, sts)
        least = None
        for st in sts:
            for carry in carries_of(st):
                least = carry if least is None else jnp.minimum(least, carry)
        alive = (jnp.min(least) <= DEAD_ABOVE).astype(jnp.int32)
        return n + 1, alive, tuple(sts)

    return lax.while_loop(lambda loop: (loop[0] < n_chunks) & (loop[1] > 0), body,
                          (jnp.int32(0), jnp.int32(1), tuple(states)))[2]


CHAINS = [(b, h) for b in range(Q_GROUP) for h in range(2)]


def _head_rows(block, lane_lo_rows):
    zero = jnp.zeros_like(block)
    return jnp.concatenate([jnp.where(lane_lo_rows, block, zero), jnp.where(lane_lo_rows, zero, block)], axis=0)


def _attn_fwd(qkv, g_attn, ymix_in, width, shards):
    t_len = qkv.shape[0]
    npair = width // LANES
    rows_per_step = Q_GROUP * Q_BLOCK
    n_steps = t_len // rows_per_step
    n_sh = len(shards)

    def body(q_ref, k_ref, v_ref, g_ref, umat_ref, ymix_in_ref, *rest):
        shard_refs, rest = rest[:n_sh], rest[n_sh:]
        o_ref, ymix_ref = rest[:2]
        gathered_refs, sems = rest[2:2 + n_sh], rest[2 + n_sh:]
        p, i = pl.program_id(0), pl.program_id(1)
        start, forward, finish = _gather_plan(shard_refs, gathered_refs, *sems)
        pl.when((p == 0) & (i == 0))(start)
        pl.when((p == npair - 1) & (i == n_steps // 2))(forward)
        lane_lo = lax.broadcasted_iota(jnp.int32, (rows_per_step, LANES), 1) < HEAD_DIM
        lane_lo_keys = lax.broadcasted_iota(jnp.int32, (KEY_CHUNK, LANES), 1) < HEAD_DIM
        umat = umat_ref[...]
        q = q_ref[...] * jnp.asarray(HEAD_DIM ** -0.5, BF16)
        q_heads = (jnp.where(lane_lo, q, jnp.zeros_like(q)), jnp.where(lane_lo, jnp.zeros_like(q), q))

        def step(starts, masks, states):
            kcs = [k_ref[pl.ds(r, KEY_CHUNK), :] for r in starts]
            zs = _scores_phase([q_heads[h][b * Q_BLOCK:(b + 1) * Q_BLOCK] for b, h in CHAINS],
                               [kcs[b] for b, _ in CHAINS], [masks[b] for b, _ in CHAINS])
            sps, lbs = _logs_phase(zs)
            suffixes, carries = _suffix_phase(sps, [states[b][0][h] for b, h in CHAINS], umat)
            weights = [jnp.exp(lb - sfx).astype(BF16) for lb, sfx in zip(lbs, suffixes)]
            new = []
            for b in range(Q_GROUP):
                v_heads = _head_rows(v_ref[pl.ds(starts[b], KEY_CHUNK), :], lane_lo_keys)
                o_acc = states[b][1] + jnp.dot(jnp.concatenate(weights[2 * b:2 * b + 2], axis=1), v_heads,
                                               preferred_element_type=F32)
                new.append(((carries[2 * b], carries[2 * b + 1]), o_acc))
            return new

        zero = jnp.zeros((Q_BLOCK, LANES), F32)
        states = _sweep(i * Q_GROUP, step, [((zero, zero), zero)] * Q_GROUP, lambda st: st[0])
        o = jnp.concatenate([st[1] for st in states], axis=0)
        o_ref[...] = o
        rs = lax.rsqrt(_group_sums(o * o, lane_lo) * (1.0 / HEAD_DIM) + RMS_EPS)
        ymix_ref[...] = (o * rs * g_ref[...]).astype(BF16)
        pl.when((p == npair - 1) & (i == n_steps - 1))(finish)

    hbm = pl.BlockSpec(memory_space=pl.ANY)
    outs = pl.pallas_call(
        body, name="attn_fwd", grid=(npair, n_steps),
        in_specs=[pl.BlockSpec((rows_per_step, LANES), lambda p, i: (i, p)),
                  pl.BlockSpec((t_len, LANES), lambda p, i: (0, npair + p)),
                  pl.BlockSpec((t_len, LANES), lambda p, i: (0, 2 * npair + p)),
                  pl.BlockSpec((1, LANES), lambda p, i: (0, p)),
                  pl.BlockSpec((2 * LANES, 2 * LANES), lambda p, i: (0, 0)),
                  hbm] + [hbm] * n_sh,
        out_specs=[pl.BlockSpec((rows_per_step, LANES), lambda p, i: (i, p)),
                   pl.BlockSpec((rows_per_step, LANES), lambda p, i: (i, npair + p))] + [hbm] * n_sh,
        out_shape=[jax.ShapeDtypeStruct((t_len, width), F32), jax.ShapeDtypeStruct(ymix_in.shape, BF16)]
        + [jax.ShapeDtypeStruct((N_DEV,) + s.shape, s.dtype) for s in shards],
        input_output_aliases={5: 1},
        scratch_shapes=_gather_scratch(n_sh),
        compiler_params=_params(("arbitrary", "arbitrary")),
    )(qkv, qkv, qkv, g_attn, _cumsum_matrix(False), ymix_in, *shards)
    return outs[0], outs[1], outs[2:]


def _attn_bwd(qkv, o, dymix, g_attn, width, exchange):
    t_len = qkv.shape[0]
    npair = width // LANES
    rows_per_step = Q_GROUP * Q_BLOCK
    n_steps = t_len // rows_per_step
    scale = HEAD_DIM ** -0.5
    n_ex = len(exchange)

    def body(q_ref, k_ref, v_ref, o_ref, dy_ref, g_ref, umat_ref, gmat_ref, *rest):
        p_refs, rest = rest[:n_ex], rest[n_ex:]
        dq_ref, dk_ref, dv_ref, gg_ref = rest[:4]
        land_refs, rest = rest[4:4 + n_ex], rest[4 + n_ex:]
        dk_s, dv_s = rest[:2]
        copies = _chip_copies(p_refs, land_refs, *rest[2:])
        p, i = pl.program_id(0), pl.program_id(1)

        @pl.when((p == 0) & (i == 0))
        def _():
            for cp in copies:
                cp.start()

        lane_lo = lax.broadcasted_iota(jnp.int32, (rows_per_step, LANES), 1) < HEAD_DIM
        lane_lo_keys = lax.broadcasted_iota(jnp.int32, (KEY_CHUNK, LANES), 1) < HEAD_DIM
        umat, gmat = umat_ref[...], gmat_ref[...]

        @pl.when(i == 0)
        def _():
            dk_s[...] = jnp.zeros_like(dk_s)
            dv_s[...] = jnp.zeros_like(dv_s)

        o_blk = o_ref[...]
        rs = lax.rsqrt(_group_sums(o_blk * o_blk, lane_lo) * (1.0 / HEAD_DIM) + RMS_EPS)
        yn = o_blk * rs
        dy_raw = dy_ref[...]
        dyn = dy_raw * g_ref[...]
        d_o = rs * (dyn - yn * (_group_sums(dyn * yn, lane_lo) * (1.0 / HEAD_DIM)))
        _accumulate(gg_ref, i, n_steps, _rows8(dy_raw * yn))
        do_bf = d_o.astype(BF16)
        do_o = do_bf.astype(F32) * o_blk
        q = q_ref[...]
        blocks = [slice(b * Q_BLOCK, (b + 1) * Q_BLOCK) for b in range(Q_GROUP)]
        lane_lo_q = lane_lo[:Q_BLOCK]
        totals = [(jnp.sum(jnp.where(lane_lo_q, do_o[r], 0.0), axis=-1, keepdims=True),
                   jnp.sum(jnp.where(lane_lo_q, 0.0, do_o[r]), axis=-1, keepdims=True)) for r in blocks]
        q_rows = [_head_rows(q[r], lane_lo_q) for r in blocks]
        do_rows = [_head_rows(do_bf[r], lane_lo_q) for r in blocks]
        q_scaled = [qr * jnp.asarray(scale, BF16) for qr in q_rows]

        def step(starts, masks, states):
            kcs = [k_ref[pl.ds(r, KEY_CHUNK), :] for r in starts]
            vcs = [v_ref[pl.ds(r, KEY_CHUNK), :] for r in starts]
            head = lambda h: slice(h * Q_BLOCK, (h + 1) * Q_BLOCK)
            zs = _scores_phase([q_scaled[b][head(h)] for b, h in CHAINS], [kcs[b] for b, _ in CHAINS],
                               [masks[b] for b, _ in CHAINS])
            das = [lax.dot_general(do_rows[b][head(h)], vcs[b], (_NT, ((), ())), preferred_element_type=F32)
                   for b, h in CHAINS]
            sps, lbs = _logs_phase(zs)
            suffixes, carries = _suffix_phase(sps, [states[b][0][h] for b, h in CHAINS], umat)
            a_bf = [jnp.exp(lb - sfx).astype(BF16) for lb, sfx in zip(lbs, suffixes)]
            gs = [a.astype(F32) * da for a, da in zip(a_bf, das)]
            suffixes_g, carries_g = _suffix_phase(gs, [states[b][1][h] for b, h in CHAINS], gmat)
            dzs = []
            for (b, h), lb, g, sfx_g in zip(CHAINS, lbs, gs, suffixes_g):
                before = totals[b][h] - sfx_g
                dzs.append((g - jnp.exp(lb) * (g + before)).astype(BF16))
            new = []
            for b in range(Q_GROUP):
                pair = slice(2 * b, 2 * b + 2)
                dq_acc = states[b][2] + jnp.dot(jnp.concatenate(dzs[pair], axis=1), _head_rows(kcs[b], lane_lo_keys),
                                                preferred_element_type=F32)
                dk_s[pl.ds(starts[b], KEY_CHUNK), :] += lax.dot_general(
                    jnp.concatenate(dzs[pair], axis=0), q_rows[b], (_TN, ((), ())), preferred_element_type=F32)
                dv_s[pl.ds(starts[b], KEY_CHUNK), :] += lax.dot_general(
                    jnp.concatenate(a_bf[pair], axis=0), do_rows[b], (_TN, ((), ())), preferred_element_type=F32)
                new.append(((carries[2 * b], carries[2 * b + 1]), (carries_g[2 * b], carries_g[2 * b + 1]), dq_acc))
            return new

        zero = jnp.zeros((Q_BLOCK, LANES), F32)
        states = _sweep(i * Q_GROUP, step, [((zero, zero), (zero, zero), zero)] * Q_GROUP, lambda st: st[0])
        dq_ref[...] = (jnp.concatenate([st[2] for st in states], axis=0) * scale).astype(BF16)

        @pl.when(i == n_steps - 1)
        def _():
            dk_ref[...] = (dk_s[...] * scale).astype(BF16)
            dv_ref[...] = dv_s[...].astype(BF16)

        @pl.when((p == npair - 1) & (i == n_steps - 1))
        def _():
            for cp in copies:
                cp.wait()

    whole = pl.BlockSpec((t_len, LANES), lambda p, i: (0, p))
    hbm = pl.BlockSpec(memory_space=pl.ANY)
    outs = pl.pallas_call(
        body, name="attn_bwd", grid=(npair, n_steps),
        in_specs=[pl.BlockSpec((rows_per_step, LANES), lambda p, i: (i, p)),
                  pl.BlockSpec((t_len, LANES), lambda p, i: (0, npair + p)),
                  pl.BlockSpec((t_len, LANES), lambda p, i: (0, 2 * npair + p)),
                  pl.BlockSpec((rows_per_step, LANES), lambda p, i: (i, p)),
                  pl.BlockSpec((rows_per_step, LANES), lambda p, i: (i, npair + p)),
                  pl.BlockSpec((1, LANES), lambda p, i: (0, p)),
                  pl.BlockSpec((2 * LANES, 2 * LANES), lambda p, i: (0, 0)),
                  pl.BlockSpec((2 * LANES, 2 * LANES), lambda p, i: (0, 0))] + [hbm] * n_ex,
        out_specs=[pl.BlockSpec((rows_per_step, LANES), lambda p, i: (i, p)), whole, whole,
                   pl.BlockSpec((8, LANES), lambda p, i: (0, p))] + [hbm] * n_ex,
        out_shape=[jax.ShapeDtypeStruct((t_len, width), BF16)] * 3 + [jax.ShapeDtypeStruct((8, width), F32)]
        + _chip_landing(exchange),
        scratch_shapes=[pltpu.VMEM((t_len, LANES), F32), pltpu.VMEM((t_len, LANES), F32)] + _chip_scratch(n_ex),
        compiler_params=_params(("arbitrary", "arbitrary")),
    )(qkv, qkv, qkv, o, dymix, g_attn, _cumsum_matrix(False), _cumsum_matrix(True), *exchange)
    return outs[0], outs[1], outs[2], outs[3], outs[4:]


def kernel(x, w_in, conv_w, g_conv, g_attn, w_out, ln1_g, ln1_b, w_up, w_down, ln2_g, ln2_b, loss_target, m_w_in, m_conv_w, m_g_conv, m_g_attn, m_w_out, m_ln1_g, m_ln1_b, m_w_up, m_w_down, m_ln2_g, m_ln2_b, v_w_in, v_conv_w, v_g_conv, v_g_attn, v_w_out, v_ln1_g, v_ln1_b, v_w_up, v_w_down, v_ln2_g, v_ln2_b):
    x2, target = x[0], loss_target[0]
    t_len, d_model = x2.shape
    width = g_conv.shape[1]
    in_shard = w_in.shape[2]
    up_shard = w_up.shape[2]
    d_ff = up_shard * N_DEV
    cw = conv_w.shape[2]
    xi, yi, ci = _place()
    dev = 4 * xi + 2 * yi + ci
    tm = 256

    win_g, convw_g, wout_bf, wup_bf, wdown_bf = _all_gather(
        "gather_w_in", [w_in[0], jnp.pad(conv_w[0], ((0, 5), (0, LANES - cw)))], [BF16, F32],
        cast_only=[w_out[0], w_up[0], w_down[0]])
    conv_full = jnp.transpose(convw_g[:, :3, :cw], (1, 0, 2)).reshape(3, width)

    n_row_tiles = t_len // tm
    rows = lambda cols: pl.BlockSpec((tm, cols), lambda i: (i, 0))
    resident = lambda arr: pl.BlockSpec(arr.shape, lambda i: (0,) * arr.ndim)
    vec = pl.BlockSpec((1, d_model), lambda i: (0, 0))
    acc8 = pl.BlockSpec((8, d_model), lambda i: (0, 0))
    tokens = lambda cols, dtype: jax.ShapeDtypeStruct((t_len, cols), dtype)

    def proj_in_epilogue(acc, extra, outs, i):
        x_ref, w_ref, gain_ref, tail_ref = extra
        gates_ref, qkv_ref, xbf_ref, ymix_ref = outs
        gates_ref[...] = acc[:, :3 * width]
        qkv_ref[...] = acc[:, 3 * width:].astype(BF16)
        xbf_ref[...] = x_ref[...].astype(BF16)

        @pl.when(i == 0)
        def _():
            tail_ref[...] = jnp.zeros_like(tail_ref)

        u = acc[:, width:2 * width] * acc[:, 2 * width:3 * width]
        ext = jnp.concatenate([tail_ref[...], u], axis=0)
        tail_ref[...] = u[tm - 8:]
        w = w_ref[...]
        y = acc[:, :width] * (w[2:3] * u + w[1:2] * pltpu.roll(ext, 1, 0)[8:] + w[0:1] * pltpu.roll(ext, 2, 0)[8:])
        lane_lo = lax.broadcasted_iota(jnp.int32, (tm, LANES), 1) < HEAD_DIM
        gain = gain_ref[...]
        for blk in range(width // LANES):
            lanes = slice(blk * LANES, (blk + 1) * LANES)
            y_blk = y[:, lanes]
            rs = lax.rsqrt(_group_sums(y_blk * y_blk, lane_lo) * (1.0 / HEAD_DIM) + RMS_EPS)
            ymix_ref[:, lanes] = (y_blk * rs * gain[:, lanes]).astype(BF16)

    half_vec = pl.BlockSpec((1, width), lambda i: (0, 0))
    gates, qkv, x_bf, ymix, wdown_g = _matmul(
        "proj_in", x2, win_g, rows(d_model), resident(win_g), _shards_wide, n_row_tiles,
        [(tokens(3 * width, F32), rows(3 * width)), (tokens(3 * width, BF16), rows(3 * width)),
         (tokens(d_model, BF16), rows(d_model)), (tokens(2 * width, BF16), rows(width))],
        proj_in_epilogue,
        extras=[(x2, rows(d_model)), (conv_full, pl.BlockSpec((3, width), lambda i: (0, 0))), (g_conv, half_vec)],
        gather=[wdown_bf], carry=[pltpu.VMEM((8, width), F32)])

    o_attn, ymix, (wout_g, wup_g) = _attn_fwd(qkv, g_attn, ymix, width, [wout_bf, wup_bf])
    wout_full = wout_g.reshape(d_model, d_model)
    wdown_full = wdown_g.reshape(d_ff, d_model)

    def out_ln1_epilogue(acc, extra, outs, i):
        x_ref, g_ref, b_ref = extra
        r1 = ALPHA * x_ref[...] + acc
        x1, _, _ = _ln_fwd(r1, g_ref[...], b_ref[...])
        outs[0][...] = r1
        outs[1][...] = x1
        outs[2][...] = x1.astype(BF16)

    r1, x1, x1_bf = _matmul(
        "proj_out_ln1", ymix, wout_full, rows(d_model), resident(wout_full), _whole(_NN), n_row_tiles,
        [(tokens(d_model, F32), rows(d_model)), (tokens(d_model, F32), rows(d_model)),
         (tokens(d_model, BF16), rows(d_model))],
        out_ln1_epilogue, extras=[(x2, rows(d_model)), (ln1_g, vec), (ln1_b, vec)])

    def up_epilogue(acc, extra, outs, i):
        outs[0][...] = acc
        relu = jnp.maximum(acc, 0.0)
        outs[1][...] = (relu * relu).astype(BF16)

    up, hid = _matmul(
        "ffn_up", x1_bf, wup_g, rows(d_model), resident(wup_g), _shards_wide, n_row_tiles,
        [(tokens(d_ff, F32), rows(d_ff)), (tokens(d_ff, BF16), rows(d_ff))], up_epilogue)

    def down_ln2_epilogue(acc, extra, outs, i):
        x1_ref, t_ref, g_ref, b_ref = extra
        dr2_ref, dr2_bf_ref, loss_ref, gg_ref, gb_ref = outs
        gain = g_ref[...]
        r2 = ALPHA * x1_ref[...] + acc
        y, xhat, rstd = _ln_fwd(r2, gain, b_ref[...])
        diff = y - t_ref[...]
        d_y = diff * (1.0 / d_model)
        dr2 = _ln_bwd(d_y, xhat, rstd, gain)
        dr2_ref[...] = dr2
        dr2_bf_ref[...] = dr2.astype(BF16)
        _accumulate(gg_ref, i, n_row_tiles, _rows8(d_y * xhat))
        _accumulate(gb_ref, i, n_row_tiles, _rows8(d_y))
        _accumulate(loss_ref, i, n_row_tiles, _rows8(diff * diff), all_lanes=True)

    small_acc = (jax.ShapeDtypeStruct((8, d_model), F32), acc8)
    dr2, dr2_bf, loss_cols, g_ln2_g, g_ln2_b = _matmul(
        "ffn_down_ln2_loss", hid, wdown_full, rows(d_ff), resident(wdown_full), _whole(_NN), n_row_tiles,
        [(tokens(d_model, F32), rows(d_model)), (tokens(d_model, BF16), rows(d_model)),
         small_acc, small_acc, small_acc],
        down_ln2_epilogue, extras=[(x1, rows(d_model)), (target, rows(d_model)), (ln2_g, vec), (ln2_b, vec)])

    def dup_epilogue(acc, extra, outs, i):
        outs[0][...] = (acc * (2.0 * jnp.maximum(extra[0][...], 0.0))).astype(BF16)

    (dup,) = _matmul(
        "ffn_dhid", dr2_bf, wdown_full, rows(d_model), resident(wdown_full), _whole(_NT), n_row_tiles,
        [(tokens(d_ff, BF16), rows(d_ff))], dup_epilogue, extras=[(up, rows(d_ff))])

    gw_down, pair_down = _weight_grad("grad_w_down", hid, dr2_bf, (d_ff // N_DEV, d_model), True)
    gw_up, pair_up = _weight_grad("grad_w_up", x1_bf, dup, (d_model, up_shard), False)

    def dx1_ln1_epilogue(acc, extra, outs, i):
        dr2_ref, r1_ref, g_ref, b_ref = extra
        dr1_ref, dr1_bf_ref, gg_ref, gb_ref = outs
        gain = g_ref[...]
        dx1 = ALPHA * dr2_ref[...] + acc
        _, xhat, rstd = _ln_fwd(r1_ref[...], gain, b_ref[...])
        dr1 = _ln_bwd(dx1, xhat, rstd, gain)
        dr1_ref[...] = dr1
        dr1_bf_ref[...] = dr1.astype(BF16)
        _accumulate(gg_ref, i, n_row_tiles, _rows8(dx1 * xhat))
        _accumulate(gb_ref, i, n_row_tiles, _rows8(dx1))

    dr1, dr1_bf, g_ln1_g, g_ln1_b = _matmul(
        "ffn_dx1_ln1", dup, wup_g, rows(d_ff), resident(wup_g), _shards_deep, n_row_tiles,
        [(tokens(d_model, F32), rows(d_model)), (tokens(d_model, BF16), rows(d_model)), small_acc, small_acc],
        dx1_ln1_epilogue, extras=[(dr2, rows(d_model)), (r1, rows(d_model)), (ln1_g, vec), (ln1_b, vec)])

    gw_out, pair_out = _weight_grad("grad_w_out", ymix, dr1_bf, (d_model // N_DEV, d_model), True)

    c_idx = jnp.reshape(ci, (1,)).astype(jnp.int32)
    q_me = jnp.reshape(2 * xi + yi, (1,)).astype(jnp.int32)

    def pair_sums(names, grads, landed):
        sums = [_pair_sum("pair_sum_" + nm, g, land, c_idx, q_me) for nm, g, land in zip(names, grads, landed)]
        return [s[0] for s in sums], [s[1] for s in sums]

    wire_ffn, own_ffn = pair_sums(["w_out", "w_up", "w_down"], [gw_out, gw_up, gw_down],
                                  [pair_out, pair_up, pair_down])

    def plain_epilogue(acc, extra, outs, i):
        outs[0][...] = acc

    (dymix,) = _matmul(
        "proj_out_bwd", dr1_bf, wout_full, rows(d_model), resident(wout_full), _whole(_NT), n_row_tiles,
        [(tokens(d_model, F32), rows(d_model))], plain_epilogue)

    d_bg, d_cg, d_h, g_gconv, g_convw = _conv_bwd(gates, dymix, conv_full, g_conv, width)
    d_q, d_k, d_v, g_gattn, landed_ffn = _attn_bwd(qkv, o_attn, dymix, g_attn, width, wire_ffn)
    dproj = jnp.concatenate([d_bg, d_cg, d_h, d_q, d_k, d_v], axis=1)

    gw_in, pair_in = _weight_grad("grad_w_in", x_bf, dproj, (d_model, in_shard), False)
    wire_in, own_in = pair_sums(["w_in"], [gw_in], [pair_in])

    def dx_epilogue(acc, extra, outs, i):
        outs[0][...] = ALPHA * extra[0][...] + acc

    spare = jnp.zeros((3, d_model - width), F32)

    def pack(l1g, l1b, l2g, l2b, gc, ga, taps, tail=spare):
        return jnp.concatenate([l1g, l1b, l2g, l2b, jnp.concatenate([gc, ga], axis=1),
                                jnp.concatenate([taps, tail], axis=1)], axis=0)

    local = pack(g_ln1_g[:1], g_ln1_b[:1], g_ln2_g[:1], g_ln2_b[:1], g_gconv[:1], g_gattn[:1], g_convw[:3],
                 spare + (0.5 / d_model) * loss_cols[0, 0])

    grad_x, landed_in, parts = _matmul(
        "proj_in_bwd", dproj, win_g, rows(6 * width), resident(win_g), _shards_deep, n_row_tiles,
        [(tokens(d_model, F32), rows(d_model))], dx_epilogue, extras=[(dr1, rows(d_model))],
        exchange=wire_in, gather=[local])

    names = ["w_in", "w_out", "w_up", "w_down"]
    big = {}
    for nm, ow, land, w, m, v in zip(names, own_in + own_ffn, [landed_in] + list(landed_ffn),
                                     (w_in, w_out, w_up, w_down),
                                     (m_w_in, m_w_out, m_w_up, m_w_down), (v_w_in, v_w_out, v_w_up, v_w_down)):
        big[nm] = [r[None] for r in _shard_update("update_" + nm, ow, land, w[0], m[0], v[0])]

    def spread(a):
        return lax.dynamic_update_slice(jnp.zeros((3, width), F32), a, (0, dev * cw))

    w_small = pack(ln1_g, ln1_b, ln2_g, ln2_b, g_conv, g_attn, spread(conv_w[0]))
    m_small = pack(m_ln1_g, m_ln1_b, m_ln2_g, m_ln2_b, m_g_conv, m_g_attn, spread(m_conv_w[0]))
    v_small = pack(v_ln1_g, v_ln1_b, v_ln2_g, v_ln2_b, v_g_conv, v_g_attn, spread(v_conv_w[0]))
    small = _small_update(parts, w_small, m_small, v_small)

    def unpack(a):
        taps = lax.dynamic_slice(a[5:8, :width], (0, dev * cw), (3, cw))[None]
        return {"ln1_g": a[0:1], "ln1_b": a[1:2], "ln2_g": a[2:3], "ln2_b": a[3:4],
                "g_conv": a[4:5, :width], "g_attn": a[4:5, width:], "conv_w": taps}

    loss = small[0][7, d_model - 1]
    small = [unpack(a) for a in small]

    order = ["w_in", "conv_w", "g_conv", "g_attn", "w_out", "ln1_g", "ln1_b", "w_up", "w_down", "ln2_g", "ln2_b"]
    result = [loss, grad_x[None]]
    for kind in range(4):
        for nm in order:
            result.append(big[nm][kind] if nm in big else small[kind][nm])
    return tuple(result)
```

```python
import functools

import jax
import jax.numpy as jnp
from jax import lax
from jax.experimental import pallas as pl
from jax.experimental.pallas import tpu as pltpu

F32 = jnp.float32
BF16 = jnp.bfloat16
MESH = pl.DeviceIdType.MESH

N_DEV = 8
HEAD_DIM = 64
LANES = 128
Q_BLOCK = 128
ALPHA = 2.0 ** 0.25
LN_EPS = 1e-5
RMS_EPS = 1e-6
ADAM_LR, ADAM_B1, ADAM_B2, ADAM_EPS, ADAM_WD, ADAM_STEP = 0.001, 0.9, 0.999, 1e-08, 0.01, 10
VMEM_LIMIT_BYTES = 48 * 1024 * 1024

_NT = ((1,), (1,))
_NN = ((1,), (0,))
_TN = ((0,), (0,))


def _params(sem=None):
    return pltpu.CompilerParams(dimension_semantics=sem, vmem_limit_bytes=VMEM_LIMIT_BYTES)


def _dot(a, b, contract):
    return lax.dot_general(a.astype(BF16), b.astype(BF16), (contract, ((), ())), preferred_element_type=F32)


def _whole(contract):
    return lambda a_ref, b_ref: _dot(a_ref[...], b_ref[...], contract)


def _shards_wide(a_ref, b_ref):
    a = a_ref[...].astype(BF16)
    return jnp.concatenate([_dot(a, b_ref[s], _NN) for s in range(b_ref.shape[0])], axis=1)


def _shards_deep(a_ref, b_ref):
    n = b_ref.shape[2]
    acc = _dot(a_ref[:, 0:n], b_ref[0], _NT)
    for s in range(1, b_ref.shape[0]):
        acc = acc + _dot(a_ref[:, s * n:(s + 1) * n], b_ref[s], _NT)
    return acc


def _matmul(name, a, b, a_spec, b_spec, product, n_steps, outs, epilogue, extras=(), exchange=(), gather=(),
            carry=(), sub=1):
    behind = list(exchange) + list(gather)
    n_extra, n_out, n_ex, n_b, n_carry = len(extras), len(outs), len(exchange), len(behind), len(carry)

    def body(a_ref, b_ref, *rest):
        i = pl.program_id(0)
        extra_refs, rest = rest[:n_extra], rest[n_extra:]
        src_refs, rest = rest[:n_b], rest[n_b:]
        out_refs, rest = rest[:n_out], rest[n_out:]
        land_refs, rest = rest[:n_b], rest[n_b:]
        extra_refs, sems = extra_refs + rest[:n_carry], rest[n_carry:]
        copies = _chip_copies(src_refs[:n_ex], land_refs[:n_ex], *sems[:2]) if exchange else []
        gather_sems = sems[2:] if exchange else sems
        plan = _gather_plan(src_refs[n_ex:], land_refs[n_ex:], *gather_sems) if gather else None

        def start():
            for cp in copies:
                cp.start()
            if plan:
                plan[0]()

        def finish():
            if plan:
                plan[1]()
                plan[2]()
            for cp in copies:
                cp.wait()

        if behind:
            pl.when(i == 0)(start)
        tile_rows = a_ref.shape[0]
        sub_rows = tile_rows // sub

        def view(ref, k):
            return ref.at[pl.ds(k * sub_rows, sub_rows)] if sub > 1 and ref.shape[0] == tile_rows else ref

        accs = [product(view(a_ref, k), b_ref) for k in range(sub)]
        for k in range(sub):
            epilogue(accs[k], tuple(view(r, k) for r in extra_refs), tuple(view(r, k) for r in out_refs), i * sub + k)
        if behind:
            pl.when(i == n_steps - 1)(finish)

    hbm = pl.BlockSpec(memory_space=pl.ANY)
    landing = _chip_landing(exchange) + [jax.ShapeDtypeStruct((N_DEV,) + s.shape, s.dtype) for s in gather]
    scratch = (list(carry) + (_chip_scratch(len(exchange)) if exchange else [])
               + (_gather_scratch(len(gather)) if gather else []))
    return pl.pallas_call(
        body, name=name, grid=(n_steps,),
        in_specs=[a_spec, b_spec] + [s for _, s in extras] + [hbm] * n_b,
        out_specs=[s for _, s in outs] + [hbm] * n_b,
        out_shape=[o for o, _ in outs] + landing,
        scratch_shapes=scratch,
        compiler_params=_params(("arbitrary",)),
    )(a, b, *[e for e, _ in extras], *behind)


def _ln_fwd(r, g, b):
    mu = jnp.mean(r, axis=-1, keepdims=True)
    xc = r - mu
    rstd = lax.rsqrt(jnp.mean(xc * xc, axis=-1, keepdims=True) + LN_EPS)
    xhat = xc * rstd
    return xhat * g + b, xhat, rstd


def _ln_bwd(dy, xhat, rstd, g):
    dxh = dy * g
    m1 = jnp.mean(dxh, axis=-1, keepdims=True)
    m2 = jnp.mean(dxh * xhat, axis=-1, keepdims=True)
    return rstd * (dxh - m1 - xhat * m2)


def _rows8(v):
    n, c = v.shape
    return jnp.sum(v.reshape(n // 8, 8, c), axis=0)


def _accumulate(ref, i, n_steps, part8, all_lanes=False):
    @pl.when(i == 0)
    def _():
        ref[...] = part8

    @pl.when(i > 0)
    def _():
        ref[...] += part8

    @pl.when(i == n_steps - 1)
    def _():
        tot = jnp.sum(ref[...], axis=0, keepdims=True)
        if all_lanes:
            tot = jnp.sum(tot, axis=1, keepdims=True)
        ref[...] = jnp.broadcast_to(tot, ref.shape)


def _group_sums(v, lane_lo):
    s0 = jnp.sum(jnp.where(lane_lo, v, 0.0), axis=-1, keepdims=True)
    s1 = jnp.sum(jnp.where(lane_lo, 0.0, v), axis=-1, keepdims=True)
    return jnp.where(lane_lo, s0, s1)


def _place():
    return lax.axis_index("x"), lax.axis_index("y"), lax.axis_index("c")


def _gather_plan(src_refs, out_refs, send_sems, recv_sems, local_sems):
    n = len(src_refs)
    x, y, c = _place()
    me, sibling = (x, y, c), (x, y, 1 - c)
    chips = [(1 - x, y), (x, 1 - y), (1 - x, 1 - y)]

    def copy(a, k, block, to, src=None):
        slot = out_refs[a].at[4 * block[0] + 2 * block[1] + block[2]]
        return pltpu.make_async_remote_copy(
            src_ref=slot if src is None else src, dst_ref=slot,
            send_sem=send_sems.at[a, k], recv_sem=recv_sems.at[a, k],
            device_id=to, device_id_type=MESH)

    mine = [pltpu.make_async_copy(src_refs[a], out_refs[a].at[4 * x + 2 * y + c], local_sems.at[a]) for a in range(n)]
    first = [[copy(a, 0, me, sibling, src=src_refs[a])]
             + [copy(a, 1 + j, me, (*chip, c), src=src_refs[a]) for j, chip in enumerate(chips)] for a in range(n)]
    passed = [[copy(a, 4 + j, (*chip, c), sibling) for j, chip in enumerate(chips)] for a in range(n)]

    def start():
        for a in range(n):
            mine[a].start()
            for cp in first[a]:
                cp.start()

    def forward():
        for j, chip in enumerate(chips):
            for a in range(n):
                copy(a, 1 + j, (*chip, c), me).wait_recv()
                passed[a][j].start()

    def finish():
        for a in range(n):
            copy(a, 0, sibling, me).wait_recv()
            for j, chip in enumerate(chips):
                copy(a, 4 + j, (*chip, 1 - c), me).wait_recv()
        for a in range(n):
            for cp in first[a] + passed[a]:
                cp.wait_send()
            mine[a].wait()

    return start, forward, finish


def _gather_scratch(n):
    return [pltpu.SemaphoreType.DMA((n, 7)), pltpu.SemaphoreType.DMA((n, 7)), pltpu.SemaphoreType.DMA((n,))]


def _all_gather(name, shards, out_dtypes, cast_only=()):
    n, m = len(shards), len(cast_only)

    def body(*refs):
        in_refs, cast_in = refs[:n], refs[n:n + m]
        out_refs, cast_out = refs[n + m:2 * n + m], refs[2 * n + m:2 * (n + m)]
        stage = refs[2 * (n + m):3 * n + 2 * m]
        for a in range(n):
            stage[a][...] = in_refs[a][...].astype(out_dtypes[a])
        start, forward, finish = _gather_plan(stage, out_refs, *refs[3 * n + 2 * m:])
        start()
        for a in range(m):
            cast_out[a][...] = cast_in[a][...].astype(BF16)
        forward()
        finish()

    return pl.pallas_call(
        body, name=name,
        in_specs=[pl.BlockSpec(memory_space=pltpu.VMEM)] * (n + m),
        out_specs=[pl.BlockSpec(memory_space=pl.ANY)] * n + [pl.BlockSpec(memory_space=pltpu.VMEM)] * m,
        out_shape=[jax.ShapeDtypeStruct((N_DEV,) + s.shape, dt) for s, dt in zip(shards, out_dtypes)]
        + [jax.ShapeDtypeStruct(s.shape, BF16) for s in cast_only],
        scratch_shapes=[pltpu.VMEM(s.shape, dt) for s, dt in zip(shards, out_dtypes)] + _gather_scratch(n),
        compiler_params=pltpu.CompilerParams(vmem_limit_bytes=VMEM_LIMIT_BYTES),
    )(*shards, *cast_only)


def _weight_grad(name, acts, cots, shard_shape, acts_by_shard):
    t_len = acts.shape[0]
    sr, sc = shard_shape
    per = max(1, 256 // sr) if acts_by_shard else 1
    assert per in (1, 2)
    n_steps = N_DEV // per

    def body(a_ref, b_ref, g_ref, land_ref, stage, send_sems, recv_sems):
        i = pl.program_id(0)
        x, y, c = _place()

        def copy(q):
            return pltpu.make_async_remote_copy(
                src_ref=stage.at[q % 2], dst_ref=land_ref.at[q], send_sem=send_sems.at[q], recv_sem=recv_sems.at[q],
                device_id=(x, y, 1 - c), device_id_type=MESH)

        acc = _dot(a_ref[...], b_ref[...], _TN).reshape(per, sr, sc)
        g_ref[...] = acc
        if per == 2:
            payload, due, q = jnp.where(c == 0, acc[1], acc[0]), i >= 0, i
        else:
            payload, due, q = acc[0], (i % 2) == (1 - c), i // 2

        @pl.when(due)
        def _():
            @pl.when(q >= 2)
            def _():
                copy(q - 2).wait_send()

            stage[q % 2] = payload.astype(BF16)
            copy(q).start()

        @pl.when(i == n_steps - 1)
        def _():
            copy(2).wait_send()
            copy(3).wait_send()
            for k in range(4):
                copy(k).wait_recv()

    tall = lambda cols: pl.BlockSpec((t_len, cols), lambda i: (0, i))
    whole = lambda arr: pl.BlockSpec(arr.shape, lambda i: (0, 0))
    a_spec, b_spec = (tall(per * sr), whole(cots)) if acts_by_shard else (whole(acts), tall(sc))
    return pl.pallas_call(
        body, name=name, grid=(n_steps,),
        in_specs=[a_spec, b_spec],
        out_specs=[pl.BlockSpec((per, sr, sc), lambda i: (i, 0, 0)), pl.BlockSpec(memory_space=pl.ANY)],
        out_shape=[jax.ShapeDtypeStruct((N_DEV, sr, sc), F32), jax.ShapeDtypeStruct((4, sr, sc), BF16)],
        scratch_shapes=[pltpu.VMEM((2, sr, sc), BF16), pltpu.SemaphoreType.DMA((4,)), pltpu.SemaphoreType.DMA((4,))],
        compiler_params=_params(("arbitrary",)),
    )(acts, cots)


def _chip_copies(p_refs, land_refs, send_sems, recv_sems):
    x, y, c = _place()
    chips = [(1 - x, y), (x, 1 - y), (1 - x, 1 - y)]
    return [pltpu.make_async_remote_copy(
        src_ref=p_refs[a].at[2 * px + py], dst_ref=land_refs[a].at[k],
        send_sem=send_sems.at[a, k], recv_sem=recv_sems.at[a, k],
        device_id=(px, py, c), device_id_type=MESH)
        for a in range(len(p_refs)) for k, (px, py) in enumerate(chips)]


def _chip_scratch(n):
    return [pltpu.SemaphoreType.DMA((n, 3)), pltpu.SemaphoreType.DMA((n, 3))]


def _chip_landing(pair_sums):
    return [jax.ShapeDtypeStruct((3,) + p.shape[1:], p.dtype) for p in pair_sums]


def _pair_sum(name, g_f32, landed, c_idx, q_me):
    _, rows, cols = g_f32.shape
    tr = min(rows, 1024)

    def body(c_ref, q_ref, g_ref, l_ref, wire_ref, own_ref):
        q = pl.program_id(1)
        s = g_ref[...] + l_ref[...].astype(F32)
        wire_ref[...] = s.astype(BF16)

        @pl.when(q == q_ref[0])
        def _():
            own_ref[...] = s

    return pl.pallas_call(
        body, name=name,
        grid_spec=pltpu.PrefetchScalarGridSpec(
            num_scalar_prefetch=2, grid=(rows // tr, 4),
            in_specs=[pl.BlockSpec((None, tr, cols), lambda i, q, c_ref, q_ref: (2 * q + c_ref[0], i, 0)),
                      pl.BlockSpec((None, tr, cols), lambda i, q, c_ref, q_ref: (q, i, 0))],
            out_specs=[pl.BlockSpec((None, tr, cols), lambda i, q, c_ref, q_ref: (q, i, 0)),
                       pl.BlockSpec((tr, cols), lambda i, q, c_ref, q_ref: (i, 0))]),
        out_shape=[jax.ShapeDtypeStruct((4, rows, cols), BF16), jax.ShapeDtypeStruct((rows, cols), F32)],
        compiler_params=_params(("parallel", "arbitrary")),
    )(c_idx, q_me, g_f32, landed)


def _adamw(w, g, m, v):
    m = ADAM_B1 * m + (1.0 - ADAM_B1) * g
    v = ADAM_B2 * v + (1.0 - ADAM_B2) * (g * g)
    m_hat = m / (1.0 - ADAM_B1 ** ADAM_STEP)
    v_hat = v / (1.0 - ADAM_B2 ** ADAM_STEP)
    delta = -ADAM_LR * (m_hat / (jnp.sqrt(v_hat) + ADAM_EPS) + ADAM_WD * w)
    return delta, m, v


def _shard_update(name, own, landed, w, m, v):
    rows, cols = own.shape
    tr = min(rows, 256)

    def body(own_ref, l_ref, w_ref, m_ref, v_ref, g_out, d_out, m_out, v_out):
        g = own_ref[...] + l_ref[0].astype(F32) + l_ref[1].astype(F32) + l_ref[2].astype(F32)
        d, m_new, v_new = _adamw(w_ref[...], g, m_ref[...], v_ref[...])
        g_out[...] = g
        d_out[...] = d
        m_out[...] = m_new
        v_out[...] = v_new

    tile = pl.BlockSpec((tr, cols), lambda i: (i, 0))
    return pl.pallas_call(
        body, name=name, grid=(rows // tr,),
        in_specs=[tile, pl.BlockSpec((3, tr, cols), lambda i: (0, i, 0)), tile, tile, tile],
        out_specs=[tile] * 4,
        out_shape=[jax.ShapeDtypeStruct((rows, cols), F32)] * 4,
        compiler_params=_params(("parallel",)),
    )(own, landed, w, m, v)


def _small_update(parts, w, m, v):
    def body(p_ref, w_ref, m_ref, v_ref, g_out, d_out, m_out, v_out):
        g = p_ref[0]
        for d in range(1, N_DEV):
            g = g + p_ref[d]
        dl, m_new, v_new = _adamw(w_ref[...], g, m_ref[...], v_ref[...])
        g_out[...] = g
        d_out[...] = dl
        m_out[...] = m_new
        v_out[...] = v_new

    return pl.pallas_call(
        body, name="small_update",
        out_shape=[jax.ShapeDtypeStruct(w.shape, F32)] * 4,
        compiler_params=pltpu.CompilerParams(vmem_limit_bytes=VMEM_LIMIT_BYTES),
    )(parts, w, m, v)


CONV_CHUNK = 512


def _conv_taps(u_s, lo, chunk):
    ext = u_s[pl.ds(lo, chunk + 8), :]
    return ext[8:], pltpu.roll(ext, 1, 0)[8:], pltpu.roll(ext, 2, 0)[8:]


def _conv_bwd(gates, dymix, conv_w, g_conv, width):
    t_len = gates.shape[0]
    nblk = width // LANES
    n_chunks = t_len // CONV_CHUNK

    def body(bg_ref, cg_ref, h_ref, dy_ref, w_ref, g_ref, dbg_ref, dcg_ref, dh_ref, gg_ref, gw_ref, u_s, dc_s):
        lane_lo = lax.broadcasted_iota(jnp.int32, (CONV_CHUNK, LANES), 1) < HEAD_DIM
        u_s[0:8, :] = jnp.zeros((8, LANES), F32)
        dc_s[t_len:t_len + 8, :] = jnp.zeros((8, LANES), F32)

        def fill(r, _):
            lo = pl.multiple_of(r * CONV_CHUNK, CONV_CHUNK)
            u_s[pl.ds(pl.multiple_of(lo + 8, 8), CONV_CHUNK), :] = cg_ref[pl.ds(lo, CONV_CHUNK), :] * h_ref[pl.ds(lo, CONV_CHUNK), :]
            return 0

        lax.fori_loop(0, n_chunks, fill, 0)
        w = w_ref[...]
        gain = g_ref[...]

        def step(r, acc):
            gg, gw0, gw1, gw2 = acc
            lo = pl.multiple_of(r * CONV_CHUNK, CONV_CHUNK)
            u0, u1, u2 = _conv_taps(u_s, lo, CONV_CHUNK)
            bg = bg_ref[pl.ds(lo, CONV_CHUNK), :]
            conv = w[2:3] * u0 + w[1:2] * u1 + w[0:1] * u2
            y = bg * conv
            rs = lax.rsqrt(_group_sums(y * y, lane_lo) * (1.0 / HEAD_DIM) + RMS_EPS)
            yn = y * rs
            dyn_raw = dy_ref[pl.ds(lo, CONV_CHUNK), :]
            dyn = dyn_raw * gain
            d_y = rs * (dyn - yn * (_group_sums(dyn * yn, lane_lo) * (1.0 / HEAD_DIM)))
            dbg_ref[pl.ds(lo, CONV_CHUNK), :] = (d_y * conv).astype(BF16)
            dconv = d_y * bg
            dc_s[pl.ds(lo, CONV_CHUNK), :] = dconv
            return (gg + _rows8(dyn_raw * yn), gw0 + _rows8(dconv * u2), gw1 + _rows8(dconv * u1),
                    gw2 + _rows8(dconv * u0))

        zero = jnp.zeros((8, LANES), F32)
        gg, gw0, gw1, gw2 = lax.fori_loop(0, n_chunks, step, (zero, zero, zero, zero))
        gg_ref[...] = jnp.broadcast_to(jnp.sum(gg, axis=0, keepdims=True), (8, LANES))
        row = lax.broadcasted_iota(jnp.int32, (8, LANES), 0)
        tot = lambda v: jnp.broadcast_to(jnp.sum(v, axis=0, keepdims=True), (8, LANES))
        gw_ref[...] = jnp.where(row == 0, tot(gw0), jnp.where(row == 1, tot(gw1), jnp.where(row == 2, tot(gw2), 0.0)))

        def back(r, _):
            lo = pl.multiple_of(r * CONV_CHUNK, CONV_CHUNK)
            ext = dc_s[pl.ds(lo, CONV_CHUNK + 8), :]
            d0 = ext[:CONV_CHUNK]
            d1 = pltpu.roll(ext, CONV_CHUNK + 7, 0)[:CONV_CHUNK]
            d2 = pltpu.roll(ext, CONV_CHUNK + 6, 0)[:CONV_CHUNK]
            du = w[2:3] * d0 + w[1:2] * d1 + w[0:1] * d2
            dcg_ref[pl.ds(lo, CONV_CHUNK), :] = (du * h_ref[pl.ds(lo, CONV_CHUNK), :]).astype(BF16)
            dh_ref[pl.ds(lo, CONV_CHUNK), :] = (du * cg_ref[pl.ds(lo, CONV_CHUNK), :]).astype(BF16)
            return 0

        lax.fori_loop(0, n_chunks, back, 0)

    col = lambda off: pl.BlockSpec((t_len, LANES), lambda p: (0, off + p))
    small = pl.BlockSpec((8, LANES), lambda p: (0, p))
    return pl.pallas_call(
        body, name="conv_bwd", grid=(nblk,),
        in_specs=[col(0), col(nblk), col(2 * nblk), col(0),
                  pl.BlockSpec((3, LANES), lambda p: (0, p)), pl.BlockSpec((1, LANES), lambda p: (0, p))],
        out_specs=[col(0), col(0), col(0), small, small],
        out_shape=[jax.ShapeDtypeStruct((t_len, width), BF16)] * 3 + [jax.ShapeDtypeStruct((8, width), F32)] * 2,
        scratch_shapes=[pltpu.VMEM((t_len + 8, LANES), F32), pltpu.VMEM((t_len + 8, LANES), F32)],
        compiler_params=_params(("parallel",)),
    )(gates, gates, gates, dymix, conv_w, g_conv)


def _cumsum_matrix(inclusive):
    j = lax.broadcasted_iota(jnp.int32, (2 * LANES, 2 * LANES), 0) % LANES
    s = lax.broadcasted_iota(jnp.int32, (2 * LANES, 2 * LANES), 1)
    keep = (s >= LANES) | ((j >= s) if inclusive else (j > s))
    return keep.astype(BF16)


KEY_CHUNK = 384
SUB_TILES = KEY_CHUNK // Q_BLOCK
Q_GROUP = 4
DEAD_ABOVE = 105.0
MASKED_Z = -1e30


def _scores_phase(qms, kcs, masks):
    zs = [lax.dot_general(q, k, (_NT, ((), ())), preferred_element_type=F32) for q, k in zip(qms, kcs)]
    return [jnp.where(m, z, MASKED_Z) for z, m in zip(zs, masks)]


def _logs_phase(zs):
    sps = [jnp.maximum(z, 0.0) + jnp.log(1.0 + jnp.exp(-jnp.abs(z))) for z in zs]
    return sps, [z - sp for z, sp in zip(zs, sps)]


def _suffix_phase(vals, carries, mat):
    def hi_lo(tile):
        hi = tile.astype(BF16)
        return jnp.concatenate([hi, (tile - hi.astype(F32)).astype(BF16)], axis=1)

    packed = [[hi_lo(v[:, t * Q_BLOCK:(t + 1) * Q_BLOCK]) for t in range(SUB_TILES)] for v in vals]
    prods = [[jnp.dot(p, mat, preferred_element_type=F32) for p in chain] for chain in packed]
    sums, new_carries = [], []
    for chain, carry in zip(prods, carries):
        parts = []
        for t in reversed(range(SUB_TILES)):
            parts.append(chain[t][:, :LANES] + carry)
            carry = carry + chain[t][:, LANES:]
        sums.append(jnp.concatenate(parts[::-1], axis=1))
        new_carries.append(carry)
    return sums, new_carries


def _sweep(first_block, step, states, carries_of):
    n_chunks = ((first_block + Q_GROUP) * Q_BLOCK + KEY_CHUNK - 1) // KEY_CHUNK
    col = lax.broadcasted_iota(jnp.int32, (Q_BLOCK, KEY_CHUNK), 1)
    row = lax.broadcasted_iota(jnp.int32, (Q_BLOCK, 1), 0)

    def body(loop):
        n, _, sts = loop
        starts, masks = [], []
        for b in range(Q_GROUP):
            i = first_block + b
            end = (i + 1) * Q_BLOCK - n * KEY_CHUNK
            starts.append(pl.multiple_of(jnp.maximum(end - KEY_CHUNK, 0), Q_BLOCK))
            masks.append(col < jnp.minimum(i * Q_BLOCK + row, end) - starts[b])
        sts = step(starts, masks, sts)
        least = None
        for st in sts:
            for carry in carries_of(st):
                least = carry if least is None else jnp.minimum(least, carry)
        alive = (jnp.min(least) <= DEAD_ABOVE).astype(jnp.int32)
        return n + 1, alive, tuple(sts)

    return lax.while_loop(lambda loop: (loop[0] < n_chunks) & (loop[1] > 0), body,
                          (jnp.int32(0), jnp.int32(1), tuple(states)))[2]


CHAINS = [(b, h) for b in range(Q_GROUP) for h in range(2)]


def _head_rows(block, lane_lo_rows):
    zero = jnp.zeros_like(block)
    return jnp.concatenate([jnp.where(lane_lo_rows, block, zero), jnp.where(lane_lo_rows, zero, block)], axis=0)


def _attn_fwd(qkv, g_attn, ymix_in, width, shards):
    t_len = qkv.shape[0]
    npair = width // LANES
    rows_per_step = Q_GROUP * Q_BLOCK
    n_steps = t_len // rows_per_step
    n_sh = len(shards)

    def body(q_ref, k_ref, v_ref, g_ref, umat_ref, ymix_in_ref, *rest):
        shard_refs, rest = rest[:n_sh], rest[n_sh:]
        o_ref, ymix_ref = rest[:2]
        gathered_refs, sems = rest[2:2 + n_sh], rest[2 + n_sh:]
        p, i = pl.program_id(0), pl.program_id(1)
        start, forward, finish = _gather_plan(shard_refs, gathered_refs, *sems)
        pl.when((p == 0) & (i == 0))(start)
        pl.when((p == npair - 1) & (i == n_steps // 2))(forward)
        lane_lo = lax.broadcasted_iota(jnp.int32, (rows_per_step, LANES), 1) < HEAD_DIM
        lane_lo_keys = lax.broadcasted_iota(jnp.int32, (KEY_CHUNK, LANES), 1) < HEAD_DIM
        umat = umat_ref[...]
        q = q_ref[...] * jnp.asarray(HEAD_DIM ** -0.5, BF16)
        q_heads = (jnp.where(lane_lo, q, jnp.zeros_like(q)), jnp.where(lane_lo, jnp.zeros_like(q), q))

        def step(starts, masks, states):
            kcs = [k_ref[pl.ds(r, KEY_CHUNK), :] for r in starts]
            zs = _scores_phase([q_heads[h][b * Q_BLOCK:(b + 1) * Q_BLOCK] for b, h in CHAINS],
                               [kcs[b] for b, _ in CHAINS], [masks[b] for b, _ in CHAINS])
            sps, lbs = _logs_phase(zs)
            suffixes, carries = _suffix_phase(sps, [states[b][0][h] for b, h in CHAINS], umat)
            weights = [jnp.exp(lb - sfx).astype(BF16) for lb, sfx in zip(lbs, suffixes)]
            new = []
            for b in range(Q_GROUP):
                v_heads = _head_rows(v_ref[pl.ds(starts[b], KEY_CHUNK), :], lane_lo_keys)
                o_acc = states[b][1] + jnp.dot(jnp.concatenate(weights[2 * b:2 * b + 2], axis=1), v_heads,
                                               preferred_element_type=F32)
                new.append(((carries[2 * b], carries[2 * b + 1]), o_acc))
            return new

        zero = jnp.zeros((Q_BLOCK, LANES), F32)
        states = _sweep(i * Q_GROUP, step, [((zero, zero), zero)] * Q_GROUP, lambda st: st[0])
        o = jnp.concatenate([st[1] for st in states], axis=0)
        o_ref[...] = o
        rs = lax.rsqrt(_group_sums(o * o, lane_lo) * (1.0 / HEAD_DIM) + RMS_EPS)
        ymix_ref[...] = (o * rs * g_ref[...]).astype(BF16)
        pl.when((p == npair - 1) & (i == n_steps - 1))(finish)

    hbm = pl.BlockSpec(memory_space=pl.ANY)
    outs = pl.pallas_call(
        body, name="attn_fwd", grid=(npair, n_steps),
        in_specs=[pl.BlockSpec((rows_per_step, LANES), lambda p, i: (i, p)),
                  pl.BlockSpec((t_len, LANES), lambda p, i: (0, npair + p)),
                  pl.BlockSpec((t_len, LANES), lambda p, i: (0, 2 * npair + p)),
                  pl.BlockSpec((1, LANES), lambda p, i: (0, p)),
                  pl.BlockSpec((2 * LANES, 2 * LANES), lambda p, i: (0, 0)),
                  hbm] + [hbm] * n_sh,
        out_specs=[pl.BlockSpec((rows_per_step, LANES), lambda p, i: (i, p)),
                   pl.BlockSpec((rows_per_step, LANES), lambda p, i: (i, npair + p))] + [hbm] * n_sh,
        out_shape=[jax.ShapeDtypeStruct((t_len, width), F32), jax.ShapeDtypeStruct(ymix_in.shape, BF16)]
        + [jax.ShapeDtypeStruct((N_DEV,) + s.shape, s.dtype) for s in shards],
        input_output_aliases={5: 1},
        scratch_shapes=_gather_scratch(n_sh),
        compiler_params=_params(("arbitrary", "arbitrary")),
    )(qkv, qkv, qkv, g_attn, _cumsum_matrix(False), ymix_in, *shards)
    return outs[0], outs[1], outs[2:]


def _attn_bwd(qkv, o, dymix, g_attn, width, exchange):
    t_len = qkv.shape[0]
    npair = width // LANES
    rows_per_step = Q_GROUP * Q_BLOCK
    n_steps = t_len // rows_per_step
    scale = HEAD_DIM ** -0.5
    n_ex = len(exchange)

    def body(q_ref, k_ref, v_ref, o_ref, dy_ref, g_ref, umat_ref, gmat_ref, *rest):
        p_refs, rest = rest[:n_ex], rest[n_ex:]
        dq_ref, dk_ref, dv_ref, gg_ref = rest[:4]
        land_refs, rest = rest[4:4 + n_ex], rest[4 + n_ex:]
        dk_s, dv_s = rest[:2]
        copies = _chip_copies(p_refs, land_refs, *rest[2:])
        p, i = pl.program_id(0), pl.program_id(1)

        @pl.when((p == 0) & (i == 0))
        def _():
            for cp in copies:
                cp.start()

        lane_lo = lax.broadcasted_iota(jnp.int32, (rows_per_step, LANES), 1) < HEAD_DIM
        lane_lo_keys = lax.broadcasted_iota(jnp.int32, (KEY_CHUNK, LANES), 1) < HEAD_DIM
        umat, gmat = umat_ref[...], gmat_ref[...]

        @pl.when(i == 0)
        def _():
            dk_s[...] = jnp.zeros_like(dk_s)
            dv_s[...] = jnp.zeros_like(dv_s)

        o_blk = o_ref[...]
        rs = lax.rsqrt(_group_sums(o_blk * o_blk, lane_lo) * (1.0 / HEAD_DIM) + RMS_EPS)
        yn = o_blk * rs
        dy_raw = dy_ref[...]
        dyn = dy_raw * g_ref[...]
        d_o = rs * (dyn - yn * (_group_sums(dyn * yn, lane_lo) * (1.0 / HEAD_DIM)))
        _accumulate(gg_ref, i, n_steps, _rows8(dy_raw * yn))
        do_bf = d_o.astype(BF16)
        do_o = do_bf.astype(F32) * o_blk
        q = q_ref[...]
        blocks = [slice(b * Q_BLOCK, (b + 1) * Q_BLOCK) for b in range(Q_GROUP)]
        lane_lo_q = lane_lo[:Q_BLOCK]
        totals = [(jnp.sum(jnp.where(lane_lo_q, do_o[r], 0.0), axis=-1, keepdims=True),
                   jnp.sum(jnp.where(lane_lo_q, 0.0, do_o[r]), axis=-1, keepdims=True)) for r in blocks]
        q_rows = [_head_rows(q[r], lane_lo_q) for r in blocks]
        do_rows = [_head_rows(do_bf[r], lane_lo_q) for r in blocks]
        q_scaled = [qr * jnp.asarray(scale, BF16) for qr in q_rows]

        def step(starts, masks, states):
            kcs = [k_ref[pl.ds(r, KEY_CHUNK), :] for r in starts]
            vcs = [v_ref[pl.ds(r, KEY_CHUNK), :] for r in starts]
            head = lambda h: slice(h * Q_BLOCK, (h + 1) * Q_BLOCK)
            zs = _scores_phase([q_scaled[b][head(h)] for b, h in CHAINS], [kcs[b] for b, _ in CHAINS],
                               [masks[b] for b, _ in CHAINS])
            das = [lax.dot_general(do_rows[b][head(h)], vcs[b], (_NT, ((), ())), preferred_element_type=F32)
                   for b, h in CHAINS]
            sps, lbs = _logs_phase(zs)
            suffixes, carries = _suffix_phase(sps, [states[b][0][h] for b, h in CHAINS], umat)
            a_bf = [jnp.exp(lb - sfx).astype(BF16) for lb, sfx in zip(lbs, suffixes)]
            gs = [a.astype(F32) * da for a, da in zip(a_bf, das)]
            suffixes_g, carries_g = _suffix_phase(gs, [states[b][1][h] for b, h in CHAINS], gmat)
            dzs = []
            for (b, h), lb, g, sfx_g in zip(CHAINS, lbs, gs, suffixes_g):
                before = totals[b][h] - sfx_g
                dzs.append((g - jnp.exp(lb) * (g + before)).astype(BF16))
            new = []
            for b in range(Q_GROUP):
                pair = slice(2 * b, 2 * b + 2)
                dq_acc = states[b][2] + jnp.dot(jnp.concatenate(dzs[pair], axis=1), _head_rows(kcs[b], lane_lo_keys),
                                                preferred_element_type=F32)
                dk_s[pl.ds(starts[b], KEY_CHUNK), :] += lax.dot_general(
                    jnp.concatenate(dzs[pair], axis=0), q_rows[b], (_TN, ((), ())), preferred_element_type=F32)
                dv_s[pl.ds(starts[b], KEY_CHUNK), :] += lax.dot_general(
                    jnp.concatenate(a_bf[pair], axis=0), do_rows[b], (_TN, ((), ())), preferred_element_type=F32)
                new.append(((carries[2 * b], carries[2 * b + 1]), (carries_g[2 * b], carries_g[2 * b + 1]), dq_acc))
            return new

        zero = jnp.zeros((Q_BLOCK, LANES), F32)
        states = _sweep(i * Q_GROUP, step, [((zero, zero), (zero, zero), zero)] * Q_GROUP, lambda st: st[0])
        dq_ref[...] = (jnp.concatenate([st[2] for st in states], axis=0) * scale).astype(BF16)

        @pl.when(i == n_steps - 1)
        def _():
            dk_ref[...] = (dk_s[...] * scale).astype(BF16)
            dv_ref[...] = dv_s[...].astype(BF16)

        @pl.when((p == npair - 1) & (i == n_steps - 1))
        def _():
            for cp in copies:
                cp.wait()

    whole = pl.BlockSpec((t_len, LANES), lambda p, i: (0, p))
    hbm = pl.BlockSpec(memory_space=pl.ANY)
    outs = pl.pallas_call(
        body, name="attn_bwd", grid=(npair, n_steps),
        in_specs=[pl.BlockSpec((rows_per_step, LANES), lambda p, i: (i, p)),
                  pl.BlockSpec((t_len, LANES), lambda p, i: (0, npair + p)),
                  pl.BlockSpec((t_len, LANES), lambda p, i: (0, 2 * npair + p)),
                  pl.BlockSpec((rows_per_step, LANES), lambda p, i: (i, p)),
                  pl.BlockSpec((rows_per_step, LANES), lambda p, i: (i, npair + p)),
                  pl.BlockSpec((1, LANES), lambda p, i: (0, p)),
                  pl.BlockSpec((2 * LANES, 2 * LANES), lambda p, i: (0, 0)),
                  pl.BlockSpec((2 * LANES, 2 * LANES), lambda p, i: (0, 0))] + [hbm] * n_ex,
        out_specs=[pl.BlockSpec((rows_per_step, LANES), lambda p, i: (i, p)), whole, whole,
                   pl.BlockSpec((8, LANES), lambda p, i: (0, p))] + [hbm] * n_ex,
        out_shape=[jax.ShapeDtypeStruct((t_len, width), BF16)] * 3 + [jax.ShapeDtypeStruct((8, width), F32)]
        + _chip_landing(exchange),
        scratch_shapes=[pltpu.VMEM((t_len, LANES), F32), pltpu.VMEM((t_len, LANES), F32)] + _chip_scratch(n_ex),
        compiler_params=_params(("arbitrary", "arbitrary")),
    )(qkv, qkv, qkv, o, dymix, g_attn, _cumsum_matrix(False), _cumsum_matrix(True), *exchange)
    return outs[0], outs[1], outs[2], outs[3], outs[4:]


def kernel(x, w_in, conv_w, g_conv, g_attn, w_out, ln1_g, ln1_b, w_up, w_down, ln2_g, ln2_b, loss_target, m_w_in, m_conv_w, m_g_conv, m_g_attn, m_w_out, m_ln1_g, m_ln1_b, m_w_up, m_w_down, m_ln2_g, m_ln2_b, v_w_in, v_conv_w, v_g_conv, v_g_attn, v_w_out, v_ln1_g, v_ln1_b, v_w_up, v_w_down, v_ln2_g, v_ln2_b):
    x2, target = x[0], loss_target[0]
    t_len, d_model = x2.shape
    width = g_conv.shape[1]
    in_shard = w_in.shape[2]
    up_shard = w_up.shape[2]
    d_ff = up_shard * N_DEV
    cw = conv_w.shape[2]
    xi, yi, ci = _place()
    dev = 4 * xi + 2 * yi + ci
    tm, sub = 512, 2
    tr = tm // sub
    mm = functools.partial(_matmul, sub=sub)

    win_g, convw_g, wout_bf, wup_bf, wdown_bf = _all_gather(
        "gather_w_in", [w_in[0], jnp.pad(conv_w[0], ((0, 5), (0, LANES - cw)))], [BF16, F32],
        cast_only=[w_out[0], w_up[0], w_down[0]])
    conv_full = jnp.transpose(convw_g[:, :3, :cw], (1, 0, 2)).reshape(3, width)

    n_row_tiles = t_len // tm
    n_sub_tiles = n_row_tiles * sub
    rows = lambda cols: pl.BlockSpec((tm, cols), lambda i: (i, 0))
    resident = lambda arr: pl.BlockSpec(arr.shape, lambda i: (0,) * arr.ndim)
    vec = pl.BlockSpec((1, d_model), lambda i: (0, 0))
    acc8 = pl.BlockSpec((8, d_model), lambda i: (0, 0))
    tokens = lambda cols, dtype: jax.ShapeDtypeStruct((t_len, cols), dtype)

    def proj_in_epilogue(acc, extra, outs, i):
        x_ref, w_ref, gain_ref, tail_ref = extra
        gates_ref, qkv_ref, xbf_ref, ymix_ref = outs
        gates_ref[...] = acc[:, :3 * width]
        qkv_ref[...] = acc[:, 3 * width:].astype(BF16)
        xbf_ref[...] = x_ref[...].astype(BF16)

        @pl.when(i == 0)
        def _():
            tail_ref[...] = jnp.zeros_like(tail_ref)

        u = acc[:, width:2 * width] * acc[:, 2 * width:3 * width]
        ext = jnp.concatenate([tail_ref[...], u], axis=0)
        tail_ref[...] = u[tr - 8:]
        w = w_ref[...]
        y = acc[:, :width] * (w[2:3] * u + w[1:2] * pltpu.roll(ext, 1, 0)[8:] + w[0:1] * pltpu.roll(ext, 2, 0)[8:])
        lane_lo = lax.broadcasted_iota(jnp.int32, (tr, LANES), 1) < HEAD_DIM
        gain = gain_ref[...]
        for blk in range(width // LANES):
            lanes = slice(blk * LANES, (blk + 1) * LANES)
            y_blk = y[:, lanes]
            rs = lax.rsqrt(_group_sums(y_blk * y_blk, lane_lo) * (1.0 / HEAD_DIM) + RMS_EPS)
            ymix_ref[:, lanes] = (y_blk * rs * gain[:, lanes]).astype(BF16)

    half_vec = pl.BlockSpec((1, width), lambda i: (0, 0))
    gates, qkv, x_bf, ymix, wdown_g = mm(
        "proj_in", x2, win_g, rows(d_model), resident(win_g), _shards_wide, n_row_tiles,
        [(tokens(3 * width, F32), rows(3 * width)), (tokens(3 * width, BF16), rows(3 * width)),
         (tokens(d_model, BF16), rows(d_model)), (tokens(2 * width, BF16), rows(width))],
        proj_in_epilogue,
        extras=[(x2, rows(d_model)), (conv_full, pl.BlockSpec((3, width), lambda i: (0, 0))), (g_conv, half_vec)],
        gather=[wdown_bf], carry=[pltpu.VMEM((8, width), F32)])

    o_attn, ymix, (wout_g, wup_g) = _attn_fwd(qkv, g_attn, ymix, width, [wout_bf, wup_bf])
    wout_full = wout_g.reshape(d_model, d_model)
    wdown_full = wdown_g.reshape(d_ff, d_model)

    def out_ln1_epilogue(acc, extra, outs, i):
        x_ref, g_ref, b_ref = extra
        r1 = ALPHA * x_ref[...] + acc
        x1, _, _ = _ln_fwd(r1, g_ref[...], b_ref[...])
        outs[0][...] = r1
        outs[1][...] = x1
        outs[2][...] = x1.astype(BF16)

    r1, x1, x1_bf = mm(
        "proj_out_ln1", ymix, wout_full, rows(d_model), resident(wout_full), _whole(_NN), n_row_tiles,
        [(tokens(d_model, F32), rows(d_model)), (tokens(d_model, F32), rows(d_model)),
         (tokens(d_model, BF16), rows(d_model))],
        out_ln1_epilogue, extras=[(x2, rows(d_model)), (ln1_g, vec), (ln1_b, vec)])

    def up_epilogue(acc, extra, outs, i):
        outs[0][...] = acc
        relu = jnp.maximum(acc, 0.0)
        outs[1][...] = (relu * relu).astype(BF16)

    up, hid = mm(
        "ffn_up", x1_bf, wup_g, rows(d_model), resident(wup_g), _shards_wide, n_row_tiles,
        [(tokens(d_ff, F32), rows(d_ff)), (tokens(d_ff, BF16), rows(d_ff))], up_epilogue)

    def down_ln2_epilogue(acc, extra, outs, i):
        x1_ref, t_ref, g_ref, b_ref = extra
        dr2_ref, dr2_bf_ref, loss_ref, gg_ref, gb_ref = outs
        gain = g_ref[...]
        r2 = ALPHA * x1_ref[...] + acc
        y, xhat, rstd = _ln_fwd(r2, gain, b_ref[...])
        diff = y - t_ref[...]
        d_y = diff * (1.0 / d_model)
        dr2 = _ln_bwd(d_y, xhat, rstd, gain)
        dr2_ref[...] = dr2
        dr2_bf_ref[...] = dr2.astype(BF16)
        _accumulate(gg_ref, i, n_sub_tiles, _rows8(d_y * xhat))
        _accumulate(gb_ref, i, n_sub_tiles, _rows8(d_y))
        _accumulate(loss_ref, i, n_sub_tiles, _rows8(diff * diff), all_lanes=True)

    small_acc = (jax.ShapeDtypeStruct((8, d_model), F32), acc8)
    dr2, dr2_bf, loss_cols, g_ln2_g, g_ln2_b = mm(
        "ffn_down_ln2_loss", hid, wdown_full, rows(d_ff), resident(wdown_full), _whole(_NN), n_row_tiles,
        [(tokens(d_model, F32), rows(d_model)), (tokens(d_model, BF16), rows(d_model)),
         small_acc, small_acc, small_acc],
        down_ln2_epilogue, extras=[(x1, rows(d_model)), (target, rows(d_model)), (ln2_g, vec), (ln2_b, vec)])

    def dup_epilogue(acc, extra, outs, i):
        outs[0][...] = (acc * (2.0 * jnp.maximum(extra[0][...], 0.0))).astype(BF16)

    (dup,) = mm(
        "ffn_dhid", dr2_bf, wdown_full, rows(d_model), resident(wdown_full), _whole(_NT), n_row_tiles,
        [(tokens(d_ff, BF16), rows(d_ff))], dup_epilogue, extras=[(up, rows(d_ff))])

    gw_down, pair_down = _weight_grad("grad_w_down", hid, dr2_bf, (d_ff // N_DEV, d_model), True)
    gw_up, pair_up = _weight_grad("grad_w_up", x1_bf, dup, (d_model, up_shard), False)

    def dx1_ln1_epilogue(acc, extra, outs, i):
        dr2_ref, r1_ref, g_ref, b_ref = extra
        dr1_ref, dr1_bf_ref, gg_ref, gb_ref = outs
        gain = g_ref[...]
        dx1 = ALPHA * dr2_ref[...] + acc
        _, xhat, rstd = _ln_fwd(r1_ref[...], gain, b_ref[...])
        dr1 = _ln_bwd(dx1, xhat, rstd, gain)
        dr1_ref[...] = dr1
        dr1_bf_ref[...] = dr1.astype(BF16)
        _accumulate(gg_ref, i, n_sub_tiles, _rows8(dx1 * xhat))
        _accumulate(gb_ref, i, n_sub_tiles, _rows8(dx1))

    dr1, dr1_bf, g_ln1_g, g_ln1_b = mm(
        "ffn_dx1_ln1", dup, wup_g, rows(d_ff), resident(wup_g), _shards_deep, n_row_tiles,
        [(tokens(d_model, F32), rows(d_model)), (tokens(d_model, BF16), rows(d_model)), small_acc, small_acc],
        dx1_ln1_epilogue, extras=[(dr2, rows(d_model)), (r1, rows(d_model)), (ln1_g, vec), (ln1_b, vec)])

    gw_out, pair_out = _weight_grad("grad_w_out", ymix, dr1_bf, (d_model // N_DEV, d_model), True)

    c_idx = jnp.reshape(ci, (1,)).astype(jnp.int32)
    q_me = jnp.reshape(2 * xi + yi, (1,)).astype(jnp.int32)

    def pair_sums(names, grads, landed):
        sums = [_pair_sum("pair_sum_" + nm, g, land, c_idx, q_me) for nm, g, land in zip(names, grads, landed)]
        return [s[0] for s in sums], [s[1] for s in sums]

    wire_ffn, own_ffn = pair_sums(["w_out", "w_up", "w_down"], [gw_out, gw_up, gw_down],
                                  [pair_out, pair_up, pair_down])

    def plain_epilogue(acc, extra, outs, i):
        outs[0][...] = acc

    (dymix,) = mm(
        "proj_out_bwd", dr1_bf, wout_full, rows(d_model), resident(wout_full), _whole(_NT), n_row_tiles,
        [(tokens(d_model, F32), rows(d_model))], plain_epilogue)

    d_bg, d_cg, d_h, g_gconv, g_convw = _conv_bwd(gates, dymix, conv_full, g_conv, width)
    d_q, d_k, d_v, g_gattn, landed_ffn = _attn_bwd(qkv, o_attn, dymix, g_attn, width, wire_ffn)
    dproj = jnp.concatenate([d_bg, d_cg, d_h, d_q, d_k, d_v], axis=1)

    gw_in, pair_in = _weight_grad("grad_w_in", x_bf, dproj, (d_model, in_shard), False)
    wire_in, own_in = pair_sums(["w_in"], [gw_in], [pair_in])

    def dx_epilogue(acc, extra, outs, i):
        outs[0][...] = ALPHA * extra[0][...] + acc

    spare = jnp.zeros((3, d_model - width), F32)

    def pack(l1g, l1b, l2g, l2b, gc, ga, taps, tail=spare):
        return jnp.concatenate([l1g, l1b, l2g, l2b, jnp.concatenate([gc, ga], axis=1),
                                jnp.concatenate([taps, tail], axis=1)], axis=0)

    local = pack(g_ln1_g[:1], g_ln1_b[:1], g_ln2_g[:1], g_ln2_b[:1], g_gconv[:1], g_gattn[:1], g_convw[:3],
                 spare + (0.5 / d_model) * loss_cols[0, 0])

    grad_x, landed_in, parts = mm(
        "proj_in_bwd", dproj, win_g, rows(6 * width), resident(win_g), _shards_deep, n_row_tiles,
        [(tokens(d_model, F32), rows(d_model))], dx_epilogue, extras=[(dr1, rows(d_model))],
        exchange=wire_in, gather=[local])

    names = ["w_in", "w_out", "w_up", "w_down"]
    big = {}
    for nm, ow, land, w, m, v in zip(names, own_in + own_ffn, [landed_in] + list(landed_ffn),
                                     (w_in, w_out, w_up, w_down),
                                     (m_w_in, m_w_out, m_w_up, m_w_down), (v_w_in, v_w_out, v_w_up, v_w_down)):
        big[nm] = [r[None] for r in _shard_update("update_" + nm, ow, land, w[0], m[0], v[0])]

    def spread(a):
        return lax.dynamic_update_slice(jnp.zeros((3, width), F32), a, (0, dev * cw))

    w_small = pack(ln1_g, ln1_b, ln2_g, ln2_b, g_conv, g_attn, spread(conv_w[0]))
    m_small = pack(m_ln1_g, m_ln1_b, m_ln2_g, m_ln2_b, m_g_conv, m_g_attn, spread(m_conv_w[0]))
    v_small = pack(v_ln1_g, v_ln1_b, v_ln2_g, v_ln2_b, v_g_conv, v_g_attn, spread(v_conv_w[0]))
    small = _small_update(parts, w_small, m_small, v_small)

    def unpack(a):
        taps = lax.dynamic_slice(a[5:8, :width], (0, dev * cw), (3, cw))[None]
        return {"ln1_g": a[0:1], "ln1_b": a[1:2], "ln2_g": a[2:3], "ln2_b": a[3:4],
                "g_conv": a[4:5, :width], "g_attn": a[4:5, width:], "conv_w": taps}

    loss = small[0][7, d_model - 1]
    small = [unpack(a) for a in small]

    order = ["w_in", "conv_w", "g_conv", "g_attn", "w_out", "ln1_g", "ln1_b", "w_up", "w_down", "ln2_g", "ln2_b"]
    result = [loss, grad_x[None]]
    for kind in range(4):
        for nm in order:
            result.append(big[nm][kind] if nm in big else small[kind][nm])
    return tuple(result)
```

```python
import functools

import jax
import jax.numpy as jnp
from jax import lax
from jax.experimental import pallas as pl
from jax.experimental.pallas import tpu as pltpu

F32 = jnp.float32
BF16 = jnp.bfloat16
MESH = pl.DeviceIdType.MESH

N_DEV = 8
HEAD_DIM = 64
LANES = 128
Q_BLOCK = 128
ALPHA = 2.0 ** 0.25
LN_EPS = 1e-5
RMS_EPS = 1e-6
ADAM_LR, ADAM_B1, ADAM_B2, ADAM_EPS, ADAM_WD, ADAM_STEP = 0.001, 0.9, 0.999, 1e-08, 0.01, 10
VMEM_LIMIT_BYTES = 48 * 1024 * 1024

_NT = ((1,), (1,))
_NN = ((1,), (0,))
_TN = ((0,), (0,))


def _params(sem=None):
    return pltpu.CompilerParams(dimension_semantics=sem, vmem_limit_bytes=VMEM_LIMIT_BYTES)


def _dot(a, b, contract):
    return lax.dot_general(a.astype(BF16), b.astype(BF16), (contract, ((), ())), preferred_element_type=F32)


def _whole(contract):
    return lambda a_ref, b_ref: _dot(a_ref[...], b_ref[...], contract)


def _shards_wide(a_ref, b_ref):
    a = a_ref[...].astype(BF16)
    return jnp.concatenate([_dot(a, b_ref[s], _NN) for s in range(b_ref.shape[0])], axis=1)


def _shards_deep(a_ref, b_ref):
    n = b_ref.shape[2]
    acc = _dot(a_ref[:, 0:n], b_ref[0], _NT)
    for s in range(1, b_ref.shape[0]):
        acc = acc + _dot(a_ref[:, s * n:(s + 1) * n], b_ref[s], _NT)
    return acc


def _matmul(name, a, b, a_spec, b_spec, product, n_steps, outs, epilogue, extras=(), exchange=(), gather=(),
            carry=(), sub=1):
    behind = list(exchange) + list(gather)
    n_extra, n_out, n_ex, n_b, n_carry = len(extras), len(outs), len(exchange), len(behind), len(carry)

    def body(a_ref, b_ref, *rest):
        i = pl.program_id(0)
        extra_refs, rest = rest[:n_extra], rest[n_extra:]
        src_refs, rest = rest[:n_b], rest[n_b:]
        out_refs, rest = rest[:n_out], rest[n_out:]
        land_refs, rest = rest[:n_b], rest[n_b:]
        extra_refs, sems = extra_refs + rest[:n_carry], rest[n_carry:]
        copies = _chip_copies(src_refs[:n_ex], land_refs[:n_ex], *sems[:2]) if exchange else []
        gather_sems = sems[2:] if exchange else sems
        plan = _gather_plan(src_refs[n_ex:], land_refs[n_ex:], *gather_sems) if gather else None

        def start():
            for cp in copies:
                cp.start()
            if plan:
                plan[0]()

        def finish():
            if plan:
                plan[1]()
                plan[2]()
            for cp in copies:
                cp.wait()

        if behind:
            pl.when(i == 0)(start)
        tile_rows = a_ref.shape[0]
        sub_rows = tile_rows // sub

        def view(ref, k):
            return ref.at[pl.ds(k * sub_rows, sub_rows)] if sub > 1 and ref.shape[0] == tile_rows else ref

        accs = [product(view(a_ref, k), b_ref) for k in range(sub)]
        for k in range(sub):
            epilogue(accs[k], tuple(view(r, k) for r in extra_refs), tuple(view(r, k) for r in out_refs), i * sub + k)
        if behind:
            pl.when(i == n_steps - 1)(finish)

    hbm = pl.BlockSpec(memory_space=pl.ANY)
    landing = _chip_landing(exchange) + [jax.ShapeDtypeStruct((N_DEV,) + s.shape, s.dtype) for s in gather]
    scratch = (list(carry) + (_chip_scratch(len(exchange)) if exchange else [])
               + (_gather_scratch(len(gather)) if gather else []))
    return pl.pallas_call(
        body, name=name, grid=(n_steps,),
        in_specs=[a_spec, b_spec] + [s for _, s in extras] + [hbm] * n_b,
        out_specs=[s for _, s in outs] + [hbm] * n_b,
        out_shape=[o for o, _ in outs] + landing,
        scratch_shapes=scratch,
        compiler_params=_params(("arbitrary",)),
    )(a, b, *[e for e, _ in extras], *behind)


def _ln_fwd(r, g, b):
    mu = jnp.mean(r, axis=-1, keepdims=True)
    xc = r - mu
    rstd = lax.rsqrt(jnp.mean(xc * xc, axis=-1, keepdims=True) + LN_EPS)
    xhat = xc * rstd
    return xhat * g + b, xhat, rstd


def _ln_bwd(dy, xhat, rstd, g):
    dxh = dy * g
    m1 = jnp.mean(dxh, axis=-1, keepdims=True)
    m2 = jnp.mean(dxh * xhat, axis=-1, keepdims=True)
    return rstd * (dxh - m1 - xhat * m2)


def _rows8(v):
    n, c = v.shape
    return jnp.sum(v.reshape(n // 8, 8, c), axis=0)


def _accumulate(ref, i, n_steps, part8, all_lanes=False):
    @pl.when(i == 0)
    def _():
        ref[...] = part8

    @pl.when(i > 0)
    def _():
        ref[...] += part8

    @pl.when(i == n_steps - 1)
    def _():
        tot = jnp.sum(ref[...], axis=0, keepdims=True)
        if all_lanes:
            tot = jnp.sum(tot, axis=1, keepdims=True)
        ref[...] = jnp.broadcast_to(tot, ref.shape)


def _group_sums(v, lane_lo):
    s0 = jnp.sum(jnp.where(lane_lo, v, 0.0), axis=-1, keepdims=True)
    s1 = jnp.sum(jnp.where(lane_lo, 0.0, v), axis=-1, keepdims=True)
    return jnp.where(lane_lo, s0, s1)


def _place():
    return lax.axis_index("x"), lax.axis_index("y"), lax.axis_index("c")


def _gather_plan(src_refs, out_refs, send_sems, recv_sems, local_sems):
    n = len(src_refs)
    x, y, c = _place()
    me, sibling = (x, y, c), (x, y, 1 - c)
    chips = [(1 - x, y), (x, 1 - y), (1 - x, 1 - y)]

    def copy(a, k, block, to, src=None):
        slot = out_refs[a].at[4 * block[0] + 2 * block[1] + block[2]]
        return pltpu.make_async_remote_copy(
            src_ref=slot if src is None else src, dst_ref=slot,
            send_sem=send_sems.at[a, k], recv_sem=recv_sems.at[a, k],
            device_id=to, device_id_type=MESH)

    mine = [pltpu.make_async_copy(src_refs[a], out_refs[a].at[4 * x + 2 * y + c], local_sems.at[a]) for a in range(n)]
    first = [[copy(a, 0, me, sibling, src=src_refs[a])]
             + [copy(a, 1 + j, me, (*chip, c), src=src_refs[a]) for j, chip in enumerate(chips)] for a in range(n)]
    passed = [[copy(a, 4 + j, (*chip, c), sibling) for j, chip in enumerate(chips)] for a in range(n)]

    def start():
        for a in range(n):
            mine[a].start()
            for cp in first[a]:
                cp.start()

    def forward():
        for j, chip in enumerate(chips):
            for a in range(n):
                copy(a, 1 + j, (*chip, c), me).wait_recv()
                passed[a][j].start()

    def finish():
        for a in range(n):
            copy(a, 0, sibling, me).wait_recv()
            for j, chip in enumerate(chips):
                copy(a, 4 + j, (*chip, 1 - c), me).wait_recv()
        for a in range(n):
            for cp in first[a] + passed[a]:
                cp.wait_send()
            mine[a].wait()

    return start, forward, finish


def _gather_scratch(n):
    return [pltpu.SemaphoreType.DMA((n, 7)), pltpu.SemaphoreType.DMA((n, 7)), pltpu.SemaphoreType.DMA((n,))]


def _all_gather(name, shards, out_dtypes, cast_only=()):
    n, m = len(shards), len(cast_only)

    def body(*refs):
        in_refs, cast_in = refs[:n], refs[n:n + m]
        out_refs, cast_out = refs[n + m:2 * n + m], refs[2 * n + m:2 * (n + m)]
        stage = refs[2 * (n + m):3 * n + 2 * m]
        for a in range(n):
            stage[a][...] = in_refs[a][...].astype(out_dtypes[a])
        start, forward, finish = _gather_plan(stage, out_refs, *refs[3 * n + 2 * m:])
        start()
        for a in range(m):
            cast_out[a][...] = cast_in[a][...].astype(BF16)
        forward()
        finish()

    return pl.pallas_call(
        body, name=name,
        in_specs=[pl.BlockSpec(memory_space=pltpu.VMEM)] * (n + m),
        out_specs=[pl.BlockSpec(memory_space=pl.ANY)] * n + [pl.BlockSpec(memory_space=pltpu.VMEM)] * m,
        out_shape=[jax.ShapeDtypeStruct((N_DEV,) + s.shape, dt) for s, dt in zip(shards, out_dtypes)]
        + [jax.ShapeDtypeStruct(s.shape, BF16) for s in cast_only],
        scratch_shapes=[pltpu.VMEM(s.shape, dt) for s, dt in zip(shards, out_dtypes)] + _gather_scratch(n),
        compiler_params=pltpu.CompilerParams(vmem_limit_bytes=VMEM_LIMIT_BYTES),
    )(*shards, *cast_only)


def _weight_grad(name, acts, cots, shard_shape, acts_by_shard):
    t_len = acts.shape[0]
    sr, sc = shard_shape
    per = max(1, 256 // sr) if acts_by_shard else 1
    assert per in (1, 2)
    n_steps = N_DEV // per

    def body(a_ref, b_ref, g_ref, land_ref, stage, send_sems, recv_sems):
        i = pl.program_id(0)
        x, y, c = _place()

        def copy(q):
            return pltpu.make_async_remote_copy(
                src_ref=stage.at[q % 2], dst_ref=land_ref.at[q], send_sem=send_sems.at[q], recv_sem=recv_sems.at[q],
                device_id=(x, y, 1 - c), device_id_type=MESH)

        acc = _dot(a_ref[...], b_ref[...], _TN).reshape(per, sr, sc)
        g_ref[...] = acc
        if per == 2:
            payload, due, q = jnp.where(c == 0, acc[1], acc[0]), i >= 0, i
        else:
            payload, due, q = acc[0], (i % 2) == (1 - c), i // 2

        @pl.when(due)
        def _():
            @pl.when(q >= 2)
            def _():
                copy(q - 2).wait_send()

            stage[q % 2] = payload.astype(BF16)
            copy(q).start()

        @pl.when(i == n_steps - 1)
        def _():
            copy(2).wait_send()
            copy(3).wait_send()
            for k in range(4):
                copy(k).wait_recv()

    tall = lambda cols: pl.BlockSpec((t_len, cols), lambda i: (0, i))
    whole = lambda arr: pl.BlockSpec(arr.shape, lambda i: (0, 0))
    a_spec, b_spec = (tall(per * sr), whole(cots)) if acts_by_shard else (whole(acts), tall(sc))
    return pl.pallas_call(
        body, name=name, grid=(n_steps,),
        in_specs=[a_spec, b_spec],
        out_specs=[pl.BlockSpec((per, sr, sc), lambda i: (i, 0, 0)), pl.BlockSpec(memory_space=pl.ANY)],
        out_shape=[jax.ShapeDtypeStruct((N_DEV, sr, sc), F32), jax.ShapeDtypeStruct((4, sr, sc), BF16)],
        scratch_shapes=[pltpu.VMEM((2, sr, sc), BF16), pltpu.SemaphoreType.DMA((4,)), pltpu.SemaphoreType.DMA((4,))],
        compiler_params=_params(("arbitrary",)),
    )(acts, cots)


def _chip_copies(p_refs, land_refs, send_sems, recv_sems):
    x, y, c = _place()
    chips = [(1 - x, y), (x, 1 - y), (1 - x, 1 - y)]
    return [pltpu.make_async_remote_copy(
        src_ref=p_refs[a].at[2 * px + py], dst_ref=land_refs[a].at[k],
        send_sem=send_sems.at[a, k], recv_sem=recv_sems.at[a, k],
        device_id=(px, py, c), device_id_type=MESH)
        for a in range(len(p_refs)) for k, (px, py) in enumerate(chips)]


def _chip_scratch(n):
    return [pltpu.SemaphoreType.DMA((n, 3)), pltpu.SemaphoreType.DMA((n, 3))]


def _chip_landing(pair_sums):
    return [jax.ShapeDtypeStruct((3,) + p.shape[1:], p.dtype) for p in pair_sums]


def _pair_sum(name, g_f32, landed, c_idx, q_me):
    _, rows, cols = g_f32.shape
    tr = min(rows, 1024)

    def body(c_ref, q_ref, g_ref, l_ref, wire_ref, own_ref):
        q = pl.program_id(1)
        s = g_ref[...] + l_ref[...].astype(F32)
        wire_ref[...] = s.astype(BF16)

        @pl.when(q == q_ref[0])
        def _():
            own_ref[...] = s

    return pl.pallas_call(
        body, name=name,
        grid_spec=pltpu.PrefetchScalarGridSpec(
            num_scalar_prefetch=2, grid=(rows // tr, 4),
            in_specs=[pl.BlockSpec((None, tr, cols), lambda i, q, c_ref, q_ref: (2 * q + c_ref[0], i, 0)),
                      pl.BlockSpec((None, tr, cols), lambda i, q, c_ref, q_ref: (q, i, 0))],
            out_specs=[pl.BlockSpec((None, tr, cols), lambda i, q, c_ref, q_ref: (q, i, 0)),
                       pl.BlockSpec((tr, cols), lambda i, q, c_ref, q_ref: (i, 0))]),
        out_shape=[jax.ShapeDtypeStruct((4, rows, cols), BF16), jax.ShapeDtypeStruct((rows, cols), F32)],
        compiler_params=_params(("parallel", "arbitrary")),
    )(c_idx, q_me, g_f32, landed)


def _adamw(w, g, m, v):
    m = ADAM_B1 * m + (1.0 - ADAM_B1) * g
    v = ADAM_B2 * v + (1.0 - ADAM_B2) * (g * g)
    m_hat = m / (1.0 - ADAM_B1 ** ADAM_STEP)
    v_hat = v / (1.0 - ADAM_B2 ** ADAM_STEP)
    delta = -ADAM_LR * (m_hat / (jnp.sqrt(v_hat) + ADAM_EPS) + ADAM_WD * w)
    return delta, m, v


def _shard_update(name, own, landed, w, m, v):
    rows, cols = own.shape
    tr = min(rows, 256)

    def body(own_ref, l_ref, w_ref, m_ref, v_ref, g_out, d_out, m_out, v_out):
        g = own_ref[...] + l_ref[0].astype(F32) + l_ref[1].astype(F32) + l_ref[2].astype(F32)
        d, m_new, v_new = _adamw(w_ref[...], g, m_ref[...], v_ref[...])
        g_out[...] = g
        d_out[...] = d
        m_out[...] = m_new
        v_out[...] = v_new

    tile = pl.BlockSpec((tr, cols), lambda i: (i, 0))
    return pl.pallas_call(
        body, name=name, grid=(rows // tr,),
        in_specs=[tile, pl.BlockSpec((3, tr, cols), lambda i: (0, i, 0)), tile, tile, tile],
        out_specs=[tile] * 4,
        out_shape=[jax.ShapeDtypeStruct((rows, cols), F32)] * 4,
        compiler_params=_params(("parallel",)),
    )(own, landed, w, m, v)


def _small_update(parts, w, m, v):
    def body(p_ref, w_ref, m_ref, v_ref, g_out, d_out, m_out, v_out):
        g = p_ref[0]
        for d in range(1, N_DEV):
            g = g + p_ref[d]
        dl, m_new, v_new = _adamw(w_ref[...], g, m_ref[...], v_ref[...])
        g_out[...] = g
        d_out[...] = dl
        m_out[...] = m_new
        v_out[...] = v_new

    return pl.pallas_call(
        body, name="small_update",
        out_shape=[jax.ShapeDtypeStruct(w.shape, F32)] * 4,
        compiler_params=pltpu.CompilerParams(vmem_limit_bytes=VMEM_LIMIT_BYTES),
    )(parts, w, m, v)


CONV_CHUNK = 512


def _conv_taps(u_s, lo, chunk):
    ext = u_s[pl.ds(lo, chunk + 8), :]
    return ext[8:], pltpu.roll(ext, 1, 0)[8:], pltpu.roll(ext, 2, 0)[8:]


def _conv_bwd(gates, dymix, conv_w, g_conv, width):
    t_len = gates.shape[0]
    nblk = width // LANES
    n_chunks = t_len // CONV_CHUNK

    def body(bg_ref, cg_ref, h_ref, dy_ref, w_ref, g_ref, dbg_ref, dcg_ref, dh_ref, gg_ref, gw_ref, u_s, dc_s):
        lane_lo = lax.broadcasted_iota(jnp.int32, (CONV_CHUNK, LANES), 1) < HEAD_DIM
        u_s[0:8, :] = jnp.zeros((8, LANES), F32)
        dc_s[t_len:t_len + 8, :] = jnp.zeros((8, LANES), F32)

        def fill(r, _):
            lo = pl.multiple_of(r * CONV_CHUNK, CONV_CHUNK)
            u_s[pl.ds(pl.multiple_of(lo + 8, 8), CONV_CHUNK), :] = cg_ref[pl.ds(lo, CONV_CHUNK), :] * h_ref[pl.ds(lo, CONV_CHUNK), :]
            return 0

        lax.fori_loop(0, n_chunks, fill, 0)
        w = w_ref[...]
        gain = g_ref[...]

        def step(r, acc):
            gg, gw0, gw1, gw2 = acc
            lo = pl.multiple_of(r * CONV_CHUNK, CONV_CHUNK)
            u0, u1, u2 = _conv_taps(u_s, lo, CONV_CHUNK)
            bg = bg_ref[pl.ds(lo, CONV_CHUNK), :]
            conv = w[2:3] * u0 + w[1:2] * u1 + w[0:1] * u2
            y = bg * conv
            rs = lax.rsqrt(_group_sums(y * y, lane_lo) * (1.0 / HEAD_DIM) + RMS_EPS)
            yn = y * rs
            dyn_raw = dy_ref[pl.ds(lo, CONV_CHUNK), :]
            dyn = dyn_raw * gain
            d_y = rs * (dyn - yn * (_group_sums(dyn * yn, lane_lo) * (1.0 / HEAD_DIM)))
            dbg_ref[pl.ds(lo, CONV_CHUNK), :] = (d_y * conv).astype(BF16)
            dconv = d_y * bg
            dc_s[pl.ds(lo, CONV_CHUNK), :] = dconv
            return (gg + _rows8(dyn_raw * yn), gw0 + _rows8(dconv * u2), gw1 + _rows8(dconv * u1),
                    gw2 + _rows8(dconv * u0))

        zero = jnp.zeros((8, LANES), F32)
        gg, gw0, gw1, gw2 = lax.fori_loop(0, n_chunks, step, (zero, zero, zero, zero))
        gg_ref[...] = jnp.broadcast_to(jnp.sum(gg, axis=0, keepdims=True), (8, LANES))
        row = lax.broadcasted_iota(jnp.int32, (8, LANES), 0)
        tot = lambda v: jnp.broadcast_to(jnp.sum(v, axis=0, keepdims=True), (8, LANES))
        gw_ref[...] = jnp.where(row == 0, tot(gw0), jnp.where(row == 1, tot(gw1), jnp.where(row == 2, tot(gw2), 0.0)))

        def back(r, _):
            lo = pl.multiple_of(r * CONV_CHUNK, CONV_CHUNK)
            ext = dc_s[pl.ds(lo, CONV_CHUNK + 8), :]
            d0 = ext[:CONV_CHUNK]
            d1 = pltpu.roll(ext, CONV_CHUNK + 7, 0)[:CONV_CHUNK]
            d2 = pltpu.roll(ext, CONV_CHUNK + 6, 0)[:CONV_CHUNK]
            du = w[2:3] * d0 + w[1:2] * d1 + w[0:1] * d2
            dcg_ref[pl.ds(lo, CONV_CHUNK), :] = (du * h_ref[pl.ds(lo, CONV_CHUNK), :]).astype(BF16)
            dh_ref[pl.ds(lo, CONV_CHUNK), :] = (du * cg_ref[pl.ds(lo, CONV_CHUNK), :]).astype(BF16)
            return 0

        lax.fori_loop(0, n_chunks, back, 0)

    col = lambda off: pl.BlockSpec((t_len, LANES), lambda p: (0, off + p))
    small = pl.BlockSpec((8, LANES), lambda p: (0, p))
    return pl.pallas_call(
        body, name="conv_bwd", grid=(nblk,),
        in_specs=[col(0), col(nblk), col(2 * nblk), col(0),
                  pl.BlockSpec((3, LANES), lambda p: (0, p)), pl.BlockSpec((1, LANES), lambda p: (0, p))],
        out_specs=[col(0), col(0), col(0), small, small],
        out_shape=[jax.ShapeDtypeStruct((t_len, width), BF16)] * 3 + [jax.ShapeDtypeStruct((8, width), F32)] * 2,
        scratch_shapes=[pltpu.VMEM((t_len + 8, LANES), F32), pltpu.VMEM((t_len + 8, LANES), F32)],
        compiler_params=_params(("parallel",)),
    )(gates, gates, gates, dymix, conv_w, g_conv)


def _cumsum_matrix(inclusive):
    j = lax.broadcasted_iota(jnp.int32, (2 * LANES, 2 * LANES), 0) % LANES
    s = lax.broadcasted_iota(jnp.int32, (2 * LANES, 2 * LANES), 1)
    keep = (s >= LANES) | ((j >= s) if inclusive else (j > s))
    return keep.astype(BF16)


KEY_CHUNK = 384
SUB_TILES = KEY_CHUNK // Q_BLOCK
Q_GROUP = 4
DEAD_ABOVE = 105.0
MASKED_Z = -1e30


def _scores_phase(qms, kcs, masks):
    zs = [lax.dot_general(q, k, (_NT, ((), ())), preferred_element_type=F32) for q, k in zip(qms, kcs)]
    return [jnp.where(m, z, MASKED_Z) for z, m in zip(zs, masks)]


def _logs_phase(zs):
    sps = [jnp.maximum(z, 0.0) + jnp.log(1.0 + jnp.exp(-jnp.abs(z))) for z in zs]
    return sps, [z - sp for z, sp in zip(zs, sps)]


def _suffix_phase(vals, carries, mat):
    def hi_lo(tile):
        hi = tile.astype(BF16)
        return jnp.concatenate([hi, (tile - hi.astype(F32)).astype(BF16)], axis=1)

    packed = [[hi_lo(v[:, t * Q_BLOCK:(t + 1) * Q_BLOCK]) for t in range(SUB_TILES)] for v in vals]
    prods = [[jnp.dot(p, mat, preferred_element_type=F32) for p in chain] for chain in packed]
    sums, new_carries = [], []
    for chain, carry in zip(prods, carries):
        parts = []
        for t in reversed(range(SUB_TILES)):
            parts.append(chain[t][:, :LANES] + carry)
            carry = carry + chain[t][:, LANES:]
        sums.append(jnp.concatenate(parts[::-1], axis=1))
        new_carries.append(carry)
    return sums, new_carries


def _sweep(first_block, step, states, carries_of):
    n_chunks = ((first_block + Q_GROUP) * Q_BLOCK + KEY_CHUNK - 1) // KEY_CHUNK
    col = lax.broadcasted_iota(jnp.int32, (Q_BLOCK, KEY_CHUNK), 1)
    row = lax.broadcasted_iota(jnp.int32, (Q_BLOCK, 1), 0)

    def body(loop):
        n, _, sts = loop
        starts, masks = [], []
        for b in range(Q_GROUP):
            i = first_block + b
            end = (i + 1) * Q_BLOCK - n * KEY_CHUNK
            starts.append(pl.multiple_of(jnp.maximum(end - KEY_CHUNK, 0), Q_BLOCK))
            masks.append(col < jnp.minimum(i * Q_BLOCK + row, end) - starts[b])
        sts = step(starts, masks, sts)
        least = None
        for st in sts:
            for carry in carries_of(st):
                least = carry if least is None else jnp.minimum(least, carry)
        alive = (jnp.min(least) <= DEAD_ABOVE).astype(jnp.int32)
        return n + 1, alive, tuple(sts)

    return lax.while_loop(lambda loop: (loop[0] < n_chunks) & (loop[1] > 0), body,
                          (jnp.int32(0), jnp.int32(1), tuple(states)))[2]


CHAINS = [(b, h) for b in range(Q_GROUP) for h in range(2)]


def _head_rows(block, lane_lo_rows):
    zero = jnp.zeros_like(block)
    return jnp.concatenate([jnp.where(lane_lo_rows, block, zero), jnp.where(lane_lo_rows, zero, block)], axis=0)


def _attn_fwd(qkv, g_attn, ymix_in, width, shards):
    t_len = qkv.shape[0]
    npair = width // LANES
    rows_per_step = Q_GROUP * Q_BLOCK
    n_steps = t_len // rows_per_step
    n_sh = len(shards)

    def body(q_ref, k_ref, v_ref, g_ref, umat_ref, ymix_in_ref, *rest):
        shard_refs, rest = rest[:n_sh], rest[n_sh:]
        o_ref, ymix_ref = rest[:2]
        gathered_refs, sems = rest[2:2 + n_sh], rest[2 + n_sh:]
        p, i = pl.program_id(0), pl.program_id(1)
        start, forward, finish = _gather_plan(shard_refs, gathered_refs, *sems)
        pl.when((p == 0) & (i == 0))(start)
        pl.when((p == npair - 1) & (i == n_steps // 2))(forward)
        lane_lo = lax.broadcasted_iota(jnp.int32, (rows_per_step, LANES), 1) < HEAD_DIM
        lane_lo_keys = lax.broadcasted_iota(jnp.int32, (KEY_CHUNK, LANES), 1) < HEAD_DIM
        umat = umat_ref[...]
        q = q_ref[...] * jnp.asarray(HEAD_DIM ** -0.5, BF16)
        q_heads = (jnp.where(lane_lo, q, jnp.zeros_like(q)), jnp.where(lane_lo, jnp.zeros_like(q), q))

        def step(starts, masks, states):
            kcs = [k_ref[pl.ds(r, KEY_CHUNK), :] for r in starts]
            zs = _scores_phase([q_heads[h][b * Q_BLOCK:(b + 1) * Q_BLOCK] for b, h in CHAINS],
                               [kcs[b] for b, _ in CHAINS], [masks[b] for b, _ in CHAINS])
            sps, lbs = _logs_phase(zs)
            suffixes, carries = _suffix_phase(sps, [states[b][0][h] for b, h in CHAINS], umat)
            weights = [jnp.exp(lb - sfx).astype(BF16) for lb, sfx in zip(lbs, suffixes)]
            new = []
            for b in range(Q_GROUP):
                v_heads = _head_rows(v_ref[pl.ds(starts[b], KEY_CHUNK), :], lane_lo_keys)
                o_acc = states[b][1] + jnp.dot(jnp.concatenate(weights[2 * b:2 * b + 2], axis=1), v_heads,
                                               preferred_element_type=F32)
                new.append(((carries[2 * b], carries[2 * b + 1]), o_acc))
            return new

        zero = jnp.zeros((Q_BLOCK, LANES), F32)
        states = _sweep(i * Q_GROUP, step, [((zero, zero), zero)] * Q_GROUP, lambda st: st[0])
        o = jnp.concatenate([st[1] for st in states], axis=0)
        o_ref[...] = o
        rs = lax.rsqrt(_group_sums(o * o, lane_lo) * (1.0 / HEAD_DIM) + RMS_EPS)
        ymix_ref[...] = (o * rs * g_ref[...]).astype(BF16)
        pl.when((p == npair - 1) & (i == n_steps - 1))(finish)

    hbm = pl.BlockSpec(memory_space=pl.ANY)
    outs = pl.pallas_call(
        body, name="attn_fwd", grid=(npair, n_steps),
        in_specs=[pl.BlockSpec((rows_per_step, LANES), lambda p, i: (i, p)),
                  pl.BlockSpec((t_len, LANES), lambda p, i: (0, npair + p)),
                  pl.BlockSpec((t_len, LANES), lambda p, i: (0, 2 * npair + p)),
                  pl.BlockSpec((1, LANES), lambda p, i: (0, p)),
                  pl.BlockSpec((2 * LANES, 2 * LANES), lambda p, i: (0, 0)),
                  hbm] + [hbm] * n_sh,
        out_specs=[pl.BlockSpec((rows_per_step, LANES), lambda p, i: (i, p)),
                   pl.BlockSpec((rows_per_step, LANES), lambda p, i: (i, npair + p))] + [hbm] * n_sh,
        out_shape=[jax.ShapeDtypeStruct((t_len, width), F32), jax.ShapeDtypeStruct(ymix_in.shape, BF16)]
        + [jax.ShapeDtypeStruct((N_DEV,) + s.shape, s.dtype) for s in shards],
        input_output_aliases={5: 1},
        scratch_shapes=_gather_scratch(n_sh),
        compiler_params=_params(("arbitrary", "arbitrary")),
    )(qkv, qkv, qkv, g_attn, _cumsum_matrix(False), ymix_in, *shards)
    return outs[0], outs[1], outs[2:]


def _attn_bwd(qkv, o, dymix, g_attn, width, exchange):
    t_len = qkv.shape[0]
    npair = width // LANES
    rows_per_step = Q_GROUP * Q_BLOCK
    n_steps = t_len // rows_per_step
    scale = HEAD_DIM ** -0.5
    n_ex = len(exchange)

    def body(q_ref, k_ref, v_ref, o_ref, dy_ref, g_ref, umat_ref, gmat_ref, *rest):
        p_refs, rest = rest[:n_ex], rest[n_ex:]
        dq_ref, dk_ref, dv_ref, gg_ref = rest[:4]
        land_refs, rest = rest[4:4 + n_ex], rest[4 + n_ex:]
        dk_s, dv_s = rest[:2]
        copies = _chip_copies(p_refs, land_refs, *rest[2:])
        p, i = pl.program_id(0), pl.program_id(1)

        @pl.when((p == 0) & (i == 0))
        def _():
            for cp in copies:
                cp.start()

        lane_lo = lax.broadcasted_iota(jnp.int32, (rows_per_step, LANES), 1) < HEAD_DIM
        lane_lo_keys = lax.broadcasted_iota(jnp.int32, (KEY_CHUNK, LANES), 1) < HEAD_DIM
        umat, gmat = umat_ref[...], gmat_ref[...]

        @pl.when(i == 0)
        def _():
            dk_s[...] = jnp.zeros_like(dk_s)
            dv_s[...] = jnp.zeros_like(dv_s)

        o_blk = o_ref[...]
        rs = lax.rsqrt(_group_sums(o_blk * o_blk, lane_lo) * (1.0 / HEAD_DIM) + RMS_EPS)
        yn = o_blk * rs
        dy_raw = dy_ref[...]
        dyn = dy_raw * g_ref[...]
        d_o = rs * (dyn - yn * (_group_sums(dyn * yn, lane_lo) * (1.0 / HEAD_DIM)))
        _accumulate(gg_ref, i, n_steps, _rows8(dy_raw * yn))
        do_bf = d_o.astype(BF16)
        do_o = do_bf.astype(F32) * o_blk
        q = q_ref[...]
        blocks = [slice(b * Q_BLOCK, (b + 1) * Q_BLOCK) for b in range(Q_GROUP)]
        lane_lo_q = lane_lo[:Q_BLOCK]
        totals = [(jnp.sum(jnp.where(lane_lo_q, do_o[r], 0.0), axis=-1, keepdims=True),
                   jnp.sum(jnp.where(lane_lo_q, 0.0, do_o[r]), axis=-1, keepdims=True)) for r in blocks]
        q_rows = [_head_rows(q[r], lane_lo_q) for r in blocks]
        do_rows = [_head_rows(do_bf[r], lane_lo_q) for r in blocks]
        q_scaled = [qr * jnp.asarray(scale, BF16) for qr in q_rows]

        def step(starts, masks, states):
            kcs = [k_ref[pl.ds(r, KEY_CHUNK), :] for r in starts]
            vcs = [v_ref[pl.ds(r, KEY_CHUNK), :] for r in starts]
            head = lambda h: slice(h * Q_BLOCK, (h + 1) * Q_BLOCK)
            zs = _scores_phase([q_scaled[b][head(h)] for b, h in CHAINS], [kcs[b] for b, _ in CHAINS],
                               [masks[b] for b, _ in CHAINS])
            das = [lax.dot_general(do_rows[b][head(h)], vcs[b], (_NT, ((), ())), preferred_element_type=F32)
                   for b, h in CHAINS]
            sps, lbs = _logs_phase(zs)
            suffixes, carries = _suffix_phase(sps, [states[b][0][h] for b, h in CHAINS], umat)
            a_bf = [jnp.exp(lb - sfx).astype(BF16) for lb, sfx in zip(lbs, suffixes)]
            gs = [a.astype(F32) * da for a, da in zip(a_bf, das)]
            suffixes_g, carries_g = _suffix_phase(gs, [states[b][1][h] for b, h in CHAINS], gmat)
            dzs = []
            for (b, h), lb, g, sfx_g in zip(CHAINS, lbs, gs, suffixes_g):
                before = totals[b][h] - sfx_g
                dzs.append((g - jnp.exp(lb) * (g + before)).astype(BF16))
            new = []
            for b in range(Q_GROUP):
                pair = slice(2 * b, 2 * b + 2)
                dq_acc = states[b][2] + jnp.dot(jnp.concatenate(dzs[pair], axis=1), _head_rows(kcs[b], lane_lo_keys),
                                                preferred_element_type=F32)
                dk_s[pl.ds(starts[b], KEY_CHUNK), :] += lax.dot_general(
                    jnp.concatenate(dzs[pair], axis=0), q_rows[b], (_TN, ((), ())), preferred_element_type=F32)
                dv_s[pl.ds(starts[b], KEY_CHUNK), :] += lax.dot_general(
                    jnp.concatenate(a_bf[pair], axis=0), do_rows[b], (_TN, ((), ())), preferred_element_type=F32)
                new.append(((carries[2 * b], carries[2 * b + 1]), (carries_g[2 * b], carries_g[2 * b + 1]), dq_acc))
            return new

        zero = jnp.zeros((Q_BLOCK, LANES), F32)
        states = _sweep(i * Q_GROUP, step, [((zero, zero), (zero, zero), zero)] * Q_GROUP, lambda st: st[0])
        dq_ref[...] = (jnp.concatenate([st[2] for st in states], axis=0) * scale).astype(BF16)

        @pl.when(i == n_steps - 1)
        def _():
            dk_ref[...] = (dk_s[...] * scale).astype(BF16)
            dv_ref[...] = dv_s[...].astype(BF16)

        @pl.when((p == npair - 1) & (i == n_steps - 1))
        def _():
            for cp in copies:
                cp.wait()

    whole = pl.BlockSpec((t_len, LANES), lambda p, i: (0, p))
    hbm = pl.BlockSpec(memory_space=pl.ANY)
    outs = pl.pallas_call(
        body, name="attn_bwd", grid=(npair, n_steps),
        in_specs=[pl.BlockSpec((rows_per_step, LANES), lambda p, i: (i, p)),
                  pl.BlockSpec((t_len, LANES), lambda p, i: (0, npair + p)),
                  pl.BlockSpec((t_len, LANES), lambda p, i: (0, 2 * npair + p)),
                  pl.BlockSpec((rows_per_step, LANES), lambda p, i: (i, p)),
                  pl.BlockSpec((rows_per_step, LANES), lambda p, i: (i, npair + p)),
                  pl.BlockSpec((1, LANES), lambda p, i: (0, p)),
                  pl.BlockSpec((2 * LANES, 2 * LANES), lambda p, i: (0, 0)),
                  pl.BlockSpec((2 * LANES, 2 * LANES), lambda p, i: (0, 0))] + [hbm] * n_ex,
        out_specs=[pl.BlockSpec((rows_per_step, LANES), lambda p, i: (i, p)), whole, whole,
                   pl.BlockSpec((8, LANES), lambda p, i: (0, p))] + [hbm] * n_ex,
        out_shape=[jax.ShapeDtypeStruct((t_len, width), BF16)] * 3 + [jax.ShapeDtypeStruct((8, width), F32)]
        + _chip_landing(exchange),
        scratch_shapes=[pltpu.VMEM((t_len, LANES), F32), pltpu.VMEM((t_len, LANES), F32)] + _chip_scratch(n_ex),
        compiler_params=_params(("arbitrary", "arbitrary")),
    )(qkv, qkv, qkv, o, dymix, g_attn, _cumsum_matrix(False), _cumsum_matrix(True), *exchange)
    return outs[0], outs[1], outs[2], outs[3], outs[4:]


def kernel(x, w_in, conv_w, g_conv, g_attn, w_out, ln1_g, ln1_b, w_up, w_down, ln2_g, ln2_b, loss_target, m_w_in, m_conv_w, m_g_conv, m_g_attn, m_w_out, m_ln1_g, m_ln1_b, m_w_up, m_w_down, m_ln2_g, m_ln2_b, v_w_in, v_conv_w, v_g_conv, v_g_attn, v_w_out, v_ln1_g, v_ln1_b, v_w_up, v_w_down, v_ln2_g, v_ln2_b):
    x2, target = x[0], loss_target[0]
    t_len, d_model = x2.shape
    width = g_conv.shape[1]
    in_shard = w_in.shape[2]
    up_shard = w_up.shape[2]
    d_ff = up_shard * N_DEV
    cw = conv_w.shape[2]
    xi, yi, ci = _place()
    dev = 4 * xi + 2 * yi + ci
    tm, sub = 512, 2
    tr = tm // sub
    mm = functools.partial(_matmul, sub=sub)

    win_g, convw_g, wout_bf, wup_bf, wdown_bf = _all_gather(
        "gather_w_in", [w_in[0], jnp.pad(conv_w[0], ((0, 5), (0, LANES - cw)))], [BF16, F32],
        cast_only=[w_out[0], w_up[0], w_down[0]])
    conv_full = jnp.transpose(convw_g[:, :3, :cw], (1, 0, 2)).reshape(3, width)

    n_row_tiles = t_len // tm
    n_sub_tiles = n_row_tiles * sub
    rows = lambda cols: pl.BlockSpec((tm, cols), lambda i: (i, 0))
    resident = lambda arr: pl.BlockSpec(arr.shape, lambda i: (0,) * arr.ndim)
    vec = pl.BlockSpec((1, d_model), lambda i: (0, 0))
    acc8 = pl.BlockSpec((8, d_model), lambda i: (0, 0))
    tokens = lambda cols, dtype: jax.ShapeDtypeStruct((t_len, cols), dtype)

    def proj_in_epilogue(acc, extra, outs, i):
        x_ref, w_ref, gain_ref, tail_ref = extra
        gates_ref, qkv_ref, xbf_ref, ymix_ref = outs
        gates_ref[...] = acc[:, :3 * width]
        qkv_ref[...] = acc[:, 3 * width:].astype(BF16)
        xbf_ref[...] = x_ref[...].astype(BF16)

        @pl.when(i == 0)
        def _():
            tail_ref[...] = jnp.zeros_like(tail_ref)

        u = acc[:, width:2 * width] * acc[:, 2 * width:3 * width]
        ext = jnp.concatenate([tail_ref[...], u], axis=0)
        tail_ref[...] = u[tr - 8:]
        w = w_ref[...]
        y = acc[:, :width] * (w[2:3] * u + w[1:2] * pltpu.roll(ext, 1, 0)[8:] + w[0:1] * pltpu.roll(ext, 2, 0)[8:])
        lane_lo = lax.broadcasted_iota(jnp.int32, (tr, LANES), 1) < HEAD_DIM
        gain = gain_ref[...]
        for blk in range(width // LANES):
            lanes = slice(blk * LANES, (blk + 1) * LANES)
            y_blk = y[:, lanes]
            rs = lax.rsqrt(_group_sums(y_blk * y_blk, lane_lo) * (1.0 / HEAD_DIM) + RMS_EPS)
            ymix_ref[:, lanes] = (y_blk * rs * gain[:, lanes]).astype(BF16)

    half_vec = pl.BlockSpec((1, width), lambda i: (0, 0))
    gates, qkv, x_bf, ymix, wdown_g = mm(
        "proj_in", x2, win_g, rows(d_model), resident(win_g), _shards_wide, n_row_tiles,
        [(tokens(3 * width, F32), rows(3 * width)), (tokens(3 * width, BF16), rows(3 * width)),
         (tokens(d_model, BF16), rows(d_model)), (tokens(2 * width, BF16), rows(width))],
        proj_in_epilogue,
        extras=[(x2, rows(d_model)), (conv_full, pl.BlockSpec((3, width), lambda i: (0, 0))), (g_conv, half_vec)],
        gather=[wdown_bf], carry=[pltpu.VMEM((8, width), F32)])

    o_attn, ymix, (wout_g, wup_g) = _attn_fwd(qkv, g_attn, ymix, width, [wout_bf, wup_bf])
    wout_full = wout_g.reshape(d_model, d_model)
    wdown_full = wdown_g.reshape(d_ff, d_model)

    def out_ln1_epilogue(acc, extra, outs, i):
        x_ref, g_ref, b_ref = extra
        r1 = ALPHA * x_ref[...] + acc
        x1, _, _ = _ln_fwd(r1, g_ref[...], b_ref[...])
        outs[0][...] = r1
        outs[1][...] = x1
        outs[2][...] = x1.astype(BF16)

    r1, x1, x1_bf = mm(
        "proj_out_ln1", ymix, wout_full, rows(d_model), resident(wout_full), _whole(_NN), n_row_tiles,
        [(tokens(d_model, F32), rows(d_model)), (tokens(d_model, F32), rows(d_model)),
         (tokens(d_model, BF16), rows(d_model))],
        out_ln1_epilogue, extras=[(x2, rows(d_model)), (ln1_g, vec), (ln1_b, vec)])

    def up_epilogue(acc, extra, outs, i):
        relu = jnp.maximum(acc, 0.0)
        outs[0][...] = (relu * relu).astype(BF16)

    (hid,) = mm(
        "ffn_up", x1_bf, wup_g, rows(d_model), resident(wup_g), _shards_wide, n_row_tiles,
        [(tokens(d_ff, BF16), rows(d_ff))], up_epilogue)

    def down_ln2_epilogue(acc, extra, outs, i):
        x1_ref, t_ref, g_ref, b_ref = extra
        dr2_ref, dr2_bf_ref, loss_ref, gg_ref, gb_ref = outs
        gain = g_ref[...]
        r2 = ALPHA * x1_ref[...] + acc
        y, xhat, rstd = _ln_fwd(r2, gain, b_ref[...])
        diff = y - t_ref[...]
        d_y = diff * (1.0 / d_model)
        dr2 = _ln_bwd(d_y, xhat, rstd, gain)
        dr2_ref[...] = dr2
        dr2_bf_ref[...] = dr2.astype(BF16)
        _accumulate(gg_ref, i, n_sub_tiles, _rows8(d_y * xhat))
        _accumulate(gb_ref, i, n_sub_tiles, _rows8(d_y))
        _accumulate(loss_ref, i, n_sub_tiles, _rows8(diff * diff), all_lanes=True)

    small_acc = (jax.ShapeDtypeStruct((8, d_model), F32), acc8)
    short = lambda cols: pl.BlockSpec((tr, cols), lambda i: (i, 0))
    dr2, dr2_bf, loss_cols, g_ln2_g, g_ln2_b = _matmul(
        "ffn_down_ln2_loss", hid, wdown_full, short(d_ff), resident(wdown_full), _whole(_NN), n_sub_tiles,
        [(tokens(d_model, F32), short(d_model)), (tokens(d_model, BF16), short(d_model)),
         small_acc, small_acc, small_acc],
        down_ln2_epilogue, extras=[(x1, short(d_model)), (target, short(d_model)), (ln2_g, vec), (ln2_b, vec)])

    def dup_epilogue(acc, extra, outs, i):
        outs[0][...] = (acc * (2.0 * jnp.sqrt(extra[0][...].astype(F32)))).astype(BF16)

    (dup,) = mm(
        "ffn_dhid", dr2_bf, wdown_full, rows(d_model), resident(wdown_full), _whole(_NT), n_row_tiles,
        [(tokens(d_ff, BF16), rows(d_ff))], dup_epilogue, extras=[(hid, rows(d_ff))])

    gw_down, pair_down = _weight_grad("grad_w_down", hid, dr2_bf, (d_ff // N_DEV, d_model), True)
    gw_up, pair_up = _weight_grad("grad_w_up", x1_bf, dup, (d_model, up_shard), False)

    def dx1_ln1_epilogue(acc, extra, outs, i):
        dr2_ref, r1_ref, g_ref, b_ref = extra
        dr1_ref, dr1_bf_ref, gg_ref, gb_ref = outs
        gain = g_ref[...]
        dx1 = ALPHA * dr2_ref[...] + acc
        _, xhat, rstd = _ln_fwd(r1_ref[...], gain, b_ref[...])
        dr1 = _ln_bwd(dx1, xhat, rstd, gain)
        dr1_ref[...] = dr1
        dr1_bf_ref[...] = dr1.astype(BF16)
        _accumulate(gg_ref, i, n_sub_tiles, _rows8(dx1 * xhat))
        _accumulate(gb_ref, i, n_sub_tiles, _rows8(dx1))

    dr1, dr1_bf, g_ln1_g, g_ln1_b = _matmul(
        "ffn_dx1_ln1", dup, wup_g, short(d_ff), resident(wup_g), _shards_deep, n_sub_tiles,
        [(tokens(d_model, F32), short(d_model)), (tokens(d_model, BF16), short(d_model)), small_acc, small_acc],
        dx1_ln1_epilogue, extras=[(dr2, short(d_model)), (r1, short(d_model)), (ln1_g, vec), (ln1_b, vec)])

    gw_out, pair_out = _weight_grad("grad_w_out", ymix, dr1_bf, (d_model // N_DEV, d_model), True)

    c_idx = jnp.reshape(ci, (1,)).astype(jnp.int32)
    q_me = jnp.reshape(2 * xi + yi, (1,)).astype(jnp.int32)

    def pair_sums(names, grads, landed):
        sums = [_pair_sum("pair_sum_" + nm, g, land, c_idx, q_me) for nm, g, land in zip(names, grads, landed)]
        return [s[0] for s in sums], [s[1] for s in sums]

    wire_ffn, own_ffn = pair_sums(["w_out", "w_up", "w_down"], [gw_out, gw_up, gw_down],
                                  [pair_out, pair_up, pair_down])

    def plain_epilogue(acc, extra, outs, i):
        outs[0][...] = acc

    (dymix,) = mm(
        "proj_out_bwd", dr1_bf, wout_full, rows(d_model), resident(wout_full), _whole(_NT), n_row_tiles,
        [(tokens(d_model, F32), rows(d_model))], plain_epilogue)

    d_bg, d_cg, d_h, g_gconv, g_convw = _conv_bwd(gates, dymix, conv_full, g_conv, width)
    d_q, d_k, d_v, g_gattn, landed_ffn = _attn_bwd(qkv, o_attn, dymix, g_attn, width, wire_ffn)
    dproj = jnp.concatenate([d_bg, d_cg, d_h, d_q, d_k, d_v], axis=1)

    gw_in, pair_in = _weight_grad("grad_w_in", x_bf, dproj, (d_model, in_shard), False)
    wire_in, own_in = pair_sums(["w_in"], [gw_in], [pair_in])

    def dx_epilogue(acc, extra, outs, i):
        outs[0][...] = ALPHA * extra[0][...] + acc

    spare = jnp.zeros((3, d_model - width), F32)

    def pack(l1g, l1b, l2g, l2b, gc, ga, taps, tail=spare):
        return jnp.concatenate([l1g, l1b, l2g, l2b, jnp.concatenate([gc, ga], axis=1),
                                jnp.concatenate([taps, tail], axis=1)], axis=0)

    local = pack(g_ln1_g[:1], g_ln1_b[:1], g_ln2_g[:1], g_ln2_b[:1], g_gconv[:1], g_gattn[:1], g_convw[:3],
                 spare + (0.5 / d_model) * loss_cols[0, 0])

    grad_x, landed_in, parts = mm(
        "proj_in_bwd", dproj, win_g, rows(6 * width), resident(win_g), _shards_deep, n_row_tiles,
        [(tokens(d_model, F32), rows(d_model))], dx_epilogue, extras=[(dr1, rows(d_model))],
        exchange=wire_in, gather=[local])

    names = ["w_in", "w_out", "w_up", "w_down"]
    big = {}
    for nm, ow, land, w, m, v in zip(names, own_in + own_ffn, [landed_in] + list(landed_ffn),
                                     (w_in, w_out, w_up, w_down),
                                     (m_w_in, m_w_out, m_w_up, m_w_down), (v_w_in, v_w_out, v_w_up, v_w_down)):
        big[nm] = [r[None] for r in _shard_update("update_" + nm, ow, land, w[0], m[0], v[0])]

    def spread(a):
        return lax.dynamic_update_slice(jnp.zeros((3, width), F32), a, (0, dev * cw))

    w_small = pack(ln1_g, ln1_b, ln2_g, ln2_b, g_conv, g_attn, spread(conv_w[0]))
    m_small = pack(m_ln1_g, m_ln1_b, m_ln2_g, m_ln2_b, m_g_conv, m_g_attn, spread(m_conv_w[0]))
    v_small = pack(v_ln1_g, v_ln1_b, v_ln2_g, v_ln2_b, v_g_conv, v_g_attn, spread(v_conv_w[0]))
    small = _small_update(parts, w_small, m_small, v_small)

    def unpack(a):
        taps = lax.dynamic_slice(a[5:8, :width], (0, dev * cw), (3, cw))[None]
        return {"ln1_g": a[0:1], "ln1_b": a[1:2], "ln2_g": a[2:3], "ln2_b": a[3:4],
                "g_conv": a[4:5, :width], "g_attn": a[4:5, width:], "conv_w": taps}

    loss = small[0][7, d_model - 1]
    small = [unpack(a) for a in small]

    order = ["w_in", "conv_w", "g_conv", "g_attn", "w_out", "ln1_g", "ln1_b", "w_up", "w_down", "ln2_g", "ln2_b"]
    result = [loss, grad_x[None]]
    for kind in range(4):
        for nm in order:
            result.append(big[nm][kind] if nm in big else small[kind][nm])
    return tuple(result)
```

```python
import functools

import jax
import jax.numpy as jnp
from jax import lax
from jax.experimental import pallas as pl
from jax.experimental.pallas import tpu as pltpu

F32 = jnp.float32
BF16 = jnp.bfloat16
MESH = pl.DeviceIdType.MESH

N_DEV = 8
HEAD_DIM = 64
LANES = 128
Q_BLOCK = 128
ALPHA = 2.0 ** 0.25
LN_EPS = 1e-5
RMS_EPS = 1e-6
ADAM_LR, ADAM_B1, ADAM_B2, ADAM_EPS, ADAM_WD, ADAM_STEP = 0.001, 0.9, 0.999, 1e-08, 0.01, 10
VMEM_LIMIT_BYTES = 48 * 1024 * 1024

_NT = ((1,), (1,))
_NN = ((1,), (0,))
_TN = ((0,), (0,))


def _params(sem=None):
    return pltpu.CompilerParams(dimension_semantics=sem, vmem_limit_bytes=VMEM_LIMIT_BYTES)


def _dot(a, b, contract):
    return lax.dot_general(a.astype(BF16), b.astype(BF16), (contract, ((), ())), preferred_element_type=F32)


def _whole(contract):
    return lambda a_ref, b_ref: _dot(a_ref[...], b_ref[...], contract)


def _shards_wide(a_ref, b_ref):
    a = a_ref[...].astype(BF16)
    return jnp.concatenate([_dot(a, b_ref[s], _NN) for s in range(b_ref.shape[0])], axis=1)


def _shards_deep(a_ref, b_ref):
    n = b_ref.shape[2]
    acc = _dot(a_ref[:, 0:n], b_ref[0], _NT)
    for s in range(1, b_ref.shape[0]):
        acc = acc + _dot(a_ref[:, s * n:(s + 1) * n], b_ref[s], _NT)
    return acc


def _matmul(name, a, b, a_spec, b_spec, product, n_steps, outs, epilogue, extras=(), exchange=(), gather=(),
            carry=(), sub=1):
    behind = list(exchange) + list(gather)
    n_extra, n_out, n_ex, n_b, n_carry = len(extras), len(outs), len(exchange), len(behind), len(carry)

    def body(a_ref, b_ref, *rest):
        i = pl.program_id(0)
        extra_refs, rest = rest[:n_extra], rest[n_extra:]
        src_refs, rest = rest[:n_b], rest[n_b:]
        out_refs, rest = rest[:n_out], rest[n_out:]
        land_refs, rest = rest[:n_b], rest[n_b:]
        extra_refs, sems = extra_refs + rest[:n_carry], rest[n_carry:]
        copies = _chip_copies(src_refs[:n_ex], land_refs[:n_ex], *sems[:2]) if exchange else []
        gather_sems = sems[2:] if exchange else sems
        plan = _gather_plan(src_refs[n_ex:], land_refs[n_ex:], *gather_sems) if gather else None

        def start():
            for cp in copies:
                cp.start()
            if plan:
                plan[0]()

        def finish():
            if plan:
                plan[1]()
                plan[2]()
            for cp in copies:
                cp.wait()

        if behind:
            pl.when(i == 0)(start)
        tile_rows = a_ref.shape[0]
        sub_rows = tile_rows // sub

        def view(ref, k):
            return ref.at[pl.ds(k * sub_rows, sub_rows)] if sub > 1 and ref.shape[0] == tile_rows else ref

        accs = [product(view(a_ref, k), b_ref) for k in range(sub)]
        for k in range(sub):
            epilogue(accs[k], tuple(view(r, k) for r in extra_refs), tuple(view(r, k) for r in out_refs), i * sub + k)
        if behind:
            pl.when(i == n_steps - 1)(finish)

    hbm = pl.BlockSpec(memory_space=pl.ANY)
    landing = _chip_landing(exchange) + [jax.ShapeDtypeStruct((N_DEV,) + s.shape, s.dtype) for s in gather]
    scratch = (list(carry) + (_chip_scratch(len(exchange)) if exchange else [])
               + (_gather_scratch(len(gather)) if gather else []))
    return pl.pallas_call(
        body, name=name, grid=(n_steps,),
        in_specs=[a_spec, b_spec] + [s for _, s in extras] + [hbm] * n_b,
        out_specs=[s for _, s in outs] + [hbm] * n_b,
        out_shape=[o for o, _ in outs] + landing,
        scratch_shapes=scratch,
        compiler_params=_params(("arbitrary",)),
    )(a, b, *[e for e, _ in extras], *behind)


def _ln_fwd(r, g, b):
    mu = jnp.mean(r, axis=-1, keepdims=True)
    xc = r - mu
    rstd = lax.rsqrt(jnp.mean(xc * xc, axis=-1, keepdims=True) + LN_EPS)
    xhat = xc * rstd
    return xhat * g + b, xhat, rstd


def _ln_bwd(dy, xhat, rstd, g):
    dxh = dy * g
    m1 = jnp.mean(dxh, axis=-1, keepdims=True)
    m2 = jnp.mean(dxh * xhat, axis=-1, keepdims=True)
    return rstd * (dxh - m1 - xhat * m2)


def _rows8(v):
    n, c = v.shape
    return jnp.sum(v.reshape(n // 8, 8, c), axis=0)


def _accumulate(ref, i, n_steps, part8, all_lanes=False):
    @pl.when(i == 0)
    def _():
        ref[...] = part8

    @pl.when(i > 0)
    def _():
        ref[...] += part8

    @pl.when(i == n_steps - 1)
    def _():
        tot = jnp.sum(ref[...], axis=0, keepdims=True)
        if all_lanes:
            tot = jnp.sum(tot, axis=1, keepdims=True)
        ref[...] = jnp.broadcast_to(tot, ref.shape)


def _group_sums(v, lane_lo):
    s0 = jnp.sum(jnp.where(lane_lo, v, 0.0), axis=-1, keepdims=True)
    s1 = jnp.sum(jnp.where(lane_lo, 0.0, v), axis=-1, keepdims=True)
    return jnp.where(lane_lo, s0, s1)


def _place():
    return lax.axis_index("x"), lax.axis_index("y"), lax.axis_index("c")


def _gather_plan(src_refs, out_refs, send_sems, recv_sems, local_sems):
    n = len(src_refs)
    x, y, c = _place()
    me, sibling = (x, y, c), (x, y, 1 - c)
    chips = [(1 - x, y), (x, 1 - y), (1 - x, 1 - y)]

    def copy(a, k, block, to, src=None):
        slot = out_refs[a].at[4 * block[0] + 2 * block[1] + block[2]]
        return pltpu.make_async_remote_copy(
            src_ref=slot if src is None else src, dst_ref=slot,
            send_sem=send_sems.at[a, k], recv_sem=recv_sems.at[a, k],
            device_id=to, device_id_type=MESH)

    mine = [pltpu.make_async_copy(src_refs[a], out_refs[a].at[4 * x + 2 * y + c], local_sems.at[a]) for a in range(n)]
    first = [[copy(a, 0, me, sibling, src=src_refs[a])]
             + [copy(a, 1 + j, me, (*chip, c), src=src_refs[a]) for j, chip in enumerate(chips)] for a in range(n)]
    passed = [[copy(a, 4 + j, (*chip, c), sibling) for j, chip in enumerate(chips)] for a in range(n)]

    def start():
        for a in range(n):
            mine[a].start()
            for cp in first[a]:
                cp.start()

    def forward():
        for j, chip in enumerate(chips):
            for a in range(n):
                copy(a, 1 + j, (*chip, c), me).wait_recv()
                passed[a][j].start()

    def finish():
        for a in range(n):
            copy(a, 0, sibling, me).wait_recv()
            for j, chip in enumerate(chips):
                copy(a, 4 + j, (*chip, 1 - c), me).wait_recv()
        for a in range(n):
            for cp in first[a] + passed[a]:
                cp.wait_send()
            mine[a].wait()

    return start, forward, finish


def _gather_scratch(n):
    return [pltpu.SemaphoreType.DMA((n, 7)), pltpu.SemaphoreType.DMA((n, 7)), pltpu.SemaphoreType.DMA((n,))]


def _all_gather(name, shards, out_dtypes, cast_only=()):
    n, m = len(shards), len(cast_only)

    def body(*refs):
        in_refs, cast_in = refs[:n], refs[n:n + m]
        out_refs, cast_out = refs[n + m:2 * n + m], refs[2 * n + m:2 * (n + m)]
        stage = refs[2 * (n + m):3 * n + 2 * m]
        for a in range(n):
            stage[a][...] = in_refs[a][...].astype(out_dtypes[a])
        start, forward, finish = _gather_plan(stage, out_refs, *refs[3 * n + 2 * m:])
        start()
        for a in range(m):
            cast_out[a][...] = cast_in[a][...].astype(BF16)
        forward()
        finish()

    return pl.pallas_call(
        body, name=name,
        in_specs=[pl.BlockSpec(memory_space=pltpu.VMEM)] * (n + m),
        out_specs=[pl.BlockSpec(memory_space=pl.ANY)] * n + [pl.BlockSpec(memory_space=pltpu.VMEM)] * m,
        out_shape=[jax.ShapeDtypeStruct((N_DEV,) + s.shape, dt) for s, dt in zip(shards, out_dtypes)]
        + [jax.ShapeDtypeStruct(s.shape, BF16) for s in cast_only],
        scratch_shapes=[pltpu.VMEM(s.shape, dt) for s, dt in zip(shards, out_dtypes)] + _gather_scratch(n),
        compiler_params=pltpu.CompilerParams(vmem_limit_bytes=VMEM_LIMIT_BYTES),
    )(*shards, *cast_only)


def _weight_grad(name, acts, cots, shard_shape, acts_by_shard):
    t_len = acts.shape[0]
    sr, sc = shard_shape
    per = max(1, 256 // sr) if acts_by_shard else 1
    assert per in (1, 2)
    n_steps = N_DEV // per

    def body(a_ref, b_ref, g_ref, land_ref, stage, send_sems, recv_sems):
        i = pl.program_id(0)
        x, y, c = _place()

        def copy(q):
            return pltpu.make_async_remote_copy(
                src_ref=stage.at[q % 2], dst_ref=land_ref.at[q], send_sem=send_sems.at[q], recv_sem=recv_sems.at[q],
                device_id=(x, y, 1 - c), device_id_type=MESH)

        acc = _dot(a_ref[...], b_ref[...], _TN).reshape(per, sr, sc)
        g_ref[...] = acc
        if per == 2:
            payload, due, q = jnp.where(c == 0, acc[1], acc[0]), i >= 0, i
        else:
            payload, due, q = acc[0], (i % 2) == (1 - c), i // 2

        @pl.when(due)
        def _():
            @pl.when(q >= 2)
            def _():
                copy(q - 2).wait_send()

            stage[q % 2] = payload.astype(BF16)
            copy(q).start()

        @pl.when(i == n_steps - 1)
        def _():
            copy(2).wait_send()
            copy(3).wait_send()
            for k in range(4):
                copy(k).wait_recv()

    tall = lambda cols: pl.BlockSpec((t_len, cols), lambda i: (0, i))
    whole = lambda arr: pl.BlockSpec(arr.shape, lambda i: (0, 0))
    a_spec, b_spec = (tall(per * sr), whole(cots)) if acts_by_shard else (whole(acts), tall(sc))
    return pl.pallas_call(
        body, name=name, grid=(n_steps,),
        in_specs=[a_spec, b_spec],
        out_specs=[pl.BlockSpec((per, sr, sc), lambda i: (i, 0, 0)), pl.BlockSpec(memory_space=pl.ANY)],
        out_shape=[jax.ShapeDtypeStruct((N_DEV, sr, sc), F32), jax.ShapeDtypeStruct((4, sr, sc), BF16)],
        scratch_shapes=[pltpu.VMEM((2, sr, sc), BF16), pltpu.SemaphoreType.DMA((4,)), pltpu.SemaphoreType.DMA((4,))],
        compiler_params=_params(("arbitrary",)),
    )(acts, cots)


def _chip_copies(p_refs, land_refs, send_sems, recv_sems):
    x, y, c = _place()
    chips = [(1 - x, y), (x, 1 - y), (1 - x, 1 - y)]
    return [pltpu.make_async_remote_copy(
        src_ref=p_refs[a].at[2 * px + py], dst_ref=land_refs[a].at[k],
        send_sem=send_sems.at[a, k], recv_sem=recv_sems.at[a, k],
        device_id=(px, py, c), device_id_type=MESH)
        for a in range(len(p_refs)) for k, (px, py) in enumerate(chips)]


def _chip_scratch(n):
    return [pltpu.SemaphoreType.DMA((n, 3)), pltpu.SemaphoreType.DMA((n, 3))]


def _chip_landing(pair_sums):
    return [jax.ShapeDtypeStruct((3,) + p.shape[1:], p.dtype) for p in pair_sums]


def _pair_sum(name, g_f32, landed, c_idx, q_me):
    _, rows, cols = g_f32.shape
    tr = min(rows, 1024)

    def body(c_ref, q_ref, g_ref, l_ref, wire_ref, own_ref):
        q = pl.program_id(1)
        s = g_ref[...] + l_ref[...].astype(F32)
        wire_ref[...] = s.astype(BF16)

        @pl.when(q == q_ref[0])
        def _():
            own_ref[...] = s

    return pl.pallas_call(
        body, name=name,
        grid_spec=pltpu.PrefetchScalarGridSpec(
            num_scalar_prefetch=2, grid=(rows // tr, 4),
            in_specs=[pl.BlockSpec((None, tr, cols), lambda i, q, c_ref, q_ref: (2 * q + c_ref[0], i, 0)),
                      pl.BlockSpec((None, tr, cols), lambda i, q, c_ref, q_ref: (q, i, 0))],
            out_specs=[pl.BlockSpec((None, tr, cols), lambda i, q, c_ref, q_ref: (q, i, 0)),
                       pl.BlockSpec((tr, cols), lambda i, q, c_ref, q_ref: (i, 0))]),
        out_shape=[jax.ShapeDtypeStruct((4, rows, cols), BF16), jax.ShapeDtypeStruct((rows, cols), F32)],
        compiler_params=_params(("parallel", "arbitrary")),
    )(c_idx, q_me, g_f32, landed)


def _adamw(w, g, m, v):
    m = ADAM_B1 * m + (1.0 - ADAM_B1) * g
    v = ADAM_B2 * v + (1.0 - ADAM_B2) * (g * g)
    m_hat = m / (1.0 - ADAM_B1 ** ADAM_STEP)
    v_hat = v / (1.0 - ADAM_B2 ** ADAM_STEP)
    delta = -ADAM_LR * (m_hat / (jnp.sqrt(v_hat) + ADAM_EPS) + ADAM_WD * w)
    return delta, m, v


def _shard_update(name, own, landed, w, m, v):
    rows, cols = own.shape
    tr = min(rows, 256)

    def body(own_ref, l_ref, w_ref, m_ref, v_ref, g_out, d_out, m_out, v_out):
        g = own_ref[...] + l_ref[0].astype(F32) + l_ref[1].astype(F32) + l_ref[2].astype(F32)
        d, m_new, v_new = _adamw(w_ref[...], g, m_ref[...], v_ref[...])
        g_out[...] = g
        d_out[...] = d
        m_out[...] = m_new
        v_out[...] = v_new

    tile = pl.BlockSpec((tr, cols), lambda i: (i, 0))
    return pl.pallas_call(
        body, name=name, grid=(rows // tr,),
        in_specs=[tile, pl.BlockSpec((3, tr, cols), lambda i: (0, i, 0)), tile, tile, tile],
        out_specs=[tile] * 4,
        out_shape=[jax.ShapeDtypeStruct((rows, cols), F32)] * 4,
        compiler_params=_params(("parallel",)),
    )(own, landed, w, m, v)


def _small_update(parts, w, m, v):
    def body(p_ref, w_ref, m_ref, v_ref, g_out, d_out, m_out, v_out):
        g = p_ref[0]
        for d in range(1, N_DEV):
            g = g + p_ref[d]
        dl, m_new, v_new = _adamw(w_ref[...], g, m_ref[...], v_ref[...])
        g_out[...] = g
        d_out[...] = dl
        m_out[...] = m_new
        v_out[...] = v_new

    return pl.pallas_call(
        body, name="small_update",
        out_shape=[jax.ShapeDtypeStruct(w.shape, F32)] * 4,
        compiler_params=pltpu.CompilerParams(vmem_limit_bytes=VMEM_LIMIT_BYTES),
    )(parts, w, m, v)


def _cumsum_matrix(inclusive):
    j = lax.broadcasted_iota(jnp.int32, (2 * LANES, 2 * LANES), 0) % LANES
    s = lax.broadcasted_iota(jnp.int32, (2 * LANES, 2 * LANES), 1)
    keep = (s >= LANES) | ((j >= s) if inclusive else (j > s))
    return keep.astype(BF16)


KEY_CHUNK = 384
SUB_TILES = KEY_CHUNK // Q_BLOCK
Q_GROUP = 4
DEAD_ABOVE = 105.0
MASKED_Z = -1e30


def _scores_phase(qms, kcs, masks):
    zs = [lax.dot_general(q, k, (_NT, ((), ())), preferred_element_type=F32) for q, k in zip(qms, kcs)]
    return [jnp.where(m, z, MASKED_Z) for z, m in zip(zs, masks)]


def _logs_phase(zs):
    sps = [jnp.maximum(z, 0.0) + jnp.log(1.0 + jnp.exp(-jnp.abs(z))) for z in zs]
    return sps, [z - sp for z, sp in zip(zs, sps)]


def _suffix_phase(vals, carries, mat):
    def hi_lo(tile):
        hi = tile.astype(BF16)
        return jnp.concatenate([hi, (tile - hi.astype(F32)).astype(BF16)], axis=1)

    packed = [[hi_lo(v[:, t * Q_BLOCK:(t + 1) * Q_BLOCK]) for t in range(SUB_TILES)] for v in vals]
    prods = [[jnp.dot(p, mat, preferred_element_type=F32) for p in chain] for chain in packed]
    sums, new_carries = [], []
    for chain, carry in zip(prods, carries):
        parts = []
        for t in reversed(range(SUB_TILES)):
            parts.append(chain[t][:, :LANES] + carry)
            carry = carry + chain[t][:, LANES:]
        sums.append(jnp.concatenate(parts[::-1], axis=1))
        new_carries.append(carry)
    return sums, new_carries


def _sweep(first_block, step, states, carries_of):
    n_chunks = ((first_block + Q_GROUP) * Q_BLOCK + KEY_CHUNK - 1) // KEY_CHUNK
    col = lax.broadcasted_iota(jnp.int32, (Q_BLOCK, KEY_CHUNK), 1)
    row = lax.broadcasted_iota(jnp.int32, (Q_BLOCK, 1), 0)

    def body(loop):
        n, _, sts = loop
        starts, masks = [], []
        for b in range(Q_GROUP):
            i = first_block + b
            end = (i + 1) * Q_BLOCK - n * KEY_CHUNK
            starts.append(pl.multiple_of(jnp.maximum(end - KEY_CHUNK, 0), Q_BLOCK))
            masks.append(col < jnp.minimum(i * Q_BLOCK + row, end) - starts[b])
        sts = step(starts, masks, sts)
        least = None
        for st in sts:
            for carry in carries_of(st):
                least = carry if least is None else jnp.minimum(least, carry)
        alive = (jnp.min(least) <= DEAD_ABOVE).astype(jnp.int32)
        return n + 1, alive, tuple(sts)

    return lax.while_loop(lambda loop: (loop[0] < n_chunks) & (loop[1] > 0), body,
                          (jnp.int32(0), jnp.int32(1), tuple(states)))[2]


CHAINS = [(b, h) for b in range(Q_GROUP) for h in range(2)]


def _head_rows(block, lane_lo_rows):
    zero = jnp.zeros_like(block)
    return jnp.concatenate([jnp.where(lane_lo_rows, block, zero), jnp.where(lane_lo_rows, zero, block)], axis=0)


def _attn_fwd(qkv, g_attn, ymix_in, width, shards):
    t_len = qkv.shape[0]
    npair = width // LANES
    rows_per_step = Q_GROUP * Q_BLOCK
    n_steps = t_len // rows_per_step
    n_sh = len(shards)

    def body(q_ref, k_ref, v_ref, g_ref, umat_ref, ymix_in_ref, *rest):
        shard_refs, rest = rest[:n_sh], rest[n_sh:]
        o_ref, ymix_ref = rest[:2]
        gathered_refs, sems = rest[2:2 + n_sh], rest[2 + n_sh:]
        p, i = pl.program_id(0), pl.program_id(1)
        start, forward, finish = _gather_plan(shard_refs, gathered_refs, *sems)
        pl.when((p == 0) & (i == 0))(start)
        pl.when((p == npair - 1) & (i == n_steps // 2))(forward)
        lane_lo = lax.broadcasted_iota(jnp.int32, (rows_per_step, LANES), 1) < HEAD_DIM
        lane_lo_keys = lax.broadcasted_iota(jnp.int32, (KEY_CHUNK, LANES), 1) < HEAD_DIM
        umat = umat_ref[...]
        q = q_ref[...] * jnp.asarray(HEAD_DIM ** -0.5, BF16)
        q_heads = (jnp.where(lane_lo, q, jnp.zeros_like(q)), jnp.where(lane_lo, jnp.zeros_like(q), q))

        def step(starts, masks, states):
            kcs = [k_ref[pl.ds(r, KEY_CHUNK), :] for r in starts]
            zs = _scores_phase([q_heads[h][b * Q_BLOCK:(b + 1) * Q_BLOCK] for b, h in CHAINS],
                               [kcs[b] for b, _ in CHAINS], [masks[b] for b, _ in CHAINS])
            sps, lbs = _logs_phase(zs)
            suffixes, carries = _suffix_phase(sps, [states[b][0][h] for b, h in CHAINS], umat)
            weights = [jnp.exp(lb - sfx).astype(BF16) for lb, sfx in zip(lbs, suffixes)]
            new = []
            for b in range(Q_GROUP):
                v_heads = _head_rows(v_ref[pl.ds(starts[b], KEY_CHUNK), :], lane_lo_keys)
                o_acc = states[b][1] + jnp.dot(jnp.concatenate(weights[2 * b:2 * b + 2], axis=1), v_heads,
                                               preferred_element_type=F32)
                new.append(((carries[2 * b], carries[2 * b + 1]), o_acc))
            return new

        zero = jnp.zeros((Q_BLOCK, LANES), F32)
        states = _sweep(i * Q_GROUP, step, [((zero, zero), zero)] * Q_GROUP, lambda st: st[0])
        o = jnp.concatenate([st[1] for st in states], axis=0)
        o_ref[...] = o
        rs = lax.rsqrt(_group_sums(o * o, lane_lo) * (1.0 / HEAD_DIM) + RMS_EPS)
        ymix_ref[...] = (o * rs * g_ref[...]).astype(BF16)
        pl.when((p == npair - 1) & (i == n_steps - 1))(finish)

    hbm = pl.BlockSpec(memory_space=pl.ANY)
    outs = pl.pallas_call(
        body, name="attn_fwd", grid=(npair, n_steps),
        in_specs=[pl.BlockSpec((rows_per_step, LANES), lambda p, i: (i, p)),
                  pl.BlockSpec((t_len, LANES), lambda p, i: (0, npair + p)),
                  pl.BlockSpec((t_len, LANES), lambda p, i: (0, 2 * npair + p)),
                  pl.BlockSpec((1, LANES), lambda p, i: (0, p)),
                  pl.BlockSpec((2 * LANES, 2 * LANES), lambda p, i: (0, 0)),
                  hbm] + [hbm] * n_sh,
        out_specs=[pl.BlockSpec((rows_per_step, LANES), lambda p, i: (i, p)),
                   pl.BlockSpec((rows_per_step, LANES), lambda p, i: (i, npair + p))] + [hbm] * n_sh,
        out_shape=[jax.ShapeDtypeStruct((t_len, width), F32), jax.ShapeDtypeStruct(ymix_in.shape, BF16)]
        + [jax.ShapeDtypeStruct((N_DEV,) + s.shape, s.dtype) for s in shards],
        input_output_aliases={5: 1},
        scratch_shapes=_gather_scratch(n_sh),
        compiler_params=_params(("arbitrary", "arbitrary")),
    )(qkv, qkv, qkv, g_attn, _cumsum_matrix(False), ymix_in, *shards)
    return outs[0], outs[1], outs[2:]


def _attn_bwd(qkv, o, dymix, g_attn, width, exchange):
    t_len = qkv.shape[0]
    npair = width // LANES
    rows_per_step = Q_GROUP * Q_BLOCK
    n_steps = t_len // rows_per_step
    scale = HEAD_DIM ** -0.5
    n_ex = len(exchange)

    def body(q_ref, k_ref, v_ref, o_ref, dy_ref, g_ref, umat_ref, gmat_ref, *rest):
        p_refs, rest = rest[:n_ex], rest[n_ex:]
        dq_ref, dk_ref, dv_ref, gg_ref = rest[:4]
        land_refs, rest = rest[4:4 + n_ex], rest[4 + n_ex:]
        dk_s, dv_s = rest[:2]
        copies = _chip_copies(p_refs, land_refs, *rest[2:])
        p, i = pl.program_id(0), pl.program_id(1)

        @pl.when((p == 0) & (i == 0))
        def _():
            for cp in copies:
                cp.start()

        lane_lo = lax.broadcasted_iota(jnp.int32, (rows_per_step, LANES), 1) < HEAD_DIM
        lane_lo_keys = lax.broadcasted_iota(jnp.int32, (KEY_CHUNK, LANES), 1) < HEAD_DIM
        umat, gmat = umat_ref[...], gmat_ref[...]

        @pl.when(i == 0)
        def _():
            dk_s[...] = jnp.zeros_like(dk_s)
            dv_s[...] = jnp.zeros_like(dv_s)

        o_blk = o_ref[...]
        rs = lax.rsqrt(_group_sums(o_blk * o_blk, lane_lo) * (1.0 / HEAD_DIM) + RMS_EPS)
        yn = o_blk * rs
        dy_raw = dy_ref[...]
        dyn = dy_raw * g_ref[...]
        d_o = rs * (dyn - yn * (_group_sums(dyn * yn, lane_lo) * (1.0 / HEAD_DIM)))
        _accumulate(gg_ref, i, n_steps, _rows8(dy_raw * yn))
        do_bf = d_o.astype(BF16)
        do_o = do_bf.astype(F32) * o_blk
        q = q_ref[...]
        blocks = [slice(b * Q_BLOCK, (b + 1) * Q_BLOCK) for b in range(Q_GROUP)]
        lane_lo_q = lane_lo[:Q_BLOCK]
        totals = [(jnp.sum(jnp.where(lane_lo_q, do_o[r], 0.0), axis=-1, keepdims=True),
                   jnp.sum(jnp.where(lane_lo_q, 0.0, do_o[r]), axis=-1, keepdims=True)) for r in blocks]
        q_rows = [_head_rows(q[r], lane_lo_q) for r in blocks]
        do_rows = [_head_rows(do_bf[r], lane_lo_q) for r in blocks]
        q_scaled = [qr * jnp.asarray(scale, BF16) for qr in q_rows]

        def step(starts, masks, states):
            kcs = [k_ref[pl.ds(r, KEY_CHUNK), :] for r in starts]
            vcs = [v_ref[pl.ds(r, KEY_CHUNK), :] for r in starts]
            head = lambda h: slice(h * Q_BLOCK, (h + 1) * Q_BLOCK)
            zs = _scores_phase([q_scaled[b][head(h)] for b, h in CHAINS], [kcs[b] for b, _ in CHAINS],
                               [masks[b] for b, _ in CHAINS])
            das = [lax.dot_general(do_rows[b][head(h)], vcs[b], (_NT, ((), ())), preferred_element_type=F32)
                   for b, h in CHAINS]
            sps, lbs = _logs_phase(zs)
            suffixes, carries = _suffix_phase(sps, [states[b][0][h] for b, h in CHAINS], umat)
            a_bf = [jnp.exp(lb - sfx).astype(BF16) for lb, sfx in zip(lbs, suffixes)]
            gs = [a.astype(F32) * da for a, da in zip(a_bf, das)]
            suffixes_g, carries_g = _suffix_phase(gs, [states[b][1][h] for b, h in CHAINS], gmat)
            dzs = []
            for (b, h), lb, g, sfx_g in zip(CHAINS, lbs, gs, suffixes_g):
                before = totals[b][h] - sfx_g
                dzs.append((g - jnp.exp(lb) * (g + before)).astype(BF16))
            new = []
            for b in range(Q_GROUP):
                pair = slice(2 * b, 2 * b + 2)
                dq_acc = states[b][2] + jnp.dot(jnp.concatenate(dzs[pair], axis=1), _head_rows(kcs[b], lane_lo_keys),
                                                preferred_element_type=F32)
                dk_s[pl.ds(starts[b], KEY_CHUNK), :] += lax.dot_general(
                    jnp.concatenate(dzs[pair], axis=0), q_rows[b], (_TN, ((), ())), preferred_element_type=F32)
                dv_s[pl.ds(starts[b], KEY_CHUNK), :] += lax.dot_general(
                    jnp.concatenate(a_bf[pair], axis=0), do_rows[b], (_TN, ((), ())), preferred_element_type=F32)
                new.append(((carries[2 * b], carries[2 * b + 1]), (carries_g[2 * b], carries_g[2 * b + 1]), dq_acc))
            return new

        zero = jnp.zeros((Q_BLOCK, LANES), F32)
        states = _sweep(i * Q_GROUP, step, [((zero, zero), (zero, zero), zero)] * Q_GROUP, lambda st: st[0])
        dq_ref[...] = (jnp.concatenate([st[2] for st in states], axis=0) * scale).astype(BF16)

        @pl.when(i == n_steps - 1)
        def _():
            dk_ref[...] = (dk_s[...] * scale).astype(BF16)
            dv_ref[...] = dv_s[...].astype(BF16)

        @pl.when((p == npair - 1) & (i == n_steps - 1))
        def _():
            for cp in copies:
                cp.wait()

    whole = pl.BlockSpec((t_len, LANES), lambda p, i: (0, p))
    hbm = pl.BlockSpec(memory_space=pl.ANY)
    outs = pl.pallas_call(
        body, name="attn_bwd", grid=(npair, n_steps),
        in_specs=[pl.BlockSpec((rows_per_step, LANES), lambda p, i: (i, p)),
                  pl.BlockSpec((t_len, LANES), lambda p, i: (0, npair + p)),
                  pl.BlockSpec((t_len, LANES), lambda p, i: (0, 2 * npair + p)),
                  pl.BlockSpec((rows_per_step, LANES), lambda p, i: (i, p)),
                  pl.BlockSpec((rows_per_step, LANES), lambda p, i: (i, p)),
                  pl.BlockSpec((1, LANES), lambda p, i: (0, p)),
                  pl.BlockSpec((2 * LANES, 2 * LANES), lambda p, i: (0, 0)),
                  pl.BlockSpec((2 * LANES, 2 * LANES), lambda p, i: (0, 0))] + [hbm] * n_ex,
        out_specs=[pl.BlockSpec((rows_per_step, LANES), lambda p, i: (i, p)), whole, whole,
                   pl.BlockSpec((8, LANES), lambda p, i: (0, p))] + [hbm] * n_ex,
        out_shape=[jax.ShapeDtypeStruct((t_len, width), BF16)] * 3 + [jax.ShapeDtypeStruct((8, width), F32)]
        + _chip_landing(exchange),
        scratch_shapes=[pltpu.VMEM((t_len, LANES), F32), pltpu.VMEM((t_len, LANES), F32)] + _chip_scratch(n_ex),
        compiler_params=_params(("arbitrary", "arbitrary")),
    )(qkv, qkv, qkv, o, dymix, g_attn, _cumsum_matrix(False), _cumsum_matrix(True), *exchange)
    return outs[0], outs[1], outs[2], outs[3], outs[4:]


def kernel(x, w_in, conv_w, g_conv, g_attn, w_out, ln1_g, ln1_b, w_up, w_down, ln2_g, ln2_b, loss_target, m_w_in, m_conv_w, m_g_conv, m_g_attn, m_w_out, m_ln1_g, m_ln1_b, m_w_up, m_w_down, m_ln2_g, m_ln2_b, v_w_in, v_conv_w, v_g_conv, v_g_attn, v_w_out, v_ln1_g, v_ln1_b, v_w_up, v_w_down, v_ln2_g, v_ln2_b):
    x2, target = x[0], loss_target[0]
    t_len, d_model = x2.shape
    width = g_conv.shape[1]
    in_shard = w_in.shape[2]
    up_shard = w_up.shape[2]
    d_ff = up_shard * N_DEV
    cw = conv_w.shape[2]
    xi, yi, ci = _place()
    dev = 4 * xi + 2 * yi + ci
    tm, sub = 512, 2
    tr = tm // sub
    mm = functools.partial(_matmul, sub=sub)

    win_g, convw_g, wout_bf, wup_bf, wdown_bf = _all_gather(
        "gather_w_in", [w_in[0], jnp.pad(conv_w[0], ((0, 5), (0, LANES - cw)))], [BF16, F32],
        cast_only=[w_out[0], w_up[0], w_down[0]])
    conv_full = jnp.transpose(convw_g[:, :3, :cw], (1, 0, 2)).reshape(3, width)

    n_row_tiles = t_len // tm
    n_sub_tiles = n_row_tiles * sub
    rows = lambda cols: pl.BlockSpec((tm, cols), lambda i: (i, 0))
    resident = lambda arr: pl.BlockSpec(arr.shape, lambda i: (0,) * arr.ndim)
    vec = pl.BlockSpec((1, d_model), lambda i: (0, 0))
    acc8 = pl.BlockSpec((8, d_model), lambda i: (0, 0))
    tokens = lambda cols, dtype: jax.ShapeDtypeStruct((t_len, cols), dtype)

    def proj_in_epilogue(acc, extra, outs, i):
        x_ref, w_ref, gain_ref, tail_ref = extra
        gates_ref, qkv_ref, xbf_ref, ymix_ref = outs
        gates_ref[...] = acc[:, :3 * width]
        qkv_ref[...] = acc[:, 3 * width:].astype(BF16)
        xbf_ref[...] = x_ref[...].astype(BF16)

        @pl.when(i == 0)
        def _():
            tail_ref[...] = jnp.zeros_like(tail_ref)

        u = acc[:, width:2 * width] * acc[:, 2 * width:3 * width]
        ext = jnp.concatenate([tail_ref[...], u], axis=0)
        tail_ref[...] = u[tr - 8:]
        w = w_ref[...]
        y = acc[:, :width] * (w[2:3] * u + w[1:2] * pltpu.roll(ext, 1, 0)[8:] + w[0:1] * pltpu.roll(ext, 2, 0)[8:])
        lane_lo = lax.broadcasted_iota(jnp.int32, (tr, LANES), 1) < HEAD_DIM
        gain = gain_ref[...]
        for blk in range(width // LANES):
            lanes = slice(blk * LANES, (blk + 1) * LANES)
            y_blk = y[:, lanes]
            rs = lax.rsqrt(_group_sums(y_blk * y_blk, lane_lo) * (1.0 / HEAD_DIM) + RMS_EPS)
            ymix_ref[:, lanes] = (y_blk * rs * gain[:, lanes]).astype(BF16)

    half_vec = pl.BlockSpec((1, width), lambda i: (0, 0))
    gates, qkv, x_bf, ymix, wdown_g = mm(
        "proj_in", x2, win_g, rows(d_model), resident(win_g), _shards_wide, n_row_tiles,
        [(tokens(3 * width, F32), rows(3 * width)), (tokens(3 * width, BF16), rows(3 * width)),
         (tokens(d_model, BF16), rows(d_model)), (tokens(2 * width, BF16), rows(width))],
        proj_in_epilogue,
        extras=[(x2, rows(d_model)), (conv_full, pl.BlockSpec((3, width), lambda i: (0, 0))), (g_conv, half_vec)],
        gather=[wdown_bf], carry=[pltpu.VMEM((8, width), F32)])

    o_attn, ymix, (wout_g, wup_g) = _attn_fwd(qkv, g_attn, ymix, width, [wout_bf, wup_bf])
    wout_full = wout_g.reshape(d_model, d_model)
    wdown_full = wdown_g.reshape(d_ff, d_model)

    def out_ln1_epilogue(acc, extra, outs, i):
        x_ref, g_ref, b_ref = extra
        r1 = ALPHA * x_ref[...] + acc
        x1, _, _ = _ln_fwd(r1, g_ref[...], b_ref[...])
        outs[0][...] = r1
        outs[1][...] = x1
        outs[2][...] = x1.astype(BF16)

    r1, x1, x1_bf = mm(
        "proj_out_ln1", ymix, wout_full, rows(d_model), resident(wout_full), _whole(_NN), n_row_tiles,
        [(tokens(d_model, F32), rows(d_model)), (tokens(d_model, F32), rows(d_model)),
         (tokens(d_model, BF16), rows(d_model))],
        out_ln1_epilogue, extras=[(x2, rows(d_model)), (ln1_g, vec), (ln1_b, vec)])

    def up_epilogue(acc, extra, outs, i):
        relu = jnp.maximum(acc, 0.0)
        outs[0][...] = (relu * relu).astype(BF16)

    (hid,) = mm(
        "ffn_up", x1_bf, wup_g, rows(d_model), resident(wup_g), _shards_wide, n_row_tiles,
        [(tokens(d_ff, BF16), rows(d_ff))], up_epilogue)

    def down_ln2_epilogue(acc, extra, outs, i):
        x1_ref, t_ref, g_ref, b_ref = extra
        dr2_ref, dr2_bf_ref, loss_ref, gg_ref, gb_ref = outs
        gain = g_ref[...]
        r2 = ALPHA * x1_ref[...] + acc
        y, xhat, rstd = _ln_fwd(r2, gain, b_ref[...])
        diff = y - t_ref[...]
        d_y = diff * (1.0 / d_model)
        dr2 = _ln_bwd(d_y, xhat, rstd, gain)
        dr2_ref[...] = dr2
        dr2_bf_ref[...] = dr2.astype(BF16)
        _accumulate(gg_ref, i, n_sub_tiles, _rows8(d_y * xhat))
        _accumulate(gb_ref, i, n_sub_tiles, _rows8(d_y))
        _accumulate(loss_ref, i, n_sub_tiles, _rows8(diff * diff), all_lanes=True)

    small_acc = (jax.ShapeDtypeStruct((8, d_model), F32), acc8)
    short = lambda cols: pl.BlockSpec((tr, cols), lambda i: (i, 0))
    dr2, dr2_bf, loss_cols, g_ln2_g, g_ln2_b = _matmul(
        "ffn_down_ln2_loss", hid, wdown_full, short(d_ff), resident(wdown_full), _whole(_NN), n_sub_tiles,
        [(tokens(d_model, F32), short(d_model)), (tokens(d_model, BF16), short(d_model)),
         small_acc, small_acc, small_acc],
        down_ln2_epilogue, extras=[(x1, short(d_model)), (target, short(d_model)), (ln2_g, vec), (ln2_b, vec)])

    def dup_epilogue(acc, extra, outs, i):
        outs[0][...] = (acc * (2.0 * jnp.sqrt(extra[0][...].astype(F32)))).astype(BF16)

    (dup,) = mm(
        "ffn_dhid", dr2_bf, wdown_full, rows(d_model), resident(wdown_full), _whole(_NT), n_row_tiles,
        [(tokens(d_ff, BF16), rows(d_ff))], dup_epilogue, extras=[(hid, rows(d_ff))])

    gw_down, pair_down = _weight_grad("grad_w_down", hid, dr2_bf, (d_ff // N_DEV, d_model), True)
    gw_up, pair_up = _weight_grad("grad_w_up", x1_bf, dup, (d_model, up_shard), False)

    def dx1_ln1_epilogue(acc, extra, outs, i):
        dr2_ref, r1_ref, g_ref, b_ref = extra
        dr1_ref, dr1_bf_ref, gg_ref, gb_ref = outs
        gain = g_ref[...]
        dx1 = ALPHA * dr2_ref[...] + acc
        _, xhat, rstd = _ln_fwd(r1_ref[...], gain, b_ref[...])
        dr1 = _ln_bwd(dx1, xhat, rstd, gain)
        dr1_ref[...] = dr1
        dr1_bf_ref[...] = dr1.astype(BF16)
        _accumulate(gg_ref, i, n_sub_tiles, _rows8(dx1 * xhat))
        _accumulate(gb_ref, i, n_sub_tiles, _rows8(dx1))

    dr1, dr1_bf, g_ln1_g, g_ln1_b = _matmul(
        "ffn_dx1_ln1", dup, wup_g, short(d_ff), resident(wup_g), _shards_deep, n_sub_tiles,
        [(tokens(d_model, F32), short(d_model)), (tokens(d_model, BF16), short(d_model)), small_acc, small_acc],
        dx1_ln1_epilogue, extras=[(dr2, short(d_model)), (r1, short(d_model)), (ln1_g, vec), (ln1_b, vec)])

    gw_out, pair_out = _weight_grad("grad_w_out", ymix, dr1_bf, (d_model // N_DEV, d_model), True)

    c_idx = jnp.reshape(ci, (1,)).astype(jnp.int32)
    q_me = jnp.reshape(2 * xi + yi, (1,)).astype(jnp.int32)

    def pair_sums(names, grads, landed):
        sums = [_pair_sum("pair_sum_" + nm, g, land, c_idx, q_me) for nm, g, land in zip(names, grads, landed)]
        return [s[0] for s in sums], [s[1] for s in sums]

    wire_ffn, own_ffn = pair_sums(["w_out", "w_up", "w_down"], [gw_out, gw_up, gw_down],
                                  [pair_out, pair_up, pair_down])

    n_back = t_len // tr
    back = lambda cols: pl.BlockSpec((tr, cols), lambda j: (n_back - 1 - j, 0))
    above = pl.BlockSpec((8, 3 * width), lambda j: (jnp.maximum((n_back - 1 - j) * (tr // 8) - 1, 0), 0))

    def out_bwd_epilogue(acc, extra, outs, j):
        gates_ref, above_ref, w_ref, gain_ref, head_ref = extra
        dbg_ref, dcg_ref, dh_ref, dya_ref, gg_ref, gw0_ref, gw1_ref, gw2_ref = outs
        dya_ref[...] = acc[:, width:]
        dy = acc[:, :width]
        bg, cg, h = gates_ref[:, :width], gates_ref[:, width:2 * width], gates_ref[:, 2 * width:]
        u = cg * h
        u_above = jnp.where(j < n_back - 1, above_ref[:, width:2 * width] * above_ref[:, 2 * width:], 0.0)
        ext = jnp.concatenate([u_above, u], axis=0)
        u1, u2 = pltpu.roll(ext, 1, 0)[8:], pltpu.roll(ext, 2, 0)[8:]
        w = w_ref[...]
        conv = w[2:3] * u + w[1:2] * u1 + w[0:1] * u2
        y = bg * conv
        lane_lo = lax.broadcasted_iota(jnp.int32, (tr, LANES), 1) < HEAD_DIM
        gain = gain_ref[...]
        d_y, yn = [], []
        for blk in range(width // LANES):
            lanes = slice(blk * LANES, (blk + 1) * LANES)
            y_blk = y[:, lanes]
            rs = lax.rsqrt(_group_sums(y_blk * y_blk, lane_lo) * (1.0 / HEAD_DIM) + RMS_EPS)
            yn_blk = y_blk * rs
            dyn = dy[:, lanes] * gain[:, lanes]
            d_y.append(rs * (dyn - yn_blk * (_group_sums(dyn * yn_blk, lane_lo) * (1.0 / HEAD_DIM))))
            yn.append(yn_blk)
        d_y, yn = jnp.concatenate(d_y, axis=1), jnp.concatenate(yn, axis=1)
        dbg_ref[...] = (d_y * conv).astype(BF16)
        dconv = d_y * bg

        @pl.when(j == 0)
        def _():
            head_ref[...] = jnp.zeros_like(head_ref)

        ext = jnp.concatenate([dconv, head_ref[...]], axis=0)
        head_ref[...] = dconv[:8]
        du = w[2:3] * dconv + w[1:2] * pltpu.roll(ext, tr + 7, 0)[:tr] + w[0:1] * pltpu.roll(ext, tr + 6, 0)[:tr]
        dcg_ref[...] = (du * h).astype(BF16)
        dh_ref[...] = (du * cg).astype(BF16)
        _accumulate(gg_ref, j, n_back, _rows8(dy * yn))
        _accumulate(gw0_ref, j, n_back, _rows8(dconv * u2))
        _accumulate(gw1_ref, j, n_back, _rows8(dconv * u1))
        _accumulate(gw2_ref, j, n_back, _rows8(dconv * u))

    half_acc = (jax.ShapeDtypeStruct((8, width), F32), pl.BlockSpec((8, width), lambda j: (0, 0)))
    d_bg, d_cg, d_h, dy_attn, g_gconv, gw0, gw1, gw2 = _matmul(
        "proj_out_bwd", dr1_bf, wout_full, back(d_model), resident(wout_full), _whole(_NT), n_back,
        [(tokens(width, BF16), back(width))] * 3 + [(tokens(width, F32), back(width))] + [half_acc] * 4,
        out_bwd_epilogue,
        extras=[(gates, back(3 * width)), (gates, above), (conv_full, pl.BlockSpec((3, width), lambda j: (0, 0))),
                (g_conv, half_vec)],
        carry=[pltpu.VMEM((8, width), F32)])
    g_convw = jnp.concatenate([gw0[:1], gw1[:1], gw2[:1]], axis=0)

    d_q, d_k, d_v, g_gattn, landed_ffn = _attn_bwd(qkv, o_attn, dy_attn, g_attn, width, wire_ffn)
    dproj = jnp.concatenate([d_bg, d_cg, d_h, d_q, d_k, d_v], axis=1)

    gw_in, pair_in = _weight_grad("grad_w_in", x_bf, dproj, (d_model, in_shard), False)
    wire_in, own_in = pair_sums(["w_in"], [gw_in], [pair_in])

    def dx_epilogue(acc, extra, outs, i):
        outs[0][...] = ALPHA * extra[0][...] + acc

    spare = jnp.zeros((3, d_model - width), F32)

    def pack(l1g, l1b, l2g, l2b, gc, ga, taps, tail=spare):
        return jnp.concatenate([l1g, l1b, l2g, l2b, jnp.concatenate([gc, ga], axis=1),
                                jnp.concatenate([taps, tail], axis=1)], axis=0)

    local = pack(g_ln1_g[:1], g_ln1_b[:1], g_ln2_g[:1], g_ln2_b[:1], g_gconv[:1], g_gattn[:1], g_convw[:3],
                 spare + (0.5 / d_model) * loss_cols[0, 0])

    grad_x, landed_in, parts = mm(
        "proj_in_bwd", dproj, win_g, rows(6 * width), resident(win_g), _shards_deep, n_row_tiles,
        [(tokens(d_model, F32), rows(d_model))], dx_epilogue, extras=[(dr1, rows(d_model))],
        exchange=wire_in, gather=[local])

    names = ["w_in", "w_out", "w_up", "w_down"]
    big = {}
    for nm, ow, land, w, m, v in zip(names, own_in + own_ffn, [landed_in] + list(landed_ffn),
                                     (w_in, w_out, w_up, w_down),
                                     (m_w_in, m_w_out, m_w_up, m_w_down), (v_w_in, v_w_out, v_w_up, v_w_down)):
        big[nm] = [r[None] for r in _shard_update("update_" + nm, ow, land, w[0], m[0], v[0])]

    def spread(a):
        return lax.dynamic_update_slice(jnp.zeros((3, width), F32), a, (0, dev * cw))

    w_small = pack(ln1_g, ln1_b, ln2_g, ln2_b, g_conv, g_attn, spread(conv_w[0]))
    m_small = pack(m_ln1_g, m_ln1_b, m_ln2_g, m_ln2_b, m_g_conv, m_g_attn, spread(m_conv_w[0]))
    v_small = pack(v_ln1_g, v_ln1_b, v_ln2_g, v_ln2_b, v_g_conv, v_g_attn, spread(v_conv_w[0]))
    small = _small_update(parts, w_small, m_small, v_small)

    def unpack(a):
        taps = lax.dynamic_slice(a[5:8, :width], (0, dev * cw), (3, cw))[None]
        return {"ln1_g": a[0:1], "ln1_b": a[1:2], "ln2_g": a[2:3], "ln2_b": a[3:4],
                "g_conv": a[4:5, :width], "g_attn": a[4:5, width:], "conv_w": taps}

    loss = small[0][7, d_model - 1]
    small = [unpack(a) for a in small]

    order = ["w_in", "conv_w", "g_conv", "g_attn", "w_out", "ln1_g", "ln1_b", "w_up", "w_down", "ln2_g", "ln2_b"]
    result = [loss, grad_x[None]]
    for kind in range(4):
        for nm in order:
            result.append(big[nm][kind] if nm in big else small[kind][nm])
    return tuple(result)
```

```python
import functools

import jax
import jax.numpy as jnp
from jax import lax
from jax.experimental import pallas as pl
from jax.experimental.pallas import tpu as pltpu

F32 = jnp.float32
BF16 = jnp.bfloat16
MESH = pl.DeviceIdType.MESH

N_DEV = 8
HEAD_DIM = 64
LANES = 128
Q_BLOCK = 128
ALPHA = 2.0 ** 0.25
LN_EPS = 1e-5
RMS_EPS = 1e-6
ADAM_LR, ADAM_B1, ADAM_B2, ADAM_EPS, ADAM_WD, ADAM_STEP = 0.001, 0.9, 0.999, 1e-08, 0.01, 10
VMEM_LIMIT_BYTES = 48 * 1024 * 1024

_NT = ((1,), (1,))
_NN = ((1,), (0,))
_TN = ((0,), (0,))


def _params(sem=None):
    return pltpu.CompilerParams(dimension_semantics=sem, vmem_limit_bytes=VMEM_LIMIT_BYTES)


def _dot(a, b, contract):
    return lax.dot_general(a.astype(BF16), b.astype(BF16), (contract, ((), ())), preferred_element_type=F32)


def _whole(contract):
    return lambda a_ref, b_ref: _dot(a_ref[...], b_ref[...], contract)


def _shards_wide(a_ref, b_ref):
    a = a_ref[...].astype(BF16)
    return jnp.concatenate([_dot(a, b_ref[s], _NN) for s in range(b_ref.shape[0])], axis=1)


def _shards_deep(a_ref, b_ref):
    n = b_ref.shape[2]
    acc = _dot(a_ref[:, 0:n], b_ref[0], _NT)
    for s in range(1, b_ref.shape[0]):
        acc = acc + _dot(a_ref[:, s * n:(s + 1) * n], b_ref[s], _NT)
    return acc


def _matmul(name, a, b, a_spec, b_spec, product, n_steps, outs, epilogue, extras=(), exchange=(), gather=(),
            carry=(), sub=1):
    behind = list(exchange) + list(gather)
    n_extra, n_out, n_ex, n_b, n_carry = len(extras), len(outs), len(exchange), len(behind), len(carry)

    def body(a_ref, b_ref, *rest):
        i = pl.program_id(0)
        extra_refs, rest = rest[:n_extra], rest[n_extra:]
        src_refs, rest = rest[:n_b], rest[n_b:]
        out_refs, rest = rest[:n_out], rest[n_out:]
        land_refs, rest = rest[:n_b], rest[n_b:]
        extra_refs, sems = extra_refs + rest[:n_carry], rest[n_carry:]
        copies = _chip_copies(src_refs[:n_ex], land_refs[:n_ex], *sems[:2]) if exchange else []
        gather_sems = sems[2:] if exchange else sems
        plan = _gather_plan(src_refs[n_ex:], land_refs[n_ex:], *gather_sems) if gather else None

        def start():
            for cp in copies:
                cp.start()
            if plan:
                plan[0]()

        def finish():
            if plan:
                plan[1]()
                plan[2]()
            for cp in copies:
                cp.wait()

        if behind:
            pl.when(i == 0)(start)
        tile_rows = a_ref.shape[0]
        sub_rows = tile_rows // sub

        def view(ref, k):
            return ref.at[pl.ds(k * sub_rows, sub_rows)] if sub > 1 and ref.shape[0] == tile_rows else ref

        accs = [product(view(a_ref, k), b_ref) for k in range(sub)]
        for k in range(sub):
            epilogue(accs[k], tuple(view(r, k) for r in extra_refs), tuple(view(r, k) for r in out_refs), i * sub + k)
        if behind:
            pl.when(i == n_steps - 1)(finish)

    hbm = pl.BlockSpec(memory_space=pl.ANY)
    landing = _chip_landing(exchange) + [jax.ShapeDtypeStruct((N_DEV,) + s.shape, s.dtype) for s in gather]
    scratch = (list(carry) + (_chip_scratch(len(exchange)) if exchange else [])
               + (_gather_scratch(len(gather)) if gather else []))
    return pl.pallas_call(
        body, name=name, grid=(n_steps,),
        in_specs=[a_spec, b_spec] + [s for _, s in extras] + [hbm] * n_b,
        out_specs=[s for _, s in outs] + [hbm] * n_b,
        out_shape=[o for o, _ in outs] + landing,
        scratch_shapes=scratch,
        compiler_params=_params(("arbitrary",)),
    )(a, b, *[e for e, _ in extras], *behind)


def _ln_fwd(r, g, b):
    mu = jnp.mean(r, axis=-1, keepdims=True)
    xc = r - mu
    rstd = lax.rsqrt(jnp.mean(xc * xc, axis=-1, keepdims=True) + LN_EPS)
    xhat = xc * rstd
    return xhat * g + b, xhat, rstd


def _ln_bwd(dy, xhat, rstd, g):
    dxh = dy * g
    m1 = jnp.mean(dxh, axis=-1, keepdims=True)
    m2 = jnp.mean(dxh * xhat, axis=-1, keepdims=True)
    return rstd * (dxh - m1 - xhat * m2)


def _rows8(v):
    n, c = v.shape
    return jnp.sum(v.reshape(n // 8, 8, c), axis=0)


def _accumulate(ref, i, n_steps, part8, all_lanes=False):
    @pl.when(i == 0)
    def _():
        ref[...] = part8

    @pl.when(i > 0)
    def _():
        ref[...] += part8

    @pl.when(i == n_steps - 1)
    def _():
        tot = jnp.sum(ref[...], axis=0, keepdims=True)
        if all_lanes:
            tot = jnp.sum(tot, axis=1, keepdims=True)
        ref[...] = jnp.broadcast_to(tot, ref.shape)


def _group_sums(v, lane_lo):
    s0 = jnp.sum(jnp.where(lane_lo, v, 0.0), axis=-1, keepdims=True)
    s1 = jnp.sum(jnp.where(lane_lo, 0.0, v), axis=-1, keepdims=True)
    return jnp.where(lane_lo, s0, s1)


def _place():
    return lax.axis_index("x"), lax.axis_index("y"), lax.axis_index("c")


def _gather_plan(src_refs, out_refs, send_sems, recv_sems, local_sems):
    n = len(src_refs)
    x, y, c = _place()
    me, sibling = (x, y, c), (x, y, 1 - c)
    chips = [(1 - x, y), (x, 1 - y), (1 - x, 1 - y)]

    def copy(a, k, block, to, src=None):
        slot = out_refs[a].at[4 * block[0] + 2 * block[1] + block[2]]
        return pltpu.make_async_remote_copy(
            src_ref=slot if src is None else src, dst_ref=slot,
            send_sem=send_sems.at[a, k], recv_sem=recv_sems.at[a, k],
            device_id=to, device_id_type=MESH)

    mine = [pltpu.make_async_copy(src_refs[a], out_refs[a].at[4 * x + 2 * y + c], local_sems.at[a]) for a in range(n)]
    first = [[copy(a, 0, me, sibling, src=src_refs[a])]
             + [copy(a, 1 + j, me, (*chip, c), src=src_refs[a]) for j, chip in enumerate(chips)] for a in range(n)]
    passed = [[copy(a, 4 + j, (*chip, c), sibling) for j, chip in enumerate(chips)] for a in range(n)]

    def start():
        for a in range(n):
            mine[a].start()
            for cp in first[a]:
                cp.start()

    def forward():
        for j, chip in enumerate(chips):
            for a in range(n):
                copy(a, 1 + j, (*chip, c), me).wait_recv()
                passed[a][j].start()

    def finish():
        for a in range(n):
            copy(a, 0, sibling, me).wait_recv()
            for j, chip in enumerate(chips):
                copy(a, 4 + j, (*chip, 1 - c), me).wait_recv()
        for a in range(n):
            for cp in first[a] + passed[a]:
                cp.wait_send()
            mine[a].wait()

    return start, forward, finish


def _gather_scratch(n):
    return [pltpu.SemaphoreType.DMA((n, 7)), pltpu.SemaphoreType.DMA((n, 7)), pltpu.SemaphoreType.DMA((n,))]


def _all_gather(name, shards, out_dtypes, cast_only=()):
    n, m = len(shards), len(cast_only)

    def body(*refs):
        in_refs, cast_in = refs[:n], refs[n:n + m]
        out_refs, cast_out = refs[n + m:2 * n + m], refs[2 * n + m:2 * (n + m)]
        stage = refs[2 * (n + m):3 * n + 2 * m]
        for a in range(n):
            stage[a][...] = in_refs[a][...].astype(out_dtypes[a])
        start, forward, finish = _gather_plan(stage, out_refs, *refs[3 * n + 2 * m:])
        start()
        for a in range(m):
            cast_out[a][...] = cast_in[a][...].astype(BF16)
        forward()
        finish()

    return pl.pallas_call(
        body, name=name,
        in_specs=[pl.BlockSpec(memory_space=pltpu.VMEM)] * (n + m),
        out_specs=[pl.BlockSpec(memory_space=pl.ANY)] * n + [pl.BlockSpec(memory_space=pltpu.VMEM)] * m,
        out_shape=[jax.ShapeDtypeStruct((N_DEV,) + s.shape, dt) for s, dt in zip(shards, out_dtypes)]
        + [jax.ShapeDtypeStruct(s.shape, BF16) for s in cast_only],
        scratch_shapes=[pltpu.VMEM(s.shape, dt) for s, dt in zip(shards, out_dtypes)] + _gather_scratch(n),
        compiler_params=pltpu.CompilerParams(vmem_limit_bytes=VMEM_LIMIT_BYTES),
    )(*shards, *cast_only)


def _weight_grad(name, acts, cots, shard_shape, acts_by_shard, c_idx):
    t_len = acts.shape[0]
    sr, sc = shard_shape
    per = max(1, 256 // sr) if acts_by_shard else 1
    assert per in (1, 2)
    n_steps = N_DEV // per

    def group(i, c_ref):
        if per == 2:
            return i
        return jnp.where(i < 4, 2 * i + 1 - c_ref[0], 2 * (i - 4) + c_ref[0])

    def body(c_ref, a_ref, b_ref, g_ref, land_ref, stage, send_sems, recv_sems):
        i = pl.program_id(0)
        x, y, c = _place()

        def copy(q):
            return pltpu.make_async_remote_copy(
                src_ref=stage.at[q % 2], dst_ref=land_ref.at[q], send_sem=send_sems.at[q], recv_sem=recv_sems.at[q],
                device_id=(x, y, 1 - c), device_id_type=MESH)

        acc = _dot(a_ref[...], b_ref[...], _TN).reshape(per, sr, sc)
        g_ref[...] = acc
        if per == 2:
            payload, due, q = jnp.where(c == 0, acc[1], acc[0]), i >= 0, i
        else:
            payload, due, q = acc[0], i < 4, jnp.minimum(i, 3)

        @pl.when(due)
        def _():
            @pl.when(q >= 2)
            def _():
                copy(q - 2).wait_send()

            stage[q % 2] = payload.astype(BF16)
            copy(q).start()

        @pl.when(i == n_steps - 1)
        def _():
            copy(2).wait_send()
            copy(3).wait_send()
            for k in range(4):
                copy(k).wait_recv()

    tall = lambda cols: pl.BlockSpec((t_len, cols), lambda i, c_ref: (0, group(i, c_ref)))
    whole = lambda arr: pl.BlockSpec(arr.shape, lambda i, c_ref: (0, 0))
    a_spec, b_spec = (tall(per * sr), whole(cots)) if acts_by_shard else (whole(acts), tall(sc))
    return pl.pallas_call(
        body, name=name,
        grid_spec=pltpu.PrefetchScalarGridSpec(
            num_scalar_prefetch=1, grid=(n_steps,),
            in_specs=[a_spec, b_spec],
            out_specs=[pl.BlockSpec((per, sr, sc), lambda i, c_ref: (group(i, c_ref), 0, 0)),
                       pl.BlockSpec(memory_space=pl.ANY)],
            scratch_shapes=[pltpu.VMEM((2, sr, sc), BF16), pltpu.SemaphoreType.DMA((4,)),
                            pltpu.SemaphoreType.DMA((4,))]),
        out_shape=[jax.ShapeDtypeStruct((N_DEV, sr, sc), F32), jax.ShapeDtypeStruct((4, sr, sc), BF16)],
        compiler_params=_params(("arbitrary",)),
    )(c_idx, acts, cots)


def _chip_copies(p_refs, land_refs, send_sems, recv_sems):
    x, y, c = _place()
    chips = [(1 - x, y), (x, 1 - y), (1 - x, 1 - y)]
    return [pltpu.make_async_remote_copy(
        src_ref=p_refs[a].at[2 * px + py], dst_ref=land_refs[a].at[k],
        send_sem=send_sems.at[a, k], recv_sem=recv_sems.at[a, k],
        device_id=(px, py, c), device_id_type=MESH)
        for a in range(len(p_refs)) for k, (px, py) in enumerate(chips)]


def _chip_scratch(n):
    return [pltpu.SemaphoreType.DMA((n, 3)), pltpu.SemaphoreType.DMA((n, 3))]


def _chip_landing(pair_sums):
    return [jax.ShapeDtypeStruct((3,) + p.shape[1:], p.dtype) for p in pair_sums]


def _pair_sum(name, g_f32, landed, c_idx, q_me):
    _, rows, cols = g_f32.shape
    tr = min(rows, 1024)

    def body(c_ref, q_ref, g_ref, l_ref, wire_ref, own_ref):
        q = pl.program_id(1)
        s = g_ref[...] + l_ref[...].astype(F32)
        wire_ref[...] = s.astype(BF16)

        @pl.when(q == q_ref[0])
        def _():
            own_ref[...] = s

    return pl.pallas_call(
        body, name=name,
        grid_spec=pltpu.PrefetchScalarGridSpec(
            num_scalar_prefetch=2, grid=(rows // tr, 4),
            in_specs=[pl.BlockSpec((None, tr, cols), lambda i, q, c_ref, q_ref: (2 * q + c_ref[0], i, 0)),
                      pl.BlockSpec((None, tr, cols), lambda i, q, c_ref, q_ref: (q, i, 0))],
            out_specs=[pl.BlockSpec((None, tr, cols), lambda i, q, c_ref, q_ref: (q, i, 0)),
                       pl.BlockSpec((tr, cols), lambda i, q, c_ref, q_ref: (i, 0))]),
        out_shape=[jax.ShapeDtypeStruct((4, rows, cols), BF16), jax.ShapeDtypeStruct((rows, cols), F32)],
        compiler_params=_params(("parallel", "arbitrary")),
    )(c_idx, q_me, g_f32, landed)


def _adamw(w, g, m, v):
    m = ADAM_B1 * m + (1.0 - ADAM_B1) * g
    v = ADAM_B2 * v + (1.0 - ADAM_B2) * (g * g)
    m_hat = m / (1.0 - ADAM_B1 ** ADAM_STEP)
    v_hat = v / (1.0 - ADAM_B2 ** ADAM_STEP)
    delta = -ADAM_LR * (m_hat / (jnp.sqrt(v_hat) + ADAM_EPS) + ADAM_WD * w)
    return delta, m, v


def _shard_update(name, own, landed, w, m, v):
    rows, cols = own.shape
    tr = min(rows, 256)

    def body(own_ref, l_ref, w_ref, m_ref, v_ref, g_out, d_out, m_out, v_out):
        g = own_ref[...] + l_ref[0].astype(F32) + l_ref[1].astype(F32) + l_ref[2].astype(F32)
        d, m_new, v_new = _adamw(w_ref[...], g, m_ref[...], v_ref[...])
        g_out[...] = g
        d_out[...] = d
        m_out[...] = m_new
        v_out[...] = v_new

    tile = pl.BlockSpec((tr, cols), lambda i: (i, 0))
    return pl.pallas_call(
        body, name=name, grid=(rows // tr,),
        in_specs=[tile, pl.BlockSpec((3, tr, cols), lambda i: (0, i, 0)), tile, tile, tile],
        out_specs=[tile] * 4,
        out_shape=[jax.ShapeDtypeStruct((rows, cols), F32)] * 4,
        compiler_params=_params(("parallel",)),
    )(own, landed, w, m, v)


def _small_update(parts, w, m, v):
    def body(p_ref, w_ref, m_ref, v_ref, g_out, d_out, m_out, v_out):
        g = p_ref[0]
        for d in range(1, N_DEV):
            g = g + p_ref[d]
        dl, m_new, v_new = _adamw(w_ref[...], g, m_ref[...], v_ref[...])
        g_out[...] = g
        d_out[...] = dl
        m_out[...] = m_new
        v_out[...] = v_new

    return pl.pallas_call(
        body, name="small_update",
        out_shape=[jax.ShapeDtypeStruct(w.shape, F32)] * 4,
        compiler_params=pltpu.CompilerParams(vmem_limit_bytes=VMEM_LIMIT_BYTES),
    )(parts, w, m, v)


def _cumsum_matrix(inclusive):
    j = lax.broadcasted_iota(jnp.int32, (2 * LANES, 2 * LANES), 0) % LANES
    s = lax.broadcasted_iota(jnp.int32, (2 * LANES, 2 * LANES), 1)
    keep = (s >= LANES) | ((j >= s) if inclusive else (j > s))
    return keep.astype(BF16)


KEY_CHUNK = 384
SUB_TILES = KEY_CHUNK // Q_BLOCK
Q_GROUP = 4
DEAD_ABOVE = 105.0
MASKED_Z = -1e30


def _scores_phase(qms, kcs, masks):
    zs = [lax.dot_general(q, k, (_NT, ((), ())), preferred_element_type=F32) for q, k in zip(qms, kcs)]
    return [jnp.where(m, z, MASKED_Z) for z, m in zip(zs, masks)]


def _logs_phase(zs):
    sps = [jnp.maximum(z, 0.0) + jnp.log(1.0 + jnp.exp(-jnp.abs(z))) for z in zs]
    return sps, [z - sp for z, sp in zip(zs, sps)]


def _suffix_phase(vals, carries, mat):
    def hi_lo(tile):
        hi = tile.astype(BF16)
        return jnp.concatenate([hi, (tile - hi.astype(F32)).astype(BF16)], axis=1)

    packed = [[hi_lo(v[:, t * Q_BLOCK:(t + 1) * Q_BLOCK]) for t in range(SUB_TILES)] for v in vals]
    prods = [[jnp.dot(p, mat, preferred_element_type=F32) for p in chain] for chain in packed]
    sums, new_carries = [], []
    for chain, carry in zip(prods, carries):
        parts = []
        for t in reversed(range(SUB_TILES)):
            parts.append(chain[t][:, :LANES] + carry)
            carry = carry + chain[t][:, LANES:]
        sums.append(jnp.concatenate(parts[::-1], axis=1))
        new_carries.append(carry)
    return sums, new_carries


def _sweep(first_block, step, states, carries_of):
    n_chunks = ((first_block + Q_GROUP) * Q_BLOCK + KEY_CHUNK - 1) // KEY_CHUNK
    col = lax.broadcasted_iota(jnp.int32, (Q_BLOCK, KEY_CHUNK), 1)
    row = lax.broadcasted_iota(jnp.int32, (Q_BLOCK, 1), 0)

    def body(loop):
        n, _, sts = loop
        starts, masks = [], []
        for b in range(Q_GROUP):
            i = first_block + b
            end = (i + 1) * Q_BLOCK - n * KEY_CHUNK
            starts.append(pl.multiple_of(jnp.maximum(end - KEY_CHUNK, 0), Q_BLOCK))
            masks.append(col < jnp.minimum(i * Q_BLOCK + row, end) - starts[b])
        sts = step(starts, masks, sts)
        least = None
        for st in sts:
            for carry in carries_of(st):
                least = carry if least is None else jnp.minimum(least, carry)
        alive = (jnp.min(least) <= DEAD_ABOVE).astype(jnp.int32)
        return n + 1, alive, tuple(sts)

    return lax.while_loop(lambda loop: (loop[0] < n_chunks) & (loop[1] > 0), body,
                          (jnp.int32(0), jnp.int32(1), tuple(states)))[2]


CHAINS = [(b, h) for b in range(Q_GROUP) for h in range(2)]


def _head_rows(block, lane_lo_rows):
    zero = jnp.zeros_like(block)
    return jnp.concatenate([jnp.where(lane_lo_rows, block, zero), jnp.where(lane_lo_rows, zero, block)], axis=0)


def _attn_fwd(qkv, g_attn, ymix_in, width, shards):
    t_len = qkv.shape[0]
    npair = width // LANES
    rows_per_step = Q_GROUP * Q_BLOCK
    n_steps = t_len // rows_per_step
    n_sh = len(shards)

    def body(q_ref, k_ref, v_ref, g_ref, umat_ref, ymix_in_ref, *rest):
        shard_refs, rest = rest[:n_sh], rest[n_sh:]
        o_ref, ymix_ref = rest[:2]
        gathered_refs, sems = rest[2:2 + n_sh], rest[2 + n_sh:]
        p, i = pl.program_id(0), pl.program_id(1)
        start, forward, finish = _gather_plan(shard_refs, gathered_refs, *sems)
        pl.when((p == 0) & (i == 0))(start)
        pl.when((p == npair - 1) & (i == n_steps // 2))(forward)
        lane_lo = lax.broadcasted_iota(jnp.int32, (rows_per_step, LANES), 1) < HEAD_DIM
        lane_lo_keys = lax.broadcasted_iota(jnp.int32, (KEY_CHUNK, LANES), 1) < HEAD_DIM
        umat = umat_ref[...]
        q = q_ref[...] * jnp.asarray(HEAD_DIM ** -0.5, BF16)
        q_heads = (jnp.where(lane_lo, q, jnp.zeros_like(q)), jnp.where(lane_lo, jnp.zeros_like(q), q))

        def step(starts, masks, states):
            kcs = [k_ref[pl.ds(r, KEY_CHUNK), :] for r in starts]
            zs = _scores_phase([q_heads[h][b * Q_BLOCK:(b + 1) * Q_BLOCK] for b, h in CHAINS],
                               [kcs[b] for b, _ in CHAINS], [masks[b] for b, _ in CHAINS])
            sps, lbs = _logs_phase(zs)
            suffixes, carries = _suffix_phase(sps, [states[b][0][h] for b, h in CHAINS], umat)
            weights = [jnp.exp(lb - sfx).astype(BF16) for lb, sfx in zip(lbs, suffixes)]
            new = []
            for b in range(Q_GROUP):
                v_heads = _head_rows(v_ref[pl.ds(starts[b], KEY_CHUNK), :], lane_lo_keys)
                o_acc = states[b][1] + jnp.dot(jnp.concatenate(weights[2 * b:2 * b + 2], axis=1), v_heads,
                                               preferred_element_type=F32)
                new.append(((carries[2 * b], carries[2 * b + 1]), o_acc))
            return new

        zero = jnp.zeros((Q_BLOCK, LANES), F32)
        states = _sweep(i * Q_GROUP, step, [((zero, zero), zero)] * Q_GROUP, lambda st: st[0])
        o = jnp.concatenate([st[1] for st in states], axis=0)
        o_ref[...] = o
        rs = lax.rsqrt(_group_sums(o * o, lane_lo) * (1.0 / HEAD_DIM) + RMS_EPS)
        ymix_ref[...] = (o * rs * g_ref[...]).astype(BF16)
        pl.when((p == npair - 1) & (i == n_steps - 1))(finish)

    hbm = pl.BlockSpec(memory_space=pl.ANY)
    outs = pl.pallas_call(
        body, name="attn_fwd", grid=(npair, n_steps),
        in_specs=[pl.BlockSpec((rows_per_step, LANES), lambda p, i: (i, p)),
                  pl.BlockSpec((t_len, LANES), lambda p, i: (0, npair + p)),
                  pl.BlockSpec((t_len, LANES), lambda p, i: (0, 2 * npair + p)),
                  pl.BlockSpec((1, LANES), lambda p, i: (0, p)),
                  pl.BlockSpec((2 * LANES, 2 * LANES), lambda p, i: (0, 0)),
                  hbm] + [hbm] * n_sh,
        out_specs=[pl.BlockSpec((rows_per_step, LANES), lambda p, i: (i, p)),
                   pl.BlockSpec((rows_per_step, LANES), lambda p, i: (i, npair + p))] + [hbm] * n_sh,
        out_shape=[jax.ShapeDtypeStruct((t_len, width), F32), jax.ShapeDtypeStruct(ymix_in.shape, BF16)]
        + [jax.ShapeDtypeStruct((N_DEV,) + s.shape, s.dtype) for s in shards],
        input_output_aliases={5: 1},
        scratch_shapes=_gather_scratch(n_sh),
        compiler_params=_params(("arbitrary", "arbitrary")),
    )(qkv, qkv, qkv, g_attn, _cumsum_matrix(False), ymix_in, *shards)
    return outs[0], outs[1], outs[2:]


def _attn_bwd(qkv, o, dymix, g_attn, width, exchange):
    t_len = qkv.shape[0]
    npair = width // LANES
    rows_per_step = Q_GROUP * Q_BLOCK
    n_steps = t_len // rows_per_step
    scale = HEAD_DIM ** -0.5
    n_ex = len(exchange)

    def body(q_ref, k_ref, v_ref, o_ref, dy_ref, g_ref, umat_ref, gmat_ref, *rest):
        p_refs, rest = rest[:n_ex], rest[n_ex:]
        dq_ref, dk_ref, dv_ref, gg_ref = rest[:4]
        land_refs, rest = rest[4:4 + n_ex], rest[4 + n_ex:]
        dk_s, dv_s = rest[:2]
        copies = _chip_copies(p_refs, land_refs, *rest[2:])
        p, i = pl.program_id(0), pl.program_id(1)

        @pl.when((p == 0) & (i == 0))
        def _():
            for cp in copies:
                cp.start()

        lane_lo = lax.broadcasted_iota(jnp.int32, (rows_per_step, LANES), 1) < HEAD_DIM
        lane_lo_keys = lax.broadcasted_iota(jnp.int32, (KEY_CHUNK, LANES), 1) < HEAD_DIM
        umat, gmat = umat_ref[...], gmat_ref[...]

        @pl.when(i == 0)
        def _():
            dk_s[...] = jnp.zeros_like(dk_s)
            dv_s[...] = jnp.zeros_like(dv_s)

        o_blk = o_ref[...]
        rs = lax.rsqrt(_group_sums(o_blk * o_blk, lane_lo) * (1.0 / HEAD_DIM) + RMS_EPS)
        yn = o_blk * rs
        dy_raw = dy_ref[...]
        dyn = dy_raw * g_ref[...]
        d_o = rs * (dyn - yn * (_group_sums(dyn * yn, lane_lo) * (1.0 / HEAD_DIM)))
        _accumulate(gg_ref, i, n_steps, _rows8(dy_raw * yn))
        do_bf = d_o.astype(BF16)
        do_o = do_bf.astype(F32) * o_blk
        q = q_ref[...]
        blocks = [slice(b * Q_BLOCK, (b + 1) * Q_BLOCK) for b in range(Q_GROUP)]
        lane_lo_q = lane_lo[:Q_BLOCK]
        totals = [(jnp.sum(jnp.where(lane_lo_q, do_o[r], 0.0), axis=-1, keepdims=True),
                   jnp.sum(jnp.where(lane_lo_q, 0.0, do_o[r]), axis=-1, keepdims=True)) for r in blocks]
        q_rows = [_head_rows(q[r], lane_lo_q) for r in blocks]
        do_rows = [_head_rows(do_bf[r], lane_lo_q) for r in blocks]
        q_scaled = [qr * jnp.asarray(scale, BF16) for qr in q_rows]

        def step(starts, masks, states):
            kcs = [k_ref[pl.ds(r, KEY_CHUNK), :] for r in starts]
            vcs = [v_ref[pl.ds(r, KEY_CHUNK), :] for r in starts]
            head = lambda h: slice(h * Q_BLOCK, (h + 1) * Q_BLOCK)
            zs = _scores_phase([q_scaled[b][head(h)] for b, h in CHAINS], [kcs[b] for b, _ in CHAINS],
                               [masks[b] for b, _ in CHAINS])
            das = [lax.dot_general(do_rows[b][head(h)], vcs[b], (_NT, ((), ())), preferred_element_type=F32)
                   for b, h in CHAINS]
            sps, lbs = _logs_phase(zs)
            suffixes, carries = _suffix_phase(sps, [states[b][0][h] for b, h in CHAINS], umat)
            a_bf = [jnp.exp(lb - sfx).astype(BF16) for lb, sfx in zip(lbs, suffixes)]
            gs = [a.astype(F32) * da for a, da in zip(a_bf, das)]
            suffixes_g, carries_g = _suffix_phase(gs, [states[b][1][h] for b, h in CHAINS], gmat)
            dzs = []
            for (b, h), lb, g, sfx_g in zip(CHAINS, lbs, gs, suffixes_g):
                before = totals[b][h] - sfx_g
                dzs.append((g - jnp.exp(lb) * (g + before)).astype(BF16))
            new = []
            for b in range(Q_GROUP):
                pair = slice(2 * b, 2 * b + 2)
                dq_acc = states[b][2] + jnp.dot(jnp.concatenate(dzs[pair], axis=1), _head_rows(kcs[b], lane_lo_keys),
                                                preferred_element_type=F32)
                dk_s[pl.ds(starts[b], KEY_CHUNK), :] += lax.dot_general(
                    jnp.concatenate(dzs[pair], axis=0), q_rows[b], (_TN, ((), ())), preferred_element_type=F32)
                dv_s[pl.ds(starts[b], KEY_CHUNK), :] += lax.dot_general(
                    jnp.concatenate(a_bf[pair], axis=0), do_rows[b], (_TN, ((), ())), preferred_element_type=F32)
                new.append(((carries[2 * b], carries[2 * b + 1]), (carries_g[2 * b], carries_g[2 * b + 1]), dq_acc))
            return new

        zero = jnp.zeros((Q_BLOCK, LANES), F32)
        states = _sweep(i * Q_GROUP, step, [((zero, zero), (zero, zero), zero)] * Q_GROUP, lambda st: st[0])
        dq_ref[...] = (jnp.concatenate([st[2] for st in states], axis=0) * scale).astype(BF16)

        @pl.when(i == n_steps - 1)
        def _():
            dk_ref[...] = (dk_s[...] * scale).astype(BF16)
            dv_ref[...] = dv_s[...].astype(BF16)

        @pl.when((p == npair - 1) & (i == n_steps - 1))
        def _():
            for cp in copies:
                cp.wait()

    whole = pl.BlockSpec((t_len, LANES), lambda p, i: (0, p))
    hbm = pl.BlockSpec(memory_space=pl.ANY)
    outs = pl.pallas_call(
        body, name="attn_bwd", grid=(npair, n_steps),
        in_specs=[pl.BlockSpec((rows_per_step, LANES), lambda p, i: (i, p)),
                  pl.BlockSpec((t_len, LANES), lambda p, i: (0, npair + p)),
                  pl.BlockSpec((t_len, LANES), lambda p, i: (0, 2 * npair + p)),
                  pl.BlockSpec((rows_per_step, LANES), lambda p, i: (i, p)),
                  pl.BlockSpec((rows_per_step, LANES), lambda p, i: (i, p)),
                  pl.BlockSpec((1, LANES), lambda p, i: (0, p)),
                  pl.BlockSpec((2 * LANES, 2 * LANES), lambda p, i: (0, 0)),
                  pl.BlockSpec((2 * LANES, 2 * LANES), lambda p, i: (0, 0))] + [hbm] * n_ex,
        out_specs=[pl.BlockSpec((rows_per_step, LANES), lambda p, i: (i, p)), whole, whole,
                   pl.BlockSpec((8, LANES), lambda p, i: (0, p))] + [hbm] * n_ex,
        out_shape=[jax.ShapeDtypeStruct((t_len, width), BF16)] * 3 + [jax.ShapeDtypeStruct((8, width), F32)]
        + _chip_landing(exchange),
        scratch_shapes=[pltpu.VMEM((t_len, LANES), F32), pltpu.VMEM((t_len, LANES), F32)] + _chip_scratch(n_ex),
        compiler_params=_params(("arbitrary", "arbitrary")),
    )(qkv, qkv, qkv, o, dymix, g_attn, _cumsum_matrix(False), _cumsum_matrix(True), *exchange)
    return outs[0], outs[1], outs[2], outs[3], outs[4:]


def kernel(x, w_in, conv_w, g_conv, g_attn, w_out, ln1_g, ln1_b, w_up, w_down, ln2_g, ln2_b, loss_target, m_w_in, m_conv_w, m_g_conv, m_g_attn, m_w_out, m_ln1_g, m_ln1_b, m_w_up, m_w_down, m_ln2_g, m_ln2_b, v_w_in, v_conv_w, v_g_conv, v_g_attn, v_w_out, v_ln1_g, v_ln1_b, v_w_up, v_w_down, v_ln2_g, v_ln2_b):
    x2, target = x[0], loss_target[0]
    t_len, d_model = x2.shape
    width = g_conv.shape[1]
    in_shard = w_in.shape[2]
    up_shard = w_up.shape[2]
    d_ff = up_shard * N_DEV
    cw = conv_w.shape[2]
    xi, yi, ci = _place()
    dev = 4 * xi + 2 * yi + ci
    tm, sub = 512, 2
    tr = tm // sub
    mm = functools.partial(_matmul, sub=sub)

    win_g, convw_g, wout_bf, wup_bf, wdown_bf = _all_gather(
        "gather_w_in", [w_in[0], jnp.pad(conv_w[0], ((0, 5), (0, LANES - cw)))], [BF16, F32],
        cast_only=[w_out[0], w_up[0], w_down[0]])
    conv_full = jnp.transpose(convw_g[:, :3, :cw], (1, 0, 2)).reshape(3, width)

    n_row_tiles = t_len // tm
    n_sub_tiles = n_row_tiles * sub
    rows = lambda cols: pl.BlockSpec((tm, cols), lambda i: (i, 0))
    resident = lambda arr: pl.BlockSpec(arr.shape, lambda i: (0,) * arr.ndim)
    vec = pl.BlockSpec((1, d_model), lambda i: (0, 0))
    acc8 = pl.BlockSpec((8, d_model), lambda i: (0, 0))
    tokens = lambda cols, dtype: jax.ShapeDtypeStruct((t_len, cols), dtype)

    def proj_in_epilogue(acc, extra, outs, i):
        x_ref, w_ref, gain_ref, tail_ref = extra
        gates_ref, qkv_ref, xbf_ref, ymix_ref = outs
        gates_ref[...] = acc[:, :3 * width]
        qkv_ref[...] = acc[:, 3 * width:].astype(BF16)
        xbf_ref[...] = x_ref[...].astype(BF16)

        @pl.when(i == 0)
        def _():
            tail_ref[...] = jnp.zeros_like(tail_ref)

        u = acc[:, width:2 * width] * acc[:, 2 * width:3 * width]
        ext = jnp.concatenate([tail_ref[...], u], axis=0)
        tail_ref[...] = u[tr - 8:]
        w = w_ref[...]
        y = acc[:, :width] * (w[2:3] * u + w[1:2] * pltpu.roll(ext, 1, 0)[8:] + w[0:1] * pltpu.roll(ext, 2, 0)[8:])
        lane_lo = lax.broadcasted_iota(jnp.int32, (tr, LANES), 1) < HEAD_DIM
        gain = gain_ref[...]
        for blk in range(width // LANES):
            lanes = slice(blk * LANES, (blk + 1) * LANES)
            y_blk = y[:, lanes]
            rs = lax.rsqrt(_group_sums(y_blk * y_blk, lane_lo) * (1.0 / HEAD_DIM) + RMS_EPS)
            ymix_ref[:, lanes] = (y_blk * rs * gain[:, lanes]).astype(BF16)

    half_vec = pl.BlockSpec((1, width), lambda i: (0, 0))
    gates, qkv, x_bf, ymix, wdown_g = mm(
        "proj_in", x2, win_g, rows(d_model), resident(win_g), _shards_wide, n_row_tiles,
        [(tokens(3 * width, F32), rows(3 * width)), (tokens(3 * width, BF16), rows(3 * width)),
         (tokens(d_model, BF16), rows(d_model)), (tokens(2 * width, BF16), rows(width))],
        proj_in_epilogue,
        extras=[(x2, rows(d_model)), (conv_full, pl.BlockSpec((3, width), lambda i: (0, 0))), (g_conv, half_vec)],
        gather=[wdown_bf], carry=[pltpu.VMEM((8, width), F32)])

    o_attn, ymix, (wout_g, wup_g) = _attn_fwd(qkv, g_attn, ymix, width, [wout_bf, wup_bf])
    wout_full = wout_g.reshape(d_model, d_model)
    wdown_full = wdown_g.reshape(d_ff, d_model)

    def out_ln1_epilogue(acc, extra, outs, i):
        x_ref, g_ref, b_ref = extra
        r1 = ALPHA * x_ref[...] + acc
        x1, _, _ = _ln_fwd(r1, g_ref[...], b_ref[...])
        outs[0][...] = r1
        outs[1][...] = x1
        outs[2][...] = x1.astype(BF16)

    r1, x1, x1_bf = mm(
        "proj_out_ln1", ymix, wout_full, rows(d_model), resident(wout_full), _whole(_NN), n_row_tiles,
        [(tokens(d_model, F32), rows(d_model)), (tokens(d_model, F32), rows(d_model)),
         (tokens(d_model, BF16), rows(d_model))],
        out_ln1_epilogue, extras=[(x2, rows(d_model)), (ln1_g, vec), (ln1_b, vec)])

    def up_epilogue(acc, extra, outs, i):
        relu = jnp.maximum(acc, 0.0)
        outs[0][...] = (relu * relu).astype(BF16)

    (hid,) = mm(
        "ffn_up", x1_bf, wup_g, rows(d_model), resident(wup_g), _shards_wide, n_row_tiles,
        [(tokens(d_ff, BF16), rows(d_ff))], up_epilogue)

    def down_ln2_epilogue(acc, extra, outs, i):
        x1_ref, t_ref, g_ref, b_ref = extra
        dr2_ref, dr2_bf_ref, loss_ref, gg_ref, gb_ref = outs
        gain = g_ref[...]
        r2 = ALPHA * x1_ref[...] + acc
        y, xhat, rstd = _ln_fwd(r2, gain, b_ref[...])
        diff = y - t_ref[...]
        d_y = diff * (1.0 / d_model)
        dr2 = _ln_bwd(d_y, xhat, rstd, gain)
        dr2_ref[...] = dr2
        dr2_bf_ref[...] = dr2.astype(BF16)
        _accumulate(gg_ref, i, n_sub_tiles, _rows8(d_y * xhat))
        _accumulate(gb_ref, i, n_sub_tiles, _rows8(d_y))
        _accumulate(loss_ref, i, n_sub_tiles, _rows8(diff * diff), all_lanes=True)

    small_acc = (jax.ShapeDtypeStruct((8, d_model), F32), acc8)
    short = lambda cols: pl.BlockSpec((tr, cols), lambda i: (i, 0))
    dr2, dr2_bf, loss_cols, g_ln2_g, g_ln2_b = _matmul(
        "ffn_down_ln2_loss", hid, wdown_full, short(d_ff), resident(wdown_full), _whole(_NN), n_sub_tiles,
        [(tokens(d_model, F32), short(d_model)), (tokens(d_model, BF16), short(d_model)),
         small_acc, small_acc, small_acc],
        down_ln2_epilogue, extras=[(x1, short(d_model)), (target, short(d_model)), (ln2_g, vec), (ln2_b, vec)])

    def dup_epilogue(acc, extra, outs, i):
        outs[0][...] = (acc * (2.0 * jnp.sqrt(extra[0][...].astype(F32)))).astype(BF16)

    (dup,) = mm(
        "ffn_dhid", dr2_bf, wdown_full, rows(d_model), resident(wdown_full), _whole(_NT), n_row_tiles,
        [(tokens(d_ff, BF16), rows(d_ff))], dup_epilogue, extras=[(hid, rows(d_ff))])

    c_idx = jnp.reshape(ci, (1,)).astype(jnp.int32)
    q_me = jnp.reshape(2 * xi + yi, (1,)).astype(jnp.int32)
    gw_down, pair_down = _weight_grad("grad_w_down", hid, dr2_bf, (d_ff // N_DEV, d_model), True, c_idx)
    gw_up, pair_up = _weight_grad("grad_w_up", x1_bf, dup, (d_model, up_shard), False, c_idx)

    def dx1_ln1_epilogue(acc, extra, outs, i):
        dr2_ref, r1_ref, g_ref, b_ref = extra
        dr1_ref, dr1_bf_ref, gg_ref, gb_ref = outs
        gain = g_ref[...]
        dx1 = ALPHA * dr2_ref[...] + acc
        _, xhat, rstd = _ln_fwd(r1_ref[...], gain, b_ref[...])
        dr1 = _ln_bwd(dx1, xhat, rstd, gain)
        dr1_ref[...] = dr1
        dr1_bf_ref[...] = dr1.astype(BF16)
        _accumulate(gg_ref, i, n_sub_tiles, _rows8(dx1 * xhat))
        _accumulate(gb_ref, i, n_sub_tiles, _rows8(dx1))

    dr1, dr1_bf, g_ln1_g, g_ln1_b = _matmul(
        "ffn_dx1_ln1", dup, wup_g, short(d_ff), resident(wup_g), _shards_deep, n_sub_tiles,
        [(tokens(d_model, F32), short(d_model)), (tokens(d_model, BF16), short(d_model)), small_acc, small_acc],
        dx1_ln1_epilogue, extras=[(dr2, short(d_model)), (r1, short(d_model)), (ln1_g, vec), (ln1_b, vec)])

    gw_out, pair_out = _weight_grad("grad_w_out", ymix, dr1_bf, (d_model // N_DEV, d_model), True, c_idx)


    def pair_sums(names, grads, landed):
        sums = [_pair_sum("pair_sum_" + nm, g, land, c_idx, q_me) for nm, g, land in zip(names, grads, landed)]
        return [s[0] for s in sums], [s[1] for s in sums]

    wire_ffn, own_ffn = pair_sums(["w_out", "w_up", "w_down"], [gw_out, gw_up, gw_down],
                                  [pair_out, pair_up, pair_down])

    n_back = t_len // tr
    back = lambda cols: pl.BlockSpec((tr, cols), lambda j: (n_back - 1 - j, 0))
    above = pl.BlockSpec((8, 3 * width), lambda j: (jnp.maximum((n_back - 1 - j) * (tr // 8) - 1, 0), 0))

    def out_bwd_epilogue(acc, extra, outs, j):
        gates_ref, above_ref, w_ref, gain_ref, head_ref = extra
        dbg_ref, dcg_ref, dh_ref, dya_ref, gg_ref, gw0_ref, gw1_ref, gw2_ref = outs
        dya_ref[...] = acc[:, width:]
        dy = acc[:, :width]
        bg, cg, h = gates_ref[:, :width], gates_ref[:, width:2 * width], gates_ref[:, 2 * width:]
        u = cg * h
        u_above = jnp.where(j < n_back - 1, above_ref[:, width:2 * width] * above_ref[:, 2 * width:], 0.0)
        ext = jnp.concatenate([u_above, u], axis=0)
        u1, u2 = pltpu.roll(ext, 1, 0)[8:], pltpu.roll(ext, 2, 0)[8:]
        w = w_ref[...]
        conv = w[2:3] * u + w[1:2] * u1 + w[0:1] * u2
        y = bg * conv
        lane_lo = lax.broadcasted_iota(jnp.int32, (tr, LANES), 1) < HEAD_DIM
        gain = gain_ref[...]
        d_y, yn = [], []
        for blk in range(width // LANES):
            lanes = slice(blk * LANES, (blk + 1) * LANES)
            y_blk = y[:, lanes]
            rs = lax.rsqrt(_group_sums(y_blk * y_blk, lane_lo) * (1.0 / HEAD_DIM) + RMS_EPS)
            yn_blk = y_blk * rs
            dyn = dy[:, lanes] * gain[:, lanes]
            d_y.append(rs * (dyn - yn_blk * (_group_sums(dyn * yn_blk, lane_lo) * (1.0 / HEAD_DIM))))
            yn.append(yn_blk)
        d_y, yn = jnp.concatenate(d_y, axis=1), jnp.concatenate(yn, axis=1)
        dbg_ref[...] = (d_y * conv).astype(BF16)
        dconv = d_y * bg

        @pl.when(j == 0)
        def _():
            head_ref[...] = jnp.zeros_like(head_ref)

        ext = jnp.concatenate([dconv, head_ref[...]], axis=0)
        head_ref[...] = dconv[:8]
        du = w[2:3] * dconv + w[1:2] * pltpu.roll(ext, tr + 7, 0)[:tr] + w[0:1] * pltpu.roll(ext, tr + 6, 0)[:tr]
        dcg_ref[...] = (du * h).astype(BF16)
        dh_ref[...] = (du * cg).astype(BF16)
        _accumulate(gg_ref, j, n_back, _rows8(dy * yn))
        _accumulate(gw0_ref, j, n_back, _rows8(dconv * u2))
        _accumulate(gw1_ref, j, n_back, _rows8(dconv * u1))
        _accumulate(gw2_ref, j, n_back, _rows8(dconv * u))

    half_acc = (jax.ShapeDtypeStruct((8, width), F32), pl.BlockSpec((8, width), lambda j: (0, 0)))
    d_bg, d_cg, d_h, dy_attn, g_gconv, gw0, gw1, gw2 = _matmul(
        "proj_out_bwd", dr1_bf, wout_full, back(d_model), resident(wout_full), _whole(_NT), n_back,
        [(tokens(width, BF16), back(width))] * 3 + [(tokens(width, F32), back(width))] + [half_acc] * 4,
        out_bwd_epilogue,
        extras=[(gates, back(3 * width)), (gates, above), (conv_full, pl.BlockSpec((3, width), lambda j: (0, 0))),
                (g_conv, half_vec)],
        carry=[pltpu.VMEM((8, width), F32)])
    g_convw = jnp.concatenate([gw0[:1], gw1[:1], gw2[:1]], axis=0)

    d_q, d_k, d_v, g_gattn, landed_ffn = _attn_bwd(qkv, o_attn, dy_attn, g_attn, width, wire_ffn)
    dproj = jnp.concatenate([d_bg, d_cg, d_h, d_q, d_k, d_v], axis=1)

    gw_in, pair_in = _weight_grad("grad_w_in", x_bf, dproj, (d_model, in_shard), False, c_idx)
    wire_in, own_in = pair_sums(["w_in"], [gw_in], [pair_in])

    def dx_epilogue(acc, extra, outs, i):
        outs[0][...] = ALPHA * extra[0][...] + acc

    spare = jnp.zeros((3, d_model - width), F32)

    def pack(l1g, l1b, l2g, l2b, gc, ga, taps, tail=spare):
        return jnp.concatenate([l1g, l1b, l2g, l2b, jnp.concatenate([gc, ga], axis=1),
                                jnp.concatenate([taps, tail], axis=1)], axis=0)

    local = pack(g_ln1_g[:1], g_ln1_b[:1], g_ln2_g[:1], g_ln2_b[:1], g_gconv[:1], g_gattn[:1], g_convw[:3],
                 spare + (0.5 / d_model) * loss_cols[0, 0])

    grad_x, landed_in, parts = mm(
        "proj_in_bwd", dproj, win_g, rows(6 * width), resident(win_g), _shards_deep, n_row_tiles,
        [(tokens(d_model, F32), rows(d_model))], dx_epilogue, extras=[(dr1, rows(d_model))],
        exchange=wire_in, gather=[local])

    names = ["w_in", "w_out", "w_up", "w_down"]
    big = {}
    for nm, ow, land, w, m, v in zip(names, own_in + own_ffn, [landed_in] + list(landed_ffn),
                                     (w_in, w_out, w_up, w_down),
                                     (m_w_in, m_w_out, m_w_up, m_w_down), (v_w_in, v_w_out, v_w_up, v_w_down)):
        big[nm] = [r[None] for r in _shard_update("update_" + nm, ow, land, w[0], m[0], v[0])]

    def spread(a):
        return lax.dynamic_update_slice(jnp.zeros((3, width), F32), a, (0, dev * cw))

    w_small = pack(ln1_g, ln1_b, ln2_g, ln2_b, g_conv, g_attn, spread(conv_w[0]))
    m_small = pack(m_ln1_g, m_ln1_b, m_ln2_g, m_ln2_b, m_g_conv, m_g_attn, spread(m_conv_w[0]))
    v_small = pack(v_ln1_g, v_ln1_b, v_ln2_g, v_ln2_b, v_g_conv, v_g_attn, spread(v_conv_w[0]))
    small = _small_update(parts, w_small, m_small, v_small)

    def unpack(a):
        taps = lax.dynamic_slice(a[5:8, :width], (0, dev * cw), (3, cw))[None]
        return {"ln1_g": a[0:1], "ln1_b": a[1:2], "ln2_g": a[2:3], "ln2_b": a[3:4],
                "g_conv": a[4:5, :width], "g_attn": a[4:5, width:], "conv_w": taps}

    loss = small[0][7, d_model - 1]
    small = [unpack(a) for a in small]

    order = ["w_in", "conv_w", "g_conv", "g_attn", "w_out", "ln1_g", "ln1_b", "w_up", "w_down", "ln2_g", "ln2_b"]
    result = [loss, grad_x[None]]
    for kind in range(4):
        for nm in order:
            result.append(big[nm][kind] if nm in big else small[kind][nm])
    return tuple(result)
```
